```python
import math
import jax, jax.numpy as jnp
from jax import lax
import numpy as np

D_MODEL = 1024
BATCH = 8
SEQ = 2048
DEPTH = 2
DEC_BATCH = 128
DEC_SEQ = 4
PAST_LEN = 16384
PAGE_SIZE = 128

N_META = 16
D_MIX = 2 * D_MODEL
D_SSD = D_MIX // 2
D_LRU = D_MIX - D_SSD
SSD_HEAD_DIM = 64
SSD_HEADS = D_SSD // SSD_HEAD_DIM
SSD_GROUPS = 2
SSD_STATE = 128
SSD_CHUNK = 128
CONV_W = 4
SSD_CONV_DIM = D_SSD + 2 * SSD_GROUPS * SSD_STATE
LRU_BLOCKS = 16
LRU_BLOCK_W = D_LRU // LRU_BLOCKS
LRU_C = 8.0
D_FF = ((8 * D_MODEL // 3 + 127) // 128) * 128
IN_COLS = D_SSD + SSD_CONV_DIM + SSD_HEADS + 2 * D_LRU
EPS = 1e-6

kernel_name = "hymba_ssd_rglru_macaron_step"


def rmsnorm(x, g):
    xf = x.astype(jnp.float32)
    y = xf * lax.rsqrt(jnp.mean(xf * xf, axis=-1, keepdims=True) + EPS) * g.astype(jnp.float32)
    return y.astype(x.dtype)


def swiglu(x, wg, wu, wd):
    return (jax.nn.silu(x @ wg) * (x @ wu)) @ wd


def causal_conv(x, buf, w, b):
    L = x.shape[1]
    xp = jnp.concatenate([buf.astype(x.dtype), x], axis=1)
    y = b + sum(xp[:, k:k + L] * w[k] for k in range(CONV_W))
    return y, xp[:, -(CONV_W - 1):]


def ssd_segment(x, dt, a, bm, cm, h0, chunk):
    b, L, H, P = x.shape
    G, N = bm.shape[2], bm.shape[3]
    R = H // G
    c = L // chunk
    xc = x.astype(jnp.float32).reshape(b, c, chunk, G, R, P)
    dtc = dt.reshape(b, c, chunk, G, R)
    bc = bm.astype(jnp.float32).reshape(b, c, chunk, G, N)
    cc = cm.astype(jnp.float32).reshape(b, c, chunk, G, N)
    cum = jnp.cumsum(dtc * a.reshape(G, R), axis=2)
    diff = cum[:, :, :, None] - cum[:, :, None, :]
    mask = jnp.tril(jnp.ones((chunk, chunk), bool))[:, :, None, None]
    decay = jnp.exp(jnp.where(mask, diff, -jnp.inf))
    cb = jnp.einsum('bcign,bcjgn->bcijg', cc, bc)
    scores = cb[..., None] * decay * dtc[:, :, None]
    y_diag = jnp.einsum('bcijgr,bcjgrp->bcigrp', scores, xc)
    w_end = jnp.exp(cum[:, :, -1:] - cum) * dtc
    states = jnp.einsum('bcjgn,bcjgr,bcjgrp->bcgrpn', bc, w_end, xc)
    chunk_decay = jnp.exp(cum[:, :, -1])

    def step(h, inp):
        s, d = inp
        return d[..., None, None] * h + s, h

    h_final, h_prev = lax.scan(step, h0.astype(jnp.float32).reshape(b, G, R, P, N),
                               (jnp.moveaxis(states, 1, 0), jnp.moveaxis(chunk_decay, 1, 0)))
    h_prev = jnp.moveaxis(h_prev, 0, 1)
    y_off = jnp.einsum('bcign,bcgrpn,bcigr->bcigrp', cc, h_prev, jnp.exp(cum))
    y = (y_diag + y_off).reshape(b, L, H, P)
    return y, h_final.reshape(b, H, P, N)


def mixer(u, ssd_h, ssd_cbuf, lru_h, lru_cbuf, segments, p):
    b, L, _ = u.shape
    proj = u @ p['w_in']
    z, xbc, dt_raw, gate, xr = jnp.split(
        proj, [D_SSD, D_SSD + SSD_CONV_DIM, D_SSD + SSD_CONV_DIM + SSD_HEADS,
               D_SSD + SSD_CONV_DIM + SSD_HEADS + D_LRU], axis=-1)
    xbc_c, new_ssd_cbuf = causal_conv(xbc, ssd_cbuf, p['ssd_conv_w'], p['ssd_conv_b'])
    xbc_c = jax.nn.silu(xbc_c)
    xs, bm, cm = jnp.split(xbc_c, [D_SSD, D_SSD + SSD_GROUPS * SSD_STATE], axis=-1)
    dt = jax.nn.softplus(dt_raw.astype(jnp.float32) + p['ssd_dt_bias'].astype(jnp.float32))
    a = -jnp.exp(p['ssd_a_log'].astype(jnp.float32))
    xh = xs.reshape(b, L, SSD_HEADS, SSD_HEAD_DIM)
    bm = bm.reshape(b, L, SSD_GROUPS, SSD_STATE)
    cm = cm.reshape(b, L, SSD_GROUPS, SSD_STATE)
    h = ssd_h
    ys = []
    start = 0
    for length, chunk in segments:
        y_seg, h = ssd_segment(xh[:, start:start + length], dt[:, start:start + length], a,
                               bm[:, start:start + length], cm[:, start:start + length], h, chunk)
        ys.append(y_seg)
        start += length
    y = jnp.concatenate(ys, axis=1) if len(ys) > 1 else ys[0]
    y = y + p['ssd_d'].astype(jnp.float32)[:, None] * xh.astype(jnp.float32)
    y = y.reshape(b, L, D_SSD) * jax.nn.silu(z.astype(jnp.float32))
    y_ssd = rmsnorm(y, p['ssd_norm_g'])
    xr_c, new_lru_cbuf = causal_conv(xr, lru_cbuf, p['lru_conv_w'], p['lru_conv_b'])
    xb = xr_c.reshape(b, L, LRU_BLOCKS, LRU_BLOCK_W)
    r = jax.nn.sigmoid((jnp.einsum('blki,kij->blkj', xb, p['lru_wa']).reshape(b, L, D_LRU)
                        + p['lru_ba']).astype(jnp.float32))
    ig = jax.nn.sigmoid((jnp.einsum('blki,kij->blkj', xb, p['lru_wx']).reshape(b, L, D_LRU)
                         + p['lru_bx']).astype(jnp.float32))
    log_a = -LRU_C * r * jax.nn.softplus(-p['lru_lambda'].astype(jnp.float32))
    a_t = jnp.exp(log_a)
    mult = jnp.sqrt(-jnp.expm1(2.0 * log_a))
    bt = mult * ig * xr_c.astype(jnp.float32)
    bt = bt.at[:, 0].add(a_t[:, 0] * lru_h.astype(jnp.float32))

    def comb(left, right):
        a1, b1 = left
        a2, b2 = right
        return a1 * a2, a2 * b1 + b2

    _, h_all = lax.associative_scan(comb, (a_t, bt), axis=1)
    y_lru = h_all * jax.nn.gelu(gate.astype(jnp.float32))
    out = jnp.concatenate([y_ssd.astype(jnp.float32), y_lru], axis=-1).astype(u.dtype) @ p['w_out']
    return out, h, new_ssd_cbuf, h_all[:, -1], new_lru_cbuf


def trunk_layer(x, ssd_h, ssd_cbuf, lru_h, lru_cbuf, segments, p):
    x = x + 0.5 * rmsnorm(swiglu(rmsnorm(x, p['ffn1_pre_g']), p['ffn1_wg'], p['ffn1_wu'], p['ffn1_wd']),
                          p['ffn1_post_g'])
    m, ssd_h, ssd_cbuf, lru_h, lru_cbuf = mixer(rmsnorm(x, p['mix_pre_g']), ssd_h, ssd_cbuf, lru_h,
                                                lru_cbuf, segments, p)
    x = x + rmsnorm(m, p['mix_post_g'])
    x = x + 0.5 * rmsnorm(swiglu(rmsnorm(x, p['ffn2_pre_g']), p['ffn2_wg'], p['ffn2_wu'], p['ffn2_wd']),
                          p['ffn2_post_g'])
    return x, ssd_h, ssd_cbuf, lru_h, lru_cbuf


def setup_inputs(seed: int = 0) -> dict:
    key = jax.random.key(seed)
    ks = jax.random.split(key, 40)
    f32 = jnp.float32
    L = DEPTH

    def nrm(k, shape, scale):
        return jax.random.normal(k, shape, f32) * scale

    def gain(k, shape):
        return 1.0 + 0.05 * jax.random.normal(k, shape, f32)

    dt0 = jnp.exp(jax.random.uniform(ks[17], (L, SSD_HEADS), f32, math.log(1e-3), math.log(1e-1)))
    a0 = jax.random.uniform(ks[27], (L, D_LRU), f32, 0.9, 0.999)
    s0 = a0 ** (1.0 / LRU_C)
    return {
        "x_prompt": nrm(ks[0], (BATCH, SEQ, D_MODEL), 1.0),
        "x_sample": nrm(ks[1], (DEC_BATCH, DEC_SEQ, D_MODEL), 1.0),
        "state_ssd": nrm(ks[2], (L, DEC_BATCH, SSD_HEADS, SSD_HEAD_DIM, SSD_STATE), 0.1),
        "state_ssd_conv": nrm(ks[3], (L, DEC_BATCH, CONV_W - 1, SSD_CONV_DIM), 1.0),
        "state_lru": nrm(ks[4], (L, DEC_BATCH, D_LRU), 0.5),
        "state_lru_conv": nrm(ks[5], (L, DEC_BATCH, CONV_W - 1, D_LRU), 1.0),
        "meta_tokens": nrm(ks[6], (N_META, D_MODEL), 1.0),
        "ffn1_pre_g": gain(ks[7], (L, D_MODEL)),
        "ffn1_post_g": gain(ks[8], (L, D_MODEL)),
        "ffn1_wg": nrm(ks[9], (L, D_MODEL, D_FF), D_MODEL ** -0.5),
        "ffn1_wu": nrm(ks[10], (L, D_MODEL, D_FF), D_MODEL ** -0.5),
        "ffn1_wd": nrm(ks[11], (L, D_FF, D_MODEL), D_FF ** -0.5),
        "mix_pre_g": gain(ks[12], (L, D_MODEL)),
        "mix_post_g": gain(ks[13], (L, D_MODEL)),
        "w_in": nrm(ks[14], (L, D_MODEL, IN_COLS), D_MODEL ** -0.5),
        "ssd_conv_w": nrm(ks[15], (L, CONV_W, SSD_CONV_DIM), CONV_W ** -0.5),
        "ssd_conv_b": nrm(ks[16], (L, SSD_CONV_DIM), 0.01),
        "ssd_dt_bias": dt0 + jnp.log(-jnp.expm1(-dt0)),
        "ssd_a_log": jnp.log(jax.random.uniform(ks[18], (L, SSD_HEADS), f32, 1.0, 16.0)),
        "ssd_d": gain(ks[19], (L, SSD_HEADS)),
        "ssd_norm_g": gain(ks[20], (L, D_SSD)),
        "lru_conv_w": nrm(ks[21], (L, CONV_W, D_LRU), CONV_W ** -0.5),
        "lru_conv_b": nrm(ks[22], (L, D_LRU), 0.01),
        "lru_wa": nrm(ks[23], (L, LRU_BLOCKS, LRU_BLOCK_W, LRU_BLOCK_W), LRU_BLOCK_W ** -0.5),
        "lru_ba": nrm(ks[24], (L, D_LRU), 0.01),
        "lru_wx": nrm(ks[25], (L, LRU_BLOCKS, LRU_BLOCK_W, LRU_BLOCK_W), LRU_BLOCK_W ** -0.5),
        "lru_bx": nrm(ks[26], (L, D_LRU), 0.01),
        "lru_lambda": jnp.log(s0) - jnp.log1p(-s0),
        "w_out": nrm(ks[28], (L, D_MIX, D_MODEL), D_MIX ** -0.5),
        "ffn2_pre_g": gain(ks[29], (L, D_MODEL)),
        "ffn2_post_g": gain(ks[30], (L, D_MODEL)),
        "ffn2_wg": nrm(ks[31], (L, D_MODEL, D_FF), D_MODEL ** -0.5),
        "ffn2_wu": nrm(ks[32], (L, D_MODEL, D_FF), D_MODEL ** -0.5),
        "ffn2_wd": nrm(ks[33], (L, D_FF, D_MODEL), D_FF ** -0.5),
    }


def reference(x_prompt, x_sample, state_ssd, state_ssd_conv, state_lru, state_lru_conv, meta_tokens,
              ffn1_pre_g, ffn1_post_g, ffn1_wg, ffn1_wu, ffn1_wd, mix_pre_g, mix_post_g, w_in,
              ssd_conv_w, ssd_conv_b, ssd_dt_bias, ssd_a_log, ssd_d, ssd_norm_g, lru_conv_w, lru_conv_b,
              lru_wa, lru_ba, lru_wx, lru_bx, lru_lambda, w_out, ffn2_pre_g, ffn2_post_g, ffn2_wg,
              ffn2_wu, ffn2_wd):
    bp, seq, _ = x_prompt.shape
    bs, dec_seq, _ = x_sample.shape
    xp = jnp.concatenate([jnp.broadcast_to(meta_tokens.astype(x_prompt.dtype), (bp, N_META, D_MODEL)),
                          x_prompt], axis=1)
    prompt_segments = [(N_META, N_META), (seq, min(SSD_CHUNK, seq))]
    sample_segments = [(dec_seq, dec_seq)]
    xs = x_sample
    p_ssd, p_ssdc, p_lru, p_lruc = [], [], [], []
    s_ssd, s_ssdc, s_lru, s_lruc = [], [], [], []
    for l in range(DEPTH):
        p = dict(ffn1_pre_g=ffn1_pre_g[l], ffn1_post_g=ffn1_post_g[l], ffn1_wg=ffn1_wg[l],
                 ffn1_wu=ffn1_wu[l], ffn1_wd=ffn1_wd[l], mix_pre_g=mix_pre_g[l], mix_post_g=mix_post_g[l],
                 w_in=w_in[l], ssd_conv_w=ssd_conv_w[l], ssd_conv_b=ssd_conv_b[l],
                 ssd_dt_bias=ssd_dt_bias[l], ssd_a_log=ssd_a_log[l], ssd_d=ssd_d[l],
                 ssd_norm_g=ssd_norm_g[l], lru_conv_w=lru_conv_w[l], lru_conv_b=lru_conv_b[l],
                 lru_wa=lru_wa[l], lru_ba=lru_ba[l], lru_wx=lru_wx[l], lru_bx=lru_bx[l],
                 lru_lambda=lru_lambda[l], w_out=w_out[l], ffn2_pre_g=ffn2_pre_g[l],
                 ffn2_post_g=ffn2_post_g[l], ffn2_wg=ffn2_wg[l], ffn2_wu=ffn2_wu[l], ffn2_wd=ffn2_wd[l])
        xp, h1, c1, h2, c2 = trunk_layer(
            xp,
            jnp.zeros((bp, SSD_HEADS, SSD_HEAD_DIM, SSD_STATE), jnp.float32),
            jnp.zeros((bp, CONV_W - 1, SSD_CONV_DIM), xp.dtype),
            jnp.zeros((bp, D_LRU), jnp.float32),
            jnp.zeros((bp, CONV_W - 1, D_LRU), xp.dtype),
            prompt_segments, p)
        p_ssd.append(h1); p_ssdc.append(c1); p_lru.append(h2); p_lruc.append(c2)
        xs, h1, c1, h2, c2 = trunk_layer(xs, state_ssd[l], state_ssd_conv[l], state_lru[l],
                                         state_lru_conv[l], sample_segments, p)
        s_ssd.append(h1); s_ssdc.append(c1); s_lru.append(h2); s_lruc.append(c2)
    y_prompt = xp[:, N_META:]
    return (y_prompt, xs,
            jnp.stack(p_ssd), jnp.stack(p_ssdc), jnp.stack(p_lru), jnp.stack(p_lruc),
            jnp.stack(s_ssd), jnp.stack(s_ssdc), jnp.stack(s_lru), jnp.stack(s_lruc))
```

```python
import functools

import jax
import jax.numpy as jnp
from jax import lax
from jax.experimental import pallas as pl
from jax.experimental.pallas import tpu as pltpu

F32 = jnp.float32
BF16 = jnp.bfloat16

D_MODEL = 1024
D_SSD = 1024
D_LRU = 1024
SSD_HEADS = 16
SSD_HEAD_DIM = 64
SSD_GROUPS = 2
SSD_STATE = 128
HEADS_PER_GROUP = SSD_HEADS // SSD_GROUPS
CONV_W = 4
CONV_TAIL = CONV_W - 1
SSD_CONV_DIM = D_SSD + 2 * SSD_GROUPS * SSD_STATE
LRU_BLOCKS = 16
LRU_BLOCK_W = D_LRU // LRU_BLOCKS
LRU_C = 8.0
EPS = 1e-6
N_META = 16
SSD_CHUNK = 128

V7X_LANES = 128
V7X_SUBLANES = 8
V7X_MXU_DIM = 256
V7X_VMEM_LIMIT_BYTES = 56 * 1024 * 1024

DT_PAD = V7X_LANES
LRU_GATE_GROUPS = D_LRU // V7X_MXU_DIM
COL_Z = 0
COL_XBC = COL_Z + D_SSD
COL_GATE = COL_XBC + SSD_CONV_DIM
COL_XR = COL_GATE + D_LRU
COL_DT = COL_XR + D_LRU
IN_COLS_PAD = COL_DT + DT_PAD
GELU_K = 0.7978845608028654

FFN_ROWS = 512
OUT_FFN_ROWS = 256
PROMPT_STEPS = 64
SAMPLE_BATCH_TILE = 64


def _rms(x, g):
    return x * lax.rsqrt(jnp.mean(x * x, axis=-1, keepdims=True) + EPS) * g


def _silu(x):
    return x * jax.nn.sigmoid(x)


def _softplus(x):
    return jnp.maximum(x, 0.0) + jnp.log1p(jnp.exp(-jnp.abs(x)))


def _gelu_tanh(x):
    return 0.5 * x * (1.0 + jnp.tanh(GELU_K * (x + 0.044715 * (x * x * x))))


def _dot(a, b):
    return jnp.dot(a, b, preferred_element_type=F32)


def _layer_spec(tail, l):
    zeros = (0,) * len(tail)
    return pl.BlockSpec((None,) + tuple(tail), lambda *_: (l,) + zeros, pipeline_mode=pl.Buffered(1))


def _params(n_axes):
    return pltpu.CompilerParams(dimension_semantics=("arbitrary",) * n_axes,
                                vmem_limit_bytes=V7X_VMEM_LIMIT_BYTES)


def _ffn_math(x, gpre, gpost, wg_ref, wu_ref, wd_ref):
    xn = _rms(x, gpre).astype(BF16)
    hg = _dot(xn, wg_ref[...])
    hu = _dot(xn, wu_ref[...])
    a = (_silu(hg) * hu).astype(BF16)
    y = _dot(a, wd_ref[...])
    return x + 0.5 * _rms(y, gpost)


def _ffn_body(x_ref, gpre_ref, gpost_ref, wg_ref, wu_ref, wd_ref, o_ref):
    o_ref[...] = _ffn_math(x_ref[...], gpre_ref[...], gpost_ref[...], wg_ref, wu_ref, wd_ref)


def _ffn_call(x, l, w, prefix, *, tm, grid, in_map, out_shape, out_map):
    d_ff = w[prefix + "_wg"].shape[-1]
    return pl.pallas_call(
        _ffn_body,
        grid=grid,
        in_specs=[
            pl.BlockSpec((tm, D_MODEL), in_map),
            _layer_spec((1, D_MODEL), l), _layer_spec((1, D_MODEL), l),
            _layer_spec((D_MODEL, d_ff), l), _layer_spec((D_MODEL, d_ff), l),
            _layer_spec((d_ff, D_MODEL), l),
        ],
        out_specs=pl.BlockSpec((tm, D_MODEL), out_map),
        out_shape=jax.ShapeDtypeStruct(out_shape, F32),
        compiler_params=_params(len(grid)),
        name=prefix,
    )(x, w[prefix + "_pre_g"], w[prefix + "_post_g"], w[prefix + "_wg"], w[prefix + "_wu"],
      w[prefix + "_wd"])


def _out_ffn_body(x_ref, ys_ref, yl_ref, wos_ref, wol_ref, gmix_ref, gpre_ref, gpost_ref,
                  wg_ref, wu_ref, wd_ref, o_ref):
    m = _dot(ys_ref[...].astype(BF16), wos_ref[...]) + _dot(yl_ref[...].astype(BF16), wol_ref[...])
    x1 = x_ref[...] + _rms(m, gmix_ref[...])
    o_ref[...] = _ffn_math(x1, gpre_ref[...], gpost_ref[...], wg_ref, wu_ref, wd_ref)


def _out_ffn_call(x, ys, yl, l, w, *, tm, grid, in_map, out_shape, out_map):
    d_ff = w["ffn2_wg"].shape[-1]
    row = pl.BlockSpec((tm, D_MODEL), in_map)
    return pl.pallas_call(
        _out_ffn_body,
        grid=grid,
        in_specs=[
            row, row, row,
            _layer_spec((D_SSD, D_MODEL), l), _layer_spec((D_LRU, D_MODEL), l),
            _layer_spec((1, D_MODEL), l), _layer_spec((1, D_MODEL), l), _layer_spec((1, D_MODEL), l),
            _layer_spec((D_MODEL, d_ff), l), _layer_spec((D_MODEL, d_ff), l),
            _layer_spec((d_ff, D_MODEL), l),
        ],
        out_specs=pl.BlockSpec((tm, D_MODEL), out_map),
        out_shape=jax.ShapeDtypeStruct(out_shape, F32),
        compiler_params=_params(len(grid)),
        name="out_ffn2",
    )(x, ys, yl, w["w_out_ssd"], w["w_out_lru"], w["mix_post_g"], w["ffn2_pre_g"], w["ffn2_post_g"],
      w["ffn2_wg"], w["ffn2_wu"], w["ffn2_wd"])


def _in_proj_body(bsub, steps, pad_steps,
                  x_ref, g_ref, w_ref, scw_ref, scb_ref, lcw_ref, lcb_ref, dtb_ref,
                  wa_ref, wx_ref, ba_ref, bx_ref, lam_ref, sconv0_ref, lconv0_ref, h0_ref,
                  z_ref, xbc_ref, dt_ref, ylru_ref, sconv_ref, lconv_ref, hout_ref,
                  sbuf, lbuf, gate_buf, ra_buf, rx_buf, h_carry):
    rows = steps * bsub
    tail = CONV_TAIL * bsub

    def flat(ref):
        return ref[...].reshape(ref.shape[0] * bsub, ref.shape[2])

    def put(ref, value):
        ref[0:steps] = value.reshape(steps, bsub, value.shape[-1])

    @pl.when(pl.program_id(1) == 0)
    def _():
        sbuf[0:tail, :] = flat(sconv0_ref)
        lbuf[0:tail, :] = flat(lconv0_ref)
        h_carry[...] = h0_ref[...]

    xn = _rms(flat(x_ref), g_ref[...]).astype(BF16)
    put(z_ref, _dot(xn, w_ref[:, COL_Z:COL_XBC]))
    sbuf[tail:tail + rows, :] = _dot(xn, w_ref[:, COL_XBC:COL_GATE])
    gate_buf[...] = _dot(xn, w_ref[:, COL_GATE:COL_XR])
    lbuf[tail:tail + rows, :] = _dot(xn, w_ref[:, COL_XR:COL_DT])
    put(dt_ref, _softplus(_dot(xn, w_ref[:, COL_DT:IN_COLS_PAD]) + dtb_ref[...]))

    sacc = scb_ref[...] + sbuf[0:rows, :] * scw_ref[0:1, :]
    lacc = lcb_ref[...] + lbuf[0:rows, :] * lcw_ref[0:1, :]
    for k in range(1, CONV_W):
        sacc = sacc + sbuf[k * bsub:k * bsub + rows, :] * scw_ref[k:k + 1, :]
        lacc = lacc + lbuf[k * bsub:k * bsub + rows, :] * lcw_ref[k:k + 1, :]
    put(xbc_ref, _silu(sacc))

    xr_bf = lacc.astype(BF16)
    for q in range(LRU_GATE_GROUPS):
        cols = slice(q * V7X_MXU_DIM, (q + 1) * V7X_MXU_DIM)
        ra_buf[:, cols] = _dot(xr_bf[:, cols], wa_ref[q])
        rx_buf[:, cols] = _dot(xr_bf[:, cols], wx_ref[q])
    new_stail = sbuf[rows:rows + tail, :]
    new_ltail = lbuf[rows:rows + tail, :]
    lbuf[tail:tail + rows, :] = lacc

    neg_c_softplus = -LRU_C * _softplus(-lam_ref[...])
    ba = ba_ref[...]
    bx = bx_ref[...]

    def group_scan(bg, carry):
        b0 = pl.multiple_of(bg * V7X_SUBLANES, V7X_SUBLANES)
        hsl = pl.ds(b0, V7X_SUBLANES)

        def step(t, h):
            r0 = pl.multiple_of(t * bsub + b0, V7X_SUBLANES)
            sl = pl.ds(r0, V7X_SUBLANES)
            xr = lbuf[pl.ds(r0 + tail, V7X_SUBLANES), :]
            r = jax.nn.sigmoid(ra_buf[sl, :] + ba)
            ig = jax.nn.sigmoid(rx_buf[sl, :] + bx)
            log_a = neg_c_softplus * r
            a = jnp.exp(log_a)
            mult = jnp.sqrt(-jnp.tanh(log_a) * (jnp.exp(2.0 * log_a) + 1.0))
            h = a * h + mult * ig * xr
            ylru_ref[t, hsl, :] = h * _gelu_tanh(gate_buf[sl, :])
            return h

        h_carry[hsl, :] = lax.fori_loop(0, steps, step, h_carry[hsl, :],
                                        unroll=4 if steps % 4 == 0 else 1)
        return carry

    lax.fori_loop(0, bsub // V7X_SUBLANES, group_scan, 0)

    sbuf[0:tail, :] = new_stail
    lbuf[0:tail, :] = new_ltail
    sconv_ref[...] = new_stail.reshape(CONV_TAIL, bsub, SSD_CONV_DIM)
    lconv_ref[...] = new_ltail.reshape(CONV_TAIL, bsub, D_LRU)
    hout_ref[...] = h_carry[...]
    if pad_steps:
        for ref in (z_ref, xbc_ref, dt_ref, ylru_ref):
            ref[steps:steps + pad_steps] = jnp.zeros((pad_steps, bsub, ref.shape[2]), F32)


def _in_proj_call(x, l, w, sconv0, lconv0, h0, *, bsub, steps, pad_steps=0):
    length, batch, _ = x.shape
    n_t = length // steps
    n_b = batch // bsub
    assert length == n_t * steps and batch == n_b * bsub and bsub % V7X_SUBLANES == 0
    assert pad_steps == 0 or n_t == 1
    out_steps = steps + pad_steps
    rows = steps * bsub
    tail = CONV_TAIL * bsub

    def tile(width):
        return pl.BlockSpec((out_steps, bsub, width), lambda b, t: (t, b, 0))

    def conv_state(width):
        return pl.BlockSpec((CONV_TAIL, bsub, width), lambda b, t: (0, b, 0))

    lru_state = pl.BlockSpec((bsub, D_LRU), lambda b, t: (b, 0))
    return pl.pallas_call(
        functools.partial(_in_proj_body, bsub, steps, pad_steps),
        grid=(n_b, n_t),
        in_specs=[
            pl.BlockSpec((steps, bsub, D_MODEL), lambda b, t: (t, b, 0)),
            _layer_spec((1, D_MODEL), l),
            _layer_spec((D_MODEL, IN_COLS_PAD), l),
            _layer_spec((CONV_W, SSD_CONV_DIM), l), _layer_spec((1, SSD_CONV_DIM), l),
            _layer_spec((CONV_W, D_LRU), l), _layer_spec((1, D_LRU), l),
            _layer_spec((1, DT_PAD), l),
            _layer_spec((LRU_GATE_GROUPS, V7X_MXU_DIM, V7X_MXU_DIM), l),
            _layer_spec((LRU_GATE_GROUPS, V7X_MXU_DIM, V7X_MXU_DIM), l),
            _layer_spec((1, D_LRU), l), _layer_spec((1, D_LRU), l), _layer_spec((1, D_LRU), l),
            conv_state(SSD_CONV_DIM), conv_state(D_LRU), lru_state,
        ],
        out_specs=[
            tile(D_SSD), tile(SSD_CONV_DIM), tile(DT_PAD), tile(D_LRU),
            conv_state(SSD_CONV_DIM), conv_state(D_LRU), lru_state,
        ],
        out_shape=[
            jax.ShapeDtypeStruct((n_t * out_steps, batch, D_SSD), F32),
            jax.ShapeDtypeStruct((n_t * out_steps, batch, SSD_CONV_DIM), F32),
            jax.ShapeDtypeStruct((n_t * out_steps, batch, DT_PAD), F32),
            jax.ShapeDtypeStruct((n_t * out_steps, batch, D_LRU), F32),
            jax.ShapeDtypeStruct((CONV_TAIL, batch, SSD_CONV_DIM), F32),
            jax.ShapeDtypeStruct((CONV_TAIL, batch, D_LRU), F32),
            jax.ShapeDtypeStruct((batch, D_LRU), F32),
        ],
        scratch_shapes=[
            pltpu.VMEM((tail + rows, SSD_CONV_DIM), F32),
            pltpu.VMEM((tail + rows, D_LRU), F32),
            pltpu.VMEM((rows, D_LRU), F32),
            pltpu.VMEM((rows, D_LRU), F32),
            pltpu.VMEM((rows, D_LRU), F32),
            pltpu.VMEM((bsub, D_LRU), F32),
        ],
        compiler_params=_params(2),
        name="in_proj_lru",
    )(x, w["mix_pre_g"], w["w_in"], w["ssd_conv_w"], w["ssd_conv_b"], w["lru_conv_w"], w["lru_conv_b"],
      w["ssd_dt_bias"], w["lru_wa"], w["lru_wx"], w["lru_ba"], w["lru_bx"], w["lru_lambda"],
      sconv0, lconv0, h0)


def _transpose_rows(x, rows):
    lanes = x.shape[1]
    if rows < lanes:
        x = jnp.concatenate([x, jnp.zeros((lanes - rows, lanes), x.dtype)], axis=0)
    return x.T[:, 0:rows]


def _ssd_body(chunk, n_chunks, xbc_ref, dt_ref, z_ref, alog_ref, dvec_ref, g_ref, h0_ref,
              y_ref, hout_ref, h_s, y_s):
    c = pl.program_id(1)

    @pl.when(c == 0)
    def _():
        h_s[...] = h0_ref[...]

    dt = dt_ref[...]
    a = -jnp.exp(alog_ref[...])
    row = lax.broadcasted_iota(jnp.int32, (chunk, chunk), 0)
    col = lax.broadcasted_iota(jnp.int32, (chunk, chunk), 1)
    causal = row >= col
    cum = jnp.dot(causal.astype(F32), dt * a, precision=lax.Precision.HIGHEST,
                  preferred_element_type=F32)
    tot = cum[chunk - 1:chunk, :]
    ecum = jnp.exp(cum)
    wend = jnp.exp(tot - cum) * dt
    cdec = jnp.exp(tot)
    cum_t = _transpose_rows(cum, chunk)
    dt_t = _transpose_rows(dt, chunk)

    for g in range(SSD_GROUPS):
        b0 = D_SSD + g * SSD_STATE
        c0 = D_SSD + SSD_GROUPS * SSD_STATE + g * SSD_STATE
        bg = xbc_ref[:, b0:b0 + SSD_STATE].astype(BF16)
        cg = xbc_ref[:, c0:c0 + SSD_STATE].astype(BF16)
        cb = lax.dot_general(cg, bg, (((1,), (1,)), ((), ())), preferred_element_type=F32)
        for r in range(HEADS_PER_GROUP):
            h = g * HEADS_PER_GROUP + r
            cols = slice(h * SSD_HEAD_DIM, (h + 1) * SSD_HEAD_DIM)
            xh = xbc_ref[:, cols]
            diff = cum[:, h:h + 1] - cum_t[h:h + 1, :]
            decay = jnp.exp(jnp.where(causal, diff, -jnp.inf))
            scores = cb * decay * dt_t[h:h + 1, :]
            y_diag = _dot(scores.astype(BF16), xh.astype(BF16))
            h_prev = h_s[h]
            y_off = lax.dot_general(cg, h_prev.astype(BF16), (((1,), (1,)), ((), ())),
                                    preferred_element_type=F32) * ecum[:, h:h + 1]
            xw = (xh * wend[:, h:h + 1]).astype(BF16)
            states = lax.dot_general(xw, bg, (((0,), (0,)), ((), ())), preferred_element_type=F32)
            h_s[h] = cdec[:, h:h + 1] * h_prev + states
            y_s[:, cols] = y_diag + y_off + dvec_ref[:, cols] * xh

    y_ref[...] = _rms(y_s[...] * _silu(z_ref[...]), g_ref[...])

    @pl.when(c == n_chunks - 1)
    def _():
        hout_ref[...] = h_s[...]


def _ssd_call(xbc, dt, z, l, w, h0, h0_map, *, chunk):
    length, batch, _ = xbc.shape
    n_chunks = length // chunk
    assert length == n_chunks * chunk

    def view(a):
        return a.reshape(length, batch * a.shape[2])

    def per_batch(width):
        return pl.BlockSpec((chunk, width), lambda b, c: (c, b))

    state_tail = (SSD_HEADS, SSD_HEAD_DIM, SSD_STATE)
    h0_block = (None,) * (h0.ndim - 3) + state_tail
    y, h_out = pl.pallas_call(
        functools.partial(_ssd_body, chunk, n_chunks),
        grid=(batch, n_chunks),
        in_specs=[
            per_batch(SSD_CONV_DIM), per_batch(DT_PAD), per_batch(D_SSD),
            _layer_spec((1, DT_PAD), l), _layer_spec((1, D_SSD), l), _layer_spec((1, D_SSD), l),
            pl.BlockSpec(h0_block, h0_map),
        ],
        out_specs=[
            per_batch(D_SSD),
            pl.BlockSpec((None,) + state_tail, lambda b, c: (b, 0, 0, 0)),
        ],
        out_shape=[
            jax.ShapeDtypeStruct((length, batch * D_SSD), F32),
            jax.ShapeDtypeStruct((batch,) + state_tail, F32),
        ],
        scratch_shapes=[
            pltpu.VMEM(state_tail, F32),
            pltpu.VMEM((chunk, D_SSD), F32),
        ],
        compiler_params=_params(2),
        name="ssd_chunk",
    )(view(xbc), view(dt), view(z), w["ssd_a_log"], w["ssd_d_cols"], w["ssd_norm_g"], h0)
    return y.reshape(length, batch, D_SSD), h_out


def _prepare_weights(p):
    depth = p["w_in"].shape[0]

    def vec(a):
        return a.reshape(depth, 1, a.shape[-1])

    def pad_heads(a):
        return jnp.pad(a, ((0, 0), (0, DT_PAD - SSD_HEADS))).reshape(depth, 1, DT_PAD)

    def block_diag(a):
        per = V7X_MXU_DIM // LRU_BLOCK_W
        a = a.reshape(depth, LRU_GATE_GROUPS, per, LRU_BLOCK_W, LRU_BLOCK_W)
        eye = jnp.eye(per, dtype=a.dtype)
        a = a[:, :, :, :, None, :] * eye[None, None, :, None, :, None]
        return a.reshape(depth, LRU_GATE_GROUPS, V7X_MXU_DIM, V7X_MXU_DIM).astype(BF16)

    w_in = p["w_in"]
    o_xbc = D_SSD
    o_dt = o_xbc + SSD_CONV_DIM
    o_gate = o_dt + SSD_HEADS
    o_xr = o_gate + D_LRU
    w_in = jnp.concatenate([
        w_in[:, :, 0:o_dt], w_in[:, :, o_gate:o_xr], w_in[:, :, o_xr:o_xr + D_LRU],
        w_in[:, :, o_dt:o_gate], jnp.zeros((depth, D_MODEL, DT_PAD - SSD_HEADS), w_in.dtype),
    ], axis=-1).astype(BF16)
    w = {
        "w_in": w_in,
        "w_out_ssd": p["w_out"][:, 0:D_SSD].astype(BF16),
        "w_out_lru": p["w_out"][:, D_SSD:].astype(BF16),
        "lru_wa": block_diag(p["lru_wa"]),
        "lru_wx": block_diag(p["lru_wx"]),
        "ssd_dt_bias": pad_heads(p["ssd_dt_bias"]),
        "ssd_a_log": pad_heads(p["ssd_a_log"]),
        "ssd_d_cols": jnp.repeat(p["ssd_d"], SSD_HEAD_DIM, axis=-1).reshape(depth, 1, D_SSD),
        "ssd_conv_w": p["ssd_conv_w"],
        "lru_conv_w": p["lru_conv_w"],
    }
    for name in ("ffn1_wg", "ffn1_wu", "ffn1_wd", "ffn2_wg", "ffn2_wu", "ffn2_wd"):
        w[name] = p[name].astype(BF16)
    for name in ("ffn1_pre_g", "ffn1_post_g", "mix_pre_g", "mix_post_g", "ffn2_pre_g", "ffn2_post_g",
                 "ssd_conv_b", "ssd_norm_g", "lru_conv_b", "lru_ba", "lru_bx", "lru_lambda"):
        w[name] = vec(p[name])
    return w


def _row_tile(rows, want):
    tm = min(rows, want)
    assert rows % tm == 0
    return tm


def _rows_map(i):
    return (i, 0)


def _segment_layer(x, l, w, states, *, bsub, steps, chunk, pad_steps, first=None, last=None):
    sconv0, lconv0, h_lru0, h_ssd0, h_ssd0_map = states
    if first is None:
        length, batch, _ = x.shape
        rows = length * batch
        tm = _row_tile(rows, FFN_ROWS)
        x = _ffn_call(x.reshape(rows, D_MODEL), l, w, "ffn1", tm=tm, grid=(rows // tm,),
                      in_map=_rows_map, out_shape=(rows, D_MODEL), out_map=_rows_map)
    else:
        length, batch = first["length"], first["batch"]
        rows = length * batch
        x = _ffn_call(x, l, w, "ffn1", tm=first["tm"], grid=first["grid"], in_map=first["in_map"],
                      out_shape=(length, batch * D_MODEL), out_map=first["out_map"])
    x = x.reshape(length, batch, D_MODEL)
    z, xbc, dt, y_lru, sconv, lconv, h_lru = _in_proj_call(
        x, l, w, sconv0, lconv0, h_lru0, bsub=bsub, steps=steps, pad_steps=pad_steps)
    y_ssd, h_ssd = _ssd_call(xbc, dt, z, l, w, h_ssd0, h_ssd0_map, chunk=chunk)
    if last is None:
        tm = _row_tile(rows, OUT_FFN_ROWS)
        flat = lambda a: a.reshape(a.shape[0] * batch, D_MODEL)
        x = _out_ffn_call(flat(x), flat(y_ssd), flat(y_lru), l, w, tm=tm, grid=(rows // tm,),
                          in_map=_rows_map, out_shape=(rows, D_MODEL), out_map=_rows_map)
        x = x.reshape(length, batch, D_MODEL)
    else:
        wide = lambda a: a.reshape(length, batch * D_MODEL)
        x = _out_ffn_call(wide(x), wide(y_ssd), wide(y_lru), l, w, tm=last["tm"], grid=last["grid"],
                          in_map=last["in_map"], out_shape=(rows, D_MODEL), out_map=last["out_map"])
    return x, (h_ssd, sconv.transpose(1, 0, 2), h_lru, lconv.transpose(1, 0, 2))


def kernel(x_prompt, x_sample, state_ssd, state_ssd_conv, state_lru, state_lru_conv, meta_tokens,
           ffn1_pre_g, ffn1_post_g, ffn1_wg, ffn1_wu, ffn1_wd, mix_pre_g, mix_post_g, w_in,
           ssd_conv_w, ssd_conv_b, ssd_dt_bias, ssd_a_log, ssd_d, ssd_norm_g, lru_conv_w, lru_conv_b,
           lru_wa, lru_ba, lru_wx, lru_bx, lru_lambda, w_out, ffn2_pre_g, ffn2_post_g, ffn2_wg,
           ffn2_wu, ffn2_wd):
    bp, seq, _ = x_prompt.shape
    bs, dec_seq, _ = x_sample.shape
    depth = w_in.shape[0]
    assert bp == V7X_SUBLANES and seq % SSD_CHUNK == 0 and bs % SAMPLE_BATCH_TILE == 0
    w = _prepare_weights(dict(
        ffn1_pre_g=ffn1_pre_g, ffn1_post_g=ffn1_post_g, ffn1_wg=ffn1_wg, ffn1_wu=ffn1_wu,
        ffn1_wd=ffn1_wd, mix_pre_g=mix_pre_g, mix_post_g=mix_post_g, w_in=w_in,
        ssd_conv_w=ssd_conv_w, ssd_conv_b=ssd_conv_b, ssd_dt_bias=ssd_dt_bias, ssd_a_log=ssd_a_log,
        ssd_d=ssd_d, ssd_norm_g=ssd_norm_g, lru_conv_w=lru_conv_w, lru_conv_b=lru_conv_b,
        lru_wa=lru_wa, lru_ba=lru_ba, lru_wx=lru_wx, lru_bx=lru_bx, lru_lambda=lru_lambda,
        w_out=w_out, ffn2_pre_g=ffn2_pre_g, ffn2_post_g=ffn2_post_g, ffn2_wg=ffn2_wg,
        ffn2_wu=ffn2_wu, ffn2_wd=ffn2_wd))

    def relayout(tm):
        n_t = seq // tm
        return dict(length=seq, batch=bp, tm=tm, grid=(n_t, bp),
                    major=lambda t, b: (b * n_t + t, 0), wide=lambda t, b: (t, b))

    r_in = relayout(_row_tile(seq, FFN_ROWS))
    first = dict(r_in, in_map=r_in["major"], out_map=r_in["wide"])
    r_out = relayout(_row_tile(seq, OUT_FFN_ROWS))
    last = dict(r_out, in_map=r_out["wide"], out_map=r_out["major"])

    xm = jnp.broadcast_to(meta_tokens.astype(F32)[:, None, :], (N_META, bp, D_MODEL))
    xp = x_prompt.reshape(bp * seq, D_MODEL)
    xs = x_sample.transpose(1, 0, 2)
    sample_steps = ((dec_seq + V7X_SUBLANES - 1) // V7X_SUBLANES) * V7X_SUBLANES
    zero_ssd = jnp.zeros((bp, SSD_HEADS, SSD_HEAD_DIM, SSD_STATE), F32)
    zero_sconv = jnp.zeros((CONV_TAIL, bp, SSD_CONV_DIM), F32)
    zero_lconv = jnp.zeros((CONV_TAIL, bp, D_LRU), F32)
    zero_lru = jnp.zeros((bp, D_LRU), F32)
    per_batch_state = lambda b, c: (b, 0, 0, 0)

    p_out = [[] for _ in range(4)]
    s_out = [[] for _ in range(4)]
    for l in range(depth):
        xm, m_st = _segment_layer(
            xm, l, w, (zero_sconv, zero_lconv, zero_lru, zero_ssd, per_batch_state),
            bsub=bp, steps=N_META, chunk=N_META, pad_steps=0)
        xp, p_st = _segment_layer(
            xp, l, w, (m_st[1].transpose(1, 0, 2), m_st[3].transpose(1, 0, 2), m_st[2], m_st[0],
                       per_batch_state),
            bsub=bp, steps=PROMPT_STEPS, chunk=SSD_CHUNK, pad_steps=0,
            first=first if l == 0 else None, last=last if l == depth - 1 else None)
        xs, s_st = _segment_layer(
            xs, l, w, (state_ssd_conv[l].transpose(1, 0, 2), state_lru_conv[l].transpose(1, 0, 2),
                       state_lru[l], state_ssd, lambda b, c, l=l: (l, b, 0, 0, 0)),
            bsub=SAMPLE_BATCH_TILE, steps=dec_seq, chunk=sample_steps,
            pad_steps=sample_steps - dec_seq)
        for acc, st in ((p_out, p_st), (s_out, s_st)):
            for k in range(4):
                acc[k].append(st[k])

    y_prompt = xp.reshape(bp, seq, D_MODEL)
    y_sample = xs.transpose(1, 0, 2)
    return (y_prompt, y_sample) + tuple(jnp.stack(a) for a in p_out) + tuple(jnp.stack(a) for a in s_out)
```

```python
import functools

import jax
import jax.numpy as jnp
from jax import lax
from jax.experimental import pallas as pl
from jax.experimental.pallas import tpu as pltpu

F32 = jnp.float32
BF16 = jnp.bfloat16

D_MODEL = 1024
D_SSD = 1024
D_LRU = 1024
SSD_HEADS = 16
SSD_HEAD_DIM = 64
SSD_GROUPS = 2
SSD_STATE = 128
HEADS_PER_GROUP = SSD_HEADS // SSD_GROUPS
CONV_W = 4
CONV_TAIL = CONV_W - 1
SSD_CONV_DIM = D_SSD + 2 * SSD_GROUPS * SSD_STATE
LRU_BLOCKS = 16
LRU_BLOCK_W = D_LRU // LRU_BLOCKS
LRU_C = 8.0
EPS = 1e-6
N_META = 16
SSD_CHUNK = 128

V7X_LANES = 128
V7X_SUBLANES = 8
V7X_MXU_DIM = 256
V7X_VMEM_LIMIT_BYTES = 56 * 1024 * 1024

DT_PAD = V7X_LANES
LRU_GATE_GROUPS = D_LRU // V7X_MXU_DIM
COL_Z = 0
COL_XBC = COL_Z + D_SSD
COL_GATE = COL_XBC + SSD_CONV_DIM
COL_XR = COL_GATE + D_LRU
COL_DT = COL_XR + D_LRU
IN_COLS_PAD = COL_DT + DT_PAD
GELU_K = 0.7978845608028654

FFN_ROWS = 512
OUT_FFN_ROWS = 256
PROMPT_STEPS = 64
SAMPLE_BATCH_TILE = 64


def _rms(x, g):
    return x * lax.rsqrt(jnp.mean(x * x, axis=-1, keepdims=True) + EPS) * g


def _silu(x):
    return x * jax.nn.sigmoid(x)


def _softplus(x):
    return jnp.maximum(x, 0.0) + jnp.log1p(jnp.exp(-jnp.abs(x)))


def _gelu_tanh(x):
    return 0.5 * x * (1.0 + jnp.tanh(GELU_K * (x + 0.044715 * (x * x * x))))


def _dot(a, b):
    return jnp.dot(a, b, preferred_element_type=F32)


def _layer_spec(tail, l):
    zeros = (0,) * len(tail)
    return pl.BlockSpec((None,) + tuple(tail), lambda *_: (l,) + zeros, pipeline_mode=pl.Buffered(1))


def _params(n_axes):
    return pltpu.CompilerParams(dimension_semantics=("arbitrary",) * n_axes,
                                vmem_limit_bytes=V7X_VMEM_LIMIT_BYTES)


def _row_tile(rows, want):
    tm = min(rows, want)
    assert rows % tm == 0
    return tm


def _rows_map(i):
    return (i, 0)


def _ffn_math(x, gpre, gpost, wg_ref, wu_ref, wd_ref):
    xn = _rms(x, gpre).astype(BF16)
    hg = _dot(xn, wg_ref[...])
    hu = _dot(xn, wu_ref[...])
    a = (_silu(hg) * hu).astype(BF16)
    y = _dot(a, wd_ref[...])
    return x + 0.5 * _rms(y, gpost)


def _ffn_body(x_ref, gpre_ref, gpost_ref, wg_ref, wu_ref, wd_ref, o_ref):
    o_ref[...] = _ffn_math(x_ref[...], gpre_ref[...], gpost_ref[...], wg_ref, wu_ref, wd_ref)


def _ffn_call(x, l, w, prefix):
    rows = x.shape[0]
    tm = _row_tile(rows, FFN_ROWS)
    d_ff = w[prefix + "_wg"].shape[-1]
    return pl.pallas_call(
        _ffn_body,
        grid=(rows // tm,),
        in_specs=[
            pl.BlockSpec((tm, D_MODEL), _rows_map),
            _layer_spec((1, D_MODEL), l), _layer_spec((1, D_MODEL), l),
            _layer_spec((D_MODEL, d_ff), l), _layer_spec((D_MODEL, d_ff), l),
            _layer_spec((d_ff, D_MODEL), l),
        ],
        out_specs=pl.BlockSpec((tm, D_MODEL), _rows_map),
        out_shape=jax.ShapeDtypeStruct((rows, D_MODEL), F32),
        compiler_params=_params(1),
        name=prefix,
    )(x, w[prefix + "_pre_g"], w[prefix + "_post_g"], w[prefix + "_wg"], w[prefix + "_wu"],
      w[prefix + "_wd"])


def _out_ffn_body(x_ref, ys_ref, yl_ref, wos_ref, wol_ref, gmix_ref, gpre_ref, gpost_ref,
                  wg_ref, wu_ref, wd_ref, o_ref):
    m = _dot(ys_ref[...].astype(BF16), wos_ref[...]) + _dot(yl_ref[...].astype(BF16), wol_ref[...])
    x1 = x_ref[...] + _rms(m, gmix_ref[...])
    o_ref[...] = _ffn_math(x1, gpre_ref[...], gpost_ref[...], wg_ref, wu_ref, wd_ref)


def _out_ffn_call(x, ys, yl, l, w):
    rows = x.shape[0]
    tm = _row_tile(rows, OUT_FFN_ROWS)
    d_ff = w["ffn2_wg"].shape[-1]
    row = pl.BlockSpec((tm, D_MODEL), _rows_map)
    return pl.pallas_call(
        _out_ffn_body,
        grid=(rows // tm,),
        in_specs=[
            row, row, row,
            _layer_spec((D_SSD, D_MODEL), l), _layer_spec((D_LRU, D_MODEL), l),
            _layer_spec((1, D_MODEL), l), _layer_spec((1, D_MODEL), l), _layer_spec((1, D_MODEL), l),
            _layer_spec((D_MODEL, d_ff), l), _layer_spec((D_MODEL, d_ff), l),
            _layer_spec((d_ff, D_MODEL), l),
        ],
        out_specs=row,
        out_shape=jax.ShapeDtypeStruct((rows, D_MODEL), F32),
        compiler_params=_params(1),
        name="out_ffn2",
    )(x, ys, yl, w["w_out_ssd"], w["w_out_lru"], w["mix_post_g"], w["ffn2_pre_g"], w["ffn2_post_g"],
      w["ffn2_wg"], w["ffn2_wu"], w["ffn2_wd"])


def _in_proj_body(bsub, steps, pad_steps,
                  x_ref, g_ref, w_ref, scw_ref, scb_ref, lcw_ref, lcb_ref, dtb_ref,
                  wa_ref, wx_ref, ba_ref, bx_ref, lam_ref, sconv0_ref, lconv0_ref, h0_ref,
                  z_ref, xbc_ref, dt_ref, ylru_ref, sconv_ref, lconv_ref, hout_ref,
                  xbuf, sbuf, lbuf, gate_buf, ra_buf, rx_buf, h_carry):
    rows = steps * bsub
    tail = CONV_TAIL * bsub

    def put(ref, value):
        for t in range(steps):
            ref[:, t, :] = value[t * bsub:(t + 1) * bsub, :]

    @pl.when(pl.program_id(1) == 0)
    def _():
        for k in range(CONV_TAIL):
            sbuf[k * bsub:(k + 1) * bsub, :] = sconv0_ref[:, k, :]
            lbuf[k * bsub:(k + 1) * bsub, :] = lconv0_ref[:, k, :]
        h_carry[...] = h0_ref[...]

    for t in range(steps):
        xbuf[t * bsub:(t + 1) * bsub, :] = x_ref[:, t, :]
    xn = _rms(xbuf[...], g_ref[...]).astype(BF16)
    put(z_ref, _dot(xn, w_ref[:, COL_Z:COL_XBC]))
    sbuf[tail:tail + rows, :] = _dot(xn, w_ref[:, COL_XBC:COL_GATE])
    gate_buf[...] = _dot(xn, w_ref[:, COL_GATE:COL_XR])
    lbuf[tail:tail + rows, :] = _dot(xn, w_ref[:, COL_XR:COL_DT])
    put(dt_ref, _softplus(_dot(xn, w_ref[:, COL_DT:IN_COLS_PAD]) + dtb_ref[...]))

    sacc = scb_ref[...] + sbuf[0:rows, :] * scw_ref[0:1, :]
    lacc = lcb_ref[...] + lbuf[0:rows, :] * lcw_ref[0:1, :]
    for k in range(1, CONV_W):
        sacc = sacc + sbuf[k * bsub:k * bsub + rows, :] * scw_ref[k:k + 1, :]
        lacc = lacc + lbuf[k * bsub:k * bsub + rows, :] * lcw_ref[k:k + 1, :]
    put(xbc_ref, _silu(sacc))

    xr_bf = lacc.astype(BF16)
    for q in range(LRU_GATE_GROUPS):
        cols = slice(q * V7X_MXU_DIM, (q + 1) * V7X_MXU_DIM)
        ra_buf[:, cols] = _dot(xr_bf[:, cols], wa_ref[q])
        rx_buf[:, cols] = _dot(xr_bf[:, cols], wx_ref[q])
    for k in range(CONV_TAIL):
        r0 = rows + k * bsub
        stail = sbuf[r0:r0 + bsub, :]
        ltail = lbuf[r0:r0 + bsub, :]
        sconv_ref[:, k, :] = stail
        lconv_ref[:, k, :] = ltail
        sbuf[k * bsub:(k + 1) * bsub, :] = stail
        lbuf[k * bsub:(k + 1) * bsub, :] = ltail
    lbuf[tail:tail + rows, :] = lacc

    neg_c_softplus = -LRU_C * _softplus(-lam_ref[...])
    ba = ba_ref[...]
    bx = bx_ref[...]

    def group_scan(bg, carry):
        b0 = pl.multiple_of(bg * V7X_SUBLANES, V7X_SUBLANES)
        hsl = pl.ds(b0, V7X_SUBLANES)

        def step(t, h):
            r0 = pl.multiple_of(t * bsub + b0, V7X_SUBLANES)
            sl = pl.ds(r0, V7X_SUBLANES)
            xr = lbuf[pl.ds(r0 + tail, V7X_SUBLANES), :]
            r = jax.nn.sigmoid(ra_buf[sl, :] + ba)
            ig = jax.nn.sigmoid(rx_buf[sl, :] + bx)
            log_a = neg_c_softplus * r
            a = jnp.exp(log_a)
            mult = jnp.sqrt(-jnp.tanh(log_a) * (jnp.exp(2.0 * log_a) + 1.0))
            h = a * h + mult * ig * xr
            gate_buf[sl, :] = h * _gelu_tanh(gate_buf[sl, :])
            return h

        h_carry[hsl, :] = lax.fori_loop(0, steps, step, h_carry[hsl, :],
                                        unroll=4 if steps % 4 == 0 else 1)
        return carry

    lax.fori_loop(0, bsub // V7X_SUBLANES, group_scan, 0)
    put(ylru_ref, gate_buf[...])
    hout_ref[...] = h_carry[...]
    for t in range(steps, steps + pad_steps):
        for ref in (z_ref, xbc_ref, dt_ref):
            ref[:, t, :] = jnp.zeros((bsub, ref.shape[2]), F32)


def _in_proj_call(x, l, w, sconv0, sconv0_map, lconv0, lconv0_map, h0, h0_map, *,
                  bsub, steps, pad_steps=0):
    batch, length, _ = x.shape
    n_t = length // steps
    n_b = batch // bsub
    assert length == n_t * steps and batch == n_b * bsub and bsub % V7X_SUBLANES == 0
    assert pad_steps == 0 or n_t == 1
    out_steps = steps + pad_steps
    rows = steps * bsub
    tail = CONV_TAIL * bsub

    def tile(n_steps, width):
        return pl.BlockSpec((bsub, n_steps, width), lambda b, t: (b, t, 0))

    def squeeze_lead(a, block):
        return (None,) * (a.ndim - len(block)) + block

    def conv_out(width):
        return pl.BlockSpec((bsub, CONV_TAIL, width), lambda b, t: (b, 0, 0))

    return pl.pallas_call(
        functools.partial(_in_proj_body, bsub, steps, pad_steps),
        grid=(n_b, n_t),
        in_specs=[
            tile(steps, D_MODEL),
            _layer_spec((1, D_MODEL), l),
            _layer_spec((D_MODEL, IN_COLS_PAD), l),
            _layer_spec((CONV_W, SSD_CONV_DIM), l), _layer_spec((1, SSD_CONV_DIM), l),
            _layer_spec((CONV_W, D_LRU), l), _layer_spec((1, D_LRU), l),
            _layer_spec((1, DT_PAD), l),
            _layer_spec((LRU_GATE_GROUPS, V7X_MXU_DIM, V7X_MXU_DIM), l),
            _layer_spec((LRU_GATE_GROUPS, V7X_MXU_DIM, V7X_MXU_DIM), l),
            _layer_spec((1, D_LRU), l), _layer_spec((1, D_LRU), l), _layer_spec((1, D_LRU), l),
            pl.BlockSpec(squeeze_lead(sconv0, (bsub, CONV_TAIL, SSD_CONV_DIM)), sconv0_map),
            pl.BlockSpec(squeeze_lead(lconv0, (bsub, CONV_TAIL, D_LRU)), lconv0_map),
            pl.BlockSpec(squeeze_lead(h0, (bsub, D_LRU)), h0_map),
        ],
        out_specs=[
            tile(out_steps, D_SSD), tile(out_steps, SSD_CONV_DIM), tile(out_steps, DT_PAD),
            tile(steps, D_LRU),
            conv_out(SSD_CONV_DIM), conv_out(D_LRU),
            pl.BlockSpec((bsub, D_LRU), lambda b, t: (b, 0)),
        ],
        out_shape=[
            jax.ShapeDtypeStruct((batch, n_t * out_steps, D_SSD), F32),
            jax.ShapeDtypeStruct((batch, n_t * out_steps, SSD_CONV_DIM), F32),
            jax.ShapeDtypeStruct((batch, n_t * out_steps, DT_PAD), F32),
            jax.ShapeDtypeStruct((batch, length, D_LRU), F32),
            jax.ShapeDtypeStruct((batch, CONV_TAIL, SSD_CONV_DIM), F32),
            jax.ShapeDtypeStruct((batch, CONV_TAIL, D_LRU), F32),
            jax.ShapeDtypeStruct((batch, D_LRU), F32),
        ],
        scratch_shapes=[
            pltpu.VMEM((rows, D_MODEL), F32),
            pltpu.VMEM((tail + rows, SSD_CONV_DIM), F32),
            pltpu.VMEM((tail + rows, D_LRU), F32),
            pltpu.VMEM((rows, D_LRU), F32),
            pltpu.VMEM((rows, D_LRU), F32),
            pltpu.VMEM((rows, D_LRU), F32),
            pltpu.VMEM((bsub, D_LRU), F32),
        ],
        compiler_params=_params(2),
        name="in_proj_lru",
    )(x, w["mix_pre_g"], w["w_in"], w["ssd_conv_w"], w["ssd_conv_b"], w["lru_conv_w"], w["lru_conv_b"],
      w["ssd_dt_bias"], w["lru_wa"], w["lru_wx"], w["lru_ba"], w["lru_bx"], w["lru_lambda"],
      sconv0, lconv0, h0)


def _transpose_rows(x, rows):
    lanes = x.shape[1]
    if rows < lanes:
        x = jnp.concatenate([x, jnp.zeros((lanes - rows, lanes), x.dtype)], axis=0)
    return x.T[:, 0:rows]


def _ssd_body(chunk, n_chunks, out_steps, xbc_ref, dt_ref, z_ref, alog_ref, dvec_ref, g_ref, h0_ref,
              y_ref, hout_ref, h_s, y_s):
    c = pl.program_id(1)

    @pl.when(c == 0)
    def _():
        h_s[...] = h0_ref[...]

    dt = dt_ref[...]
    a = -jnp.exp(alog_ref[...])
    row = lax.broadcasted_iota(jnp.int32, (chunk, chunk), 0)
    col = lax.broadcasted_iota(jnp.int32, (chunk, chunk), 1)
    causal = row >= col
    cum = jnp.dot(causal.astype(F32), dt * a, precision=lax.Precision.HIGHEST,
                  preferred_element_type=F32)
    tot = cum[chunk - 1:chunk, :]
    ecum = jnp.exp(cum)
    wend = jnp.exp(tot - cum) * dt
    cdec = jnp.exp(tot)
    cum_t = _transpose_rows(cum, chunk)
    dt_t = _transpose_rows(dt, chunk)

    for g in range(SSD_GROUPS):
        b0 = D_SSD + g * SSD_STATE
        c0 = D_SSD + SSD_GROUPS * SSD_STATE + g * SSD_STATE
        bg = xbc_ref[:, b0:b0 + SSD_STATE].astype(BF16)
        cg = xbc_ref[:, c0:c0 + SSD_STATE].astype(BF16)
        cb = lax.dot_general(cg, bg, (((1,), (1,)), ((), ())), preferred_element_type=F32)
        for r in range(HEADS_PER_GROUP):
            h = g * HEADS_PER_GROUP + r
            cols = slice(h * SSD_HEAD_DIM, (h + 1) * SSD_HEAD_DIM)
            xh = xbc_ref[:, cols]
            diff = cum[:, h:h + 1] - cum_t[h:h + 1, :]
            decay = jnp.exp(jnp.where(causal, diff, -jnp.inf))
            scores = cb * decay * dt_t[h:h + 1, :]
            y_diag = _dot(scores.astype(BF16), xh.astype(BF16))
            h_prev = h_s[h]
            y_off = lax.dot_general(cg, h_prev.astype(BF16), (((1,), (1,)), ((), ())),
                                    preferred_element_type=F32) * ecum[:, h:h + 1]
            xw = (xh * wend[:, h:h + 1]).astype(BF16)
            states = lax.dot_general(xw, bg, (((0,), (0,)), ((), ())), preferred_element_type=F32)
            h_s[h] = cdec[:, h:h + 1] * h_prev + states
            y_s[:, cols] = y_diag + y_off + dvec_ref[:, cols] * xh

    y = _rms(y_s[...] * _silu(z_ref[...]), g_ref[...])
    y_ref[...] = y[0:out_steps, :]

    @pl.when(c == n_chunks - 1)
    def _():
        hout_ref[...] = h_s[...]


def _ssd_call(xbc, dt, z, l, w, h0, h0_map, *, chunk, out_len):
    batch, length, _ = xbc.shape
    n_chunks = length // chunk
    assert length == n_chunks * chunk and (out_len == length or n_chunks == 1)
    out_steps = min(chunk, out_len)

    def per_seq(n_steps, width):
        return pl.BlockSpec((None, n_steps, width), lambda b, c: (b, c, 0))

    state_tail = (SSD_HEADS, SSD_HEAD_DIM, SSD_STATE)
    h0_block = (None,) * (h0.ndim - 3) + state_tail
    return pl.pallas_call(
        functools.partial(_ssd_body, chunk, n_chunks, out_steps),
        grid=(batch, n_chunks),
        in_specs=[
            per_seq(chunk, SSD_CONV_DIM), per_seq(chunk, DT_PAD), per_seq(chunk, D_SSD),
            _layer_spec((1, DT_PAD), l), _layer_spec((1, D_SSD), l), _layer_spec((1, D_SSD), l),
            pl.BlockSpec(h0_block, h0_map),
        ],
        out_specs=[
            per_seq(out_steps, D_SSD),
            pl.BlockSpec((None,) + state_tail, lambda b, c: (b, 0, 0, 0)),
        ],
        out_shape=[
            jax.ShapeDtypeStruct((batch, out_len, D_SSD), F32),
            jax.ShapeDtypeStruct((batch,) + state_tail, F32),
        ],
        scratch_shapes=[
            pltpu.VMEM(state_tail, F32),
            pltpu.VMEM((chunk, D_SSD), F32),
        ],
        compiler_params=_params(2),
        name="ssd_chunk",
    )(xbc, dt, z, w["ssd_a_log"], w["ssd_d_cols"], w["ssd_norm_g"], h0)


def _prepare_weights(p):
    depth = p["w_in"].shape[0]

    def vec(a):
        return a.reshape(depth, 1, a.shape[-1])

    def pad_heads(a):
        return jnp.pad(a, ((0, 0), (0, DT_PAD - SSD_HEADS))).reshape(depth, 1, DT_PAD)

    def block_diag(a):
        per = V7X_MXU_DIM // LRU_BLOCK_W
        a = a.reshape(depth, LRU_GATE_GROUPS, per, LRU_BLOCK_W, LRU_BLOCK_W)
        eye = jnp.eye(per, dtype=a.dtype)
        a = a[:, :, :, :, None, :] * eye[None, None, :, None, :, None]
        return a.reshape(depth, LRU_GATE_GROUPS, V7X_MXU_DIM, V7X_MXU_DIM).astype(BF16)

    w_in = p["w_in"]
    o_xbc = D_SSD
    o_dt = o_xbc + SSD_CONV_DIM
    o_gate = o_dt + SSD_HEADS
    o_xr = o_gate + D_LRU
    w_in = jnp.concatenate([
        w_in[:, :, 0:o_dt], w_in[:, :, o_gate:o_xr], w_in[:, :, o_xr:o_xr + D_LRU],
        w_in[:, :, o_dt:o_gate], jnp.zeros((depth, D_MODEL, DT_PAD - SSD_HEADS), w_in.dtype),
    ], axis=-1).astype(BF16)
    w = {
        "w_in": w_in,
        "w_out_ssd": p["w_out"][:, 0:D_SSD].astype(BF16),
        "w_out_lru": p["w_out"][:, D_SSD:].astype(BF16),
        "lru_wa": block_diag(p["lru_wa"]),
        "lru_wx": block_diag(p["lru_wx"]),
        "ssd_dt_bias": pad_heads(p["ssd_dt_bias"]),
        "ssd_a_log": pad_heads(p["ssd_a_log"]),
        "ssd_d_cols": jnp.repeat(p["ssd_d"], SSD_HEAD_DIM, axis=-1).reshape(depth, 1, D_SSD),
        "ssd_conv_w": p["ssd_conv_w"],
        "lru_conv_w": p["lru_conv_w"],
    }
    for name in ("ffn1_wg", "ffn1_wu", "ffn1_wd", "ffn2_wg", "ffn2_wu", "ffn2_wd"):
        w[name] = p[name].astype(BF16)
    for name in ("ffn1_pre_g", "ffn1_post_g", "mix_pre_g", "mix_post_g", "ffn2_pre_g", "ffn2_post_g",
                 "ssd_conv_b", "ssd_norm_g", "lru_conv_b", "lru_ba", "lru_bx", "lru_lambda"):
        w[name] = vec(p[name])
    return w


def _segment_layer(x, l, w, states, *, bsub, steps, chunk, pad_steps):
    (sconv0, sconv0_map), (lconv0, lconv0_map), (h_lru0, h_lru0_map), (h_ssd0, h_ssd0_map) = states
    batch, length, _ = x.shape
    rows = batch * length
    x = _ffn_call(x.reshape(rows, D_MODEL), l, w, "ffn1")
    z, xbc, dt, y_lru, sconv, lconv, h_lru = _in_proj_call(
        x.reshape(batch, length, D_MODEL), l, w, sconv0, sconv0_map, lconv0, lconv0_map,
        h_lru0, h_lru0_map, bsub=bsub, steps=steps, pad_steps=pad_steps)
    y_ssd, h_ssd = _ssd_call(xbc, dt, z, l, w, h_ssd0, h_ssd0_map, chunk=chunk, out_len=length)
    x = _out_ffn_call(x, y_ssd.reshape(rows, D_SSD), y_lru.reshape(rows, D_LRU), l, w)
    return x.reshape(batch, length, D_MODEL), (h_ssd, sconv, h_lru, lconv)


def kernel(x_prompt, x_sample, state_ssd, state_ssd_conv, state_lru, state_lru_conv, meta_tokens,
           ffn1_pre_g, ffn1_post_g, ffn1_wg, ffn1_wu, ffn1_wd, mix_pre_g, mix_post_g, w_in,
           ssd_conv_w, ssd_conv_b, ssd_dt_bias, ssd_a_log, ssd_d, ssd_norm_g, lru_conv_w, lru_conv_b,
           lru_wa, lru_ba, lru_wx, lru_bx, lru_lambda, w_out, ffn2_pre_g, ffn2_post_g, ffn2_wg,
           ffn2_wu, ffn2_wd):
    bp, seq, _ = x_prompt.shape
    bs, dec_seq, _ = x_sample.shape
    depth = w_in.shape[0]
    assert bp == V7X_SUBLANES and seq % SSD_CHUNK == 0 and bs % SAMPLE_BATCH_TILE == 0
    w = _prepare_weights(dict(
        ffn1_pre_g=ffn1_pre_g, ffn1_post_g=ffn1_post_g, ffn1_wg=ffn1_wg, ffn1_wu=ffn1_wu,
        ffn1_wd=ffn1_wd, mix_pre_g=mix_pre_g, mix_post_g=mix_post_g, w_in=w_in,
        ssd_conv_w=ssd_conv_w, ssd_conv_b=ssd_conv_b, ssd_dt_bias=ssd_dt_bias, ssd_a_log=ssd_a_log,
        ssd_d=ssd_d, ssd_norm_g=ssd_norm_g, lru_conv_w=lru_conv_w, lru_conv_b=lru_conv_b,
        lru_wa=lru_wa, lru_ba=lru_ba, lru_wx=lru_wx, lru_bx=lru_bx, lru_lambda=lru_lambda,
        w_out=w_out, ffn2_pre_g=ffn2_pre_g, ffn2_post_g=ffn2_post_g, ffn2_wg=ffn2_wg,
        ffn2_wu=ffn2_wu, ffn2_wd=ffn2_wd))

    xm = jnp.broadcast_to(meta_tokens.astype(F32)[None], (bp, N_META, D_MODEL))
    xp = x_prompt
    xs = x_sample
    sample_steps = ((dec_seq + V7X_SUBLANES - 1) // V7X_SUBLANES) * V7X_SUBLANES
    tail3 = lambda b, t: (b, 0, 0)
    vec2 = lambda b, t: (b, 0)
    seq_state = lambda b, c: (b, 0, 0, 0)
    zero_states = ((jnp.zeros((bp, CONV_TAIL, SSD_CONV_DIM), F32), tail3),
                   (jnp.zeros((bp, CONV_TAIL, D_LRU), F32), tail3),
                   (jnp.zeros((bp, D_LRU), F32), vec2),
                   (jnp.zeros((bp, SSD_HEADS, SSD_HEAD_DIM, SSD_STATE), F32), seq_state))

    p_out = [[] for _ in range(4)]
    s_out = [[] for _ in range(4)]
    for l in range(depth):
        xm, m_st = _segment_layer(xm, l, w, zero_states,
                                  bsub=bp, steps=N_META, chunk=N_META, pad_steps=0)
        xp, p_st = _segment_layer(
            xp, l, w, ((m_st[1], tail3), (m_st[3], tail3), (m_st[2], vec2), (m_st[0], seq_state)),
            bsub=bp, steps=PROMPT_STEPS, chunk=SSD_CHUNK, pad_steps=0)
        xs, s_st = _segment_layer(
            xs, l, w, ((state_ssd_conv, lambda b, t, l=l: (l, b, 0, 0)),
                       (state_lru_conv, lambda b, t, l=l: (l, b, 0, 0)),
                       (state_lru, lambda b, t, l=l: (l, b, 0)),
                       (state_ssd, lambda b, c, l=l: (l, b, 0, 0, 0))),
            bsub=SAMPLE_BATCH_TILE, steps=dec_seq, chunk=sample_steps,
            pad_steps=sample_steps - dec_seq)
        for acc, st in ((p_out, p_st), (s_out, s_st)):
            for k in range(4):
                acc[k].append(st[k])

    return (xp, xs) + tuple(jnp.stack(a) for a in p_out) + tuple(jnp.stack(a) for a in s_out)
```

```python
import functools

import jax
import jax.numpy as jnp
from jax import lax
from jax.experimental import pallas as pl
from jax.experimental.pallas import tpu as pltpu

F32 = jnp.float32
BF16 = jnp.bfloat16

D_MODEL = 1024
D_SSD = 1024
D_LRU = 1024
SSD_HEADS = 16
SSD_HEAD_DIM = 64
SSD_GROUPS = 2
SSD_STATE = 128
HEADS_PER_GROUP = SSD_HEADS // SSD_GROUPS
CONV_W = 4
CONV_TAIL = CONV_W - 1
SSD_CONV_DIM = D_SSD + 2 * SSD_GROUPS * SSD_STATE
LRU_BLOCKS = 16
LRU_BLOCK_W = D_LRU // LRU_BLOCKS
LRU_C = 8.0
EPS = 1e-6
N_META = 16
SSD_CHUNK = 128

V7X_LANES = 128
V7X_SUBLANES = 8
V7X_MXU_DIM = 256
V7X_VMEM_LIMIT_BYTES = 56 * 1024 * 1024

DT_PAD = V7X_LANES
LRU_GATE_GROUPS = D_LRU // V7X_MXU_DIM
COL_Z = 0
COL_XBC = COL_Z + D_SSD
COL_GATE = COL_XBC + SSD_CONV_DIM
COL_XR = COL_GATE + D_LRU
COL_DT = COL_XR + D_LRU
IN_COLS_PAD = COL_DT + DT_PAD
GELU_K = 0.7978845608028654

FFN_ROWS = 512
OUT_FFN_ROWS = 256
PROMPT_STEPS = 64
SAMPLE_BATCH_TILE = 64
ELEMENTWISE_ROWS = 32


def _rms(x, g):
    return x * lax.rsqrt(jnp.mean(x * x, axis=-1, keepdims=True) + EPS) * g


def _silu(x):
    return x * _sigmoid(x)


def _sigmoid(x):
    return 0.5 * jnp.tanh(0.5 * x) + 0.5


def _softplus(x):
    return jnp.maximum(x, 0.0) + jnp.log1p(jnp.exp(-jnp.abs(x)))


def _gelu_tanh(x):
    return 0.5 * x * (1.0 + jnp.tanh(GELU_K * (x + 0.044715 * (x * x * x))))


def _dot(a, b):
    return jnp.dot(a, b, preferred_element_type=F32)


def _layer_spec(tail, l):
    zeros = (0,) * len(tail)
    return pl.BlockSpec((None,) + tuple(tail), lambda *_: (l,) + zeros, pipeline_mode=pl.Buffered(1))


def _params(n_axes):
    return pltpu.CompilerParams(dimension_semantics=("arbitrary",) * n_axes,
                                vmem_limit_bytes=V7X_VMEM_LIMIT_BYTES)


def _row_tile(rows, want):
    tm = min(rows, want)
    assert rows % tm == 0
    return tm


def _rows_map(i):
    return (i, 0)


def _ffn_math(x, gpre, gpost, wg_ref, wu_ref, wd_ref):
    xn = _rms(x, gpre).astype(BF16)
    hg = _dot(xn, wg_ref[...])
    hu = _dot(xn, wu_ref[...])
    a = (_silu(hg) * hu).astype(BF16)
    y = _dot(a, wd_ref[...])
    return x + 0.5 * _rms(y, gpost)


def _ffn_body(x_ref, gpre_ref, gpost_ref, wg_ref, wu_ref, wd_ref, o_ref):
    o_ref[...] = _ffn_math(x_ref[...], gpre_ref[...], gpost_ref[...], wg_ref, wu_ref, wd_ref)


def _ffn_call(x, l, w, prefix):
    rows = x.shape[0]
    tm = _row_tile(rows, FFN_ROWS)
    d_ff = w[prefix + "_wg"].shape[-1]
    return pl.pallas_call(
        _ffn_body,
        grid=(rows // tm,),
        in_specs=[
            pl.BlockSpec((tm, D_MODEL), _rows_map),
            _layer_spec((1, D_MODEL), l), _layer_spec((1, D_MODEL), l),
            _layer_spec((D_MODEL, d_ff), l), _layer_spec((D_MODEL, d_ff), l),
            _layer_spec((d_ff, D_MODEL), l),
        ],
        out_specs=pl.BlockSpec((tm, D_MODEL), _rows_map),
        out_shape=jax.ShapeDtypeStruct((rows, D_MODEL), F32),
        compiler_params=_params(1),
        name=prefix,
    )(x, w[prefix + "_pre_g"], w[prefix + "_post_g"], w[prefix + "_wg"], w[prefix + "_wu"],
      w[prefix + "_wd"])


def _out_ffn_body(x_ref, ys_ref, yl_ref, wos_ref, wol_ref, gmix_ref, gpre_ref, gpost_ref,
                  wg_ref, wu_ref, wd_ref, o_ref):
    m = _dot(ys_ref[...].astype(BF16), wos_ref[...]) + _dot(yl_ref[...].astype(BF16), wol_ref[...])
    x1 = x_ref[...] + _rms(m, gmix_ref[...])
    o_ref[...] = _ffn_math(x1, gpre_ref[...], gpost_ref[...], wg_ref, wu_ref, wd_ref)


def _out_ffn_call(x, ys, yl, l, w):
    rows = x.shape[0]
    tm = _row_tile(rows, OUT_FFN_ROWS)
    d_ff = w["ffn2_wg"].shape[-1]
    row = pl.BlockSpec((tm, D_MODEL), _rows_map)
    return pl.pallas_call(
        _out_ffn_body,
        grid=(rows // tm,),
        in_specs=[
            row, row, row,
            _layer_spec((D_SSD, D_MODEL), l), _layer_spec((D_LRU, D_MODEL), l),
            _layer_spec((1, D_MODEL), l), _layer_spec((1, D_MODEL), l), _layer_spec((1, D_MODEL), l),
            _layer_spec((D_MODEL, d_ff), l), _layer_spec((D_MODEL, d_ff), l),
            _layer_spec((d_ff, D_MODEL), l),
        ],
        out_specs=row,
        out_shape=jax.ShapeDtypeStruct((rows, D_MODEL), F32),
        compiler_params=_params(1),
        name="out_ffn2",
    )(x, ys, yl, w["w_out_ssd"], w["w_out_lru"], w["mix_post_g"], w["ffn2_pre_g"], w["ffn2_post_g"],
      w["ffn2_wg"], w["ffn2_wu"], w["ffn2_wd"])


def _in_proj_body(bsub, steps, pad_steps,
                  x_ref, g_ref, w_ref, scw_ref, scb_ref, lcw_ref, lcb_ref, dtb_ref,
                  wa_ref, wx_ref, ba_ref, bx_ref, lam_ref, sconv0_ref, lconv0_ref, h0_ref,
                  z_ref, xbc_ref, dt_ref, ylru_ref, sconv_ref, lconv_ref, hout_ref,
                  stage, xbuf, sbuf, lbuf, gate_buf, ra_buf, rx_buf, h_carry):
    rows = steps * bsub
    tail = CONV_TAIL * bsub
    rc = min(rows, ELEMENTWISE_ROWS)
    strided = steps >= bsub

    def lane_tile(j):
        return slice(j * V7X_LANES, (j + 1) * V7X_LANES)

    def load_block(ref):
        if not strided:
            for t in range(steps):
                xbuf[t * bsub:(t + 1) * bsub, :] = ref[:, t, :]
            return xbuf[...]
        n_tiles = ref.shape[2] // V7X_LANES
        for j in range(n_tiles):
            for b in range(bsub):
                stage[j, pl.ds(b, steps, stride=bsub), :] = ref[b, :, lane_tile(j)]
        return jnp.concatenate([stage[j] for j in range(n_tiles)], axis=1)

    def store_block(ref, read):
        if not strided:
            value = read(slice(0, ref.shape[2]))
            for t in range(steps):
                ref[:, t, :] = value[t * bsub:(t + 1) * bsub, :]
            return
        n_tiles = ref.shape[2] // V7X_LANES
        for j in range(n_tiles):
            stage[j] = read(lane_tile(j))
        for j in range(n_tiles):
            for b in range(bsub):
                ref[b, :, lane_tile(j)] = stage[j, pl.ds(b, steps, stride=bsub), :]

    def conv_in_place(buf, w_ref, b_ref, act):
        sub = V7X_SUBLANES
        taps = [w_ref[k * sub:(k + 1) * sub, :] for k in range(CONV_W)]
        bias = b_ref[...]
        for r0 in range(0, rows, sub):
            acc = bias + buf[r0:r0 + sub, :] * taps[0]
            for k in range(1, CONV_W):
                acc = acc + buf[r0 + k * bsub:r0 + k * bsub + sub, :] * taps[k]
            buf[r0:r0 + sub, :] = act(acc)

    @pl.when(pl.program_id(1) == 0)
    def _():
        for k in range(CONV_TAIL):
            sbuf[k * bsub:(k + 1) * bsub, :] = sconv0_ref[:, k, :]
            lbuf[k * bsub:(k + 1) * bsub, :] = lconv0_ref[:, k, :]
        h_carry[...] = h0_ref[...]

    xn = _rms(load_block(x_ref), g_ref[...]).astype(BF16)

    lbuf[tail:tail + rows, :] = _dot(xn, w_ref[:, COL_XR:COL_DT])
    gate_buf[...] = _dot(xn, w_ref[:, COL_GATE:COL_XR])
    for k in range(CONV_TAIL):
        lconv_ref[:, k, :] = lbuf[rows + k * bsub:rows + (k + 1) * bsub, :]
    conv_in_place(lbuf, lcw_ref, lcb_ref, lambda v: v)
    for q in range(LRU_GATE_GROUPS):
        cols = slice(q * V7X_MXU_DIM, (q + 1) * V7X_MXU_DIM)
        xr_bf = lbuf[0:rows, cols].astype(BF16)
        ra_buf[:, cols] = _dot(xr_bf, wa_ref[q])
        rx_buf[:, cols] = _dot(xr_bf, wx_ref[q])

    z = _dot(xn, w_ref[:, COL_Z:COL_XBC])
    store_block(z_ref, lambda cols: z[:, cols])
    sbuf[tail:tail + rows, :] = _dot(xn, w_ref[:, COL_XBC:COL_GATE])
    dt = _softplus(_dot(xn, w_ref[:, COL_DT:IN_COLS_PAD]) + dtb_ref[...])
    store_block(dt_ref, lambda cols: dt[:, cols])

    neg_c_softplus = -LRU_C * _softplus(-lam_ref[...])
    for r0 in range(0, rows, rc):
        sl = slice(r0, r0 + rc)
        log_a = neg_c_softplus * _sigmoid(ra_buf[sl, :] + ba_ref[...])
        a = jnp.exp(log_a)
        mult = jnp.sqrt(-jnp.tanh(log_a) * (a * a + 1.0))
        rx_buf[sl, :] = mult * _sigmoid(rx_buf[sl, :] + bx_ref[...]) * lbuf[sl, :]
        ra_buf[sl, :] = a
        gate_buf[sl, :] = _gelu_tanh(gate_buf[sl, :])

    for k in range(CONV_TAIL):
        sconv_ref[:, k, :] = sbuf[rows + k * bsub:rows + (k + 1) * bsub, :]
    conv_in_place(sbuf, scw_ref, scb_ref, _silu)
    store_block(xbc_ref, lambda cols: sbuf[0:rows, cols])

    def group_scan(bg, carry):
        b0 = pl.multiple_of(bg * V7X_SUBLANES, V7X_SUBLANES)
        hsl = pl.ds(b0, V7X_SUBLANES)

        def step(t, h):
            sl = pl.ds(pl.multiple_of(t * bsub + b0, V7X_SUBLANES), V7X_SUBLANES)
            h = ra_buf[sl, :] * h + rx_buf[sl, :]
            gate_buf[sl, :] = h * gate_buf[sl, :]
            return h

        h_carry[hsl, :] = lax.fori_loop(0, steps, step, h_carry[hsl, :],
                                        unroll=8 if steps % 8 == 0 else steps)
        return carry

    lax.fori_loop(0, bsub // V7X_SUBLANES, group_scan, 0)
    store_block(ylru_ref, lambda cols: gate_buf[:, cols])
    hout_ref[...] = h_carry[...]
    sbuf[0:tail, :] = sbuf[rows:rows + tail, :]
    lbuf[0:tail, :] = lbuf[rows:rows + tail, :]
    for t in range(steps, steps + pad_steps):
        for ref in (z_ref, xbc_ref, dt_ref):
            ref[:, t, :] = jnp.zeros((bsub, ref.shape[2]), F32)


def _in_proj_call(x, l, w, sconv0, sconv0_map, lconv0, lconv0_map, h0, h0_map, *,
                  bsub, steps, pad_steps=0):
    batch, length, _ = x.shape
    n_t = length // steps
    n_b = batch // bsub
    assert length == n_t * steps and batch == n_b * bsub and bsub % V7X_SUBLANES == 0
    assert pad_steps == 0 or n_t == 1
    assert steps >= CONV_TAIL
    out_steps = steps + pad_steps
    rows = steps * bsub
    tail = CONV_TAIL * bsub

    def tile(n_steps, width):
        return pl.BlockSpec((bsub, n_steps, width), lambda b, t: (b, t, 0))

    def squeeze_lead(a, block):
        return (None,) * (a.ndim - len(block)) + block

    def conv_out(width):
        return pl.BlockSpec((bsub, CONV_TAIL, width), lambda b, t: (b, 0, 0))

    return pl.pallas_call(
        functools.partial(_in_proj_body, bsub, steps, pad_steps),
        grid=(n_b, n_t),
        in_specs=[
            tile(steps, D_MODEL),
            _layer_spec((1, D_MODEL), l),
            _layer_spec((D_MODEL, IN_COLS_PAD), l),
            _layer_spec((CONV_W * V7X_SUBLANES, SSD_CONV_DIM), l),
            _layer_spec((V7X_SUBLANES, SSD_CONV_DIM), l),
            _layer_spec((CONV_W * V7X_SUBLANES, D_LRU), l), _layer_spec((V7X_SUBLANES, D_LRU), l),
            _layer_spec((1, DT_PAD), l),
            _layer_spec((LRU_GATE_GROUPS, V7X_MXU_DIM, V7X_MXU_DIM), l),
            _layer_spec((LRU_GATE_GROUPS, V7X_MXU_DIM, V7X_MXU_DIM), l),
            _layer_spec((1, D_LRU), l), _layer_spec((1, D_LRU), l), _layer_spec((1, D_LRU), l),
            pl.BlockSpec(squeeze_lead(sconv0, (bsub, CONV_TAIL, SSD_CONV_DIM)), sconv0_map),
            pl.BlockSpec(squeeze_lead(lconv0, (bsub, CONV_TAIL, D_LRU)), lconv0_map),
            pl.BlockSpec(squeeze_lead(h0, (bsub, D_LRU)), h0_map),
        ],
        out_specs=[
            tile(out_steps, D_SSD), tile(out_steps, SSD_CONV_DIM), tile(out_steps, DT_PAD),
            tile(steps, D_LRU),
            conv_out(SSD_CONV_DIM), conv_out(D_LRU),
            pl.BlockSpec((bsub, D_LRU), lambda b, t: (b, 0)),
        ],
        out_shape=[
            jax.ShapeDtypeStruct((batch, n_t * out_steps, D_SSD), F32),
            jax.ShapeDtypeStruct((batch, n_t * out_steps, SSD_CONV_DIM), F32),
            jax.ShapeDtypeStruct((batch, n_t * out_steps, DT_PAD), F32),
            jax.ShapeDtypeStruct((batch, length, D_LRU), F32),
            jax.ShapeDtypeStruct((batch, CONV_TAIL, SSD_CONV_DIM), F32),
            jax.ShapeDtypeStruct((batch, CONV_TAIL, D_LRU), F32),
            jax.ShapeDtypeStruct((batch, D_LRU), F32),
        ],
        scratch_shapes=[
            pltpu.VMEM((SSD_CONV_DIM // V7X_LANES, rows, V7X_LANES) if steps >= bsub
                       else (1, V7X_SUBLANES, V7X_LANES), F32),
            pltpu.VMEM((V7X_SUBLANES, V7X_LANES) if steps >= bsub else (rows, D_MODEL), F32),
            pltpu.VMEM((tail + rows, SSD_CONV_DIM), F32),
            pltpu.VMEM((tail + rows, D_LRU), F32),
            pltpu.VMEM((rows, D_LRU), F32),
            pltpu.VMEM((rows, D_LRU), F32),
            pltpu.VMEM((rows, D_LRU), F32),
            pltpu.VMEM((bsub, D_LRU), F32),
        ],
        compiler_params=_params(2),
        name="in_proj_lru",
    )(x, w["mix_pre_g"], w["w_in"], w["ssd_conv_w"], w["ssd_conv_b"], w["lru_conv_w"], w["lru_conv_b"],
      w["ssd_dt_bias"], w["lru_wa"], w["lru_wx"], w["lru_ba"], w["lru_bx"], w["lru_lambda"],
      sconv0, lconv0, h0)


def _transpose_rows(x, rows):
    lanes = x.shape[1]
    if rows < lanes:
        x = jnp.concatenate([x, jnp.zeros((lanes - rows, lanes), x.dtype)], axis=0)
    return x.T[:, 0:rows]


def _split3(x):
    hi = x.astype(BF16)
    rest = x - hi.astype(F32)
    mid = rest.astype(BF16)
    return hi, mid, (rest - mid.astype(F32)).astype(BF16)


def _ssd_body(chunk, n_chunks, out_steps, xbc_ref, dt_ref, z_ref, alog_ref, dvec_ref, g_ref, e_ref,
              h0_ref, y_ref, hout_ref, ht_s):
    c = pl.program_id(1)
    group_cols = HEADS_PER_GROUP * SSD_HEAD_DIM

    @pl.when(c == 0)
    def _():
        ht_s[...] = h0_ref[...].reshape(D_SSD, SSD_STATE).T

    dt = dt_ref[...]
    a = -jnp.exp(alog_ref[...])
    row = lax.broadcasted_iota(jnp.int32, (chunk, chunk), 0)
    col = lax.broadcasted_iota(jnp.int32, (chunk, chunk), 1)
    causal = row >= col
    ones_lower = jnp.where(causal, 1.0, 0.0).astype(BF16)
    da_hi, da_mid, da_lo = _split3(dt * a)
    cum = _dot(ones_lower, da_hi) + _dot(ones_lower, da_mid) + _dot(ones_lower, da_lo)
    cum_t = _transpose_rows(cum, chunk)
    dt_t = _transpose_rows(dt, chunk)
    s_hi, s_mid, s_lo = _split3(jnp.concatenate([cum, dt], axis=0))
    spread = _dot(s_hi, e_ref[...]) + _dot(s_mid, e_ref[...]) + _dot(s_lo, e_ref[...])
    cum_x = spread[0:chunk, :]
    dt_x = spread[chunk:2 * chunk, :]
    tot_x = cum_x[chunk - 1:chunk, :]
    ecum_x = jnp.exp(cum_x)
    wend_x = jnp.exp(tot_x - cum_x) * dt_x
    cdec_x = jnp.exp(tot_x)
    low_half = lax.broadcasted_iota(jnp.int32, (chunk, V7X_LANES), 1) < SSD_HEAD_DIM

    y_groups = []
    for g in range(SSD_GROUPS):
        b0 = D_SSD + g * SSD_STATE
        c0 = D_SSD + SSD_GROUPS * SSD_STATE + g * SSD_STATE
        gcols = slice(g * group_cols, (g + 1) * group_cols)
        bg_t = _transpose_rows(xbc_ref[:, b0:b0 + SSD_STATE], chunk).astype(BF16)
        cg = xbc_ref[:, c0:c0 + SSD_STATE].astype(BF16)
        cb = _dot(cg, bg_t)
        y_pairs = []
        for pair in range(HEADS_PER_GROUP // 2):
            h = g * HEADS_PER_GROUP + 2 * pair
            x_pair = xbc_ref[:, h * SSD_HEAD_DIM:(h + 2) * SSD_HEAD_DIM]
            scores = []
            for hh in (h, h + 1):
                diff = cum[:, hh:hh + 1] - cum_t[hh:hh + 1, :]
                decay = jnp.exp(jnp.where(causal, diff, -jnp.inf))
                scores.append((cb * decay * dt_t[hh:hh + 1, :]).astype(BF16))
            x_lo = jnp.where(low_half, x_pair, 0.0).astype(BF16)
            x_hi = jnp.where(low_half, 0.0, x_pair).astype(BF16)
            if chunk % V7X_LANES == 0:
                y_pairs.append(_dot(jnp.concatenate(scores, axis=1),
                                    jnp.concatenate([x_lo, x_hi], axis=0)))
            else:
                y_pairs.append(_dot(scores[0], x_lo) + _dot(scores[1], x_hi))
        xg = xbc_ref[:, gcols]
        ht_prev = ht_s[:, gcols]
        y_off = _dot(cg, ht_prev.astype(BF16)) * ecum_x[:, gcols]
        y_groups.append(jnp.concatenate(y_pairs, axis=1) + y_off + dvec_ref[:, gcols] * xg)
        xw = (xg * wend_x[:, gcols]).astype(BF16)
        ht_s[:, gcols] = cdec_x[:, gcols] * ht_prev + _dot(bg_t, xw)

    y = _rms(jnp.concatenate(y_groups, axis=1) * _silu(z_ref[...]), g_ref[...])
    y_ref[...] = y[0:out_steps, :]

    @pl.when(c == n_chunks - 1)
    def _():
        hout_ref[...] = ht_s[...].T.reshape(SSD_HEADS, SSD_HEAD_DIM, SSD_STATE)


def _ssd_call(xbc, dt, z, l, w, h0, h0_map, *, chunk, out_len):
    batch, length, _ = xbc.shape
    n_chunks = length // chunk
    assert length == n_chunks * chunk and (out_len == length or n_chunks == 1)
    out_steps = min(chunk, out_len)

    def per_seq(n_steps, width):
        return pl.BlockSpec((None, n_steps, width), lambda b, c: (b, c, 0))

    state_tail = (SSD_HEADS, SSD_HEAD_DIM, SSD_STATE)
    h0_block = (None,) * (h0.ndim - 3) + state_tail
    return pl.pallas_call(
        functools.partial(_ssd_body, chunk, n_chunks, out_steps),
        grid=(batch, n_chunks),
        in_specs=[
            per_seq(chunk, SSD_CONV_DIM), per_seq(chunk, DT_PAD), per_seq(chunk, D_SSD),
            _layer_spec((1, DT_PAD), l), _layer_spec((1, D_SSD), l), _layer_spec((1, D_SSD), l),
            pl.BlockSpec((DT_PAD, D_SSD), lambda b, c: (0, 0), pipeline_mode=pl.Buffered(1)),
            pl.BlockSpec(h0_block, h0_map),
        ],
        out_specs=[
            per_seq(out_steps, D_SSD),
            pl.BlockSpec((None,) + state_tail, lambda b, c: (b, 0, 0, 0)),
        ],
        out_shape=[
            jax.ShapeDtypeStruct((batch, out_len, D_SSD), F32),
            jax.ShapeDtypeStruct((batch,) + state_tail, F32),
        ],
        scratch_shapes=[pltpu.VMEM((SSD_STATE, D_SSD), F32)],
        compiler_params=_params(2),
        name="ssd_chunk",
    )(xbc, dt, z, w["ssd_a_log"], w["ssd_d_cols"], w["ssd_norm_g"], w["head_spread"], h0)


def _prepare_weights(p):
    depth = p["w_in"].shape[0]

    def vec(a):
        return a.reshape(depth, 1, a.shape[-1])

    def pad_heads(a):
        return jnp.pad(a, ((0, 0), (0, DT_PAD - SSD_HEADS))).reshape(depth, 1, DT_PAD)

    def block_diag(a):
        per = V7X_MXU_DIM // LRU_BLOCK_W
        a = a.reshape(depth, LRU_GATE_GROUPS, per, LRU_BLOCK_W, LRU_BLOCK_W)
        eye = jnp.eye(per, dtype=a.dtype)
        a = a[:, :, :, :, None, :] * eye[None, None, :, None, :, None]
        return a.reshape(depth, LRU_GATE_GROUPS, V7X_MXU_DIM, V7X_MXU_DIM).astype(BF16)

    w_in = p["w_in"]
    o_xbc = D_SSD
    o_dt = o_xbc + SSD_CONV_DIM
    o_gate = o_dt + SSD_HEADS
    o_xr = o_gate + D_LRU
    w_in = jnp.concatenate([
        w_in[:, :, 0:o_dt], w_in[:, :, o_gate:o_xr], w_in[:, :, o_xr:o_xr + D_LRU],
        w_in[:, :, o_dt:o_gate], jnp.zeros((depth, D_MODEL, DT_PAD - SSD_HEADS), w_in.dtype),
    ], axis=-1).astype(BF16)
    w = {
        "w_in": w_in,
        "w_out_ssd": p["w_out"][:, 0:D_SSD].astype(BF16),
        "w_out_lru": p["w_out"][:, D_SSD:].astype(BF16),
        "lru_wa": block_diag(p["lru_wa"]),
        "lru_wx": block_diag(p["lru_wx"]),
        "ssd_dt_bias": pad_heads(p["ssd_dt_bias"]),
        "ssd_a_log": pad_heads(p["ssd_a_log"]),
        "ssd_d_cols": jnp.repeat(p["ssd_d"], SSD_HEAD_DIM, axis=-1).reshape(depth, 1, D_SSD),
        "head_spread": (jnp.arange(DT_PAD)[:, None] == jnp.arange(D_SSD)[None, :] // SSD_HEAD_DIM
                        ).astype(BF16),
    }
    for name in ("ssd_conv_w", "lru_conv_w"):
        w[name] = jnp.repeat(p[name], V7X_SUBLANES, axis=1)
    for name in ("ssd_conv_b", "lru_conv_b"):
        w[name] = jnp.repeat(p[name][:, None, :], V7X_SUBLANES, axis=1)
    for name in ("ffn1_wg", "ffn1_wu", "ffn1_wd", "ffn2_wg", "ffn2_wu", "ffn2_wd"):
        w[name] = p[name].astype(BF16)
    for name in ("ffn1_pre_g", "ffn1_post_g", "mix_pre_g", "mix_post_g", "ffn2_pre_g", "ffn2_post_g",
                 "ssd_norm_g", "lru_ba", "lru_bx", "lru_lambda"):
        w[name] = vec(p[name])
    return w


def _segment_layer(x, l, w, states, *, bsub, steps, chunk, pad_steps):
    (sconv0, sconv0_map), (lconv0, lconv0_map), (h_lru0, h_lru0_map), (h_ssd0, h_ssd0_map) = states
    batch, length, _ = x.shape
    rows = batch * length
    x = _ffn_call(x.reshape(rows, D_MODEL), l, w, "ffn1")
    z, xbc, dt, y_lru, sconv, lconv, h_lru = _in_proj_call(
        x.reshape(batch, length, D_MODEL), l, w, sconv0, sconv0_map, lconv0, lconv0_map,
        h_lru0, h_lru0_map, bsub=bsub, steps=steps, pad_steps=pad_steps)
    y_ssd, h_ssd = _ssd_call(xbc, dt, z, l, w, h_ssd0, h_ssd0_map, chunk=chunk, out_len=length)
    x = _out_ffn_call(x, y_ssd.reshape(rows, D_SSD), y_lru.reshape(rows, D_LRU), l, w)
    return x.reshape(batch, length, D_MODEL), (h_ssd, sconv, h_lru, lconv)


def kernel(x_prompt, x_sample, state_ssd, state_ssd_conv, state_lru, state_lru_conv, meta_tokens,
           ffn1_pre_g, ffn1_post_g, ffn1_wg, ffn1_wu, ffn1_wd, mix_pre_g, mix_post_g, w_in,
           ssd_conv_w, ssd_conv_b, ssd_dt_bias, ssd_a_log, ssd_d, ssd_norm_g, lru_conv_w, lru_conv_b,
           lru_wa, lru_ba, lru_wx, lru_bx, lru_lambda, w_out, ffn2_pre_g, ffn2_post_g, ffn2_wg,
           ffn2_wu, ffn2_wd):
    bp, seq, _ = x_prompt.shape
    bs, dec_seq, _ = x_sample.shape
    depth = w_in.shape[0]
    assert bp == V7X_SUBLANES and seq % SSD_CHUNK == 0 and bs % SAMPLE_BATCH_TILE == 0
    w = _prepare_weights(dict(
        ffn1_pre_g=ffn1_pre_g, ffn1_post_g=ffn1_post_g, ffn1_wg=ffn1_wg, ffn1_wu=ffn1_wu,
        ffn1_wd=ffn1_wd, mix_pre_g=mix_pre_g, mix_post_g=mix_post_g, w_in=w_in,
        ssd_conv_w=ssd_conv_w, ssd_conv_b=ssd_conv_b, ssd_dt_bias=ssd_dt_bias, ssd_a_log=ssd_a_log,
        ssd_d=ssd_d, ssd_norm_g=ssd_norm_g, lru_conv_w=lru_conv_w, lru_conv_b=lru_conv_b,
        lru_wa=lru_wa, lru_ba=lru_ba, lru_wx=lru_wx, lru_bx=lru_bx, lru_lambda=lru_lambda,
        w_out=w_out, ffn2_pre_g=ffn2_pre_g, ffn2_post_g=ffn2_post_g, ffn2_wg=ffn2_wg,
        ffn2_wu=ffn2_wu, ffn2_wd=ffn2_wd))

    xm = jnp.broadcast_to(meta_tokens.astype(F32)[None], (bp, N_META, D_MODEL))
    xp = x_prompt
    xs = x_sample
    sample_steps = ((dec_seq + V7X_SUBLANES - 1) // V7X_SUBLANES) * V7X_SUBLANES
    tail3 = lambda b, t: (b, 0, 0)
    vec2 = lambda b, t: (b, 0)
    seq_state = lambda b, c: (b, 0, 0, 0)
    zero_states = ((jnp.zeros((bp, CONV_TAIL, SSD_CONV_DIM), F32), tail3),
                   (jnp.zeros((bp, CONV_TAIL, D_LRU), F32), tail3),
                   (jnp.zeros((bp, D_LRU), F32), vec2),
                   (jnp.zeros((bp, SSD_HEADS, SSD_HEAD_DIM, SSD_STATE), F32), seq_state))

    p_out = [[] for _ in range(4)]
    s_out = [[] for _ in range(4)]
    for l in range(depth):
        xm, m_st = _segment_layer(xm, l, w, zero_states,
                                  bsub=bp, steps=N_META, chunk=N_META, pad_steps=0)
        xp, p_st = _segment_layer(
            xp, l, w, ((m_st[1], tail3), (m_st[3], tail3), (m_st[2], vec2), (m_st[0], seq_state)),
            bsub=bp, steps=PROMPT_STEPS, chunk=SSD_CHUNK, pad_steps=0)
        xs, s_st = _segment_layer(
            xs, l, w, ((state_ssd_conv, lambda b, t, l=l: (l, b, 0, 0)),
                       (state_lru_conv, lambda b, t, l=l: (l, b, 0, 0)),
                       (state_lru, lambda b, t, l=l: (l, b, 0)),
                       (state_ssd, lambda b, c, l=l: (l, b, 0, 0, 0))),
            bsub=SAMPLE_BATCH_TILE, steps=dec_seq, chunk=sample_steps,
            pad_steps=sample_steps - dec_seq)
        for acc, st in ((p_out, p_st), (s_out, s_st)):
            for k in range(4):
                acc[k].append(st[k])

    return (xp, xs) + tuple(jnp.stack(a) for a in p_out) + tuple(jnp.stack(a) for a in s_out)
```

```python
import functools
import itertools

import jax
import jax.numpy as jnp
from jax import lax
from jax.experimental import pallas as pl
from jax.experimental.pallas import tpu as pltpu

F32 = jnp.float32
BF16 = jnp.bfloat16

D_MODEL = 1024
D_SSD = 1024
D_LRU = 1024
SSD_HEADS = 16
SSD_HEAD_DIM = 64
SSD_GROUPS = 2
SSD_STATE = 128
HEADS_PER_GROUP = SSD_HEADS // SSD_GROUPS
CONV_W = 4
CONV_TAIL = CONV_W - 1
SSD_CONV_DIM = D_SSD + 2 * SSD_GROUPS * SSD_STATE
LRU_BLOCKS = 16
LRU_BLOCK_W = D_LRU // LRU_BLOCKS
LRU_C = 8.0
EPS = 1e-6
N_META = 16
SSD_CHUNK = 128

V7X_LANES = 128
V7X_SUBLANES = 8
V7X_MXU_DIM = 256
V7X_VMEM_LIMIT_BYTES = 56 * 1024 * 1024

DT_PAD = V7X_LANES
LRU_GATE_GROUPS = D_LRU // V7X_MXU_DIM
COL_Z = 0
COL_XBC = COL_Z + D_SSD
COL_GATE = COL_XBC + SSD_CONV_DIM
COL_XR = COL_GATE + D_LRU
COL_DT = COL_XR + D_LRU
IN_COLS_PAD = COL_DT + DT_PAD
GELU_K = 0.7978845608028654

FFN_ROWS = 512
OUT_FFN_ROWS = 256
PROMPT_STEPS = 64
SAMPLE_BATCH_TILE = 64
PROMPT_SUB_TILES = 2
SCAN_UNROLL_LIMIT = 128
PROMPT_SEQS = 2
SAMPLE_SEQS = 8
ELEMENTWISE_ROWS = 8
ELEMENTWISE_COLS = 256


def _rms(x, g):
    return x * lax.rsqrt(jnp.mean(x * x, axis=-1, keepdims=True) + EPS) * g


def _silu(x):
    return x * _sigmoid(x)


def _sigmoid(x):
    return 0.5 * jnp.tanh(0.5 * x) + 0.5


def _softplus(x):
    return jnp.maximum(x, 0.0) + jnp.log1p(jnp.exp(-jnp.abs(x)))


def _gelu_tanh(x):
    return 0.5 * x * (1.0 + jnp.tanh(GELU_K * (x + 0.044715 * (x * x * x))))


def _dot(a, b):
    return jnp.dot(a, b, preferred_element_type=F32)


def _layer_spec(tail, l):
    zeros = (0,) * len(tail)
    return pl.BlockSpec((None,) + tuple(tail), lambda *_: (l,) + zeros, pipeline_mode=pl.Buffered(1))


def _params(n_axes):
    return pltpu.CompilerParams(dimension_semantics=("arbitrary",) * n_axes,
                                vmem_limit_bytes=V7X_VMEM_LIMIT_BYTES)


def _row_tile(rows, want):
    tm = min(rows, want)
    assert rows % tm == 0
    return tm


def _rows_map(i):
    return (i, 0)


def _ffn_math(x, gpre, gpost, wg_ref, wu_ref, wd_ref):
    xn = _rms(x, gpre).astype(BF16)
    hg = _dot(xn, wg_ref[...])
    hu = _dot(xn, wu_ref[...])
    a = (_silu(hg) * hu).astype(BF16)
    y = _dot(a, wd_ref[...])
    return x + 0.5 * _rms(y, gpost)


def _ffn_body(x_ref, gpre_ref, gpost_ref, wg_ref, wu_ref, wd_ref, o_ref):
    o_ref[...] = _ffn_math(x_ref[...], gpre_ref[...], gpost_ref[...], wg_ref, wu_ref, wd_ref)


def _ffn_call(x, l, w, prefix):
    rows = x.shape[0]
    tm = _row_tile(rows, FFN_ROWS)
    d_ff = w[prefix + "_wg"].shape[-1]
    return pl.pallas_call(
        _ffn_body,
        grid=(rows // tm,),
        in_specs=[
            pl.BlockSpec((tm, D_MODEL), _rows_map),
            _layer_spec((1, D_MODEL), l), _layer_spec((1, D_MODEL), l),
            _layer_spec((D_MODEL, d_ff), l), _layer_spec((D_MODEL, d_ff), l),
            _layer_spec((d_ff, D_MODEL), l),
        ],
        out_specs=pl.BlockSpec((tm, D_MODEL), _rows_map),
        out_shape=jax.ShapeDtypeStruct((rows, D_MODEL), F32),
        compiler_params=_params(1),
        name=prefix,
    )(x, w[prefix + "_pre_g"], w[prefix + "_post_g"], w[prefix + "_wg"], w[prefix + "_wu"],
      w[prefix + "_wd"])


def _out_ffn_body(x_ref, ys_ref, yl_ref, wos_ref, wol_ref, gmix_ref, gpre_ref, gpost_ref,
                  wg_ref, wu_ref, wd_ref, o_ref):
    m = _dot(ys_ref[...].astype(BF16), wos_ref[...]) + _dot(yl_ref[...].astype(BF16), wol_ref[...])
    x1 = x_ref[...] + _rms(m, gmix_ref[...])
    o_ref[...] = _ffn_math(x1, gpre_ref[...], gpost_ref[...], wg_ref, wu_ref, wd_ref)


def _out_ffn_call(x, ys, yl, l, w):
    rows = x.shape[0]
    tm = _row_tile(rows, OUT_FFN_ROWS)
    d_ff = w["ffn2_wg"].shape[-1]
    row = pl.BlockSpec((tm, D_MODEL), _rows_map)
    return pl.pallas_call(
        _out_ffn_body,
        grid=(rows // tm,),
        in_specs=[
            row, row, row,
            _layer_spec((D_SSD, D_MODEL), l), _layer_spec((D_LRU, D_MODEL), l),
            _layer_spec((1, D_MODEL), l), _layer_spec((1, D_MODEL), l), _layer_spec((1, D_MODEL), l),
            _layer_spec((D_MODEL, d_ff), l), _layer_spec((D_MODEL, d_ff), l),
            _layer_spec((d_ff, D_MODEL), l),
        ],
        out_specs=row,
        out_shape=jax.ShapeDtypeStruct((rows, D_MODEL), F32),
        compiler_params=_params(1),
        name="out_ffn2",
    )(x, ys, yl, w["w_out_ssd"], w["w_out_lru"], w["mix_post_g"], w["ffn2_pre_g"], w["ffn2_post_g"],
      w["ffn2_wg"], w["ffn2_wu"], w["ffn2_wd"])


def _in_proj_body(bsub, steps, pad_steps, n_sub,
                  x_ref, g_ref, w_ref, scw_ref, scb_ref, lcw_ref, lcb_ref, dtb_ref,
                  wa_ref, wx_ref, ba_ref, bx_ref, lam_ref, sconv0_ref, lconv0_ref, h0_ref,
                  z_ref, xbc_ref, dt_ref, ylru_ref, sconv_ref, lconv_ref, hout_ref,
                  stage, xbuf, sbuf, lbuf, gate_buf, ra_buf, rx_buf, h_carry):
    rows = steps * bsub
    tail = CONV_TAIL * bsub
    sub_steps = steps // n_sub
    sub_rows = sub_steps * bsub
    rc = min(sub_rows, ELEMENTWISE_ROWS)
    strided = sub_steps >= bsub

    def lane_tile(j):
        return slice(j * V7X_LANES, (j + 1) * V7X_LANES)

    def load_block(ref, t0):
        r0 = t0 * bsub
        if not strided:
            for t in range(sub_steps):
                xbuf[r0 + t * bsub:r0 + (t + 1) * bsub, :] = ref[:, t0 + t, :]
            return xbuf[r0:r0 + sub_rows, :]
        n_tiles = ref.shape[2] // V7X_LANES
        for j in range(n_tiles):
            for b in range(bsub):
                stage[j, pl.ds(r0 + b, sub_steps, stride=bsub), :] = ref[b, t0:t0 + sub_steps, lane_tile(j)]
        return jnp.concatenate([stage[j, r0:r0 + sub_rows, :] for j in range(n_tiles)], axis=1)

    def store_block(ref, t0, read):
        r0 = t0 * bsub
        if not strided:
            value = read(slice(0, ref.shape[2]))
            for t in range(sub_steps):
                ref[:, t0 + t, :] = value[t * bsub:(t + 1) * bsub, :]
            return
        n_tiles = ref.shape[2] // V7X_LANES
        for j in range(n_tiles):
            stage[j, r0:r0 + sub_rows, :] = read(lane_tile(j))
        for j in range(n_tiles):
            for b in range(bsub):
                ref[b, t0:t0 + sub_steps, lane_tile(j)] = stage[j, pl.ds(r0 + b, sub_steps, stride=bsub), :]

    def conv_in_place(buf, w_ref, b_ref, act, r_lo):
        sub = V7X_SUBLANES
        for c0 in range(0, buf.shape[1], ELEMENTWISE_COLS):
            cols = slice(c0, c0 + ELEMENTWISE_COLS)
            taps = [w_ref[k * sub:(k + 1) * sub, cols] for k in range(CONV_W)]
            bias = b_ref[:, cols]
            for r0 in range(r_lo, r_lo + sub_rows, sub):
                acc = bias + buf[r0:r0 + sub, cols] * taps[0]
                for k in range(1, CONV_W):
                    acc = acc + buf[r0 + k * bsub:r0 + k * bsub + sub, cols] * taps[k]
                buf[r0:r0 + sub, cols] = act(acc)

    @pl.when(pl.program_id(1) == 0)
    def _():
        for k in range(CONV_TAIL):
            sbuf[k * bsub:(k + 1) * bsub, :] = sconv0_ref[:, k, :]
            lbuf[k * bsub:(k + 1) * bsub, :] = lconv0_ref[:, k, :]
        h_carry[...] = h0_ref[...]

    neg_c_softplus = -LRU_C * _softplus(-lam_ref[...])

    def sub_tile(i):
        t0 = i * sub_steps
        r_lo = t0 * bsub
        out = slice(r_lo, r_lo + sub_rows)
        pre = slice(tail + r_lo, tail + r_lo + sub_rows)
        xn = _rms(load_block(x_ref, t0), g_ref[...]).astype(BF16)
        yield
        lbuf[pre, :] = _dot(xn, w_ref[:, COL_XR:COL_DT])
        gate_buf[out, :] = _dot(xn, w_ref[:, COL_GATE:COL_XR])
        yield
        conv_in_place(lbuf, lcw_ref, lcb_ref, lambda v: v, r_lo)
        yield
        for q in range(LRU_GATE_GROUPS):
            cols = slice(q * V7X_MXU_DIM, (q + 1) * V7X_MXU_DIM)
            xr_bf = lbuf[out, cols].astype(BF16)
            ra_buf[out, cols] = _dot(xr_bf, wa_ref[q])
            rx_buf[out, cols] = _dot(xr_bf, wx_ref[q])
        yield
        for c0 in range(0, D_LRU, ELEMENTWISE_COLS):
            cols = slice(c0, c0 + ELEMENTWISE_COLS)
            ncs, ba, bx = neg_c_softplus[:, cols], ba_ref[:, cols], bx_ref[:, cols]
            for r0 in range(r_lo, r_lo + sub_rows, rc):
                sl = slice(r0, r0 + rc)
                log_a = ncs * _sigmoid(ra_buf[sl, cols] + ba)
                a = jnp.exp(log_a)
                mult = jnp.sqrt(-jnp.tanh(log_a) * (a * a + 1.0))
                rx_buf[sl, cols] = mult * _sigmoid(rx_buf[sl, cols] + bx) * lbuf[sl, cols]
                ra_buf[sl, cols] = a
                gate_buf[sl, cols] = _gelu_tanh(gate_buf[sl, cols])
        yield
        z = _dot(xn, w_ref[:, COL_Z:COL_XBC])
        sbuf[pre, :] = _dot(xn, w_ref[:, COL_XBC:COL_GATE])
        dt = _softplus(_dot(xn, w_ref[:, COL_DT:IN_COLS_PAD]) + dtb_ref[...])
        yield
        store_block(z_ref, t0, lambda cols: z[:, cols])
        store_block(dt_ref, t0, lambda cols: dt[:, cols])
        conv_in_place(sbuf, scw_ref, scb_ref, _silu, r_lo)
        yield
        store_block(xbc_ref, t0, lambda cols: sbuf[out, cols])

    waiting = [sub_tile(i) for i in range(n_sub)]
    running = []
    while waiting or running:
        if waiting:
            running.append(waiting.pop(0))
        for stages in list(running):
            if next(stages, "done") == "done":
                running.remove(stages)

    for k in range(CONV_TAIL):
        sconv_ref[:, k, :] = sbuf[rows + k * bsub:rows + (k + 1) * bsub, :]
        lconv_ref[:, k, :] = lbuf[rows + k * bsub:rows + (k + 1) * bsub, :]

    def scan_step(b0, t, h):
        start = t * bsub + b0
        if not isinstance(start, int):
            start = pl.multiple_of(start, V7X_SUBLANES)
        sl = pl.ds(start, V7X_SUBLANES)
        h = ra_buf[sl, :] * h + rx_buf[sl, :]
        gate_buf[sl, :] = h * gate_buf[sl, :]
        return h

    n_groups = bsub // V7X_SUBLANES
    if n_groups * steps <= SCAN_UNROLL_LIMIT:
        for bg in range(n_groups):
            hsl = slice(bg * V7X_SUBLANES, (bg + 1) * V7X_SUBLANES)
            h = h_carry[hsl, :]
            for t in range(steps):
                h = scan_step(bg * V7X_SUBLANES, t, h)
            h_carry[hsl, :] = h
    else:
        def group_scan(bg, carry):
            b0 = pl.multiple_of(bg * V7X_SUBLANES, V7X_SUBLANES)
            hsl = pl.ds(b0, V7X_SUBLANES)
            h_carry[hsl, :] = lax.fori_loop(0, steps, functools.partial(scan_step, b0), h_carry[hsl, :])
            return carry

        lax.fori_loop(0, n_groups, group_scan, 0)
    for i in range(n_sub):
        store_block(ylru_ref, i * sub_steps,
                    lambda cols, i=i: gate_buf[i * sub_rows:(i + 1) * sub_rows, cols])
    hout_ref[...] = h_carry[...]
    sbuf[0:tail, :] = sbuf[rows:rows + tail, :]
    lbuf[0:tail, :] = lbuf[rows:rows + tail, :]
    for t in range(steps, steps + pad_steps):
        for ref in (z_ref, xbc_ref, dt_ref):
            ref[:, t, :] = jnp.zeros((bsub, ref.shape[2]), F32)


def _in_proj_call(x, l, w, sconv0, sconv0_map, lconv0, lconv0_map, h0, h0_map, *,
                  bsub, steps, pad_steps=0, n_sub=1):
    batch, length, _ = x.shape
    n_t = length // steps
    n_b = batch // bsub
    assert length == n_t * steps and batch == n_b * bsub and bsub % V7X_SUBLANES == 0
    assert pad_steps == 0 or n_t == 1
    assert steps >= CONV_TAIL
    assert steps % n_sub == 0
    strided = steps // n_sub >= bsub
    out_steps = steps + pad_steps
    rows = steps * bsub
    tail = CONV_TAIL * bsub

    def tile(n_steps, width):
        return pl.BlockSpec((bsub, n_steps, width), lambda b, t: (b, t, 0))

    def squeeze_lead(a, block):
        return (None,) * (a.ndim - len(block)) + block

    def conv_out(width):
        return pl.BlockSpec((bsub, CONV_TAIL, width), lambda b, t: (b, 0, 0))

    return pl.pallas_call(
        functools.partial(_in_proj_body, bsub, steps, pad_steps, n_sub),
        grid=(n_b, n_t),
        in_specs=[
            tile(steps, D_MODEL),
            _layer_spec((1, D_MODEL), l),
            _layer_spec((D_MODEL, IN_COLS_PAD), l),
            _layer_spec((CONV_W * V7X_SUBLANES, SSD_CONV_DIM), l),
            _layer_spec((V7X_SUBLANES, SSD_CONV_DIM), l),
            _layer_spec((CONV_W * V7X_SUBLANES, D_LRU), l), _layer_spec((V7X_SUBLANES, D_LRU), l),
            _layer_spec((1, DT_PAD), l),
            _layer_spec((LRU_GATE_GROUPS, V7X_MXU_DIM, V7X_MXU_DIM), l),
            _layer_spec((LRU_GATE_GROUPS, V7X_MXU_DIM, V7X_MXU_DIM), l),
            _layer_spec((1, D_LRU), l), _layer_spec((1, D_LRU), l), _layer_spec((1, D_LRU), l),
            pl.BlockSpec(squeeze_lead(sconv0, (bsub, CONV_TAIL, SSD_CONV_DIM)), sconv0_map),
            pl.BlockSpec(squeeze_lead(lconv0, (bsub, CONV_TAIL, D_LRU)), lconv0_map),
            pl.BlockSpec(squeeze_lead(h0, (bsub, D_LRU)), h0_map),
        ],
        out_specs=[
            tile(out_steps, D_SSD), tile(out_steps, SSD_CONV_DIM), tile(out_steps, DT_PAD),
            tile(steps, D_LRU),
            conv_out(SSD_CONV_DIM), conv_out(D_LRU),
            pl.BlockSpec((bsub, D_LRU), lambda b, t: (b, 0)),
        ],
        out_shape=[
            jax.ShapeDtypeStruct((batch, n_t * out_steps, D_SSD), F32),
            jax.ShapeDtypeStruct((batch, n_t * out_steps, SSD_CONV_DIM), F32),
            jax.ShapeDtypeStruct((batch, n_t * out_steps, DT_PAD), F32),
            jax.ShapeDtypeStruct((batch, length, D_LRU), F32),
            jax.ShapeDtypeStruct((batch, CONV_TAIL, SSD_CONV_DIM), F32),
            jax.ShapeDtypeStruct((batch, CONV_TAIL, D_LRU), F32),
            jax.ShapeDtypeStruct((batch, D_LRU), F32),
        ],
        scratch_shapes=[
            pltpu.VMEM((SSD_CONV_DIM // V7X_LANES, rows, V7X_LANES) if strided
                       else (1, V7X_SUBLANES, V7X_LANES), F32),
            pltpu.VMEM((V7X_SUBLANES, V7X_LANES) if strided else (rows, D_MODEL), F32),
            pltpu.VMEM((tail + rows, SSD_CONV_DIM), F32),
            pltpu.VMEM((tail + rows, D_LRU), F32),
            pltpu.VMEM((rows, D_LRU), F32),
            pltpu.VMEM((rows, D_LRU), F32),
            pltpu.VMEM((rows, D_LRU), F32),
            pltpu.VMEM((bsub, D_LRU), F32),
        ],
        compiler_params=_params(2),
        name="in_proj_lru",
    )(x, w["mix_pre_g"], w["w_in"], w["ssd_conv_w"], w["ssd_conv_b"], w["lru_conv_w"], w["lru_conv_b"],
      w["ssd_dt_bias"], w["lru_wa"], w["lru_wx"], w["lru_ba"], w["lru_bx"], w["lru_lambda"],
      sconv0, lconv0, h0)


def _transpose_rows(x, rows):
    lanes = x.shape[1]
    if rows < lanes:
        x = jnp.concatenate([x, jnp.zeros((lanes - rows, lanes), x.dtype)], axis=0)
    return x.T[:, 0:rows]


def _split_bf16(x, terms):
    parts = []
    for _ in range(terms - 1):
        parts.append(x.astype(BF16))
        x = x - parts[-1].astype(F32)
    return parts + [x.astype(BF16)]


def _ssd_body(chunk, n_chunks, out_steps, seqs, xbc_ref, dt_ref, z_ref, alog_ref, dvec_ref, g_ref,
              e_ref, h0_ref, *rest):
    (y_ref, hout_ref, ht_s) = rest[-3:]
    stages = [_ssd_sequence(chunk, n_chunks, out_steps, xbc_ref.at[s], dt_ref.at[s], z_ref.at[s],
                            alog_ref, dvec_ref, g_ref, e_ref, h0_ref.at[s], y_ref.at[s],
                            hout_ref.at[s], ht_s.at[s]) for s in range(seqs)]
    for _ in itertools.zip_longest(*stages):
        pass


def _ssd_sequence(chunk, n_chunks, out_steps, xbc_ref, dt_ref, z_ref, alog_ref, dvec_ref, g_ref, e_ref,
                  h0_ref, y_ref, hout_ref, ht_s):
    c = pl.program_id(1)
    group_cols = HEADS_PER_GROUP * SSD_HEAD_DIM

    single = n_chunks == 1
    if single:
        h_given = h0_ref[...].reshape(D_SSD, SSD_STATE)
        ht_bf = h_given.astype(BF16).T
    else:
        @pl.when(c == 0)
        def _():
            ht_s[...] = h0_ref[...].reshape(D_SSD, SSD_STATE).T

    dt = dt_ref[...]
    a = -jnp.exp(alog_ref[...])
    row = lax.broadcasted_iota(jnp.int32, (chunk, chunk), 0)
    col = lax.broadcasted_iota(jnp.int32, (chunk, chunk), 1)
    causal = row >= col
    ones_lower = jnp.where(causal, 1.0, 0.0).astype(BF16)
    da_hi, da_mid, da_lo = _split_bf16(dt * a, 3)
    cum = _dot(ones_lower, da_hi) + _dot(ones_lower, da_mid) + _dot(ones_lower, da_lo)
    yield
    cum_t = _transpose_rows(cum, chunk)
    dt_t = _transpose_rows(dt, chunk)
    tot = cum[chunk - 1:chunk, :]
    cdec = jnp.exp(tot)
    s_hi, s_lo = _split_bf16(jnp.concatenate([jnp.exp(cum), jnp.exp(tot - cum) * dt], axis=0), 2)
    spread = _dot(s_hi, e_ref[...]) + _dot(s_lo, e_ref[...])
    yield
    ecum_x = spread[0:chunk, :]
    wend_x = spread[chunk:2 * chunk, :]
    cdec_x = ecum_x[chunk - 1:chunk, :]
    low_half = lax.broadcasted_iota(jnp.int32, (chunk, V7X_LANES), 1) < SSD_HEAD_DIM

    y_groups = []
    for g in range(SSD_GROUPS):
        b0 = D_SSD + g * SSD_STATE
        c0 = D_SSD + SSD_GROUPS * SSD_STATE + g * SSD_STATE
        gcols = slice(g * group_cols, (g + 1) * group_cols)
        bg_t = _transpose_rows(xbc_ref[:, b0:b0 + SSD_STATE], chunk).astype(BF16)
        cg = xbc_ref[:, c0:c0 + SSD_STATE].astype(BF16)
        cb = _dot(cg, bg_t)
        yield
        y_pairs = []
        for pair in range(HEADS_PER_GROUP // 2):
            h = g * HEADS_PER_GROUP + 2 * pair
            x_pair = xbc_ref[:, h * SSD_HEAD_DIM:(h + 2) * SSD_HEAD_DIM]
            scores = []
            for hh in (h, h + 1):
                diff = cum[:, hh:hh + 1] - cum_t[hh:hh + 1, :]
                decay = jnp.exp(jnp.where(causal, diff, -jnp.inf))
                scores.append((cb * decay * dt_t[hh:hh + 1, :]).astype(BF16))
            x_lo = jnp.where(low_half, x_pair, 0.0).astype(BF16)
            x_hi = jnp.where(low_half, 0.0, x_pair).astype(BF16)
            if chunk % V7X_LANES == 0:
                y_pairs.append(_dot(jnp.concatenate(scores, axis=1),
                                    jnp.concatenate([x_lo, x_hi], axis=0)))
            else:
                y_pairs.append(_dot(scores[0], x_lo) + _dot(scores[1], x_hi))
            yield
        xg = xbc_ref[:, gcols]
        xw = (xg * wend_x[:, gcols]).astype(BF16)
        if single:
            y_off = _dot(cg, ht_bf[:, gcols]) * ecum_x[:, gcols]
            bg = xbc_ref[:, b0:b0 + SSD_STATE].astype(BF16)
            update = lax.dot_general(xw, bg, (((0,), (0,)), ((), ())), preferred_element_type=F32)
            heads = range(g * HEADS_PER_GROUP, (g + 1) * HEADS_PER_GROUP)
            decay_rows = jnp.concatenate(
                [jnp.broadcast_to(cdec[:, h:h + 1], (SSD_HEAD_DIM, SSD_STATE)) for h in heads], axis=0)
            h_new = decay_rows * h_given[gcols, :] + update
            hout_ref[g * HEADS_PER_GROUP:(g + 1) * HEADS_PER_GROUP] = h_new.reshape(
                HEADS_PER_GROUP, SSD_HEAD_DIM, SSD_STATE)
        else:
            ht_prev = ht_s[:, gcols]
            y_off = _dot(cg, ht_prev.astype(BF16)) * ecum_x[:, gcols]
            ht_s[:, gcols] = cdec_x[:, gcols] * ht_prev + _dot(bg_t, xw)
        y_groups.append(jnp.concatenate(y_pairs, axis=1) + y_off + dvec_ref[:, gcols] * xg)
        yield

    y = _rms(jnp.concatenate(y_groups, axis=1) * _silu(z_ref[...]), g_ref[...])
    y_ref[...] = y[0:out_steps, :]

    if not single:
        @pl.when(c == n_chunks - 1)
        def _():
            hout_ref[...] = ht_s[...].T.reshape(SSD_HEADS, SSD_HEAD_DIM, SSD_STATE)


def _ssd_call(xbc, dt, z, l, w, h0, h0_map, h_stack, *, chunk, out_len, seqs):
    batch, length, _ = xbc.shape
    depth = w["ssd_a_log"].shape[0]
    n_chunks = length // chunk
    assert length == n_chunks * chunk and (out_len == length or n_chunks == 1) and batch % seqs == 0
    out_steps = min(chunk, out_len)

    def per_seq(n_steps, width):
        return pl.BlockSpec((seqs, n_steps, width), lambda b, c: (b, c, 0))

    state_tail = (SSD_HEADS, SSD_HEAD_DIM, SSD_STATE)
    h0_block = (None,) * (h0.ndim - 4) + (seqs,) + state_tail
    operands = [xbc, dt, z, w["ssd_a_log"], w["ssd_d_cols"], w["ssd_norm_g"], w["head_spread"], h0]
    in_specs = [
        per_seq(chunk, SSD_CONV_DIM), per_seq(chunk, DT_PAD), per_seq(chunk, D_SSD),
        _layer_spec((1, DT_PAD), l), _layer_spec((1, D_SSD), l), _layer_spec((1, D_SSD), l),
        pl.BlockSpec((DT_PAD, D_SSD), lambda b, c: (0, 0), pipeline_mode=pl.Buffered(1)),
        pl.BlockSpec(h0_block, h0_map),
    ]
    aliases = {}
    if h_stack is not None:
        aliases = {len(operands): 1}
        operands.append(h_stack)
        in_specs.append(pl.BlockSpec(memory_space=pl.ANY))
    return pl.pallas_call(
        functools.partial(_ssd_body, chunk, n_chunks, out_steps, seqs),
        grid=(batch // seqs, n_chunks),
        in_specs=in_specs,
        out_specs=[
            per_seq(out_steps, D_SSD),
            pl.BlockSpec((None, seqs) + state_tail, lambda b, c: (l, b, 0, 0, 0)),
        ],
        out_shape=[
            jax.ShapeDtypeStruct((batch, out_len, D_SSD), F32),
            jax.ShapeDtypeStruct((depth, batch) + state_tail, F32),
        ],
        scratch_shapes=[pltpu.VMEM((seqs, SSD_STATE, D_SSD), F32)],
        input_output_aliases=aliases,
        compiler_params=_params(2),
        name="ssd_chunk",
    )(*operands)


def _prepare_weights(p):
    depth = p["w_in"].shape[0]

    def vec(a):
        return a.reshape(depth, 1, a.shape[-1])

    def pad_heads(a):
        return jnp.pad(a, ((0, 0), (0, DT_PAD - SSD_HEADS))).reshape(depth, 1, DT_PAD)

    def block_diag(a):
        per = V7X_MXU_DIM // LRU_BLOCK_W
        a = a.reshape(depth, LRU_GATE_GROUPS, per, LRU_BLOCK_W, LRU_BLOCK_W)
        eye = jnp.eye(per, dtype=a.dtype)
        a = a[:, :, :, :, None, :] * eye[None, None, :, None, :, None]
        return a.reshape(depth, LRU_GATE_GROUPS, V7X_MXU_DIM, V7X_MXU_DIM).astype(BF16)

    w_in = p["w_in"]
    o_xbc = D_SSD
    o_dt = o_xbc + SSD_CONV_DIM
    o_gate = o_dt + SSD_HEADS
    o_xr = o_gate + D_LRU
    w_in = jnp.concatenate([
        w_in[:, :, 0:o_dt], w_in[:, :, o_gate:o_xr], w_in[:, :, o_xr:o_xr + D_LRU],
        w_in[:, :, o_dt:o_gate], jnp.zeros((depth, D_MODEL, DT_PAD - SSD_HEADS), w_in.dtype),
    ], axis=-1).astype(BF16)
    w = {
        "w_in": w_in,
        "w_out_ssd": p["w_out"][:, 0:D_SSD].astype(BF16),
        "w_out_lru": p["w_out"][:, D_SSD:].astype(BF16),
        "lru_wa": block_diag(p["lru_wa"]),
        "lru_wx": block_diag(p["lru_wx"]),
        "ssd_dt_bias": pad_heads(p["ssd_dt_bias"]),
        "ssd_a_log": pad_heads(p["ssd_a_log"]),
        "ssd_d_cols": jnp.repeat(p["ssd_d"], SSD_HEAD_DIM, axis=-1).reshape(depth, 1, D_SSD),
        "head_spread": (jnp.arange(DT_PAD)[:, None] == jnp.arange(D_SSD)[None, :] // SSD_HEAD_DIM
                        ).astype(BF16),
    }
    for name in ("ssd_conv_w", "lru_conv_w"):
        w[name] = jnp.repeat(p[name], V7X_SUBLANES, axis=1)
    for name in ("ssd_conv_b", "lru_conv_b"):
        w[name] = jnp.repeat(p[name][:, None, :], V7X_SUBLANES, axis=1)
    for name in ("ffn1_wg", "ffn1_wu", "ffn1_wd", "ffn2_wg", "ffn2_wu", "ffn2_wd"):
        w[name] = p[name].astype(BF16)
    for name in ("ffn1_pre_g", "ffn1_post_g", "mix_pre_g", "mix_post_g", "ffn2_pre_g", "ffn2_post_g",
                 "ssd_norm_g", "lru_ba", "lru_bx", "lru_lambda"):
        w[name] = vec(p[name])
    return w


def _segment_layer(x, l, w, states, ssd_stack, *, bsub, steps, chunk, pad_steps, seqs, n_sub=1):
    (sconv0, sconv0_map), (lconv0, lconv0_map), (h_lru0, h_lru0_map), (h_ssd0, h_ssd0_map) = states
    batch, length, _ = x.shape
    rows = batch * length
    x = _ffn_call(x.reshape(rows, D_MODEL), l, w, "ffn1")
    z, xbc, dt, y_lru, sconv, lconv, h_lru = _in_proj_call(
        x.reshape(batch, length, D_MODEL), l, w, sconv0, sconv0_map, lconv0, lconv0_map,
        h_lru0, h_lru0_map, bsub=bsub, steps=steps, pad_steps=pad_steps, n_sub=n_sub)
    y_ssd, ssd_stack = _ssd_call(xbc, dt, z, l, w, h_ssd0, h_ssd0_map, ssd_stack,
                                 chunk=chunk, out_len=length, seqs=seqs)
    x = _out_ffn_call(x, y_ssd.reshape(rows, D_SSD), y_lru.reshape(rows, D_LRU), l, w)
    return x.reshape(batch, length, D_MODEL), (sconv, h_lru, lconv), ssd_stack


def kernel(x_prompt, x_sample, state_ssd, state_ssd_conv, state_lru, state_lru_conv, meta_tokens,
           ffn1_pre_g, ffn1_post_g, ffn1_wg, ffn1_wu, ffn1_wd, mix_pre_g, mix_post_g, w_in,
           ssd_conv_w, ssd_conv_b, ssd_dt_bias, ssd_a_log, ssd_d, ssd_norm_g, lru_conv_w, lru_conv_b,
           lru_wa, lru_ba, lru_wx, lru_bx, lru_lambda, w_out, ffn2_pre_g, ffn2_post_g, ffn2_wg,
           ffn2_wu, ffn2_wd):
    bp, seq, _ = x_prompt.shape
    bs, dec_seq, _ = x_sample.shape
    depth = w_in.shape[0]
    assert bp == V7X_SUBLANES and seq % SSD_CHUNK == 0 and bs % SAMPLE_BATCH_TILE == 0
    w = _prepare_weights(dict(
        ffn1_pre_g=ffn1_pre_g, ffn1_post_g=ffn1_post_g, ffn1_wg=ffn1_wg, ffn1_wu=ffn1_wu,
        ffn1_wd=ffn1_wd, mix_pre_g=mix_pre_g, mix_post_g=mix_post_g, w_in=w_in,
        ssd_conv_w=ssd_conv_w, ssd_conv_b=ssd_conv_b, ssd_dt_bias=ssd_dt_bias, ssd_a_log=ssd_a_log,
        ssd_d=ssd_d, ssd_norm_g=ssd_norm_g, lru_conv_w=lru_conv_w, lru_conv_b=lru_conv_b,
        lru_wa=lru_wa, lru_ba=lru_ba, lru_wx=lru_wx, lru_bx=lru_bx, lru_lambda=lru_lambda,
        w_out=w_out, ffn2_pre_g=ffn2_pre_g, ffn2_post_g=ffn2_post_g, ffn2_wg=ffn2_wg,
        ffn2_wu=ffn2_wu, ffn2_wd=ffn2_wd))

    xm = jnp.broadcast_to(meta_tokens.astype(F32)[None], (bp, N_META, D_MODEL))
    xp = x_prompt
    xs = x_sample
    sample_steps = ((dec_seq + V7X_SUBLANES - 1) // V7X_SUBLANES) * V7X_SUBLANES
    tail3 = lambda b, t: (b, 0, 0)
    vec2 = lambda b, t: (b, 0)
    seq_state = lambda b, c: (b, 0, 0, 0)
    zero_states = ((jnp.zeros((bp, CONV_TAIL, SSD_CONV_DIM), F32), tail3),
                   (jnp.zeros((bp, CONV_TAIL, D_LRU), F32), tail3),
                   (jnp.zeros((bp, D_LRU), F32), vec2),
                   (jnp.zeros((bp, SSD_HEADS, SSD_HEAD_DIM, SSD_STATE), F32), seq_state))

    layer_state = lambda b, c, l: (l, b, 0, 0, 0)
    p_out = [[] for _ in range(3)]
    s_out = [[] for _ in range(3)]
    m_ssd = p_ssd = s_ssd = None
    for l in range(depth):
        at_l = functools.partial(layer_state, l=l)
        xm, m_st, m_ssd = _segment_layer(
            xm, l, w, zero_states, m_ssd,
            bsub=bp, steps=N_META, chunk=N_META, pad_steps=0, seqs=PROMPT_SEQS)
        xp, p_st, p_ssd = _segment_layer(
            xp, l, w, ((m_st[0], tail3), (m_st[2], tail3), (m_st[1], vec2), (m_ssd, at_l)), p_ssd,
            bsub=bp, steps=PROMPT_STEPS, chunk=SSD_CHUNK, pad_steps=0, seqs=PROMPT_SEQS,
            n_sub=PROMPT_SUB_TILES)
        xs, s_st, s_ssd = _segment_layer(
            xs, l, w, ((state_ssd_conv, lambda b, t, l=l: (l, b, 0, 0)),
                       (state_lru_conv, lambda b, t, l=l: (l, b, 0, 0)),
                       (state_lru, lambda b, t, l=l: (l, b, 0)),
                       (state_ssd, at_l)), s_ssd,
            bsub=SAMPLE_BATCH_TILE, steps=dec_seq, chunk=sample_steps,
            pad_steps=sample_steps - dec_seq, seqs=SAMPLE_SEQS)
        for acc, st in ((p_out, p_st), (s_out, s_st)):
            for k in range(3):
                acc[k].append(st[k])

    p_conv, p_lru, p_lconv = (jnp.stack(a) for a in p_out)
    s_conv, s_lru, s_lconv = (jnp.stack(a) for a in s_out)
    return (xp, xs, p_ssd, p_conv, p_lru, p_lconv, s_ssd, s_conv, s_lru, s_lconv)
```

```python
import functools
import itertools

import jax
import jax.numpy as jnp
from jax import lax
from jax.experimental import pallas as pl
from jax.experimental.pallas import tpu as pltpu

F32 = jnp.float32
BF16 = jnp.bfloat16

D_MODEL = 1024
D_SSD = 1024
D_LRU = 1024
SSD_HEADS = 16
SSD_HEAD_DIM = 64
SSD_GROUPS = 2
SSD_STATE = 128
HEADS_PER_GROUP = SSD_HEADS // SSD_GROUPS
CONV_W = 4
CONV_TAIL = CONV_W - 1
SSD_CONV_DIM = D_SSD + 2 * SSD_GROUPS * SSD_STATE
LRU_BLOCKS = 16
LRU_BLOCK_W = D_LRU // LRU_BLOCKS
LRU_C = 8.0
EPS = 1e-6
N_META = 16
SSD_CHUNK = 128

V7X_LANES = 128
V7X_SUBLANES = 8
V7X_MXU_DIM = 256
V7X_VMEM_LIMIT_BYTES = 56 * 1024 * 1024

DT_PAD = V7X_LANES
LRU_GATE_GROUPS = D_LRU // V7X_MXU_DIM
COL_Z = 0
COL_XBC = COL_Z + D_SSD
COL_GATE = COL_XBC + SSD_CONV_DIM
COL_XR = COL_GATE + D_LRU
COL_DT = COL_XR + D_LRU
IN_COLS_PAD = COL_DT + DT_PAD
GELU_K = 0.7978845608028654

FFN_ROWS = 512
OUT_FFN_ROWS = 512
PROMPT_STEPS = 64
SAMPLE_BATCH_TILE = 64
PROMPT_SUB_TILES = 2
SCAN_UNROLL_LIMIT = 128
PROMPT_SEQS = 2
SAMPLE_SEQS = 8
ELEMENTWISE_ROWS = 16
ELEMENTWISE_COLS = 512


def _rms(x, g):
    return x * lax.rsqrt(jnp.mean(x * x, axis=-1, keepdims=True) + EPS) * g


def _silu(x):
    return x * _sigmoid(x)


def _sigmoid(x):
    return 0.5 * jnp.tanh(0.5 * x) + 0.5


def _softplus(x):
    return jnp.maximum(x, 0.0) + jnp.log1p(jnp.exp(-jnp.abs(x)))


def _gelu_tanh(x):
    return 0.5 * x * (1.0 + jnp.tanh(GELU_K * (x + 0.044715 * (x * x * x))))


def _dot(a, b):
    return jnp.dot(a, b, preferred_element_type=F32)


def _layer_spec(tail, l):
    zeros = (0,) * len(tail)
    return pl.BlockSpec((None,) + tuple(tail), lambda *_: (l,) + zeros, pipeline_mode=pl.Buffered(1))


def _params(n_axes):
    return pltpu.CompilerParams(dimension_semantics=("arbitrary",) * n_axes,
                                vmem_limit_bytes=V7X_VMEM_LIMIT_BYTES)


def _row_tile(rows, want):
    tm = min(rows, want)
    assert rows % tm == 0
    return tm


def _rows_map(i):
    return (i, 0)


def _ffn_math(x, gpre, gpost, wg_ref, wu_ref, wd_ref):
    xn = _rms(x, gpre).astype(BF16)
    hg = _dot(xn, wg_ref[...])
    hu = _dot(xn, wu_ref[...])
    a = (_silu(hg) * hu).astype(BF16)
    y = _dot(a, wd_ref[...])
    return x + 0.5 * _rms(y, gpost)


def _ffn_body(x_ref, gpre_ref, gpost_ref, wg_ref, wu_ref, wd_ref, o_ref):
    o_ref[...] = _ffn_math(x_ref[...], gpre_ref[...], gpost_ref[...], wg_ref, wu_ref, wd_ref)


def _ffn_call(x, l, w, prefix):
    rows = x.shape[0]
    tm = _row_tile(rows, FFN_ROWS)
    d_ff = w[prefix + "_wg"].shape[-1]
    return pl.pallas_call(
        _ffn_body,
        grid=(rows // tm,),
        in_specs=[
            pl.BlockSpec((tm, D_MODEL), _rows_map),
            _layer_spec((1, D_MODEL), l), _layer_spec((1, D_MODEL), l),
            _layer_spec((D_MODEL, d_ff), l), _layer_spec((D_MODEL, d_ff), l),
            _layer_spec((d_ff, D_MODEL), l),
        ],
        out_specs=pl.BlockSpec((tm, D_MODEL), _rows_map),
        out_shape=jax.ShapeDtypeStruct((rows, D_MODEL), F32),
        compiler_params=_params(1),
        name=prefix,
    )(x, w[prefix + "_pre_g"], w[prefix + "_post_g"], w[prefix + "_wg"], w[prefix + "_wu"],
      w[prefix + "_wd"])


def _out_ffn_body(x_ref, ys_ref, yl_ref, wos_ref, wol_ref, gmix_ref, gpre_ref, gpost_ref,
                  wg_ref, wu_ref, wd_ref, o_ref):
    m = _dot(ys_ref[...].astype(BF16), wos_ref[...]) + _dot(yl_ref[...].astype(BF16), wol_ref[...])
    x1 = x_ref[...] + _rms(m, gmix_ref[...])
    o_ref[...] = _ffn_math(x1, gpre_ref[...], gpost_ref[...], wg_ref, wu_ref, wd_ref)


def _out_ffn_call(x, ys, yl, l, w):
    rows = x.shape[0]
    tm = _row_tile(rows, OUT_FFN_ROWS)
    d_ff = w["ffn2_wg"].shape[-1]
    row = pl.BlockSpec((tm, D_MODEL), _rows_map)
    return pl.pallas_call(
        _out_ffn_body,
        grid=(rows // tm,),
        in_specs=[
            row, row, row,
            pl.BlockSpec((None, D_SSD, D_MODEL), lambda i: (l, 0, 0), pipeline_mode=pl.Buffered(1)),
            pl.BlockSpec((None, D_LRU, D_MODEL), lambda i: (l, D_SSD // D_LRU, 0),
                         pipeline_mode=pl.Buffered(1)),
            _layer_spec((1, D_MODEL), l), _layer_spec((1, D_MODEL), l), _layer_spec((1, D_MODEL), l),
            _layer_spec((D_MODEL, d_ff), l), _layer_spec((D_MODEL, d_ff), l),
            _layer_spec((d_ff, D_MODEL), l),
        ],
        out_specs=row,
        out_shape=jax.ShapeDtypeStruct((rows, D_MODEL), F32),
        compiler_params=_params(1),
        name="out_ffn2",
    )(x, ys, yl, w["w_out"], w["w_out"], w["mix_post_g"], w["ffn2_pre_g"], w["ffn2_post_g"],
      w["ffn2_wg"], w["ffn2_wu"], w["ffn2_wd"])


def _in_proj_body(bsub, steps, pad_steps, n_sub,
                  x_ref, g_ref, w_ref, scw_ref, scb_ref, lcw_ref, lcb_ref, dtb_ref,
                  wa_ref, wx_ref, ba_ref, bx_ref, lam_ref, sconv0_ref, lconv0_ref, h0_ref,
                  z_ref, xbc_ref, dt_ref, ylru_ref, sconv_ref, lconv_ref, hout_ref,
                  stage, xbuf, sbuf, lbuf, gate_buf, ra_buf, rx_buf, h_carry):
    rows = steps * bsub
    tail = CONV_TAIL * bsub
    sub_steps = steps // n_sub
    sub_rows = sub_steps * bsub
    rc = min(sub_rows, ELEMENTWISE_ROWS)
    strided = sub_steps >= bsub

    def lane_tile(j):
        return slice(j * V7X_LANES, (j + 1) * V7X_LANES)

    def load_block(ref, t0):
        r0 = t0 * bsub
        if not strided:
            for t in range(sub_steps):
                xbuf[r0 + t * bsub:r0 + (t + 1) * bsub, :] = ref[:, t0 + t, :]
            return xbuf[r0:r0 + sub_rows, :]
        n_tiles = ref.shape[2] // V7X_LANES
        for j in range(n_tiles):
            for b in range(bsub):
                stage[j, pl.ds(r0 + b, sub_steps, stride=bsub), :] = ref[b, t0:t0 + sub_steps, lane_tile(j)]
        return jnp.concatenate([stage[j, r0:r0 + sub_rows, :] for j in range(n_tiles)], axis=1)

    def store_block(ref, t0, read):
        r0 = t0 * bsub
        if not strided:
            value = read(slice(0, ref.shape[2]))
            for t in range(sub_steps):
                ref[:, t0 + t, :] = value[t * bsub:(t + 1) * bsub, :]
            return
        n_tiles = ref.shape[2] // V7X_LANES
        for j in range(n_tiles):
            stage[j, r0:r0 + sub_rows, :] = read(lane_tile(j))
        for j in range(n_tiles):
            for b in range(bsub):
                ref[b, t0:t0 + sub_steps, lane_tile(j)] = stage[j, pl.ds(r0 + b, sub_steps, stride=bsub), :]

    def conv_in_place(buf, w_ref, b_ref, act, r_lo):
        sub = V7X_SUBLANES
        for c0 in range(0, buf.shape[1], ELEMENTWISE_COLS):
            cols = slice(c0, c0 + ELEMENTWISE_COLS)
            taps = [w_ref[k * sub:(k + 1) * sub, cols] for k in range(CONV_W)]
            bias = b_ref[:, cols]
            for r0 in range(r_lo, r_lo + sub_rows, sub):
                acc = bias + buf[r0:r0 + sub, cols] * taps[0]
                for k in range(1, CONV_W):
                    acc = acc + buf[r0 + k * bsub:r0 + k * bsub + sub, cols] * taps[k]
                buf[r0:r0 + sub, cols] = act(acc)

    @pl.when(pl.program_id(1) == 0)
    def _():
        for k in range(CONV_TAIL):
            sbuf[k * bsub:(k + 1) * bsub, :] = sconv0_ref[:, k, :]
            lbuf[k * bsub:(k + 1) * bsub, :] = lconv0_ref[:, k, :]
        h_carry[...] = h0_ref[...]

    neg_c_softplus = -LRU_C * _softplus(-lam_ref[...])

    def sub_tile(i):
        t0 = i * sub_steps
        r_lo = t0 * bsub
        out = slice(r_lo, r_lo + sub_rows)
        pre = slice(tail + r_lo, tail + r_lo + sub_rows)
        xn = _rms(load_block(x_ref, t0), g_ref[...]).astype(BF16)
        yield
        lbuf[pre, :] = _dot(xn, w_ref[:, COL_XR:COL_DT])
        gate_buf[out, :] = _dot(xn, w_ref[:, COL_GATE:COL_XR])
        yield
        conv_in_place(lbuf, lcw_ref, lcb_ref, lambda v: v, r_lo)
        yield
        for q in range(LRU_GATE_GROUPS):
            cols = slice(q * V7X_MXU_DIM, (q + 1) * V7X_MXU_DIM)
            xr_bf = lbuf[out, cols].astype(BF16)
            ra_buf[out, cols] = _dot(xr_bf, wa_ref[q])
            rx_buf[out, cols] = _dot(xr_bf, wx_ref[q])
        yield
        for c0 in range(0, D_LRU, ELEMENTWISE_COLS):
            cols = slice(c0, c0 + ELEMENTWISE_COLS)
            ncs, ba, bx = neg_c_softplus[:, cols], ba_ref[:, cols], bx_ref[:, cols]
            for r0 in range(r_lo, r_lo + sub_rows, rc):
                sl = slice(r0, r0 + rc)
                log_a = ncs * _sigmoid(ra_buf[sl, cols] + ba)
                a = jnp.exp(log_a)
                m = -jnp.tanh(log_a) * (a * a + 1.0)
                mult = jnp.where(m > 0.0, m * lax.rsqrt(m), 0.0)
                rx_buf[sl, cols] = mult * _sigmoid(rx_buf[sl, cols] + bx) * lbuf[sl, cols]
                ra_buf[sl, cols] = a
                gate_buf[sl, cols] = _gelu_tanh(gate_buf[sl, cols])
        yield
        z = _dot(xn, w_ref[:, COL_Z:COL_XBC])
        sbuf[pre, :] = _dot(xn, w_ref[:, COL_XBC:COL_GATE])
        dt = _softplus(_dot(xn, w_ref[:, COL_DT:IN_COLS_PAD]) + dtb_ref[...])
        yield
        store_block(z_ref, t0, lambda cols: z[:, cols])
        store_block(dt_ref, t0, lambda cols: dt[:, cols])
        conv_in_place(sbuf, scw_ref, scb_ref, _silu, r_lo)
        yield
        store_block(xbc_ref, t0, lambda cols: sbuf[out, cols])

    waiting = [sub_tile(i) for i in range(n_sub)]
    running = []
    while waiting or running:
        if waiting:
            running.append(waiting.pop(0))
        for stages in list(running):
            if next(stages, "done") == "done":
                running.remove(stages)

    for k in range(CONV_TAIL):
        sconv_ref[:, k, :] = sbuf[rows + k * bsub:rows + (k + 1) * bsub, :]
        lconv_ref[:, k, :] = lbuf[rows + k * bsub:rows + (k + 1) * bsub, :]

    def scan_step(b0, t, h):
        start = t * bsub + b0
        if not isinstance(start, int):
            start = pl.multiple_of(start, V7X_SUBLANES)
        sl = pl.ds(start, V7X_SUBLANES)
        h = ra_buf[sl, :] * h + rx_buf[sl, :]
        gate_buf[sl, :] = h * gate_buf[sl, :]
        return h

    n_groups = bsub // V7X_SUBLANES
    if n_groups * steps <= SCAN_UNROLL_LIMIT:
        for bg in range(n_groups):
            hsl = slice(bg * V7X_SUBLANES, (bg + 1) * V7X_SUBLANES)
            h = h_carry[hsl, :]
            for t in range(steps):
                h = scan_step(bg * V7X_SUBLANES, t, h)
            h_carry[hsl, :] = h
    else:
        def group_scan(bg, carry):
            b0 = pl.multiple_of(bg * V7X_SUBLANES, V7X_SUBLANES)
            hsl = pl.ds(b0, V7X_SUBLANES)
            h_carry[hsl, :] = lax.fori_loop(0, steps, functools.partial(scan_step, b0), h_carry[hsl, :])
            return carry

        lax.fori_loop(0, n_groups, group_scan, 0)
    for i in range(n_sub):
        store_block(ylru_ref, i * sub_steps,
                    lambda cols, i=i: gate_buf[i * sub_rows:(i + 1) * sub_rows, cols])
    hout_ref[...] = h_carry[...]
    sbuf[0:tail, :] = sbuf[rows:rows + tail, :]
    lbuf[0:tail, :] = lbuf[rows:rows + tail, :]
    for t in range(steps, steps + pad_steps):
        for ref in (z_ref, xbc_ref, dt_ref):
            ref[:, t, :] = jnp.zeros((bsub, ref.shape[2]), F32)


def _in_proj_call(x, l, w, sconv0, sconv0_map, lconv0, lconv0_map, h0, h0_map, *,
                  bsub, steps, pad_steps=0, n_sub=1):
    batch, length, _ = x.shape
    n_t = length // steps
    n_b = batch // bsub
    assert length == n_t * steps and batch == n_b * bsub and bsub % V7X_SUBLANES == 0
    assert pad_steps == 0 or n_t == 1
    assert steps >= CONV_TAIL
    assert steps % n_sub == 0
    strided = steps // n_sub >= bsub
    out_steps = steps + pad_steps
    rows = steps * bsub
    tail = CONV_TAIL * bsub

    def tile(n_steps, width):
        return pl.BlockSpec((bsub, n_steps, width), lambda b, t: (b, t, 0))

    def squeeze_lead(a, block):
        return (None,) * (a.ndim - len(block)) + block

    def conv_out(width):
        return pl.BlockSpec((bsub, CONV_TAIL, width), lambda b, t: (b, 0, 0))

    return pl.pallas_call(
        functools.partial(_in_proj_body, bsub, steps, pad_steps, n_sub),
        grid=(n_b, n_t),
        in_specs=[
            tile(steps, D_MODEL),
            _layer_spec((1, D_MODEL), l),
            _layer_spec((D_MODEL, IN_COLS_PAD), l),
            _layer_spec((CONV_W * V7X_SUBLANES, SSD_CONV_DIM), l),
            _layer_spec((V7X_SUBLANES, SSD_CONV_DIM), l),
            _layer_spec((CONV_W * V7X_SUBLANES, D_LRU), l), _layer_spec((V7X_SUBLANES, D_LRU), l),
            _layer_spec((1, DT_PAD), l),
            _layer_spec((LRU_GATE_GROUPS, V7X_MXU_DIM, V7X_MXU_DIM), l),
            _layer_spec((LRU_GATE_GROUPS, V7X_MXU_DIM, V7X_MXU_DIM), l),
            _layer_spec((1, D_LRU), l), _layer_spec((1, D_LRU), l), _layer_spec((1, D_LRU), l),
            pl.BlockSpec(squeeze_lead(sconv0, (bsub, CONV_TAIL, SSD_CONV_DIM)), sconv0_map),
            pl.BlockSpec(squeeze_lead(lconv0, (bsub, CONV_TAIL, D_LRU)), lconv0_map),
            pl.BlockSpec(squeeze_lead(h0, (bsub, D_LRU)), h0_map),
        ],
        out_specs=[
            tile(out_steps, D_SSD), tile(out_steps, SSD_CONV_DIM), tile(out_steps, DT_PAD),
            tile(steps, D_LRU),
            conv_out(SSD_CONV_DIM), conv_out(D_LRU),
            pl.BlockSpec((bsub, D_LRU), lambda b, t: (b, 0)),
        ],
        out_shape=[
            jax.ShapeDtypeStruct((batch, n_t * out_steps, D_SSD), F32),
            jax.ShapeDtypeStruct((batch, n_t * out_steps, SSD_CONV_DIM), F32),
            jax.ShapeDtypeStruct((batch, n_t * out_steps, DT_PAD), F32),
            jax.ShapeDtypeStruct((batch, length, D_LRU), F32),
            jax.ShapeDtypeStruct((batch, CONV_TAIL, SSD_CONV_DIM), F32),
            jax.ShapeDtypeStruct((batch, CONV_TAIL, D_LRU), F32),
            jax.ShapeDtypeStruct((batch, D_LRU), F32),
        ],
        scratch_shapes=[
            pltpu.VMEM((SSD_CONV_DIM // V7X_LANES, rows, V7X_LANES) if strided
                       else (1, V7X_SUBLANES, V7X_LANES), F32),
            pltpu.VMEM((V7X_SUBLANES, V7X_LANES) if strided else (rows, D_MODEL), F32),
            pltpu.VMEM((tail + rows, SSD_CONV_DIM), F32),
            pltpu.VMEM((tail + rows, D_LRU), F32),
            pltpu.VMEM((rows, D_LRU), F32),
            pltpu.VMEM((rows, D_LRU), F32),
            pltpu.VMEM((rows, D_LRU), F32),
            pltpu.VMEM((bsub, D_LRU), F32),
        ],
        compiler_params=_params(2),
        name="in_proj_lru",
    )(x, w["mix_pre_g"], w["w_in"], w["ssd_conv_w"], w["ssd_conv_b"], w["lru_conv_w"], w["lru_conv_b"],
      w["ssd_dt_bias"], w["lru_wa"], w["lru_wx"], w["lru_ba"], w["lru_bx"], w["lru_lambda"],
      sconv0, lconv0, h0)


def _transpose_rows(x, rows):
    lanes = x.shape[1]
    if rows < lanes:
        x = jnp.concatenate([x, jnp.zeros((lanes - rows, lanes), x.dtype)], axis=0)
    return x.T[:, 0:rows]


def _split_bf16(x, terms):
    parts = []
    for _ in range(terms - 1):
        parts.append(x.astype(BF16))
        x = x - parts[-1].astype(F32)
    return parts + [x.astype(BF16)]


def _ssd_body(chunk, n_chunks, out_steps, seqs, xbc_ref, dt_ref, z_ref, alog_ref, dvec_ref, g_ref,
              e_ref, h0_ref, *rest):
    (y_ref, hout_ref, ht_s) = rest[-3:]
    stages = [_ssd_sequence(chunk, n_chunks, out_steps, xbc_ref.at[s], dt_ref.at[s], z_ref.at[s],
                            alog_ref, dvec_ref, g_ref, e_ref, h0_ref.at[s], y_ref.at[s],
                            hout_ref.at[s], ht_s.at[s]) for s in range(seqs)]
    for _ in itertools.zip_longest(*stages):
        pass


def _ssd_sequence(chunk, n_chunks, out_steps, xbc_ref, dt_ref, z_ref, alog_ref, dvec_ref, g_ref, e_ref,
                  h0_ref, y_ref, hout_ref, ht_s):
    c = pl.program_id(1)
    group_cols = HEADS_PER_GROUP * SSD_HEAD_DIM

    single = n_chunks == 1
    if single:
        h_given = h0_ref[...].reshape(D_SSD, SSD_STATE)
        ht_bf = h_given.astype(BF16).T
    else:
        @pl.when(c == 0)
        def _():
            ht_s[...] = h0_ref[...].reshape(D_SSD, SSD_STATE).T

    dt = dt_ref[...]
    a = -jnp.exp(alog_ref[...])
    row = lax.broadcasted_iota(jnp.int32, (chunk, chunk), 0)
    col = lax.broadcasted_iota(jnp.int32, (chunk, chunk), 1)
    causal = row >= col
    ones_lower = jnp.where(causal, 1.0, 0.0).astype(BF16)
    da_hi, da_mid, da_lo = _split_bf16(dt * a, 3)
    cum = _dot(ones_lower, da_hi) + _dot(ones_lower, da_mid) + _dot(ones_lower, da_lo)
    yield
    cum_t = _transpose_rows(cum, chunk)
    dt_t = _transpose_rows(dt, chunk)
    tot = cum[chunk - 1:chunk, :]
    cdec = jnp.exp(tot)
    s_hi, s_lo = _split_bf16(jnp.concatenate([jnp.exp(cum), jnp.exp(tot - cum) * dt], axis=0), 2)
    spread = _dot(s_hi, e_ref[...]) + _dot(s_lo, e_ref[...])
    yield
    ecum_x = spread[0:chunk, :]
    wend_x = spread[chunk:2 * chunk, :]
    cdec_x = ecum_x[chunk - 1:chunk, :]
    low_half = lax.broadcasted_iota(jnp.int32, (chunk, V7X_LANES), 1) < SSD_HEAD_DIM

    y_groups = []
    for g in range(SSD_GROUPS):
        b0 = D_SSD + g * SSD_STATE
        c0 = D_SSD + SSD_GROUPS * SSD_STATE + g * SSD_STATE
        gcols = slice(g * group_cols, (g + 1) * group_cols)
        bg_t = _transpose_rows(xbc_ref[:, b0:b0 + SSD_STATE], chunk).astype(BF16)
        cg = xbc_ref[:, c0:c0 + SSD_STATE].astype(BF16)
        cb = _dot(cg, bg_t)
        yield
        y_pairs = []
        for pair in range(HEADS_PER_GROUP // 2):
            h = g * HEADS_PER_GROUP + 2 * pair
            x_pair = xbc_ref[:, h * SSD_HEAD_DIM:(h + 2) * SSD_HEAD_DIM]
            scores = []
            for hh in (h, h + 1):
                diff = cum[:, hh:hh + 1] - cum_t[hh:hh + 1, :]
                decay = jnp.exp(jnp.where(causal, diff, -jnp.inf))
                scores.append((cb * decay * dt_t[hh:hh + 1, :]).astype(BF16))
            x_lo = jnp.where(low_half, x_pair, 0.0).astype(BF16)
            x_hi = jnp.where(low_half, 0.0, x_pair).astype(BF16)
            if chunk % V7X_LANES == 0:
                y_pairs.append(_dot(jnp.concatenate(scores, axis=1),
                                    jnp.concatenate([x_lo, x_hi], axis=0)))
            else:
                y_pairs.append(_dot(scores[0], x_lo) + _dot(scores[1], x_hi))
            yield
        xg = xbc_ref[:, gcols]
        xw = (xg * wend_x[:, gcols]).astype(BF16)
        if single:
            y_off = _dot(cg, ht_bf[:, gcols]) * ecum_x[:, gcols]
            bg = xbc_ref[:, b0:b0 + SSD_STATE].astype(BF16)
            update = lax.dot_general(xw, bg, (((0,), (0,)), ((), ())), preferred_element_type=F32)
            heads = range(g * HEADS_PER_GROUP, (g + 1) * HEADS_PER_GROUP)
            decay_rows = jnp.concatenate(
                [jnp.broadcast_to(cdec[:, h:h + 1], (SSD_HEAD_DIM, SSD_STATE)) for h in heads], axis=0)
            h_new = decay_rows * h_given[gcols, :] + update
            hout_ref[g * HEADS_PER_GROUP:(g + 1) * HEADS_PER_GROUP] = h_new.reshape(
                HEADS_PER_GROUP, SSD_HEAD_DIM, SSD_STATE)
        else:
            ht_prev = ht_s[:, gcols]
            y_off = _dot(cg, ht_prev.astype(BF16)) * ecum_x[:, gcols]
            ht_s[:, gcols] = cdec_x[:, gcols] * ht_prev + _dot(bg_t, xw)
        y_groups.append(jnp.concatenate(y_pairs, axis=1) + y_off + dvec_ref[:, gcols] * xg)
        yield

    y = _rms(jnp.concatenate(y_groups, axis=1) * _silu(z_ref[...]), g_ref[...])
    y_ref[...] = y[0:out_steps, :]

    if not single:
        @pl.when(c == n_chunks - 1)
        def _():
            hout_ref[...] = ht_s[...].T.reshape(SSD_HEADS, SSD_HEAD_DIM, SSD_STATE)


def _ssd_call(xbc, dt, z, l, w, h0, h0_map, h_stack, *, chunk, out_len, seqs):
    batch, length, _ = xbc.shape
    depth = w["ssd_a_log"].shape[0]
    n_chunks = length // chunk
    assert length == n_chunks * chunk and (out_len == length or n_chunks == 1) and batch % seqs == 0
    out_steps = min(chunk, out_len)

    def per_seq(n_steps, width):
        return pl.BlockSpec((seqs, n_steps, width), lambda b, c: (b, c, 0))

    state_tail = (SSD_HEADS, SSD_HEAD_DIM, SSD_STATE)
    h0_block = (None,) * (h0.ndim - 4) + (seqs,) + state_tail
    operands = [xbc, dt, z, w["ssd_a_log"], w["ssd_d_cols"], w["ssd_norm_g"], w["head_spread"], h0]
    in_specs = [
        per_seq(chunk, SSD_CONV_DIM), per_seq(chunk, DT_PAD), per_seq(chunk, D_SSD),
        _layer_spec((1, DT_PAD), l), _layer_spec((1, D_SSD), l), _layer_spec((1, D_SSD), l),
        pl.BlockSpec((DT_PAD, D_SSD), lambda b, c: (0, 0), pipeline_mode=pl.Buffered(1)),
        pl.BlockSpec(h0_block, h0_map),
    ]
    aliases = {}
    if h_stack is not None:
        aliases = {len(operands): 1}
        operands.append(h_stack)
        in_specs.append(pl.BlockSpec(memory_space=pl.ANY))
    return pl.pallas_call(
        functools.partial(_ssd_body, chunk, n_chunks, out_steps, seqs),
        grid=(batch // seqs, n_chunks),
        in_specs=in_specs,
        out_specs=[
            per_seq(out_steps, D_SSD),
            pl.BlockSpec((None, seqs) + state_tail, lambda b, c: (l, b, 0, 0, 0)),
        ],
        out_shape=[
            jax.ShapeDtypeStruct((batch, out_len, D_SSD), F32),
            jax.ShapeDtypeStruct((depth, batch) + state_tail, F32),
        ],
        scratch_shapes=[pltpu.VMEM((seqs, SSD_STATE, D_SSD), F32)],
        input_output_aliases=aliases,
        compiler_params=_params(2),
        name="ssd_chunk",
    )(*operands)


def _prepare_weights(p):
    depth = p["w_in"].shape[0]

    def vec(a):
        return a.reshape(depth, 1, a.shape[-1])

    def pad_heads(a):
        return jnp.pad(a, ((0, 0), (0, DT_PAD - SSD_HEADS))).reshape(depth, 1, DT_PAD)

    def block_diag(a):
        per = V7X_MXU_DIM // LRU_BLOCK_W
        a = a.reshape(depth, LRU_GATE_GROUPS, per, LRU_BLOCK_W, LRU_BLOCK_W)
        eye = jnp.eye(per, dtype=a.dtype)
        a = a[:, :, :, :, None, :] * eye[None, None, :, None, :, None]
        return a.reshape(depth, LRU_GATE_GROUPS, V7X_MXU_DIM, V7X_MXU_DIM).astype(BF16)

    w_in = p["w_in"].astype(BF16)
    o_xbc = D_SSD
    o_dt = o_xbc + SSD_CONV_DIM
    o_gate = o_dt + SSD_HEADS
    o_xr = o_gate + D_LRU
    w_in = jnp.concatenate([
        w_in[:, :, 0:o_dt], w_in[:, :, o_gate:o_xr], w_in[:, :, o_xr:o_xr + D_LRU],
        w_in[:, :, o_dt:o_gate], jnp.zeros((depth, D_MODEL, DT_PAD - SSD_HEADS), w_in.dtype),
    ], axis=-1)
    w = {
        "w_in": w_in,
        "w_out": p["w_out"].astype(BF16),
        "lru_wa": block_diag(p["lru_wa"]),
        "lru_wx": block_diag(p["lru_wx"]),
        "ssd_dt_bias": pad_heads(p["ssd_dt_bias"]),
        "ssd_a_log": pad_heads(p["ssd_a_log"]),
        "ssd_d_cols": jnp.repeat(p["ssd_d"], SSD_HEAD_DIM, axis=-1).reshape(depth, 1, D_SSD),
        "head_spread": (jnp.arange(DT_PAD)[:, None] == jnp.arange(D_SSD)[None, :] // SSD_HEAD_DIM
                        ).astype(BF16),
    }
    for name in ("ssd_conv_w", "lru_conv_w"):
        w[name] = jnp.repeat(p[name], V7X_SUBLANES, axis=1)
    for name in ("ssd_conv_b", "lru_conv_b"):
        w[name] = jnp.repeat(p[name][:, None, :], V7X_SUBLANES, axis=1)
    for name in ("ffn1_wg", "ffn1_wu", "ffn1_wd", "ffn2_wg", "ffn2_wu", "ffn2_wd"):
        w[name] = p[name].astype(BF16)
    for name in ("ffn1_pre_g", "ffn1_post_g", "mix_pre_g", "mix_post_g", "ffn2_pre_g", "ffn2_post_g",
                 "ssd_norm_g", "lru_ba", "lru_bx", "lru_lambda"):
        w[name] = vec(p[name])
    return w


def _segment_layer(x, l, w, states, ssd_stack, *, bsub, steps, chunk, pad_steps, seqs, n_sub=1):
    (sconv0, sconv0_map), (lconv0, lconv0_map), (h_lru0, h_lru0_map), (h_ssd0, h_ssd0_map) = states
    batch, length, _ = x.shape
    rows = batch * length
    x = _ffn_call(x.reshape(rows, D_MODEL), l, w, "ffn1")
    z, xbc, dt, y_lru, sconv, lconv, h_lru = _in_proj_call(
        x.reshape(batch, length, D_MODEL), l, w, sconv0, sconv0_map, lconv0, lconv0_map,
        h_lru0, h_lru0_map, bsub=bsub, steps=steps, pad_steps=pad_steps, n_sub=n_sub)
    y_ssd, ssd_stack = _ssd_call(xbc, dt, z, l, w, h_ssd0, h_ssd0_map, ssd_stack,
                                 chunk=chunk, out_len=length, seqs=seqs)
    x = _out_ffn_call(x, y_ssd.reshape(rows, D_SSD), y_lru.reshape(rows, D_LRU), l, w)
    return x.reshape(batch, length, D_MODEL), (sconv, h_lru, lconv), ssd_stack


def kernel(x_prompt, x_sample, state_ssd, state_ssd_conv, state_lru, state_lru_conv, meta_tokens,
           ffn1_pre_g, ffn1_post_g, ffn1_wg, ffn1_wu, ffn1_wd, mix_pre_g, mix_post_g, w_in,
           ssd_conv_w, ssd_conv_b, ssd_dt_bias, ssd_a_log, ssd_d, ssd_norm_g, lru_conv_w, lru_conv_b,
           lru_wa, lru_ba, lru_wx, lru_bx, lru_lambda, w_out, ffn2_pre_g, ffn2_post_g, ffn2_wg,
           ffn2_wu, ffn2_wd):
    bp, seq, _ = x_prompt.shape
    bs, dec_seq, _ = x_sample.shape
    depth = w_in.shape[0]
    assert bp == V7X_SUBLANES and seq % SSD_CHUNK == 0 and bs % SAMPLE_BATCH_TILE == 0
    w = _prepare_weights(dict(
        ffn1_pre_g=ffn1_pre_g, ffn1_post_g=ffn1_post_g, ffn1_wg=ffn1_wg, ffn1_wu=ffn1_wu,
        ffn1_wd=ffn1_wd, mix_pre_g=mix_pre_g, mix_post_g=mix_post_g, w_in=w_in,
        ssd_conv_w=ssd_conv_w, ssd_conv_b=ssd_conv_b, ssd_dt_bias=ssd_dt_bias, ssd_a_log=ssd_a_log,
        ssd_d=ssd_d, ssd_norm_g=ssd_norm_g, lru_conv_w=lru_conv_w, lru_conv_b=lru_conv_b,
        lru_wa=lru_wa, lru_ba=lru_ba, lru_wx=lru_wx, lru_bx=lru_bx, lru_lambda=lru_lambda,
        w_out=w_out, ffn2_pre_g=ffn2_pre_g, ffn2_post_g=ffn2_post_g, ffn2_wg=ffn2_wg,
        ffn2_wu=ffn2_wu, ffn2_wd=ffn2_wd))

    xm = jnp.broadcast_to(meta_tokens.astype(F32)[None], (bp, N_META, D_MODEL))
    xp = x_prompt
    xs = x_sample
    sample_steps = ((dec_seq + V7X_SUBLANES - 1) // V7X_SUBLANES) * V7X_SUBLANES
    tail3 = lambda b, t: (b, 0, 0)
    vec2 = lambda b, t: (b, 0)
    seq_state = lambda b, c: (b, 0, 0, 0)
    zero_states = ((jnp.zeros((bp, CONV_TAIL, SSD_CONV_DIM), F32), tail3),
                   (jnp.zeros((bp, CONV_TAIL, D_LRU), F32), tail3),
                   (jnp.zeros((bp, D_LRU), F32), vec2),
                   (jnp.zeros((bp, SSD_HEADS, SSD_HEAD_DIM, SSD_STATE), F32), seq_state))

    layer_state = lambda b, c, l: (l, b, 0, 0, 0)
    p_out = [[] for _ in range(3)]
    s_out = [[] for _ in range(3)]
    m_ssd = p_ssd = s_ssd = None
    for l in range(depth):
        at_l = functools.partial(layer_state, l=l)
        xm, m_st, m_ssd = _segment_layer(
            xm, l, w, zero_states, m_ssd,
            bsub=bp, steps=N_META, chunk=N_META, pad_steps=0, seqs=PROMPT_SEQS)
        xp, p_st, p_ssd = _segment_layer(
            xp, l, w, ((m_st[0], tail3), (m_st[2], tail3), (m_st[1], vec2), (m_ssd, at_l)), p_ssd,
            bsub=bp, steps=PROMPT_STEPS, chunk=SSD_CHUNK, pad_steps=0, seqs=PROMPT_SEQS,
            n_sub=PROMPT_SUB_TILES)
        xs, s_st, s_ssd = _segment_layer(
            xs, l, w, ((state_ssd_conv, lambda b, t, l=l: (l, b, 0, 0)),
                       (state_lru_conv, lambda b, t, l=l: (l, b, 0, 0)),
                       (state_lru, lambda b, t, l=l: (l, b, 0)),
                       (state_ssd, at_l)), s_ssd,
            bsub=SAMPLE_BATCH_TILE, steps=dec_seq, chunk=sample_steps,
            pad_steps=sample_steps - dec_seq, seqs=SAMPLE_SEQS)
        for acc, st in ((p_out, p_st), (s_out, s_st)):
            for k in range(3):
                acc[k].append(st[k])

    p_conv, p_lru, p_lconv = (jnp.stack(a) for a in p_out)
    s_conv, s_lru, s_lconv = (jnp.stack(a) for a in s_out)
    return (xp, xs, p_ssd, p_conv, p_lru, p_lconv, s_ssd, s_conv, s_lru, s_lconv)
```

```python
import functools
import itertools

import jax
import jax.numpy as jnp
from jax import lax
from jax.experimental import pallas as pl
from jax.experimental.pallas import tpu as pltpu

F32 = jnp.float32
BF16 = jnp.bfloat16

D_MODEL = 1024
D_SSD = 1024
D_LRU = 1024
SSD_HEADS = 16
SSD_HEAD_DIM = 64
SSD_GROUPS = 2
SSD_STATE = 128
HEADS_PER_GROUP = SSD_HEADS // SSD_GROUPS
CONV_W = 4
CONV_TAIL = CONV_W - 1
SSD_CONV_DIM = D_SSD + 2 * SSD_GROUPS * SSD_STATE
LRU_BLOCKS = 16
LRU_BLOCK_W = D_LRU // LRU_BLOCKS
LRU_C = 8.0
EPS = 1e-6
N_META = 16
SSD_CHUNK = 128

V7X_LANES = 128
V7X_SUBLANES = 8
V7X_MXU_DIM = 256
V7X_VMEM_LIMIT_BYTES = 56 * 1024 * 1024

DT_PAD = V7X_LANES
LRU_GATE_GROUPS = D_LRU // V7X_MXU_DIM
COL_Z = 0
COL_XBC = COL_Z + D_SSD
COL_GATE = COL_XBC + SSD_CONV_DIM
COL_XR = COL_GATE + D_LRU
COL_DT = COL_XR + D_LRU
IN_COLS_PAD = COL_DT + DT_PAD
GELU_K = 0.7978845608028654

FFN_ROWS = 512
OUT_FFN_ROWS = 512
PROMPT_STEPS = 64
SAMPLE_BATCH_TILE = 64
PROMPT_SUB_TILES = 2
SCAN_UNROLL_LIMIT = 128
PROMPT_SEQS = 2
SAMPLE_SEQS = 8
ELEMENTWISE_ROWS = 16
ELEMENTWISE_COLS = 512


def _rms(x, g):
    return x * lax.rsqrt(jnp.mean(x * x, axis=-1, keepdims=True) + EPS) * g


def _silu(x):
    return x * _sigmoid(x)


def _sigmoid(x):
    return 0.5 * jnp.tanh(0.5 * x) + 0.5


def _softplus(x):
    return jnp.maximum(x, 0.0) + jnp.log1p(jnp.exp(-jnp.abs(x)))


def _gelu_tanh(x):
    return 0.5 * x * (1.0 + jnp.tanh(GELU_K * (x + 0.044715 * (x * x * x))))


def _dot(a, b):
    return jnp.dot(a, b, preferred_element_type=F32)


def _layer_spec(tail, l):
    zeros = (0,) * len(tail)
    return pl.BlockSpec((None,) + tuple(tail), lambda *_: (l,) + zeros, pipeline_mode=pl.Buffered(1))


def _params(n_axes):
    return pltpu.CompilerParams(dimension_semantics=("arbitrary",) * n_axes,
                                vmem_limit_bytes=V7X_VMEM_LIMIT_BYTES)


def _row_tile(rows, want):
    tm = min(rows, want)
    assert rows % tm == 0
    return tm


def _rows_map(i):
    return (i, 0)


def _ffn_math(x, gpre, gpost, wg_ref, wu_ref, wd_ref):
    xn = _rms(x, gpre).astype(BF16)
    hg = _dot(xn, wg_ref[...])
    hu = _dot(xn, wu_ref[...])
    a = (_silu(hg) * hu).astype(BF16)
    y = _dot(a, wd_ref[...])
    return x + 0.5 * _rms(y, gpost)


def _token_call(math, segments, weights, weight_specs, name, tile_rows):
    n_ops = len(segments[0])
    tiles = [_row_tile(seg[0].shape[0], tile_rows) for seg in segments]
    counts = [seg[0].shape[0] // tm for seg, tm in zip(segments, tiles)]
    starts = [sum(counts[:k]) for k in range(len(segments))]

    def seg_spec(k):
        mode = {} if counts[k] > 1 else dict(pipeline_mode=pl.Buffered(1))
        return pl.BlockSpec((tiles[k], D_MODEL),
                            lambda i: (jnp.clip(i - starts[k], 0, counts[k] - 1), 0), **mode)

    def body(*refs):
        n_in = n_ops * len(segments)
        x_refs, w_refs, o_refs = refs[:n_in], refs[n_in:n_in + len(weights)], refs[n_in + len(weights):]
        i = pl.program_id(0)

        def run(k):
            operands = [r[...] for r in x_refs[k * n_ops:(k + 1) * n_ops]]
            o_refs[k][...] = math(*operands, *w_refs)

        for k in range(len(segments)):
            pl.when((i >= starts[k]) & (i < starts[k] + counts[k]))(functools.partial(run, k))

    return pl.pallas_call(
        body,
        grid=(sum(counts),),
        in_specs=[seg_spec(k) for k in range(len(segments)) for _ in range(n_ops)] + weight_specs,
        out_specs=[seg_spec(k) for k in range(len(segments))],
        out_shape=[jax.ShapeDtypeStruct(seg[0].shape, F32) for seg in segments],
        compiler_params=_params(1),
        name=name,
    )(*[a for seg in segments for a in seg], *weights)


def _ffn_math_refs(x, gpre_ref, gpost_ref, wg_ref, wu_ref, wd_ref):
    return _ffn_math(x, gpre_ref[...], gpost_ref[...], wg_ref, wu_ref, wd_ref)


def _ffn_call(xs, l, w, prefix):
    d_ff = w[prefix + "_wg"].shape[-1]
    return _token_call(
        _ffn_math_refs, [(x,) for x in xs],
        [w[prefix + "_pre_g"], w[prefix + "_post_g"], w[prefix + "_wg"], w[prefix + "_wu"],
         w[prefix + "_wd"]],
        [_layer_spec((1, D_MODEL), l), _layer_spec((1, D_MODEL), l),
         _layer_spec((D_MODEL, d_ff), l), _layer_spec((D_MODEL, d_ff), l),
         _layer_spec((d_ff, D_MODEL), l)],
        prefix, FFN_ROWS)


def _out_ffn_math(x, ys, yl, wos_ref, wol_ref, gmix_ref, gpre_ref, gpost_ref, wg_ref, wu_ref, wd_ref):
    m = _dot(ys.astype(BF16), wos_ref[...]) + _dot(yl.astype(BF16), wol_ref[...])
    x1 = x + _rms(m, gmix_ref[...])
    return _ffn_math(x1, gpre_ref[...], gpost_ref[...], wg_ref, wu_ref, wd_ref)


def _out_ffn_call(segments, l, w):
    d_ff = w["ffn2_wg"].shape[-1]
    return _token_call(
        _out_ffn_math, segments,
        [w["w_out"], w["w_out"], w["mix_post_g"], w["ffn2_pre_g"], w["ffn2_post_g"],
         w["ffn2_wg"], w["ffn2_wu"], w["ffn2_wd"]],
        [pl.BlockSpec((None, D_SSD, D_MODEL), lambda i: (l, 0, 0), pipeline_mode=pl.Buffered(1)),
         pl.BlockSpec((None, D_LRU, D_MODEL), lambda i: (l, D_SSD // D_LRU, 0),
                      pipeline_mode=pl.Buffered(1)),
         _layer_spec((1, D_MODEL), l), _layer_spec((1, D_MODEL), l), _layer_spec((1, D_MODEL), l),
         _layer_spec((D_MODEL, d_ff), l), _layer_spec((D_MODEL, d_ff), l),
         _layer_spec((d_ff, D_MODEL), l)],
        "out_ffn2", OUT_FFN_ROWS)


def _in_proj_body(bsub, steps, pad_steps, n_sub,
                  x_ref, g_ref, w_ref, scw_ref, scb_ref, lcw_ref, lcb_ref, dtb_ref,
                  wa_ref, wx_ref, ba_ref, bx_ref, lam_ref, sconv0_ref, lconv0_ref, h0_ref,
                  z_ref, xbc_ref, dt_ref, ylru_ref, sconv_ref, lconv_ref, hout_ref,
                  stage, xbuf, sbuf, lbuf, gate_buf, ra_buf, rx_buf, h_carry):
    rows = steps * bsub
    tail = CONV_TAIL * bsub
    sub_steps = steps // n_sub
    sub_rows = sub_steps * bsub
    rc = min(sub_rows, ELEMENTWISE_ROWS)
    strided = sub_steps >= bsub

    def lane_tile(j):
        return slice(j * V7X_LANES, (j + 1) * V7X_LANES)

    def load_block(ref, t0):
        r0 = t0 * bsub
        if not strided:
            for t in range(sub_steps):
                xbuf[r0 + t * bsub:r0 + (t + 1) * bsub, :] = ref[:, t0 + t, :]
            return xbuf[r0:r0 + sub_rows, :]
        n_tiles = ref.shape[2] // V7X_LANES
        for j in range(n_tiles):
            for b in range(bsub):
                stage[j, pl.ds(r0 + b, sub_steps, stride=bsub), :] = ref[b, t0:t0 + sub_steps, lane_tile(j)]
        return jnp.concatenate([stage[j, r0:r0 + sub_rows, :] for j in range(n_tiles)], axis=1)

    def store_block(ref, t0, read):
        r0 = t0 * bsub
        if not strided:
            value = read(slice(0, ref.shape[2]))
            for t in range(sub_steps):
                ref[:, t0 + t, :] = value[t * bsub:(t + 1) * bsub, :]
            return
        n_tiles = ref.shape[2] // V7X_LANES
        for j in range(n_tiles):
            stage[j, r0:r0 + sub_rows, :] = read(lane_tile(j))
        for j in range(n_tiles):
            for b in range(bsub):
                ref[b, t0:t0 + sub_steps, lane_tile(j)] = stage[j, pl.ds(r0 + b, sub_steps, stride=bsub), :]

    def conv_in_place(buf, w_ref, b_ref, act, r_lo):
        sub = V7X_SUBLANES
        for c0 in range(0, buf.shape[1], ELEMENTWISE_COLS):
            cols = slice(c0, c0 + ELEMENTWISE_COLS)
            taps = [w_ref[k * sub:(k + 1) * sub, cols] for k in range(CONV_W)]
            bias = b_ref[:, cols]
            for r0 in range(r_lo, r_lo + sub_rows, sub):
                acc = bias + buf[r0:r0 + sub, cols] * taps[0]
                for k in range(1, CONV_W):
                    acc = acc + buf[r0 + k * bsub:r0 + k * bsub + sub, cols] * taps[k]
                buf[r0:r0 + sub, cols] = act(acc)

    @pl.when(pl.program_id(1) == 0)
    def _():
        for k in range(CONV_TAIL):
            sbuf[k * bsub:(k + 1) * bsub, :] = sconv0_ref[:, k, :]
            lbuf[k * bsub:(k + 1) * bsub, :] = lconv0_ref[:, k, :]
        h_carry[...] = h0_ref[...]

    neg_c_softplus = -LRU_C * _softplus(-lam_ref[...])

    def sub_tile(i):
        t0 = i * sub_steps
        r_lo = t0 * bsub
        out = slice(r_lo, r_lo + sub_rows)
        pre = slice(tail + r_lo, tail + r_lo + sub_rows)
        xn = _rms(load_block(x_ref, t0), g_ref[...]).astype(BF16)
        yield
        lbuf[pre, :] = _dot(xn, w_ref[:, COL_XR:COL_DT])
        gate_buf[out, :] = _dot(xn, w_ref[:, COL_GATE:COL_XR])
        yield
        conv_in_place(lbuf, lcw_ref, lcb_ref, lambda v: v, r_lo)
        yield
        for q in range(LRU_GATE_GROUPS):
            cols = slice(q * V7X_MXU_DIM, (q + 1) * V7X_MXU_DIM)
            xr_bf = lbuf[out, cols].astype(BF16)
            ra_buf[out, cols] = _dot(xr_bf, wa_ref[q])
            rx_buf[out, cols] = _dot(xr_bf, wx_ref[q])
        yield
        for c0 in range(0, D_LRU, ELEMENTWISE_COLS):
            cols = slice(c0, c0 + ELEMENTWISE_COLS)
            ncs, ba, bx = neg_c_softplus[:, cols], ba_ref[:, cols], bx_ref[:, cols]
            for r0 in range(r_lo, r_lo + sub_rows, rc):
                sl = slice(r0, r0 + rc)
                log_a = ncs * _sigmoid(ra_buf[sl, cols] + ba)
                a = jnp.exp(log_a)
                m = -jnp.tanh(log_a) * (a * a + 1.0)
                mult = jnp.where(m > 0.0, m * lax.rsqrt(m), 0.0)
                rx_buf[sl, cols] = mult * _sigmoid(rx_buf[sl, cols] + bx) * lbuf[sl, cols]
                ra_buf[sl, cols] = a
                gate_buf[sl, cols] = _gelu_tanh(gate_buf[sl, cols])
        yield
        z = _dot(xn, w_ref[:, COL_Z:COL_XBC])
        sbuf[pre, :] = _dot(xn, w_ref[:, COL_XBC:COL_GATE])
        dt = _softplus(_dot(xn, w_ref[:, COL_DT:IN_COLS_PAD]) + dtb_ref[...])
        yield
        store_block(z_ref, t0, lambda cols: z[:, cols])
        store_block(dt_ref, t0, lambda cols: dt[:, cols])
        conv_in_place(sbuf, scw_ref, scb_ref, _silu, r_lo)
        yield
        store_block(xbc_ref, t0, lambda cols: sbuf[out, cols])

    waiting = [sub_tile(i) for i in range(n_sub)]
    running = []
    while waiting or running:
        if waiting:
            running.append(waiting.pop(0))
        for stages in list(running):
            if next(stages, "done") == "done":
                running.remove(stages)

    for k in range(CONV_TAIL):
        sconv_ref[:, k, :] = sbuf[rows + k * bsub:rows + (k + 1) * bsub, :]
        lconv_ref[:, k, :] = lbuf[rows + k * bsub:rows + (k + 1) * bsub, :]

    def scan_step(b0, t, h):
        start = t * bsub + b0
        if not isinstance(start, int):
            start = pl.multiple_of(start, V7X_SUBLANES)
        sl = pl.ds(start, V7X_SUBLANES)
        h = ra_buf[sl, :] * h + rx_buf[sl, :]
        gate_buf[sl, :] = h * gate_buf[sl, :]
        return h

    n_groups = bsub // V7X_SUBLANES
    if n_groups * steps <= SCAN_UNROLL_LIMIT:
        for bg in range(n_groups):
            hsl = slice(bg * V7X_SUBLANES, (bg + 1) * V7X_SUBLANES)
            h = h_carry[hsl, :]
            for t in range(steps):
                h = scan_step(bg * V7X_SUBLANES, t, h)
            h_carry[hsl, :] = h
    else:
        def group_scan(bg, carry):
            b0 = pl.multiple_of(bg * V7X_SUBLANES, V7X_SUBLANES)
            hsl = pl.ds(b0, V7X_SUBLANES)
            h_carry[hsl, :] = lax.fori_loop(0, steps, functools.partial(scan_step, b0), h_carry[hsl, :])
            return carry

        lax.fori_loop(0, n_groups, group_scan, 0)
    for i in range(n_sub):
        store_block(ylru_ref, i * sub_steps,
                    lambda cols, i=i: gate_buf[i * sub_rows:(i + 1) * sub_rows, cols])
    hout_ref[...] = h_carry[...]
    sbuf[0:tail, :] = sbuf[rows:rows + tail, :]
    lbuf[0:tail, :] = lbuf[rows:rows + tail, :]
    for t in range(steps, steps + pad_steps):
        for ref in (z_ref, xbc_ref, dt_ref):
            ref[:, t, :] = jnp.zeros((bsub, ref.shape[2]), F32)


def _in_proj_call(x, l, w, sconv0, sconv0_map, lconv0, lconv0_map, h0, h0_map, *,
                  bsub, steps, pad_steps=0, n_sub=1):
    batch, length, _ = x.shape
    n_t = length // steps
    n_b = batch // bsub
    assert length == n_t * steps and batch == n_b * bsub and bsub % V7X_SUBLANES == 0
    assert pad_steps == 0 or n_t == 1
    assert steps >= CONV_TAIL
    assert steps % n_sub == 0
    strided = steps // n_sub >= bsub
    out_steps = steps + pad_steps
    rows = steps * bsub
    tail = CONV_TAIL * bsub

    def tile(n_steps, width):
        return pl.BlockSpec((bsub, n_steps, width), lambda b, t: (b, t, 0))

    def squeeze_lead(a, block):
        return (None,) * (a.ndim - len(block)) + block

    def conv_out(width):
        return pl.BlockSpec((bsub, CONV_TAIL, width), lambda b, t: (b, 0, 0))

    return pl.pallas_call(
        functools.partial(_in_proj_body, bsub, steps, pad_steps, n_sub),
        grid=(n_b, n_t),
        in_specs=[
            tile(steps, D_MODEL),
            _layer_spec((1, D_MODEL), l),
            _layer_spec((D_MODEL, IN_COLS_PAD), l),
            _layer_spec((CONV_W * V7X_SUBLANES, SSD_CONV_DIM), l),
            _layer_spec((V7X_SUBLANES, SSD_CONV_DIM), l),
            _layer_spec((CONV_W * V7X_SUBLANES, D_LRU), l), _layer_spec((V7X_SUBLANES, D_LRU), l),
            _layer_spec((1, DT_PAD), l),
            _layer_spec((LRU_GATE_GROUPS, V7X_MXU_DIM, V7X_MXU_DIM), l),
            _layer_spec((LRU_GATE_GROUPS, V7X_MXU_DIM, V7X_MXU_DIM), l),
            _layer_spec((1, D_LRU), l), _layer_spec((1, D_LRU), l), _layer_spec((1, D_LRU), l),
            pl.BlockSpec(squeeze_lead(sconv0, (bsub, CONV_TAIL, SSD_CONV_DIM)), sconv0_map),
            pl.BlockSpec(squeeze_lead(lconv0, (bsub, CONV_TAIL, D_LRU)), lconv0_map),
            pl.BlockSpec(squeeze_lead(h0, (bsub, D_LRU)), h0_map),
        ],
        out_specs=[
            tile(out_steps, D_SSD), tile(out_steps, SSD_CONV_DIM), tile(out_steps, DT_PAD),
            tile(steps, D_LRU),
            conv_out(SSD_CONV_DIM), conv_out(D_LRU),
            pl.BlockSpec((bsub, D_LRU), lambda b, t: (b, 0)),
        ],
        out_shape=[
            jax.ShapeDtypeStruct((batch, n_t * out_steps, D_SSD), F32),
            jax.ShapeDtypeStruct((batch, n_t * out_steps, SSD_CONV_DIM), F32),
            jax.ShapeDtypeStruct((batch, n_t * out_steps, DT_PAD), F32),
            jax.ShapeDtypeStruct((batch, length, D_LRU), F32),
            jax.ShapeDtypeStruct((batch, CONV_TAIL, SSD_CONV_DIM), F32),
            jax.ShapeDtypeStruct((batch, CONV_TAIL, D_LRU), F32),
            jax.ShapeDtypeStruct((batch, D_LRU), F32),
        ],
        scratch_shapes=[
            pltpu.VMEM((SSD_CONV_DIM // V7X_LANES, rows, V7X_LANES) if strided
                       else (1, V7X_SUBLANES, V7X_LANES), F32),
            pltpu.VMEM((V7X_SUBLANES, V7X_LANES) if strided else (rows, D_MODEL), F32),
            pltpu.VMEM((tail + rows, SSD_CONV_DIM), F32),
            pltpu.VMEM((tail + rows, D_LRU), F32),
            pltpu.VMEM((rows, D_LRU), F32),
            pltpu.VMEM((rows, D_LRU), F32),
            pltpu.VMEM((rows, D_LRU), F32),
            pltpu.VMEM((bsub, D_LRU), F32),
        ],
        compiler_params=_params(2),
        name="in_proj_lru",
    )(x, w["mix_pre_g"], w["w_in"], w["ssd_conv_w"], w["ssd_conv_b"], w["lru_conv_w"], w["lru_conv_b"],
      w["ssd_dt_bias"], w["lru_wa"], w["lru_wx"], w["lru_ba"], w["lru_bx"], w["lru_lambda"],
      sconv0, lconv0, h0)


def _transpose_rows(x, rows):
    lanes = x.shape[1]
    if rows < lanes:
        x = jnp.concatenate([x, jnp.zeros((lanes - rows, lanes), x.dtype)], axis=0)
    return x.T[:, 0:rows]


def _split_bf16(x, terms):
    parts = []
    for _ in range(terms - 1):
        parts.append(x.astype(BF16))
        x = x - parts[-1].astype(F32)
    return parts + [x.astype(BF16)]


def _ssd_body(chunk, n_chunks, out_steps, seqs, first_layer, xbc_ref, dt_ref, z_ref, alog_ref, dvec_ref,
              g_ref, e_ref, h0_ref, *rest):
    (y_ref, hout_ref, ht_s) = rest[-3:]
    if first_layer is not None:
        @pl.when(pl.program_id(1) == 0)
        def _():
            for k in range(hout_ref.shape[0]):
                if k != first_layer:
                    hout_ref[k] = jnp.zeros(hout_ref.shape[1:], F32)
        hout_ref = hout_ref.at[first_layer]
    stages = [_ssd_sequence(chunk, n_chunks, out_steps, xbc_ref.at[s], dt_ref.at[s], z_ref.at[s],
                            alog_ref, dvec_ref, g_ref, e_ref, h0_ref.at[s], y_ref.at[s],
                            hout_ref.at[s], ht_s.at[s]) for s in range(seqs)]
    for _ in itertools.zip_longest(*stages):
        pass


def _ssd_sequence(chunk, n_chunks, out_steps, xbc_ref, dt_ref, z_ref, alog_ref, dvec_ref, g_ref, e_ref,
                  h0_ref, y_ref, hout_ref, ht_s):
    c = pl.program_id(1)
    group_cols = HEADS_PER_GROUP * SSD_HEAD_DIM

    single = n_chunks == 1
    if single:
        h_given = h0_ref[...].reshape(D_SSD, SSD_STATE)
        ht_bf = h_given.astype(BF16).T
    else:
        @pl.when(c == 0)
        def _():
            ht_s[...] = h0_ref[...].reshape(D_SSD, SSD_STATE).T

    dt = dt_ref[...]
    a = -jnp.exp(alog_ref[...])
    row = lax.broadcasted_iota(jnp.int32, (chunk, chunk), 0)
    col = lax.broadcasted_iota(jnp.int32, (chunk, chunk), 1)
    causal = row >= col
    ones_lower = jnp.where(causal, 1.0, 0.0).astype(BF16)
    da_hi, da_mid, da_lo = _split_bf16(dt * a, 3)
    cum = _dot(ones_lower, da_hi) + _dot(ones_lower, da_mid) + _dot(ones_lower, da_lo)
    yield
    cum_t = _transpose_rows(cum, chunk)
    dt_t = _transpose_rows(dt, chunk)
    tot = cum[chunk - 1:chunk, :]
    cdec = jnp.exp(tot)
    factors = jnp.concatenate([jnp.exp(cum), jnp.exp(tot - cum) * dt], axis=0)
    spread = _dot(factors.astype(BF16), e_ref[...])
    yield
    ecum_x = spread[0:chunk, :]
    wend_x = spread[chunk:2 * chunk, :]
    if not single:
        cdec_parts = _split_bf16(jnp.broadcast_to(cdec, (V7X_SUBLANES, DT_PAD)), 3)
        cdec_x = (_dot(cdec_parts[0], e_ref[...]) + _dot(cdec_parts[1], e_ref[...])
                  + _dot(cdec_parts[2], e_ref[...]))[0:1, :]
    low_half = lax.broadcasted_iota(jnp.int32, (chunk, V7X_LANES), 1) < SSD_HEAD_DIM

    y_groups = []
    for g in range(SSD_GROUPS):
        b0 = D_SSD + g * SSD_STATE
        c0 = D_SSD + SSD_GROUPS * SSD_STATE + g * SSD_STATE
        gcols = slice(g * group_cols, (g + 1) * group_cols)
        bg_t = _transpose_rows(xbc_ref[:, b0:b0 + SSD_STATE], chunk).astype(BF16)
        cg = xbc_ref[:, c0:c0 + SSD_STATE].astype(BF16)
        cb = _dot(cg, bg_t)
        yield
        y_pairs = []
        for pair in range(HEADS_PER_GROUP // 2):
            h = g * HEADS_PER_GROUP + 2 * pair
            x_pair = xbc_ref[:, h * SSD_HEAD_DIM:(h + 2) * SSD_HEAD_DIM]
            scores = []
            for hh in (h, h + 1):
                diff = cum[:, hh:hh + 1] - cum_t[hh:hh + 1, :]
                decay = jnp.exp(jnp.where(causal, diff, -jnp.inf))
                scores.append((cb * decay * dt_t[hh:hh + 1, :]).astype(BF16))
            x_lo = jnp.where(low_half, x_pair, 0.0).astype(BF16)
            x_hi = jnp.where(low_half, 0.0, x_pair).astype(BF16)
            if chunk % V7X_LANES == 0:
                y_pairs.append(_dot(jnp.concatenate(scores, axis=1),
                                    jnp.concatenate([x_lo, x_hi], axis=0)))
            else:
                y_pairs.append(_dot(scores[0], x_lo) + _dot(scores[1], x_hi))
            yield
        xg = xbc_ref[:, gcols]
        xw = (xg * wend_x[:, gcols]).astype(BF16)
        if single:
            y_off = _dot(cg, ht_bf[:, gcols]) * ecum_x[:, gcols]
            bg = xbc_ref[:, b0:b0 + SSD_STATE].astype(BF16)
            update = lax.dot_general(xw, bg, (((0,), (0,)), ((), ())), preferred_element_type=F32)
            heads = range(g * HEADS_PER_GROUP, (g + 1) * HEADS_PER_GROUP)
            decay_rows = jnp.concatenate(
                [jnp.broadcast_to(cdec[:, h:h + 1], (SSD_HEAD_DIM, SSD_STATE)) for h in heads], axis=0)
            h_new = decay_rows * h_given[gcols, :] + update
            hout_ref[g * HEADS_PER_GROUP:(g + 1) * HEADS_PER_GROUP] = h_new.reshape(
                HEADS_PER_GROUP, SSD_HEAD_DIM, SSD_STATE)
        else:
            ht_prev = ht_s[:, gcols]
            y_off = _dot(cg, ht_prev.astype(BF16)) * ecum_x[:, gcols]
            ht_s[:, gcols] = cdec_x[:, gcols] * ht_prev + _dot(bg_t, xw)
        y_groups.append(jnp.concatenate(y_pairs, axis=1) + y_off + dvec_ref[:, gcols] * xg)
        yield

    y = _rms(jnp.concatenate(y_groups, axis=1) * _silu(z_ref[...]), g_ref[...])
    y_ref[...] = y[0:out_steps, :]

    if not single:
        @pl.when(c == n_chunks - 1)
        def _():
            hout_ref[...] = ht_s[...].T.reshape(SSD_HEADS, SSD_HEAD_DIM, SSD_STATE)


def _ssd_call(xbc, dt, z, l, w, h0, h0_map, h_stack, *, chunk, out_len, seqs):
    batch, length, _ = xbc.shape
    depth = w["ssd_a_log"].shape[0]
    n_chunks = length // chunk
    assert length == n_chunks * chunk and (out_len == length or n_chunks == 1) and batch % seqs == 0
    out_steps = min(chunk, out_len)

    def per_seq(n_steps, width):
        return pl.BlockSpec((seqs, n_steps, width), lambda b, c: (b, c, 0))

    state_tail = (SSD_HEADS, SSD_HEAD_DIM, SSD_STATE)
    h0_block = (None,) * (h0.ndim - 4) + (seqs,) + state_tail
    operands = [xbc, dt, z, w["ssd_a_log"], w["ssd_d_cols"], w["ssd_norm_g"], w["head_spread"], h0]
    in_specs = [
        per_seq(chunk, SSD_CONV_DIM), per_seq(chunk, DT_PAD), per_seq(chunk, D_SSD),
        _layer_spec((1, DT_PAD), l), _layer_spec((1, D_SSD), l), _layer_spec((1, D_SSD), l),
        pl.BlockSpec((DT_PAD, D_SSD), lambda b, c: (0, 0), pipeline_mode=pl.Buffered(1)),
        pl.BlockSpec(h0_block, h0_map),
    ]
    aliases = {}
    if h_stack is not None:
        aliases = {len(operands): 1}
        operands.append(h_stack)
        in_specs.append(pl.BlockSpec(memory_space=pl.ANY))
    first = h_stack is None
    return pl.pallas_call(
        functools.partial(_ssd_body, chunk, n_chunks, out_steps, seqs, l if first else None),
        grid=(batch // seqs, n_chunks),
        in_specs=in_specs,
        out_specs=[
            per_seq(out_steps, D_SSD),
            pl.BlockSpec((depth, seqs) + state_tail, lambda b, c: (0, b, 0, 0, 0)) if first
            else pl.BlockSpec((None, seqs) + state_tail, lambda b, c: (l, b, 0, 0, 0)),
        ],
        out_shape=[
            jax.ShapeDtypeStruct((batch, out_len, D_SSD), F32),
            jax.ShapeDtypeStruct((depth, batch) + state_tail, F32),
        ],
        scratch_shapes=[pltpu.VMEM((seqs, SSD_STATE, D_SSD), F32)],
        input_output_aliases=aliases,
        compiler_params=_params(2),
        name="ssd_chunk",
    )(*operands)


def _prepare_weights(p):
    depth = p["w_in"].shape[0]

    def vec(a):
        return a.reshape(depth, 1, a.shape[-1])

    def pad_heads(a):
        return jnp.pad(a, ((0, 0), (0, DT_PAD - SSD_HEADS))).reshape(depth, 1, DT_PAD)

    def block_diag(a):
        per = V7X_MXU_DIM // LRU_BLOCK_W
        a = a.reshape(depth, LRU_GATE_GROUPS, per, LRU_BLOCK_W, LRU_BLOCK_W)
        eye = jnp.eye(per, dtype=a.dtype)
        a = a[:, :, :, :, None, :] * eye[None, None, :, None, :, None]
        return a.reshape(depth, LRU_GATE_GROUPS, V7X_MXU_DIM, V7X_MXU_DIM).astype(BF16)

    w_in = p["w_in"].astype(BF16)
    o_xbc = D_SSD
    o_dt = o_xbc + SSD_CONV_DIM
    o_gate = o_dt + SSD_HEADS
    o_xr = o_gate + D_LRU
    w_in = jnp.concatenate([
        w_in[:, :, 0:o_dt], w_in[:, :, o_gate:o_xr], w_in[:, :, o_xr:o_xr + D_LRU],
        w_in[:, :, o_dt:o_gate], jnp.zeros((depth, D_MODEL, DT_PAD - SSD_HEADS), w_in.dtype),
    ], axis=-1)
    w = {
        "w_in": w_in,
        "w_out": p["w_out"].astype(BF16),
        "lru_wa": block_diag(p["lru_wa"]),
        "lru_wx": block_diag(p["lru_wx"]),
        "ssd_dt_bias": pad_heads(p["ssd_dt_bias"]),
        "ssd_a_log": pad_heads(p["ssd_a_log"]),
        "ssd_d_cols": jnp.repeat(p["ssd_d"], SSD_HEAD_DIM, axis=-1).reshape(depth, 1, D_SSD),
        "head_spread": (jnp.arange(DT_PAD)[:, None] == jnp.arange(D_SSD)[None, :] // SSD_HEAD_DIM
                        ).astype(BF16),
    }
    for name in ("ssd_conv_w", "lru_conv_w"):
        w[name] = jnp.repeat(p[name], V7X_SUBLANES, axis=1)
    for name in ("ssd_conv_b", "lru_conv_b"):
        w[name] = jnp.repeat(p[name][:, None, :], V7X_SUBLANES, axis=1)
    for name in ("ffn1_wg", "ffn1_wu", "ffn1_wd", "ffn2_wg", "ffn2_wu", "ffn2_wd"):
        w[name] = p[name].astype(BF16)
    for name in ("ffn1_pre_g", "ffn1_post_g", "mix_pre_g", "mix_post_g", "ffn2_pre_g", "ffn2_post_g",
                 "ssd_norm_g", "lru_ba", "lru_bx", "lru_lambda"):
        w[name] = vec(p[name])
    return w


def _segment_mixer(x, l, w, states, ssd_stack, *, bsub, steps, chunk, pad_steps, seqs, n_sub=1):
    (sconv0, sconv0_map), (lconv0, lconv0_map), (h_lru0, h_lru0_map), (h_ssd0, h_ssd0_map) = states
    length = x.shape[1]
    z, xbc, dt, y_lru, sconv, lconv, h_lru = _in_proj_call(
        x, l, w, sconv0, sconv0_map, lconv0, lconv0_map,
        h_lru0, h_lru0_map, bsub=bsub, steps=steps, pad_steps=pad_steps, n_sub=n_sub)
    y_ssd, ssd_stack = _ssd_call(xbc, dt, z, l, w, h_ssd0, h_ssd0_map, ssd_stack,
                                 chunk=chunk, out_len=length, seqs=seqs)
    return y_ssd, y_lru, (sconv, h_lru, lconv), ssd_stack


def kernel(x_prompt, x_sample, state_ssd, state_ssd_conv, state_lru, state_lru_conv, meta_tokens,
           ffn1_pre_g, ffn1_post_g, ffn1_wg, ffn1_wu, ffn1_wd, mix_pre_g, mix_post_g, w_in,
           ssd_conv_w, ssd_conv_b, ssd_dt_bias, ssd_a_log, ssd_d, ssd_norm_g, lru_conv_w, lru_conv_b,
           lru_wa, lru_ba, lru_wx, lru_bx, lru_lambda, w_out, ffn2_pre_g, ffn2_post_g, ffn2_wg,
           ffn2_wu, ffn2_wd):
    bp, seq, _ = x_prompt.shape
    bs, dec_seq, _ = x_sample.shape
    depth = w_in.shape[0]
    assert bp == V7X_SUBLANES and seq % SSD_CHUNK == 0 and bs % SAMPLE_BATCH_TILE == 0
    w = _prepare_weights(dict(
        ffn1_pre_g=ffn1_pre_g, ffn1_post_g=ffn1_post_g, ffn1_wg=ffn1_wg, ffn1_wu=ffn1_wu,
        ffn1_wd=ffn1_wd, mix_pre_g=mix_pre_g, mix_post_g=mix_post_g, w_in=w_in,
        ssd_conv_w=ssd_conv_w, ssd_conv_b=ssd_conv_b, ssd_dt_bias=ssd_dt_bias, ssd_a_log=ssd_a_log,
        ssd_d=ssd_d, ssd_norm_g=ssd_norm_g, lru_conv_w=lru_conv_w, lru_conv_b=lru_conv_b,
        lru_wa=lru_wa, lru_ba=lru_ba, lru_wx=lru_wx, lru_bx=lru_bx, lru_lambda=lru_lambda,
        w_out=w_out, ffn2_pre_g=ffn2_pre_g, ffn2_post_g=ffn2_post_g, ffn2_wg=ffn2_wg,
        ffn2_wu=ffn2_wu, ffn2_wd=ffn2_wd))

    xm = jnp.broadcast_to(meta_tokens.astype(F32)[None], (bp, N_META, D_MODEL))
    xp = x_prompt
    xs = x_sample
    sample_steps = ((dec_seq + V7X_SUBLANES - 1) // V7X_SUBLANES) * V7X_SUBLANES
    tail3 = lambda b, t: (b, 0, 0)
    vec2 = lambda b, t: (b, 0)
    seq_state = lambda b, c: (b, 0, 0, 0)
    zero_states = ((jnp.zeros((bp, CONV_TAIL, SSD_CONV_DIM), F32), tail3),
                   (jnp.zeros((bp, CONV_TAIL, D_LRU), F32), tail3),
                   (jnp.zeros((bp, D_LRU), F32), vec2),
                   (jnp.zeros((bp, SSD_HEADS, SSD_HEAD_DIM, SSD_STATE), F32), seq_state))

    layer_state = lambda b, c, l: (l, b, 0, 0, 0)
    p_out = [[] for _ in range(3)]
    s_out = [[] for _ in range(3)]
    m_ssd = p_ssd = s_ssd = None
    def flat(a):
        return a.reshape(a.shape[0] * a.shape[1], a.shape[2])

    for l in range(depth):
        at_l = functools.partial(layer_state, l=l)
        shapes = (xp.shape, xs.shape, xm.shape)
        xp, xs, xm = (a.reshape(s) for a, s in
                      zip(_ffn_call([flat(xp), flat(xs), flat(xm)], l, w, "ffn1"), shapes))
        m_ssd_y, m_lru_y, m_st, m_ssd = _segment_mixer(
            xm, l, w, zero_states, m_ssd,
            bsub=bp, steps=N_META, chunk=N_META, pad_steps=0, seqs=PROMPT_SEQS)
        p_ssd_y, p_lru_y, p_st, p_ssd = _segment_mixer(
            xp, l, w, ((m_st[0], tail3), (m_st[2], tail3), (m_st[1], vec2), (m_ssd, at_l)), p_ssd,
            bsub=bp, steps=PROMPT_STEPS, chunk=SSD_CHUNK, pad_steps=0, seqs=PROMPT_SEQS,
            n_sub=PROMPT_SUB_TILES)
        s_ssd_y, s_lru_y, s_st, s_ssd = _segment_mixer(
            xs, l, w, ((state_ssd_conv, lambda b, t, l=l: (l, b, 0, 0)),
                       (state_lru_conv, lambda b, t, l=l: (l, b, 0, 0)),
                       (state_lru, lambda b, t, l=l: (l, b, 0)),
                       (state_ssd, at_l)), s_ssd,
            bsub=SAMPLE_BATCH_TILE, steps=dec_seq, chunk=sample_steps,
            pad_steps=sample_steps - dec_seq, seqs=SAMPLE_SEQS)
        (xp,) = _out_ffn_call([(flat(xp), flat(p_ssd_y), flat(p_lru_y))], l, w)
        xs, xm = _out_ffn_call([(flat(xs), flat(s_ssd_y), flat(s_lru_y)),
                                (flat(xm), flat(m_ssd_y), flat(m_lru_y))], l, w)
        xp, xs, xm = (a.reshape(s) for a, s in zip((xp, xs, xm), shapes))
        for acc, st in ((p_out, p_st), (s_out, s_st)):
            for k in range(3):
                acc[k].append(st[k])

    p_conv, p_lru, p_lconv = (jnp.stack(a) for a in p_out)
    s_conv, s_lru, s_lconv = (jnp.stack(a) for a in s_out)
    return (xp, xs, p_ssd, p_conv, p_lru, p_lconv, s_ssd, s_conv, s_lru, s_lconv)
```

```python
import functools
import itertools

import jax
import jax.numpy as jnp
from jax import lax
from jax.experimental import pallas as pl
from jax.experimental.pallas import tpu as pltpu

F32 = jnp.float32
BF16 = jnp.bfloat16

D_MODEL = 1024
D_SSD = 1024
D_LRU = 1024
SSD_HEADS = 16
SSD_HEAD_DIM = 64
SSD_GROUPS = 2
SSD_STATE = 128
HEADS_PER_GROUP = SSD_HEADS // SSD_GROUPS
CONV_W = 4
CONV_TAIL = CONV_W - 1
SSD_CONV_DIM = D_SSD + 2 * SSD_GROUPS * SSD_STATE
LRU_BLOCKS = 16
LRU_BLOCK_W = D_LRU // LRU_BLOCKS
LRU_C = 8.0
EPS = 1e-6
N_META = 16
SSD_CHUNK = 128

V7X_LANES = 128
V7X_SUBLANES = 8
V7X_MXU_DIM = 256
V7X_VMEM_LIMIT_BYTES = 56 * 1024 * 1024

DT_PAD = V7X_LANES
LRU_GATE_GROUPS = D_LRU // V7X_MXU_DIM
COL_Z = 0
COL_XBC = COL_Z + D_SSD
COL_GATE = COL_XBC + SSD_CONV_DIM
COL_XR = COL_GATE + D_LRU
COL_DT = COL_XR + D_LRU
IN_COLS_PAD = COL_DT + DT_PAD
GELU_K = 0.7978845608028654

FFN_ROWS = 512
OUT_FFN_ROWS = 512
PROMPT_STEPS = 64
SAMPLE_BATCH_TILE = 64
PROMPT_SUB_TILES = 2
SCAN_UNROLL_LIMIT = 128
W_IN_PREP_ROWS = 256
PROMPT_SEQS = 4
SAMPLE_SEQS = 8
ELEMENTWISE_ROWS = 16
ELEMENTWISE_COLS = 512


def _rms(x, g):
    return x * lax.rsqrt(jnp.mean(x * x, axis=-1, keepdims=True) + EPS) * g


def _silu(x):
    return x * _sigmoid(x)


def _sigmoid(x):
    return 0.5 * jnp.tanh(0.5 * x) + 0.5


def _softplus(x):
    return jnp.maximum(x, 0.0) + jnp.log1p(jnp.exp(-jnp.abs(x)))


def _gelu_tanh(x):
    return 0.5 * x * (1.0 + jnp.tanh(GELU_K * (x + 0.044715 * (x * x * x))))


def _dot(a, b):
    return jnp.dot(a, b, preferred_element_type=F32)


def _layer_spec(tail, l):
    zeros = (0,) * len(tail)
    return pl.BlockSpec((None,) + tuple(tail), lambda *_: (l,) + zeros, pipeline_mode=pl.Buffered(1))


def _params(n_axes):
    return pltpu.CompilerParams(dimension_semantics=("arbitrary",) * n_axes,
                                vmem_limit_bytes=V7X_VMEM_LIMIT_BYTES)


def _row_tile(rows, want):
    tm = min(rows, want)
    assert rows % tm == 0
    return tm


def _rows_map(i):
    return (i, 0)


def _ffn_math(x, gpre, gpost, wg_ref, wu_ref, wd_ref):
    xn = _rms(x, gpre).astype(BF16)
    hg = _dot(xn, wg_ref[...])
    hu = _dot(xn, wu_ref[...])
    a = (_silu(hg) * hu).astype(BF16)
    y = _dot(a, wd_ref[...])
    return x + 0.5 * _rms(y, gpost)


def _token_call(math, segments, weights, weight_specs, name, tile_rows):
    n_ops = len(segments[0])
    tiles = [_row_tile(seg[0].shape[0], tile_rows) for seg in segments]
    counts = [seg[0].shape[0] // tm for seg, tm in zip(segments, tiles)]
    starts = [sum(counts[:k]) for k in range(len(segments))]

    def seg_spec(k):
        mode = {} if counts[k] > 1 else dict(pipeline_mode=pl.Buffered(1))
        return pl.BlockSpec((tiles[k], D_MODEL),
                            lambda i: (jnp.clip(i - starts[k], 0, counts[k] - 1), 0), **mode)

    def body(*refs):
        n_in = n_ops * len(segments)
        x_refs, w_refs, o_refs = refs[:n_in], refs[n_in:n_in + len(weights)], refs[n_in + len(weights):]
        i = pl.program_id(0)

        def run(k):
            operands = [r[...] for r in x_refs[k * n_ops:(k + 1) * n_ops]]
            o_refs[k][...] = math(*operands, *w_refs)

        for k in range(len(segments)):
            pl.when((i >= starts[k]) & (i < starts[k] + counts[k]))(functools.partial(run, k))

    return pl.pallas_call(
        body,
        grid=(sum(counts),),
        in_specs=[seg_spec(k) for k in range(len(segments)) for _ in range(n_ops)] + weight_specs,
        out_specs=[seg_spec(k) for k in range(len(segments))],
        out_shape=[jax.ShapeDtypeStruct(seg[0].shape, F32) for seg in segments],
        compiler_params=_params(1),
        name=name,
    )(*[a for seg in segments for a in seg], *weights)


def _ffn_math_refs(x, gpre_ref, gpost_ref, wg_ref, wu_ref, wd_ref):
    return _ffn_math(x, gpre_ref[...], gpost_ref[...], wg_ref, wu_ref, wd_ref)


def _ffn_call(xs, l, w, prefix):
    d_ff = w[prefix + "_wg"].shape[-1]
    return _token_call(
        _ffn_math_refs, [(x,) for x in xs],
        [w[prefix + "_pre_g"], w[prefix + "_post_g"], w[prefix + "_wg"], w[prefix + "_wu"],
         w[prefix + "_wd"]],
        [_layer_spec((1, D_MODEL), l), _layer_spec((1, D_MODEL), l),
         _layer_spec((D_MODEL, d_ff), l), _layer_spec((D_MODEL, d_ff), l),
         _layer_spec((d_ff, D_MODEL), l)],
        prefix, FFN_ROWS)


def _out_ffn_math(x, ys, yl, wos_ref, wol_ref, gmix_ref, gpre_ref, gpost_ref, wg_ref, wu_ref, wd_ref):
    m = _dot(ys.astype(BF16), wos_ref[...]) + _dot(yl.astype(BF16), wol_ref[...])
    x1 = x + _rms(m, gmix_ref[...])
    return _ffn_math(x1, gpre_ref[...], gpost_ref[...], wg_ref, wu_ref, wd_ref)


def _out_ffn_call(segments, l, w):
    d_ff = w["ffn2_wg"].shape[-1]
    return _token_call(
        _out_ffn_math, segments,
        [w["w_out"], w["w_out"], w["mix_post_g"], w["ffn2_pre_g"], w["ffn2_post_g"],
         w["ffn2_wg"], w["ffn2_wu"], w["ffn2_wd"]],
        [pl.BlockSpec((None, D_SSD, D_MODEL), lambda i: (l, 0, 0), pipeline_mode=pl.Buffered(1)),
         pl.BlockSpec((None, D_LRU, D_MODEL), lambda i: (l, D_SSD // D_LRU, 0),
                      pipeline_mode=pl.Buffered(1)),
         _layer_spec((1, D_MODEL), l), _layer_spec((1, D_MODEL), l), _layer_spec((1, D_MODEL), l),
         _layer_spec((D_MODEL, d_ff), l), _layer_spec((D_MODEL, d_ff), l),
         _layer_spec((d_ff, D_MODEL), l)],
        "out_ffn2", OUT_FFN_ROWS)


def _in_proj_body(bsub, steps, pad_steps, n_sub,
                  x_ref, g_ref, w_ref, scw_ref, scb_ref, lcw_ref, lcb_ref, dtb_ref,
                  wa_ref, wx_ref, ba_ref, bx_ref, lam_ref, sconv0_ref, lconv0_ref, h0_ref,
                  z_ref, xbc_ref, dt_ref, ylru_ref, sconv_ref, lconv_ref, hout_ref,
                  stage, xbuf, sbuf, lbuf, gate_buf, ra_buf, rx_buf, h_carry):
    rows = steps * bsub
    tail = CONV_TAIL * bsub
    sub_steps = steps // n_sub
    sub_rows = sub_steps * bsub
    rc = min(sub_rows, ELEMENTWISE_ROWS)
    strided = sub_steps >= bsub

    def lane_tile(j):
        return slice(j * V7X_LANES, (j + 1) * V7X_LANES)

    def load_block(ref, t0):
        r0 = t0 * bsub
        if not strided:
            for t in range(sub_steps):
                xbuf[r0 + t * bsub:r0 + (t + 1) * bsub, :] = ref[:, t0 + t, :]
            return xbuf[r0:r0 + sub_rows, :]
        n_tiles = ref.shape[2] // V7X_LANES
        for j in range(n_tiles):
            for b in range(bsub):
                stage[j, pl.ds(r0 + b, sub_steps, stride=bsub), :] = ref[b, t0:t0 + sub_steps, lane_tile(j)]
        return jnp.concatenate([stage[j, r0:r0 + sub_rows, :] for j in range(n_tiles)], axis=1)

    def store_block(ref, t0, read):
        r0 = t0 * bsub
        if not strided:
            value = read(slice(0, ref.shape[2]))
            for t in range(sub_steps):
                ref[:, t0 + t, :] = value[t * bsub:(t + 1) * bsub, :]
            return
        n_tiles = ref.shape[2] // V7X_LANES
        for j in range(n_tiles):
            stage[j, r0:r0 + sub_rows, :] = read(lane_tile(j))
        for j in range(n_tiles):
            for b in range(bsub):
                ref[b, t0:t0 + sub_steps, lane_tile(j)] = stage[j, pl.ds(r0 + b, sub_steps, stride=bsub), :]

    def conv_in_place(buf, w_ref, b_ref, act, r_lo):
        sub = V7X_SUBLANES
        for c0 in range(0, buf.shape[1], ELEMENTWISE_COLS):
            cols = slice(c0, c0 + ELEMENTWISE_COLS)
            taps = [w_ref[k * sub:(k + 1) * sub, cols] for k in range(CONV_W)]
            bias = b_ref[:, cols]
            for r0 in range(r_lo, r_lo + sub_rows, sub):
                acc = bias + buf[r0:r0 + sub, cols] * taps[0]
                for k in range(1, CONV_W):
                    acc = acc + buf[r0 + k * bsub:r0 + k * bsub + sub, cols] * taps[k]
                buf[r0:r0 + sub, cols] = act(acc)

    @pl.when(pl.program_id(1) == 0)
    def _():
        for k in range(CONV_TAIL):
            sbuf[k * bsub:(k + 1) * bsub, :] = sconv0_ref[:, k, :]
            lbuf[k * bsub:(k + 1) * bsub, :] = lconv0_ref[:, k, :]
        h_carry[...] = h0_ref[...]

    neg_c_softplus = -LRU_C * _softplus(-lam_ref[...])

    def sub_tile(i):
        t0 = i * sub_steps
        r_lo = t0 * bsub
        out = slice(r_lo, r_lo + sub_rows)
        pre = slice(tail + r_lo, tail + r_lo + sub_rows)
        xn = _rms(load_block(x_ref, t0), g_ref[...]).astype(BF16)
        yield
        lbuf[pre, :] = _dot(xn, w_ref[:, COL_XR:COL_DT])
        gate_buf[out, :] = _dot(xn, w_ref[:, COL_GATE:COL_XR])
        yield
        conv_in_place(lbuf, lcw_ref, lcb_ref, lambda v: v, r_lo)
        yield
        for q in range(LRU_GATE_GROUPS):
            cols = slice(q * V7X_MXU_DIM, (q + 1) * V7X_MXU_DIM)
            xr_bf = lbuf[out, cols].astype(BF16)
            ra_buf[out, cols] = _dot(xr_bf, wa_ref[q])
            rx_buf[out, cols] = _dot(xr_bf, wx_ref[q])
        yield
        for c0 in range(0, D_LRU, ELEMENTWISE_COLS):
            cols = slice(c0, c0 + ELEMENTWISE_COLS)
            ncs, ba, bx = neg_c_softplus[:, cols], ba_ref[:, cols], bx_ref[:, cols]
            for r0 in range(r_lo, r_lo + sub_rows, rc):
                sl = slice(r0, r0 + rc)
                log_a = ncs * _sigmoid(ra_buf[sl, cols] + ba)
                a = jnp.exp(log_a)
                m = -jnp.tanh(log_a) * (a * a + 1.0)
                mult = jnp.where(m > 0.0, m * lax.rsqrt(m), 0.0)
                rx_buf[sl, cols] = mult * _sigmoid(rx_buf[sl, cols] + bx) * lbuf[sl, cols]
                ra_buf[sl, cols] = a
                gate_buf[sl, cols] = _gelu_tanh(gate_buf[sl, cols])
        yield
        z = _dot(xn, w_ref[:, COL_Z:COL_XBC])
        sbuf[pre, :] = _dot(xn, w_ref[:, COL_XBC:COL_GATE])
        dt = _softplus(_dot(xn, w_ref[:, COL_DT:IN_COLS_PAD]) + dtb_ref[...])
        yield
        store_block(z_ref, t0, lambda cols: z[:, cols])
        store_block(dt_ref, t0, lambda cols: dt[:, cols])
        conv_in_place(sbuf, scw_ref, scb_ref, _silu, r_lo)
        yield
        store_block(xbc_ref, t0, lambda cols: sbuf[out, cols])

    waiting = [sub_tile(i) for i in range(n_sub)]
    running = []
    while waiting or running:
        if waiting:
            running.append(waiting.pop(0))
        for stages in list(running):
            if next(stages, "done") == "done":
                running.remove(stages)

    for k in range(CONV_TAIL):
        sconv_ref[:, k, :] = sbuf[rows + k * bsub:rows + (k + 1) * bsub, :]
        lconv_ref[:, k, :] = lbuf[rows + k * bsub:rows + (k + 1) * bsub, :]

    def scan_step(b0, t, h):
        start = t * bsub + b0
        if not isinstance(start, int):
            start = pl.multiple_of(start, V7X_SUBLANES)
        sl = pl.ds(start, V7X_SUBLANES)
        h = ra_buf[sl, :] * h + rx_buf[sl, :]
        gate_buf[sl, :] = h * gate_buf[sl, :]
        return h

    n_groups = bsub // V7X_SUBLANES
    if n_groups * steps <= SCAN_UNROLL_LIMIT:
        for bg in range(n_groups):
            hsl = slice(bg * V7X_SUBLANES, (bg + 1) * V7X_SUBLANES)
            h = h_carry[hsl, :]
            for t in range(steps):
                h = scan_step(bg * V7X_SUBLANES, t, h)
            h_carry[hsl, :] = h
    else:
        def group_scan(bg, carry):
            b0 = pl.multiple_of(bg * V7X_SUBLANES, V7X_SUBLANES)
            hsl = pl.ds(b0, V7X_SUBLANES)
            h_carry[hsl, :] = lax.fori_loop(0, steps, functools.partial(scan_step, b0), h_carry[hsl, :])
            return carry

        lax.fori_loop(0, n_groups, group_scan, 0)
    for i in range(n_sub):
        store_block(ylru_ref, i * sub_steps,
                    lambda cols, i=i: gate_buf[i * sub_rows:(i + 1) * sub_rows, cols])
    hout_ref[...] = h_carry[...]
    sbuf[0:tail, :] = sbuf[rows:rows + tail, :]
    lbuf[0:tail, :] = lbuf[rows:rows + tail, :]
    for t in range(steps, steps + pad_steps):
        for ref in (z_ref, xbc_ref, dt_ref):
            ref[:, t, :] = jnp.zeros((bsub, ref.shape[2]), F32)


def _in_proj_call(x, l, w, sconv0, sconv0_map, lconv0, lconv0_map, h0, h0_map, *,
                  bsub, steps, pad_steps=0, n_sub=1):
    batch, length, _ = x.shape
    n_t = length // steps
    n_b = batch // bsub
    assert length == n_t * steps and batch == n_b * bsub and bsub % V7X_SUBLANES == 0
    assert pad_steps == 0 or n_t == 1
    assert steps >= CONV_TAIL
    assert steps % n_sub == 0
    strided = steps // n_sub >= bsub
    out_steps = steps + pad_steps
    rows = steps * bsub
    tail = CONV_TAIL * bsub

    def tile(n_steps, width):
        return pl.BlockSpec((bsub, n_steps, width), lambda b, t: (b, t, 0))

    def squeeze_lead(a, block):
        return (None,) * (a.ndim - len(block)) + block

    def conv_out(width):
        return pl.BlockSpec((bsub, CONV_TAIL, width), lambda b, t: (b, 0, 0))

    return pl.pallas_call(
        functools.partial(_in_proj_body, bsub, steps, pad_steps, n_sub),
        grid=(n_b, n_t),
        in_specs=[
            tile(steps, D_MODEL),
            _layer_spec((1, D_MODEL), l),
            _layer_spec((D_MODEL, IN_COLS_PAD), l),
            _layer_spec((CONV_W * V7X_SUBLANES, SSD_CONV_DIM), l),
            _layer_spec((V7X_SUBLANES, SSD_CONV_DIM), l),
            _layer_spec((CONV_W * V7X_SUBLANES, D_LRU), l), _layer_spec((V7X_SUBLANES, D_LRU), l),
            _layer_spec((1, DT_PAD), l),
            _layer_spec((LRU_GATE_GROUPS, V7X_MXU_DIM, V7X_MXU_DIM), l),
            _layer_spec((LRU_GATE_GROUPS, V7X_MXU_DIM, V7X_MXU_DIM), l),
            _layer_spec((1, D_LRU), l), _layer_spec((1, D_LRU), l), _layer_spec((1, D_LRU), l),
            pl.BlockSpec(squeeze_lead(sconv0, (bsub, CONV_TAIL, SSD_CONV_DIM)), sconv0_map),
            pl.BlockSpec(squeeze_lead(lconv0, (bsub, CONV_TAIL, D_LRU)), lconv0_map),
            pl.BlockSpec(squeeze_lead(h0, (bsub, D_LRU)), h0_map),
        ],
        out_specs=[
            tile(out_steps, D_SSD), tile(out_steps, SSD_CONV_DIM), tile(out_steps, DT_PAD),
            tile(steps, D_LRU),
            conv_out(SSD_CONV_DIM), conv_out(D_LRU),
            pl.BlockSpec((bsub, D_LRU), lambda b, t: (b, 0)),
        ],
        out_shape=[
            jax.ShapeDtypeStruct((batch, n_t * out_steps, D_SSD), F32),
            jax.ShapeDtypeStruct((batch, n_t * out_steps, SSD_CONV_DIM), F32),
            jax.ShapeDtypeStruct((batch, n_t * out_steps, DT_PAD), F32),
            jax.ShapeDtypeStruct((batch, length, D_LRU), F32),
            jax.ShapeDtypeStruct((batch, CONV_TAIL, SSD_CONV_DIM), F32),
            jax.ShapeDtypeStruct((batch, CONV_TAIL, D_LRU), F32),
            jax.ShapeDtypeStruct((batch, D_LRU), F32),
        ],
        scratch_shapes=[
            pltpu.VMEM((SSD_CONV_DIM // V7X_LANES, rows, V7X_LANES) if strided
                       else (1, V7X_SUBLANES, V7X_LANES), F32),
            pltpu.VMEM((V7X_SUBLANES, V7X_LANES) if strided else (rows, D_MODEL), F32),
            pltpu.VMEM((tail + rows, SSD_CONV_DIM), F32),
            pltpu.VMEM((tail + rows, D_LRU), F32),
            pltpu.VMEM((rows, D_LRU), F32),
            pltpu.VMEM((rows, D_LRU), F32),
            pltpu.VMEM((rows, D_LRU), F32),
            pltpu.VMEM((bsub, D_LRU), F32),
        ],
        compiler_params=_params(2),
        name="in_proj_lru",
    )(x, w["mix_pre_g"], w["w_in"], w["ssd_conv_w"], w["ssd_conv_b"], w["lru_conv_w"], w["lru_conv_b"],
      w["ssd_dt_bias"], w["lru_wa"], w["lru_wx"], w["lru_ba"], w["lru_bx"], w["lru_lambda"],
      sconv0, lconv0, h0)


def _transpose_rows(x, rows):
    lanes = x.shape[1]
    if rows < lanes:
        x = jnp.concatenate([x, jnp.zeros((lanes - rows, lanes), x.dtype)], axis=0)
    return x.T[:, 0:rows]


def _split_bf16(x, terms):
    parts = []
    for _ in range(terms - 1):
        parts.append(x.astype(BF16))
        x = x - parts[-1].astype(F32)
    return parts + [x.astype(BF16)]


def _ssd_body(chunk, n_chunks, out_steps, seqs, first_layer, xbc_ref, dt_ref, z_ref, alog_ref, dvec_ref,
              g_ref, e_ref, h0_ref, *rest):
    (y_ref, hout_ref, ht_s) = rest[-3:]
    if first_layer is not None:
        @pl.when(pl.program_id(1) == 0)
        def _():
            for k in range(hout_ref.shape[0]):
                if k != first_layer:
                    hout_ref[k] = jnp.zeros(hout_ref.shape[1:], F32)
        hout_ref = hout_ref.at[first_layer]
    stages = [_ssd_sequence(chunk, n_chunks, out_steps, xbc_ref.at[s], dt_ref.at[s], z_ref.at[s],
                            alog_ref, dvec_ref, g_ref, e_ref, h0_ref.at[s], y_ref.at[s],
                            hout_ref.at[s], ht_s.at[s]) for s in range(seqs)]
    for _ in itertools.zip_longest(*stages):
        pass


def _ssd_sequence(chunk, n_chunks, out_steps, xbc_ref, dt_ref, z_ref, alog_ref, dvec_ref, g_ref, e_ref,
                  h0_ref, y_ref, hout_ref, ht_s):
    c = pl.program_id(1)
    group_cols = HEADS_PER_GROUP * SSD_HEAD_DIM

    single = n_chunks == 1
    if single:
        h_given = h0_ref[...].reshape(D_SSD, SSD_STATE)
        ht_bf = h_given.astype(BF16).T
    else:
        @pl.when(c == 0)
        def _():
            ht_s[...] = h0_ref[...].reshape(D_SSD, SSD_STATE).T

    dt = dt_ref[...]
    a = -jnp.exp(alog_ref[...])
    row = lax.broadcasted_iota(jnp.int32, (chunk, chunk), 0)
    col = lax.broadcasted_iota(jnp.int32, (chunk, chunk), 1)
    causal = row >= col
    ones_lower = jnp.where(causal, 1.0, 0.0).astype(BF16)
    da_hi, da_mid, da_lo = _split_bf16(dt * a, 3)
    cum = _dot(ones_lower, da_hi) + _dot(ones_lower, da_mid) + _dot(ones_lower, da_lo)
    yield
    cum_t = _transpose_rows(cum, chunk)
    dt_t = _transpose_rows(dt, chunk)
    tot = cum[chunk - 1:chunk, :]
    cdec = jnp.exp(tot)
    factors = jnp.concatenate([jnp.exp(cum), jnp.exp(tot - cum) * dt], axis=0)
    spread = _dot(factors.astype(BF16), e_ref[...])
    yield
    ecum_x = spread[0:chunk, :]
    wend_x = spread[chunk:2 * chunk, :]
    if not single:
        cdec_parts = _split_bf16(jnp.broadcast_to(cdec, (V7X_SUBLANES, DT_PAD)), 3)
        cdec_x = (_dot(cdec_parts[0], e_ref[...]) + _dot(cdec_parts[1], e_ref[...])
                  + _dot(cdec_parts[2], e_ref[...]))[0:1, :]
    low_half = lax.broadcasted_iota(jnp.int32, (chunk, V7X_LANES), 1) < SSD_HEAD_DIM

    y_groups = []
    for g in range(SSD_GROUPS):
        b0 = D_SSD + g * SSD_STATE
        c0 = D_SSD + SSD_GROUPS * SSD_STATE + g * SSD_STATE
        gcols = slice(g * group_cols, (g + 1) * group_cols)
        bg_t = _transpose_rows(xbc_ref[:, b0:b0 + SSD_STATE], chunk).astype(BF16)
        cg = xbc_ref[:, c0:c0 + SSD_STATE].astype(BF16)
        cb = _dot(cg, bg_t)
        yield
        y_pairs = []
        for pair in range(HEADS_PER_GROUP // 2):
            h = g * HEADS_PER_GROUP + 2 * pair
            x_pair = xbc_ref[:, h * SSD_HEAD_DIM:(h + 2) * SSD_HEAD_DIM]
            scores = []
            for hh in (h, h + 1):
                diff = cum[:, hh:hh + 1] - cum_t[hh:hh + 1, :]
                decay = jnp.exp(jnp.where(causal, diff, -jnp.inf))
                scores.append((cb * decay * dt_t[hh:hh + 1, :]).astype(BF16))
            x_lo = jnp.where(low_half, x_pair, 0.0).astype(BF16)
            x_hi = jnp.where(low_half, 0.0, x_pair).astype(BF16)
            if chunk % V7X_LANES == 0:
                y_pairs.append(_dot(jnp.concatenate(scores, axis=1),
                                    jnp.concatenate([x_lo, x_hi], axis=0)))
            else:
                y_pairs.append(_dot(scores[0], x_lo) + _dot(scores[1], x_hi))
            yield
        xg = xbc_ref[:, gcols]
        xw = (xg * wend_x[:, gcols]).astype(BF16)
        if single:
            y_off = _dot(cg, ht_bf[:, gcols]) * ecum_x[:, gcols]
            bg = xbc_ref[:, b0:b0 + SSD_STATE].astype(BF16)
            update = lax.dot_general(xw, bg, (((0,), (0,)), ((), ())), preferred_element_type=F32)
            heads = range(g * HEADS_PER_GROUP, (g + 1) * HEADS_PER_GROUP)
            decay_rows = jnp.concatenate(
                [jnp.broadcast_to(cdec[:, h:h + 1], (SSD_HEAD_DIM, SSD_STATE)) for h in heads], axis=0)
            h_new = decay_rows * h_given[gcols, :] + update
            hout_ref[g * HEADS_PER_GROUP:(g + 1) * HEADS_PER_GROUP] = h_new.reshape(
                HEADS_PER_GROUP, SSD_HEAD_DIM, SSD_STATE)
        else:
            ht_prev = ht_s[:, gcols]
            y_off = _dot(cg, ht_prev.astype(BF16)) * ecum_x[:, gcols]
            ht_s[:, gcols] = cdec_x[:, gcols] * ht_prev + _dot(bg_t, xw)
        y_groups.append(jnp.concatenate(y_pairs, axis=1) + y_off + dvec_ref[:, gcols] * xg)
        yield

    y = _rms(jnp.concatenate(y_groups, axis=1) * _silu(z_ref[...]), g_ref[...])
    y_ref[...] = y[0:out_steps, :]

    if not single:
        @pl.when(c == n_chunks - 1)
        def _():
            hout_ref[...] = ht_s[...].T.reshape(SSD_HEADS, SSD_HEAD_DIM, SSD_STATE)


def _ssd_call(xbc, dt, z, l, w, h0, h0_map, h_stack, *, chunk, out_len, seqs):
    batch, length, _ = xbc.shape
    depth = w["ssd_a_log"].shape[0]
    n_chunks = length // chunk
    assert length == n_chunks * chunk and (out_len == length or n_chunks == 1) and batch % seqs == 0
    out_steps = min(chunk, out_len)

    def per_seq(n_steps, width):
        return pl.BlockSpec((seqs, n_steps, width), lambda b, c: (b, c, 0))

    state_tail = (SSD_HEADS, SSD_HEAD_DIM, SSD_STATE)
    h0_block = (None,) * (h0.ndim - 4) + (seqs,) + state_tail
    operands = [xbc, dt, z, w["ssd_a_log"], w["ssd_d_cols"], w["ssd_norm_g"], w["head_spread"], h0]
    in_specs = [
        per_seq(chunk, SSD_CONV_DIM), per_seq(chunk, DT_PAD), per_seq(chunk, D_SSD),
        _layer_spec((1, DT_PAD), l), _layer_spec((1, D_SSD), l), _layer_spec((1, D_SSD), l),
        pl.BlockSpec((DT_PAD, D_SSD), lambda b, c: (0, 0), pipeline_mode=pl.Buffered(1)),
        pl.BlockSpec(h0_block, h0_map),
    ]
    aliases = {}
    if h_stack is not None:
        aliases = {len(operands): 1}
        operands.append(h_stack)
        in_specs.append(pl.BlockSpec(memory_space=pl.ANY))
    first = h_stack is None
    return pl.pallas_call(
        functools.partial(_ssd_body, chunk, n_chunks, out_steps, seqs, l if first else None),
        grid=(batch // seqs, n_chunks),
        in_specs=in_specs,
        out_specs=[
            per_seq(out_steps, D_SSD),
            pl.BlockSpec((depth, seqs) + state_tail, lambda b, c: (0, b, 0, 0, 0)) if first
            else pl.BlockSpec((None, seqs) + state_tail, lambda b, c: (l, b, 0, 0, 0)),
        ],
        out_shape=[
            jax.ShapeDtypeStruct((batch, out_len, D_SSD), F32),
            jax.ShapeDtypeStruct((depth, batch) + state_tail, F32),
        ],
        scratch_shapes=[pltpu.VMEM((seqs, SSD_STATE, D_SSD), F32)],
        input_output_aliases=aliases,
        compiler_params=_params(2),
        name="ssd_chunk",
    )(*operands)


def _regroup_w_in_body(w_ref, o_ref):
    o_dt = COL_XBC + SSD_CONV_DIM
    o_gate = o_dt + SSD_HEADS
    o_ref[:, COL_Z:COL_GATE] = w_ref[:, COL_Z:o_dt].astype(BF16)
    o_ref[:, COL_GATE:COL_DT] = w_ref[:, o_gate:o_gate + D_LRU + D_LRU].astype(BF16)
    dt_tile = w_ref[:, o_dt:o_dt + DT_PAD]
    lane = lax.broadcasted_iota(jnp.int32, dt_tile.shape, 1)
    o_ref[:, COL_DT:IN_COLS_PAD] = jnp.where(lane < SSD_HEADS, dt_tile, 0.0).astype(BF16)


def _regroup_w_in(w_in):
    depth, d_model, in_cols = w_in.shape
    assert in_cols == D_SSD + SSD_CONV_DIM + SSD_HEADS + 2 * D_LRU
    rows = _row_tile(d_model, W_IN_PREP_ROWS)
    return pl.pallas_call(
        _regroup_w_in_body,
        grid=(depth, d_model // rows),
        in_specs=[pl.BlockSpec((None, rows, in_cols), lambda l, i: (l, i, 0))],
        out_specs=pl.BlockSpec((None, rows, IN_COLS_PAD), lambda l, i: (l, i, 0)),
        out_shape=jax.ShapeDtypeStruct((depth, d_model, IN_COLS_PAD), BF16),
        compiler_params=_params(2),
        name="regroup_w_in",
    )(w_in)


def _prepare_weights(p):
    depth = p["w_in"].shape[0]

    def vec(a):
        return a.reshape(depth, 1, a.shape[-1])

    def pad_heads(a):
        return jnp.pad(a, ((0, 0), (0, DT_PAD - SSD_HEADS))).reshape(depth, 1, DT_PAD)

    def block_diag(a):
        per = V7X_MXU_DIM // LRU_BLOCK_W
        a = a.reshape(depth, LRU_GATE_GROUPS, per, LRU_BLOCK_W, LRU_BLOCK_W)
        eye = jnp.eye(per, dtype=a.dtype)
        a = a[:, :, :, :, None, :] * eye[None, None, :, None, :, None]
        return a.reshape(depth, LRU_GATE_GROUPS, V7X_MXU_DIM, V7X_MXU_DIM).astype(BF16)

    w = {
        "w_in": _regroup_w_in(p["w_in"]),
        "w_out": p["w_out"].astype(BF16),
        "lru_wa": block_diag(p["lru_wa"]),
        "lru_wx": block_diag(p["lru_wx"]),
        "ssd_dt_bias": pad_heads(p["ssd_dt_bias"]),
        "ssd_a_log": pad_heads(p["ssd_a_log"]),
        "ssd_d_cols": jnp.repeat(p["ssd_d"], SSD_HEAD_DIM, axis=-1).reshape(depth, 1, D_SSD),
        "head_spread": (jnp.arange(DT_PAD)[:, None] == jnp.arange(D_SSD)[None, :] // SSD_HEAD_DIM
                        ).astype(BF16),
    }
    for name in ("ssd_conv_w", "lru_conv_w"):
        w[name] = jnp.repeat(p[name], V7X_SUBLANES, axis=1)
    for name in ("ssd_conv_b", "lru_conv_b"):
        w[name] = jnp.repeat(p[name][:, None, :], V7X_SUBLANES, axis=1)
    for name in ("ffn1_wg", "ffn1_wu", "ffn1_wd", "ffn2_wg", "ffn2_wu", "ffn2_wd"):
        w[name] = p[name].astype(BF16)
    for name in ("ffn1_pre_g", "ffn1_post_g", "mix_pre_g", "mix_post_g", "ffn2_pre_g", "ffn2_post_g",
                 "ssd_norm_g", "lru_ba", "lru_bx", "lru_lambda"):
        w[name] = vec(p[name])
    return w


def _segment_mixer(x, l, w, states, ssd_stack, *, bsub, steps, chunk, pad_steps, seqs, n_sub=1):
    (sconv0, sconv0_map), (lconv0, lconv0_map), (h_lru0, h_lru0_map), (h_ssd0, h_ssd0_map) = states
    length = x.shape[1]
    z, xbc, dt, y_lru, sconv, lconv, h_lru = _in_proj_call(
        x, l, w, sconv0, sconv0_map, lconv0, lconv0_map,
        h_lru0, h_lru0_map, bsub=bsub, steps=steps, pad_steps=pad_steps, n_sub=n_sub)
    y_ssd, ssd_stack = _ssd_call(xbc, dt, z, l, w, h_ssd0, h_ssd0_map, ssd_stack,
                                 chunk=chunk, out_len=length, seqs=seqs)
    return y_ssd, y_lru, (sconv, h_lru, lconv), ssd_stack


def kernel(x_prompt, x_sample, state_ssd, state_ssd_conv, state_lru, state_lru_conv, meta_tokens,
           ffn1_pre_g, ffn1_post_g, ffn1_wg, ffn1_wu, ffn1_wd, mix_pre_g, mix_post_g, w_in,
           ssd_conv_w, ssd_conv_b, ssd_dt_bias, ssd_a_log, ssd_d, ssd_norm_g, lru_conv_w, lru_conv_b,
           lru_wa, lru_ba, lru_wx, lru_bx, lru_lambda, w_out, ffn2_pre_g, ffn2_post_g, ffn2_wg,
           ffn2_wu, ffn2_wd):
    bp, seq, _ = x_prompt.shape
    bs, dec_seq, _ = x_sample.shape
    depth = w_in.shape[0]
    assert bp == V7X_SUBLANES and seq % SSD_CHUNK == 0 and bs % SAMPLE_BATCH_TILE == 0
    w = _prepare_weights(dict(
        ffn1_pre_g=ffn1_pre_g, ffn1_post_g=ffn1_post_g, ffn1_wg=ffn1_wg, ffn1_wu=ffn1_wu,
        ffn1_wd=ffn1_wd, mix_pre_g=mix_pre_g, mix_post_g=mix_post_g, w_in=w_in,
        ssd_conv_w=ssd_conv_w, ssd_conv_b=ssd_conv_b, ssd_dt_bias=ssd_dt_bias, ssd_a_log=ssd_a_log,
        ssd_d=ssd_d, ssd_norm_g=ssd_norm_g, lru_conv_w=lru_conv_w, lru_conv_b=lru_conv_b,
        lru_wa=lru_wa, lru_ba=lru_ba, lru_wx=lru_wx, lru_bx=lru_bx, lru_lambda=lru_lambda,
        w_out=w_out, ffn2_pre_g=ffn2_pre_g, ffn2_post_g=ffn2_post_g, ffn2_wg=ffn2_wg,
        ffn2_wu=ffn2_wu, ffn2_wd=ffn2_wd))

    xm = jnp.broadcast_to(meta_tokens.astype(F32)[None], (bp, N_META, D_MODEL))
    xp = x_prompt
    xs = x_sample
    sample_steps = ((dec_seq + V7X_SUBLANES - 1) // V7X_SUBLANES) * V7X_SUBLANES
    tail3 = lambda b, t: (b, 0, 0)
    vec2 = lambda b, t: (b, 0)
    seq_state = lambda b, c: (b, 0, 0, 0)
    zero_states = ((jnp.zeros((bp, CONV_TAIL, SSD_CONV_DIM), F32), tail3),
                   (jnp.zeros((bp, CONV_TAIL, D_LRU), F32), tail3),
                   (jnp.zeros((bp, D_LRU), F32), vec2),
                   (jnp.zeros((bp, SSD_HEADS, SSD_HEAD_DIM, SSD_STATE), F32), seq_state))

    layer_state = lambda b, c, l: (l, b, 0, 0, 0)
    p_out = [[] for _ in range(3)]
    s_out = [[] for _ in range(3)]
    m_ssd = p_ssd = s_ssd = None
    def flat(a):
        return a.reshape(a.shape[0] * a.shape[1], a.shape[2])

    for l in range(depth):
        at_l = functools.partial(layer_state, l=l)
        shapes = (xp.shape, xs.shape, xm.shape)
        xp, xs, xm = (a.reshape(s) for a, s in
                      zip(_ffn_call([flat(xp), flat(xs), flat(xm)], l, w, "ffn1"), shapes))
        m_ssd_y, m_lru_y, m_st, m_ssd = _segment_mixer(
            xm, l, w, zero_states, m_ssd,
            bsub=bp, steps=N_META, chunk=N_META, pad_steps=0, seqs=PROMPT_SEQS)
        p_ssd_y, p_lru_y, p_st, p_ssd = _segment_mixer(
            xp, l, w, ((m_st[0], tail3), (m_st[2], tail3), (m_st[1], vec2), (m_ssd, at_l)), p_ssd,
            bsub=bp, steps=PROMPT_STEPS, chunk=SSD_CHUNK, pad_steps=0, seqs=PROMPT_SEQS,
            n_sub=PROMPT_SUB_TILES)
        s_ssd_y, s_lru_y, s_st, s_ssd = _segment_mixer(
            xs, l, w, ((state_ssd_conv, lambda b, t, l=l: (l, b, 0, 0)),
                       (state_lru_conv, lambda b, t, l=l: (l, b, 0, 0)),
                       (state_lru, lambda b, t, l=l: (l, b, 0)),
                       (state_ssd, at_l)), s_ssd,
            bsub=SAMPLE_BATCH_TILE, steps=dec_seq, chunk=sample_steps,
            pad_steps=sample_steps - dec_seq, seqs=SAMPLE_SEQS)
        (xp,) = _out_ffn_call([(flat(xp), flat(p_ssd_y), flat(p_lru_y))], l, w)
        xs, xm = _out_ffn_call([(flat(xs), flat(s_ssd_y), flat(s_lru_y)),
                                (flat(xm), flat(m_ssd_y), flat(m_lru_y))], l, w)
        xp, xs, xm = (a.reshape(s) for a, s in zip((xp, xs, xm), shapes))
        for acc, st in ((p_out, p_st), (s_out, s_st)):
            for k in range(3):
                acc[k].append(st[k])

    p_conv, p_lru, p_lconv = (jnp.stack(a) for a in p_out)
    s_conv, s_lru, s_lconv = (jnp.stack(a) for a in s_out)
    return (xp, xs, p_ssd, p_conv, p_lru, p_lconv, s_ssd, s_conv, s_lru, s_lconv)
```

```python
import functools
import itertools

import jax
import jax.numpy as jnp
from jax import lax
from jax.experimental import pallas as pl
from jax.experimental.pallas import tpu as pltpu

F32 = jnp.float32
BF16 = jnp.bfloat16

D_MODEL = 1024
D_SSD = 1024
D_LRU = 1024
SSD_HEADS = 16
SSD_HEAD_DIM = 64
SSD_GROUPS = 2
SSD_STATE = 128
HEADS_PER_GROUP = SSD_HEADS // SSD_GROUPS
CONV_W = 4
CONV_TAIL = CONV_W - 1
SSD_CONV_DIM = D_SSD + 2 * SSD_GROUPS * SSD_STATE
LRU_BLOCKS = 16
LRU_BLOCK_W = D_LRU // LRU_BLOCKS
LRU_C = 8.0
EPS = 1e-6
N_META = 16
SSD_CHUNK = 128

V7X_LANES = 128
V7X_SUBLANES = 8
V7X_MXU_DIM = 256
V7X_VMEM_LIMIT_BYTES = 56 * 1024 * 1024

DT_PAD = V7X_LANES
LRU_GATE_GROUPS = D_LRU // V7X_MXU_DIM
COL_Z = 0
COL_XBC = COL_Z + D_SSD
COL_GATE = COL_XBC + SSD_CONV_DIM
COL_XR = COL_GATE + D_LRU
COL_DT = COL_XR + D_LRU
IN_COLS_PAD = COL_DT + DT_PAD
GELU_K = 0.7978845608028654

FFN_ROWS = 512
OUT_FFN_ROWS = 512
PROMPT_STEPS = 64
SAMPLE_BATCH_TILE = 64
PROMPT_SUB_TILES = 2
SCAN_UNROLL_LIMIT = 128
WEIGHT_CAST_ROWS = 128
W_IN_PREP_ROWS = 256
PROMPT_SEQS = 4
SAMPLE_SEQS = 8
ELEMENTWISE_ROWS = 16
ELEMENTWISE_COLS = 512


def _rms(x, g):
    return x * lax.rsqrt(jnp.mean(x * x, axis=-1, keepdims=True) + EPS) * g


def _silu(x):
    return x * _sigmoid(x)


def _sigmoid(x):
    return 0.5 * jnp.tanh(0.5 * x) + 0.5


def _softplus(x):
    return jnp.maximum(x, 0.0) + jnp.log1p(jnp.exp(-jnp.abs(x)))


def _gelu_tanh(x):
    return 0.5 * x * (1.0 + jnp.tanh(GELU_K * (x + 0.044715 * (x * x * x))))


def _dot(a, b):
    return jnp.dot(a, b, preferred_element_type=F32)


def _layer_spec(tail, l):
    zeros = (0,) * len(tail)
    return pl.BlockSpec((None,) + tuple(tail), lambda *_: (l,) + zeros, pipeline_mode=pl.Buffered(1))


def _params(n_axes):
    return pltpu.CompilerParams(dimension_semantics=("arbitrary",) * n_axes,
                                vmem_limit_bytes=V7X_VMEM_LIMIT_BYTES)


def _row_tile(rows, want):
    tm = min(rows, want)
    assert rows % tm == 0
    return tm


def _rows_map(i):
    return (i, 0)


def _ffn_math(x, gpre, gpost, wg_ref, wu_ref, wd_ref):
    xn = _rms(x, gpre).astype(BF16)
    hg = _dot(xn, wg_ref[...])
    hu = _dot(xn, wu_ref[...])
    a = (_silu(hg) * hu).astype(BF16)
    y = _dot(a, wd_ref[...])
    return x + 0.5 * _rms(y, gpost)


class _CastWeight:
    def __init__(self, array, layer, chunk, row0=0, rows=None):
        self.array, self.layer, self.chunk, self.row0 = array, layer, chunk, row0
        self.rows = array.shape[1] - row0 if rows is None else rows
        self.cols = array.shape[2]
        assert self.rows % chunk == 0 and chunk % (2 * V7X_SUBLANES) == 0


def _stream_cast(cw, hbm_ref, dst_ref, stage_ref, sem_ref):
    n = cw.rows // cw.chunk

    def copy(k):
        return pltpu.make_async_copy(
            hbm_ref.at[cw.layer, pl.ds(cw.row0 + k * cw.chunk, cw.chunk), :],
            stage_ref.at[k % 2, 0:cw.chunk, 0:cw.cols], sem_ref.at[k % 2])

    copy(0).start()
    for k in range(n):
        if k + 1 < n:
            copy(k + 1).start()
        copy(k).wait()
        dst_ref[k * cw.chunk:(k + 1) * cw.chunk, :] = stage_ref[k % 2, 0:cw.chunk, 0:cw.cols].astype(BF16)


def _token_call(math, segments, weights, weight_specs, name, tile_rows):
    n_ops = len(segments[0])
    tiles = [_row_tile(seg[0].shape[0], tile_rows) for seg in segments]
    counts = [seg[0].shape[0] // tm for seg, tm in zip(segments, tiles)]
    starts = [sum(counts[:k]) for k in range(len(segments))]
    cast = [cw for cw in weights if isinstance(cw, _CastWeight)]
    stage_rows = max([cw.chunk for cw in cast], default=0)
    stage_cols = max([cw.cols for cw in cast], default=0)

    def seg_spec(k):
        mode = {} if counts[k] > 1 else dict(pipeline_mode=pl.Buffered(1))
        return pl.BlockSpec((tiles[k], D_MODEL),
                            lambda i: (jnp.clip(i - starts[k], 0, counts[k] - 1), 0), **mode)

    def body(*refs):
        n_in = n_ops * len(segments)
        n_out = len(segments)
        x_refs, w_in_refs = refs[:n_in], refs[n_in:n_in + len(weights)]
        o_refs = refs[n_in + len(weights):n_in + len(weights) + n_out]
        scratch = list(refs[n_in + len(weights) + n_out:])
        i = pl.program_id(0)
        w_refs = list(w_in_refs)
        if cast:
            stage_ref, sem_ref = scratch[-2:]
            cast_dst = {id(cw): scratch[n] for n, cw in enumerate(cast)}

            @pl.when(i == 0)
            def _():
                for n, cw in enumerate(weights):
                    if isinstance(cw, _CastWeight):
                        _stream_cast(cw, w_in_refs[n], cast_dst[id(cw)], stage_ref, sem_ref)

            w_refs = [cast_dst[id(cw)] if isinstance(cw, _CastWeight) else w_in_refs[n]
                      for n, cw in enumerate(weights)]

        def run(k):
            operands = [r[...] for r in x_refs[k * n_ops:(k + 1) * n_ops]]
            o_refs[k][...] = math(*operands, *w_refs)

        for k in range(len(segments)):
            pl.when((i >= starts[k]) & (i < starts[k] + counts[k]))(functools.partial(run, k))

    scratch_shapes = [pltpu.VMEM((cw.rows, cw.cols), BF16) for cw in cast]
    if cast:
        scratch_shapes += [pltpu.VMEM((2, stage_rows, stage_cols), F32), pltpu.SemaphoreType.DMA((2,))]
    return pl.pallas_call(
        body,
        grid=(sum(counts),),
        in_specs=[seg_spec(k) for k in range(len(segments)) for _ in range(n_ops)]
        + [pl.BlockSpec(memory_space=pl.ANY) if isinstance(cw, _CastWeight) else spec
           for cw, spec in zip(weights, weight_specs)],
        out_specs=[seg_spec(k) for k in range(len(segments))],
        out_shape=[jax.ShapeDtypeStruct(seg[0].shape, F32) for seg in segments],
        scratch_shapes=scratch_shapes,
        compiler_params=_params(1),
        name=name,
    )(*[a for seg in segments for a in seg],
      *[cw.array if isinstance(cw, _CastWeight) else cw for cw in weights])


def _ffn_math_refs(x, gpre_ref, gpost_ref, wg_ref, wu_ref, wd_ref):
    return _ffn_math(x, gpre_ref[...], gpost_ref[...], wg_ref, wu_ref, wd_ref)


def _ffn_call(xs, l, w, prefix):
    return _token_call(
        _ffn_math_refs, [(x,) for x in xs],
        [w[prefix + "_pre_g"], w[prefix + "_post_g"]]
        + [_CastWeight(w[prefix + name], l, WEIGHT_CAST_ROWS) for name in ("_wg", "_wu", "_wd")],
        [_layer_spec((1, D_MODEL), l), _layer_spec((1, D_MODEL), l), None, None, None],
        prefix, FFN_ROWS)


def _out_ffn_math(x, ys, yl, wos_ref, wol_ref, gmix_ref, gpre_ref, gpost_ref, wg_ref, wu_ref, wd_ref):
    m = _dot(ys.astype(BF16), wos_ref[...]) + _dot(yl.astype(BF16), wol_ref[...])
    x1 = x + _rms(m, gmix_ref[...])
    return _ffn_math(x1, gpre_ref[...], gpost_ref[...], wg_ref, wu_ref, wd_ref)


def _out_ffn_call(segments, l, w):
    return _token_call(
        _out_ffn_math, segments,
        [_CastWeight(w["w_out"], l, WEIGHT_CAST_ROWS, 0, D_SSD),
         _CastWeight(w["w_out"], l, WEIGHT_CAST_ROWS, D_SSD, D_LRU),
         w["mix_post_g"], w["ffn2_pre_g"], w["ffn2_post_g"]]
        + [_CastWeight(w["ffn2" + name], l, WEIGHT_CAST_ROWS) for name in ("_wg", "_wu", "_wd")],
        [None, None, _layer_spec((1, D_MODEL), l), _layer_spec((1, D_MODEL), l),
         _layer_spec((1, D_MODEL), l), None, None, None],
        "out_ffn2", OUT_FFN_ROWS)


def _in_proj_body(bsub, steps, pad_steps, n_sub,
                  x_ref, g_ref, w_ref, scw_ref, scb_ref, lcw_ref, lcb_ref, dtb_ref,
                  wa_ref, wx_ref, ba_ref, bx_ref, lam_ref, sconv0_ref, lconv0_ref, h0_ref,
                  z_ref, xbc_ref, dt_ref, ylru_ref, sconv_ref, lconv_ref, hout_ref,
                  stage, xbuf, sbuf, lbuf, gate_buf, ra_buf, rx_buf, h_carry):
    rows = steps * bsub
    tail = CONV_TAIL * bsub
    sub_steps = steps // n_sub
    sub_rows = sub_steps * bsub
    rc = min(sub_rows, ELEMENTWISE_ROWS)
    strided = sub_steps >= bsub

    def lane_tile(j):
        return slice(j * V7X_LANES, (j + 1) * V7X_LANES)

    def load_block(ref, t0):
        r0 = t0 * bsub
        if not strided:
            for t in range(sub_steps):
                xbuf[r0 + t * bsub:r0 + (t + 1) * bsub, :] = ref[:, t0 + t, :]
            return xbuf[r0:r0 + sub_rows, :]
        n_tiles = ref.shape[2] // V7X_LANES
        for j in range(n_tiles):
            for b in range(bsub):
                stage[j, pl.ds(r0 + b, sub_steps, stride=bsub), :] = ref[b, t0:t0 + sub_steps, lane_tile(j)]
        return jnp.concatenate([stage[j, r0:r0 + sub_rows, :] for j in range(n_tiles)], axis=1)

    def store_block(ref, t0, read):
        r0 = t0 * bsub
        if not strided:
            value = read(slice(0, ref.shape[2]))
            for t in range(sub_steps):
                ref[:, t0 + t, :] = value[t * bsub:(t + 1) * bsub, :]
            return
        n_tiles = ref.shape[2] // V7X_LANES
        for j in range(n_tiles):
            stage[j, r0:r0 + sub_rows, :] = read(lane_tile(j))
        for j in range(n_tiles):
            for b in range(bsub):
                ref[b, t0:t0 + sub_steps, lane_tile(j)] = stage[j, pl.ds(r0 + b, sub_steps, stride=bsub), :]

    def conv_in_place(buf, w_ref, b_ref, act, r_lo):
        sub = V7X_SUBLANES
        for c0 in range(0, buf.shape[1], ELEMENTWISE_COLS):
            cols = slice(c0, c0 + ELEMENTWISE_COLS)
            taps = [w_ref[k * sub:(k + 1) * sub, cols] for k in range(CONV_W)]
            bias = b_ref[:, cols]
            for r0 in range(r_lo, r_lo + sub_rows, sub):
                acc = bias + buf[r0:r0 + sub, cols] * taps[0]
                for k in range(1, CONV_W):
                    acc = acc + buf[r0 + k * bsub:r0 + k * bsub + sub, cols] * taps[k]
                buf[r0:r0 + sub, cols] = act(acc)

    @pl.when(pl.program_id(1) == 0)
    def _():
        for k in range(CONV_TAIL):
            sbuf[k * bsub:(k + 1) * bsub, :] = sconv0_ref[:, k, :]
            lbuf[k * bsub:(k + 1) * bsub, :] = lconv0_ref[:, k, :]
        h_carry[...] = h0_ref[...]

    neg_c_softplus = -LRU_C * _softplus(-lam_ref[...])

    def sub_tile(i):
        t0 = i * sub_steps
        r_lo = t0 * bsub
        out = slice(r_lo, r_lo + sub_rows)
        pre = slice(tail + r_lo, tail + r_lo + sub_rows)
        xn = _rms(load_block(x_ref, t0), g_ref[...]).astype(BF16)
        yield
        lbuf[pre, :] = _dot(xn, w_ref[:, COL_XR:COL_DT])
        gate_buf[out, :] = _dot(xn, w_ref[:, COL_GATE:COL_XR])
        yield
        conv_in_place(lbuf, lcw_ref, lcb_ref, lambda v: v, r_lo)
        yield
        for q in range(LRU_GATE_GROUPS):
            cols = slice(q * V7X_MXU_DIM, (q + 1) * V7X_MXU_DIM)
            xr_bf = lbuf[out, cols].astype(BF16)
            ra_buf[out, cols] = _dot(xr_bf, wa_ref[q])
            rx_buf[out, cols] = _dot(xr_bf, wx_ref[q])
        yield
        for c0 in range(0, D_LRU, ELEMENTWISE_COLS):
            cols = slice(c0, c0 + ELEMENTWISE_COLS)
            ncs, ba, bx = neg_c_softplus[:, cols], ba_ref[:, cols], bx_ref[:, cols]
            for r0 in range(r_lo, r_lo + sub_rows, rc):
                sl = slice(r0, r0 + rc)
                log_a = ncs * _sigmoid(ra_buf[sl, cols] + ba)
                a = jnp.exp(log_a)
                m = -jnp.tanh(log_a) * (a * a + 1.0)
                mult = jnp.where(m > 0.0, m * lax.rsqrt(m), 0.0)
                rx_buf[sl, cols] = mult * _sigmoid(rx_buf[sl, cols] + bx) * lbuf[sl, cols]
                ra_buf[sl, cols] = a
                gate_buf[sl, cols] = _gelu_tanh(gate_buf[sl, cols])
        yield
        z = _dot(xn, w_ref[:, COL_Z:COL_XBC])
        sbuf[pre, :] = _dot(xn, w_ref[:, COL_XBC:COL_GATE])
        dt = _softplus(_dot(xn, w_ref[:, COL_DT:IN_COLS_PAD]) + dtb_ref[...])
        yield
        store_block(z_ref, t0, lambda cols: z[:, cols])
        store_block(dt_ref, t0, lambda cols: dt[:, cols])
        conv_in_place(sbuf, scw_ref, scb_ref, _silu, r_lo)
        yield
        store_block(xbc_ref, t0, lambda cols: sbuf[out, cols])

    waiting = [sub_tile(i) for i in range(n_sub)]
    running = []
    while waiting or running:
        if waiting:
            running.append(waiting.pop(0))
        for stages in list(running):
            if next(stages, "done") == "done":
                running.remove(stages)

    for k in range(CONV_TAIL):
        sconv_ref[:, k, :] = sbuf[rows + k * bsub:rows + (k + 1) * bsub, :]
        lconv_ref[:, k, :] = lbuf[rows + k * bsub:rows + (k + 1) * bsub, :]

    def scan_step(b0, t, h):
        start = t * bsub + b0
        if not isinstance(start, int):
            start = pl.multiple_of(start, V7X_SUBLANES)
        sl = pl.ds(start, V7X_SUBLANES)
        h = ra_buf[sl, :] * h + rx_buf[sl, :]
        gate_buf[sl, :] = h * gate_buf[sl, :]
        return h

    n_groups = bsub // V7X_SUBLANES
    if n_groups * steps <= SCAN_UNROLL_LIMIT:
        for bg in range(n_groups):
            hsl = slice(bg * V7X_SUBLANES, (bg + 1) * V7X_SUBLANES)
            h = h_carry[hsl, :]
            for t in range(steps):
                h = scan_step(bg * V7X_SUBLANES, t, h)
            h_carry[hsl, :] = h
    else:
        def group_scan(bg, carry):
            b0 = pl.multiple_of(bg * V7X_SUBLANES, V7X_SUBLANES)
            hsl = pl.ds(b0, V7X_SUBLANES)
            h_carry[hsl, :] = lax.fori_loop(0, steps, functools.partial(scan_step, b0), h_carry[hsl, :])
            return carry

        lax.fori_loop(0, n_groups, group_scan, 0)
    for i in range(n_sub):
        store_block(ylru_ref, i * sub_steps,
                    lambda cols, i=i: gate_buf[i * sub_rows:(i + 1) * sub_rows, cols])
    hout_ref[...] = h_carry[...]
    sbuf[0:tail, :] = sbuf[rows:rows + tail, :]
    lbuf[0:tail, :] = lbuf[rows:rows + tail, :]
    for t in range(steps, steps + pad_steps):
        for ref in (z_ref, xbc_ref, dt_ref):
            ref[:, t, :] = jnp.zeros((bsub, ref.shape[2]), F32)


def _in_proj_call(x, l, w, sconv0, sconv0_map, lconv0, lconv0_map, h0, h0_map, *,
                  bsub, steps, pad_steps=0, n_sub=1):
    batch, length, _ = x.shape
    n_t = length // steps
    n_b = batch // bsub
    assert length == n_t * steps and batch == n_b * bsub and bsub % V7X_SUBLANES == 0
    assert pad_steps == 0 or n_t == 1
    assert steps >= CONV_TAIL
    assert steps % n_sub == 0
    strided = steps // n_sub >= bsub
    out_steps = steps + pad_steps
    rows = steps * bsub
    tail = CONV_TAIL * bsub

    def tile(n_steps, width):
        return pl.BlockSpec((bsub, n_steps, width), lambda b, t: (b, t, 0))

    def squeeze_lead(a, block):
        return (None,) * (a.ndim - len(block)) + block

    def conv_out(width):
        return pl.BlockSpec((bsub, CONV_TAIL, width), lambda b, t: (b, 0, 0))

    return pl.pallas_call(
        functools.partial(_in_proj_body, bsub, steps, pad_steps, n_sub),
        grid=(n_b, n_t),
        in_specs=[
            tile(steps, D_MODEL),
            _layer_spec((1, D_MODEL), l),
            _layer_spec((D_MODEL, IN_COLS_PAD), l),
            _layer_spec((CONV_W * V7X_SUBLANES, SSD_CONV_DIM), l),
            _layer_spec((V7X_SUBLANES, SSD_CONV_DIM), l),
            _layer_spec((CONV_W * V7X_SUBLANES, D_LRU), l), _layer_spec((V7X_SUBLANES, D_LRU), l),
            _layer_spec((1, DT_PAD), l),
            _layer_spec((LRU_GATE_GROUPS, V7X_MXU_DIM, V7X_MXU_DIM), l),
            _layer_spec((LRU_GATE_GROUPS, V7X_MXU_DIM, V7X_MXU_DIM), l),
            _layer_spec((1, D_LRU), l), _layer_spec((1, D_LRU), l), _layer_spec((1, D_LRU), l),
            pl.BlockSpec(squeeze_lead(sconv0, (bsub, CONV_TAIL, SSD_CONV_DIM)), sconv0_map),
            pl.BlockSpec(squeeze_lead(lconv0, (bsub, CONV_TAIL, D_LRU)), lconv0_map),
            pl.BlockSpec(squeeze_lead(h0, (bsub, D_LRU)), h0_map),
        ],
        out_specs=[
            tile(out_steps, D_SSD), tile(out_steps, SSD_CONV_DIM), tile(out_steps, DT_PAD),
            tile(steps, D_LRU),
            conv_out(SSD_CONV_DIM), conv_out(D_LRU),
            pl.BlockSpec((bsub, D_LRU), lambda b, t: (b, 0)),
        ],
        out_shape=[
            jax.ShapeDtypeStruct((batch, n_t * out_steps, D_SSD), F32),
            jax.ShapeDtypeStruct((batch, n_t * out_steps, SSD_CONV_DIM), F32),
            jax.ShapeDtypeStruct((batch, n_t * out_steps, DT_PAD), F32),
            jax.ShapeDtypeStruct((batch, length, D_LRU), F32),
            jax.ShapeDtypeStruct((batch, CONV_TAIL, SSD_CONV_DIM), F32),
            jax.ShapeDtypeStruct((batch, CONV_TAIL, D_LRU), F32),
            jax.ShapeDtypeStruct((batch, D_LRU), F32),
        ],
        scratch_shapes=[
            pltpu.VMEM((SSD_CONV_DIM // V7X_LANES, rows, V7X_LANES) if strided
                       else (1, V7X_SUBLANES, V7X_LANES), F32),
            pltpu.VMEM((V7X_SUBLANES, V7X_LANES) if strided else (rows, D_MODEL), F32),
            pltpu.VMEM((tail + rows, SSD_CONV_DIM), F32),
            pltpu.VMEM((tail + rows, D_LRU), F32),
            pltpu.VMEM((rows, D_LRU), F32),
            pltpu.VMEM((rows, D_LRU), F32),
            pltpu.VMEM((rows, D_LRU), F32),
            pltpu.VMEM((bsub, D_LRU), F32),
        ],
        compiler_params=_params(2),
        name="in_proj_lru",
    )(x, w["mix_pre_g"], w["w_in"], w["ssd_conv_w"], w["ssd_conv_b"], w["lru_conv_w"], w["lru_conv_b"],
      w["ssd_dt_bias"], w["lru_wa"], w["lru_wx"], w["lru_ba"], w["lru_bx"], w["lru_lambda"],
      sconv0, lconv0, h0)


def _transpose_rows(x, rows):
    lanes = x.shape[1]
    if rows < lanes:
        x = jnp.concatenate([x, jnp.zeros((lanes - rows, lanes), x.dtype)], axis=0)
    return x.T[:, 0:rows]


def _split_bf16(x, terms):
    parts = []
    for _ in range(terms - 1):
        parts.append(x.astype(BF16))
        x = x - parts[-1].astype(F32)
    return parts + [x.astype(BF16)]


def _ssd_body(chunk, n_chunks, out_steps, seqs, first_layer, xbc_ref, dt_ref, z_ref, alog_ref, dvec_ref,
              g_ref, e_ref, h0_ref, *rest):
    (y_ref, hout_ref, ht_s) = rest[-3:]
    if first_layer is not None:
        @pl.when(pl.program_id(1) == 0)
        def _():
            for k in range(hout_ref.shape[0]):
                if k != first_layer:
                    hout_ref[k] = jnp.zeros(hout_ref.shape[1:], F32)
        hout_ref = hout_ref.at[first_layer]
    stages = [_ssd_sequence(chunk, n_chunks, out_steps, xbc_ref.at[s], dt_ref.at[s], z_ref.at[s],
                            alog_ref, dvec_ref, g_ref, e_ref, h0_ref.at[s], y_ref.at[s],
                            hout_ref.at[s], ht_s.at[s]) for s in range(seqs)]
    for _ in itertools.zip_longest(*stages):
        pass


def _ssd_sequence(chunk, n_chunks, out_steps, xbc_ref, dt_ref, z_ref, alog_ref, dvec_ref, g_ref, e_ref,
                  h0_ref, y_ref, hout_ref, ht_s):
    c = pl.program_id(1)
    group_cols = HEADS_PER_GROUP * SSD_HEAD_DIM

    single = n_chunks == 1
    if single:
        h_given = h0_ref[...].reshape(D_SSD, SSD_STATE)
        ht_bf = h_given.astype(BF16).T
    else:
        @pl.when(c == 0)
        def _():
            ht_s[...] = h0_ref[...].reshape(D_SSD, SSD_STATE).T

    dt = dt_ref[...]
    a = -jnp.exp(alog_ref[...])
    row = lax.broadcasted_iota(jnp.int32, (chunk, chunk), 0)
    col = lax.broadcasted_iota(jnp.int32, (chunk, chunk), 1)
    causal = row >= col
    ones_lower = jnp.where(causal, 1.0, 0.0).astype(BF16)
    da_hi, da_mid, da_lo = _split_bf16(dt * a, 3)
    cum = _dot(ones_lower, da_hi) + _dot(ones_lower, da_mid) + _dot(ones_lower, da_lo)
    yield
    cum_t = _transpose_rows(cum, chunk)
    dt_t = _transpose_rows(dt, chunk)
    tot = cum[chunk - 1:chunk, :]
    cdec = jnp.exp(tot)
    factors = jnp.concatenate([jnp.exp(cum), jnp.exp(tot - cum) * dt], axis=0)
    spread = _dot(factors.astype(BF16), e_ref[...])
    yield
    ecum_x = spread[0:chunk, :]
    wend_x = spread[chunk:2 * chunk, :]
    if not single:
        cdec_parts = _split_bf16(jnp.broadcast_to(cdec, (V7X_SUBLANES, DT_PAD)), 3)
        cdec_x = (_dot(cdec_parts[0], e_ref[...]) + _dot(cdec_parts[1], e_ref[...])
                  + _dot(cdec_parts[2], e_ref[...]))[0:1, :]
    low_half = lax.broadcasted_iota(jnp.int32, (chunk, V7X_LANES), 1) < SSD_HEAD_DIM

    y_groups = []
    for g in range(SSD_GROUPS):
        b0 = D_SSD + g * SSD_STATE
        c0 = D_SSD + SSD_GROUPS * SSD_STATE + g * SSD_STATE
        gcols = slice(g * group_cols, (g + 1) * group_cols)
        bg_t = _transpose_rows(xbc_ref[:, b0:b0 + SSD_STATE], chunk).astype(BF16)
        cg = xbc_ref[:, c0:c0 + SSD_STATE].astype(BF16)
        cb = _dot(cg, bg_t)
        yield
        y_pairs = []
        for pair in range(HEADS_PER_GROUP // 2):
            h = g * HEADS_PER_GROUP + 2 * pair
            x_pair = xbc_ref[:, h * SSD_HEAD_DIM:(h + 2) * SSD_HEAD_DIM]
            scores = []
            for hh in (h, h + 1):
                diff = cum[:, hh:hh + 1] - cum_t[hh:hh + 1, :]
                decay = jnp.exp(jnp.where(causal, diff, -jnp.inf))
                scores.append((cb * decay * dt_t[hh:hh + 1, :]).astype(BF16))
            x_lo = jnp.where(low_half, x_pair, 0.0).astype(BF16)
            x_hi = jnp.where(low_half, 0.0, x_pair).astype(BF16)
            if chunk % V7X_LANES == 0:
                y_pairs.append(_dot(jnp.concatenate(scores, axis=1),
                                    jnp.concatenate([x_lo, x_hi], axis=0)))
            else:
                y_pairs.append(_dot(scores[0], x_lo) + _dot(scores[1], x_hi))
            yield
        xg = xbc_ref[:, gcols]
        xw = (xg * wend_x[:, gcols]).astype(BF16)
        if single:
            y_off = _dot(cg, ht_bf[:, gcols]) * ecum_x[:, gcols]
            bg = xbc_ref[:, b0:b0 + SSD_STATE].astype(BF16)
            update = lax.dot_general(xw, bg, (((0,), (0,)), ((), ())), preferred_element_type=F32)
            heads = range(g * HEADS_PER_GROUP, (g + 1) * HEADS_PER_GROUP)
            decay_rows = jnp.concatenate(
                [jnp.broadcast_to(cdec[:, h:h + 1], (SSD_HEAD_DIM, SSD_STATE)) for h in heads], axis=0)
            h_new = decay_rows * h_given[gcols, :] + update
            hout_ref[g * HEADS_PER_GROUP:(g + 1) * HEADS_PER_GROUP] = h_new.reshape(
                HEADS_PER_GROUP, SSD_HEAD_DIM, SSD_STATE)
        else:
            ht_prev = ht_s[:, gcols]
            y_off = _dot(cg, ht_prev.astype(BF16)) * ecum_x[:, gcols]
            ht_s[:, gcols] = cdec_x[:, gcols] * ht_prev + _dot(bg_t, xw)
        y_groups.append(jnp.concatenate(y_pairs, axis=1) + y_off + dvec_ref[:, gcols] * xg)
        yield

    y = _rms(jnp.concatenate(y_groups, axis=1) * _silu(z_ref[...]), g_ref[...])
    y_ref[...] = y[0:out_steps, :]

    if not single:
        @pl.when(c == n_chunks - 1)
        def _():
            hout_ref[...] = ht_s[...].T.reshape(SSD_HEADS, SSD_HEAD_DIM, SSD_STATE)


def _ssd_call(xbc, dt, z, l, w, h0, h0_map, h_stack, *, chunk, out_len, seqs):
    batch, length, _ = xbc.shape
    depth = w["ssd_a_log"].shape[0]
    n_chunks = length // chunk
    assert length == n_chunks * chunk and (out_len == length or n_chunks == 1) and batch % seqs == 0
    out_steps = min(chunk, out_len)

    def per_seq(n_steps, width):
        return pl.BlockSpec((seqs, n_steps, width), lambda b, c: (b, c, 0))

    state_tail = (SSD_HEADS, SSD_HEAD_DIM, SSD_STATE)
    h0_block = (None,) * (h0.ndim - 4) + (seqs,) + state_tail
    operands = [xbc, dt, z, w["ssd_a_log"], w["ssd_d_cols"], w["ssd_norm_g"], w["head_spread"], h0]
    in_specs = [
        per_seq(chunk, SSD_CONV_DIM), per_seq(chunk, DT_PAD), per_seq(chunk, D_SSD),
        _layer_spec((1, DT_PAD), l), _layer_spec((1, D_SSD), l), _layer_spec((1, D_SSD), l),
        pl.BlockSpec((DT_PAD, D_SSD), lambda b, c: (0, 0), pipeline_mode=pl.Buffered(1)),
        pl.BlockSpec(h0_block, h0_map),
    ]
    aliases = {}
    if h_stack is not None:
        aliases = {len(operands): 1}
        operands.append(h_stack)
        in_specs.append(pl.BlockSpec(memory_space=pl.ANY))
    first = h_stack is None
    return pl.pallas_call(
        functools.partial(_ssd_body, chunk, n_chunks, out_steps, seqs, l if first else None),
        grid=(batch // seqs, n_chunks),
        in_specs=in_specs,
        out_specs=[
            per_seq(out_steps, D_SSD),
            pl.BlockSpec((depth, seqs) + state_tail, lambda b, c: (0, b, 0, 0, 0)) if first
            else pl.BlockSpec((None, seqs) + state_tail, lambda b, c: (l, b, 0, 0, 0)),
        ],
        out_shape=[
            jax.ShapeDtypeStruct((batch, out_len, D_SSD), F32),
            jax.ShapeDtypeStruct((depth, batch) + state_tail, F32),
        ],
        scratch_shapes=[pltpu.VMEM((seqs, SSD_STATE, D_SSD), F32)],
        input_output_aliases=aliases,
        compiler_params=_params(2),
        name="ssd_chunk",
    )(*operands)


def _regroup_w_in_body(w_ref, o_ref):
    o_dt = COL_XBC + SSD_CONV_DIM
    o_gate = o_dt + SSD_HEADS
    o_ref[:, COL_Z:COL_GATE] = w_ref[:, COL_Z:o_dt].astype(BF16)
    o_ref[:, COL_GATE:COL_DT] = w_ref[:, o_gate:o_gate + D_LRU + D_LRU].astype(BF16)
    dt_tile = w_ref[:, o_dt:o_dt + DT_PAD]
    lane = lax.broadcasted_iota(jnp.int32, dt_tile.shape, 1)
    o_ref[:, COL_DT:IN_COLS_PAD] = jnp.where(lane < SSD_HEADS, dt_tile, 0.0).astype(BF16)


def _regroup_w_in(w_in):
    depth, d_model, in_cols = w_in.shape
    assert in_cols == D_SSD + SSD_CONV_DIM + SSD_HEADS + 2 * D_LRU
    rows = _row_tile(d_model, W_IN_PREP_ROWS)
    return pl.pallas_call(
        _regroup_w_in_body,
        grid=(depth, d_model // rows),
        in_specs=[pl.BlockSpec((None, rows, in_cols), lambda l, i: (l, i, 0))],
        out_specs=pl.BlockSpec((None, rows, IN_COLS_PAD), lambda l, i: (l, i, 0)),
        out_shape=jax.ShapeDtypeStruct((depth, d_model, IN_COLS_PAD), BF16),
        compiler_params=_params(2),
        name="regroup_w_in",
    )(w_in)


def _prepare_weights(p):
    depth = p["w_in"].shape[0]

    def vec(a):
        return a.reshape(depth, 1, a.shape[-1])

    def pad_heads(a):
        return jnp.pad(a, ((0, 0), (0, DT_PAD - SSD_HEADS))).reshape(depth, 1, DT_PAD)

    def block_diag(a):
        per = V7X_MXU_DIM // LRU_BLOCK_W
        a = a.reshape(depth, LRU_GATE_GROUPS, per, LRU_BLOCK_W, LRU_BLOCK_W)
        eye = jnp.eye(per, dtype=a.dtype)
        a = a[:, :, :, :, None, :] * eye[None, None, :, None, :, None]
        return a.reshape(depth, LRU_GATE_GROUPS, V7X_MXU_DIM, V7X_MXU_DIM).astype(BF16)

    w = {
        "w_in": _regroup_w_in(p["w_in"]),
        "w_out": p["w_out"],
        "lru_wa": block_diag(p["lru_wa"]),
        "lru_wx": block_diag(p["lru_wx"]),
        "ssd_dt_bias": pad_heads(p["ssd_dt_bias"]),
        "ssd_a_log": pad_heads(p["ssd_a_log"]),
        "ssd_d_cols": jnp.repeat(p["ssd_d"], SSD_HEAD_DIM, axis=-1).reshape(depth, 1, D_SSD),
        "head_spread": (jnp.arange(DT_PAD)[:, None] == jnp.arange(D_SSD)[None, :] // SSD_HEAD_DIM
                        ).astype(BF16),
    }
    for name in ("ssd_conv_w", "lru_conv_w"):
        w[name] = jnp.repeat(p[name], V7X_SUBLANES, axis=1)
    for name in ("ssd_conv_b", "lru_conv_b"):
        w[name] = jnp.repeat(p[name][:, None, :], V7X_SUBLANES, axis=1)
    for name in ("ffn1_wg", "ffn1_wu", "ffn1_wd", "ffn2_wg", "ffn2_wu", "ffn2_wd"):
        w[name] = p[name]
    for name in ("ffn1_pre_g", "ffn1_post_g", "mix_pre_g", "mix_post_g", "ffn2_pre_g", "ffn2_post_g",
                 "ssd_norm_g", "lru_ba", "lru_bx", "lru_lambda"):
        w[name] = vec(p[name])
    return w


def _segment_mixer(x, l, w, states, ssd_stack, *, bsub, steps, chunk, pad_steps, seqs, n_sub=1):
    (sconv0, sconv0_map), (lconv0, lconv0_map), (h_lru0, h_lru0_map), (h_ssd0, h_ssd0_map) = states
    length = x.shape[1]
    z, xbc, dt, y_lru, sconv, lconv, h_lru = _in_proj_call(
        x, l, w, sconv0, sconv0_map, lconv0, lconv0_map,
        h_lru0, h_lru0_map, bsub=bsub, steps=steps, pad_steps=pad_steps, n_sub=n_sub)
    y_ssd, ssd_stack = _ssd_call(xbc, dt, z, l, w, h_ssd0, h_ssd0_map, ssd_stack,
                                 chunk=chunk, out_len=length, seqs=seqs)
    return y_ssd, y_lru, (sconv, h_lru, lconv), ssd_stack


def kernel(x_prompt, x_sample, state_ssd, state_ssd_conv, state_lru, state_lru_conv, meta_tokens,
           ffn1_pre_g, ffn1_post_g, ffn1_wg, ffn1_wu, ffn1_wd, mix_pre_g, mix_post_g, w_in,
           ssd_conv_w, ssd_conv_b, ssd_dt_bias, ssd_a_log, ssd_d, ssd_norm_g, lru_conv_w, lru_conv_b,
           lru_wa, lru_ba, lru_wx, lru_bx, lru_lambda, w_out, ffn2_pre_g, ffn2_post_g, ffn2_wg,
           ffn2_wu, ffn2_wd):
    bp, seq, _ = x_prompt.shape
    bs, dec_seq, _ = x_sample.shape
    depth = w_in.shape[0]
    assert bp == V7X_SUBLANES and seq % SSD_CHUNK == 0 and bs % SAMPLE_BATCH_TILE == 0
    w = _prepare_weights(dict(
        ffn1_pre_g=ffn1_pre_g, ffn1_post_g=ffn1_post_g, ffn1_wg=ffn1_wg, ffn1_wu=ffn1_wu,
        ffn1_wd=ffn1_wd, mix_pre_g=mix_pre_g, mix_post_g=mix_post_g, w_in=w_in,
        ssd_conv_w=ssd_conv_w, ssd_conv_b=ssd_conv_b, ssd_dt_bias=ssd_dt_bias, ssd_a_log=ssd_a_log,
        ssd_d=ssd_d, ssd_norm_g=ssd_norm_g, lru_conv_w=lru_conv_w, lru_conv_b=lru_conv_b,
        lru_wa=lru_wa, lru_ba=lru_ba, lru_wx=lru_wx, lru_bx=lru_bx, lru_lambda=lru_lambda,
        w_out=w_out, ffn2_pre_g=ffn2_pre_g, ffn2_post_g=ffn2_post_g, ffn2_wg=ffn2_wg,
        ffn2_wu=ffn2_wu, ffn2_wd=ffn2_wd))

    xm = jnp.broadcast_to(meta_tokens.astype(F32)[None], (bp, N_META, D_MODEL))
    xp = x_prompt
    xs = x_sample
    sample_steps = ((dec_seq + V7X_SUBLANES - 1) // V7X_SUBLANES) * V7X_SUBLANES
    tail3 = lambda b, t: (b, 0, 0)
    vec2 = lambda b, t: (b, 0)
    seq_state = lambda b, c: (b, 0, 0, 0)
    zero_states = ((jnp.zeros((bp, CONV_TAIL, SSD_CONV_DIM), F32), tail3),
                   (jnp.zeros((bp, CONV_TAIL, D_LRU), F32), tail3),
                   (jnp.zeros((bp, D_LRU), F32), vec2),
                   (jnp.zeros((bp, SSD_HEADS, SSD_HEAD_DIM, SSD_STATE), F32), seq_state))

    layer_state = lambda b, c, l: (l, b, 0, 0, 0)
    p_out = [[] for _ in range(3)]
    s_out = [[] for _ in range(3)]
    m_ssd = p_ssd = s_ssd = None
    def flat(a):
        return a.reshape(a.shape[0] * a.shape[1], a.shape[2])

    for l in range(depth):
        at_l = functools.partial(layer_state, l=l)
        shapes = (xp.shape, xs.shape, xm.shape)
        xp, xs, xm = (a.reshape(s) for a, s in
                      zip(_ffn_call([flat(xp), flat(xs), flat(xm)], l, w, "ffn1"), shapes))
        m_ssd_y, m_lru_y, m_st, m_ssd = _segment_mixer(
            xm, l, w, zero_states, m_ssd,
            bsub=bp, steps=N_META, chunk=N_META, pad_steps=0, seqs=PROMPT_SEQS)
        p_ssd_y, p_lru_y, p_st, p_ssd = _segment_mixer(
            xp, l, w, ((m_st[0], tail3), (m_st[2], tail3), (m_st[1], vec2), (m_ssd, at_l)), p_ssd,
            bsub=bp, steps=PROMPT_STEPS, chunk=SSD_CHUNK, pad_steps=0, seqs=PROMPT_SEQS,
            n_sub=PROMPT_SUB_TILES)
        s_ssd_y, s_lru_y, s_st, s_ssd = _segment_mixer(
            xs, l, w, ((state_ssd_conv, lambda b, t, l=l: (l, b, 0, 0)),
                       (state_lru_conv, lambda b, t, l=l: (l, b, 0, 0)),
                       (state_lru, lambda b, t, l=l: (l, b, 0)),
                       (state_ssd, at_l)), s_ssd,
            bsub=SAMPLE_BATCH_TILE, steps=dec_seq, chunk=sample_steps,
            pad_steps=sample_steps - dec_seq, seqs=SAMPLE_SEQS)
        (xp,) = _out_ffn_call([(flat(xp), flat(p_ssd_y), flat(p_lru_y))], l, w)
        xs, xm = _out_ffn_call([(flat(xs), flat(s_ssd_y), flat(s_lru_y)),
                                (flat(xm), flat(m_ssd_y), flat(m_lru_y))], l, w)
        xp, xs, xm = (a.reshape(s) for a, s in zip((xp, xs, xm), shapes))
        for acc, st in ((p_out, p_st), (s_out, s_st)):
            for k in range(3):
                acc[k].append(st[k])

    p_conv, p_lru, p_lconv = (jnp.stack(a) for a in p_out)
    s_conv, s_lru, s_lconv = (jnp.stack(a) for a in s_out)
    return (xp, xs, p_ssd, p_conv, p_lru, p_lconv, s_ssd, s_conv, s_lru, s_lconv)
```

```python
import functools
import itertools

import jax
import jax.numpy as jnp
from jax import lax
from jax.experimental import pallas as pl
from jax.experimental.pallas import tpu as pltpu

F32 = jnp.float32
BF16 = jnp.bfloat16

D_MODEL = 1024
D_SSD = 1024
D_LRU = 1024
SSD_HEADS = 16
SSD_HEAD_DIM = 64
SSD_GROUPS = 2
SSD_STATE = 128
HEADS_PER_GROUP = SSD_HEADS // SSD_GROUPS
CONV_W = 4
CONV_TAIL = CONV_W - 1
SSD_CONV_DIM = D_SSD + 2 * SSD_GROUPS * SSD_STATE
LRU_BLOCKS = 16
LRU_BLOCK_W = D_LRU // LRU_BLOCKS
LRU_C = 8.0
EPS = 1e-6
N_META = 16
SSD_CHUNK = 128

V7X_LANES = 128
V7X_SUBLANES = 8
V7X_MXU_DIM = 256
V7X_VMEM_LIMIT_BYTES = 56 * 1024 * 1024

DT_PAD = V7X_LANES
LRU_GATE_GROUPS = D_LRU // V7X_MXU_DIM
COL_Z = 0
COL_XBC = COL_Z + D_SSD
COL_GATE = COL_XBC + SSD_CONV_DIM
COL_XR = COL_GATE + D_LRU
COL_DT = COL_XR + D_LRU
IN_COLS_PAD = COL_DT + DT_PAD
GELU_K = 0.7978845608028654

FFN_ROWS = 512
OUT_FFN_ROWS = 512
PROMPT_STEPS = 64
SAMPLE_BATCH_TILE = 64
PROMPT_SUB_TILES = 2
SCAN_UNROLL_LIMIT = 128
WEIGHT_CAST_ROWS = 128
WEIGHT_CAST_DEPTH = 3
W_IN_PREP_ROWS = 256
PROMPT_SEQS = 4
SAMPLE_SEQS = 8
ELEMENTWISE_ROWS = 16
ELEMENTWISE_COLS = 512


def _rms(x, g):
    return x * lax.rsqrt(jnp.mean(x * x, axis=-1, keepdims=True) + EPS) * g


def _silu(x):
    return x * _sigmoid(x)


def _sigmoid(x):
    return 0.5 * jnp.tanh(0.5 * x) + 0.5


def _softplus(x):
    return jnp.maximum(x, 0.0) + jnp.log1p(jnp.exp(-jnp.abs(x)))


def _gelu_tanh(x):
    return 0.5 * x * (1.0 + jnp.tanh(GELU_K * (x + 0.044715 * (x * x * x))))


def _dot(a, b):
    return jnp.dot(a, b, preferred_element_type=F32)


def _layer_spec(tail, l):
    zeros = (0,) * len(tail)
    return pl.BlockSpec((None,) + tuple(tail), lambda *_: (l,) + zeros, pipeline_mode=pl.Buffered(1))


def _params(n_axes):
    return pltpu.CompilerParams(dimension_semantics=("arbitrary",) * n_axes,
                                vmem_limit_bytes=V7X_VMEM_LIMIT_BYTES)


def _row_tile(rows, want):
    tm = min(rows, want)
    assert rows % tm == 0
    return tm


def _rows_map(i):
    return (i, 0)


def _ffn_math(x, gpre, gpost, wg_ref, wu_ref, wd_ref):
    xn = _rms(x, gpre).astype(BF16)
    hg = _dot(xn, wg_ref[...])
    hu = _dot(xn, wu_ref[...])
    a = (_silu(hg) * hu).astype(BF16)
    y = _dot(a, wd_ref[...])
    return x + 0.5 * _rms(y, gpost)


class _CastWeight:
    def __init__(self, array, layer, chunk, row0=0, rows=None):
        self.array, self.layer, self.chunk, self.row0 = array, layer, chunk, row0
        self.rows = array.shape[1] - row0 if rows is None else rows
        self.cols = array.shape[2]
        assert self.rows % chunk == 0 and chunk % (2 * V7X_SUBLANES) == 0


def _stream_cast(casts, stage_ref, sem_ref):
    pieces = [(cw, hbm, dst, k) for cw, hbm, dst in casts for k in range(cw.rows // cw.chunk)]
    slots = stage_ref.shape[0]
    assert slots > WEIGHT_CAST_DEPTH

    def copy(p):
        cw, hbm, _, k = pieces[p]
        return pltpu.make_async_copy(
            hbm.at[cw.layer, pl.ds(cw.row0 + k * cw.chunk, cw.chunk), :],
            stage_ref.at[p % slots, 0:cw.chunk, 0:cw.cols], sem_ref.at[p % slots])

    for p in range(min(WEIGHT_CAST_DEPTH, len(pieces))):
        copy(p).start()
    for p, (cw, _, dst, k) in enumerate(pieces):
        if p + WEIGHT_CAST_DEPTH < len(pieces):
            copy(p + WEIGHT_CAST_DEPTH).start()
        copy(p).wait()
        dst[k * cw.chunk:(k + 1) * cw.chunk, :] = stage_ref[p % slots, 0:cw.chunk, 0:cw.cols].astype(BF16)


def _token_call(math, segments, weights, weight_specs, name, tile_rows):
    n_ops = len(segments[0])
    tiles = [_row_tile(seg[0].shape[0], tile_rows) for seg in segments]
    counts = [seg[0].shape[0] // tm for seg, tm in zip(segments, tiles)]
    starts = [sum(counts[:k]) for k in range(len(segments))]
    cast = [cw for cw in weights if isinstance(cw, _CastWeight)]
    stage_rows = max([cw.chunk for cw in cast], default=0)
    stage_cols = max([cw.cols for cw in cast], default=0)

    def seg_spec(k):
        mode = {} if counts[k] > 1 else dict(pipeline_mode=pl.Buffered(1))
        return pl.BlockSpec((tiles[k], D_MODEL),
                            lambda i: (jnp.clip(i - starts[k], 0, counts[k] - 1), 0), **mode)

    def body(*refs):
        n_in = n_ops * len(segments)
        n_out = len(segments)
        x_refs, w_in_refs = refs[:n_in], refs[n_in:n_in + len(weights)]
        o_refs = refs[n_in + len(weights):n_in + len(weights) + n_out]
        scratch = list(refs[n_in + len(weights) + n_out:])
        i = pl.program_id(0)
        w_refs = list(w_in_refs)
        if cast:
            stage_ref, sem_ref = scratch[-2:]
            cast_dst = {id(cw): scratch[n] for n, cw in enumerate(cast)}

            @pl.when(i == 0)
            def _():
                _stream_cast([(cw, w_in_refs[n], cast_dst[id(cw)]) for n, cw in enumerate(weights)
                              if isinstance(cw, _CastWeight)], stage_ref, sem_ref)

            w_refs = [cast_dst[id(cw)] if isinstance(cw, _CastWeight) else w_in_refs[n]
                      for n, cw in enumerate(weights)]

        def run(k):
            operands = [r[...] for r in x_refs[k * n_ops:(k + 1) * n_ops]]
            o_refs[k][...] = math(*operands, *w_refs)

        for k in range(len(segments)):
            pl.when((i >= starts[k]) & (i < starts[k] + counts[k]))(functools.partial(run, k))

    scratch_shapes = [pltpu.VMEM((cw.rows, cw.cols), BF16) for cw in cast]
    if cast:
        slots = WEIGHT_CAST_DEPTH + 1
        scratch_shapes += [pltpu.VMEM((slots, stage_rows, stage_cols), F32),
                           pltpu.SemaphoreType.DMA((slots,))]
    return pl.pallas_call(
        body,
        grid=(sum(counts),),
        in_specs=[seg_spec(k) for k in range(len(segments)) for _ in range(n_ops)]
        + [pl.BlockSpec(memory_space=pl.ANY) if isinstance(cw, _CastWeight) else spec
           for cw, spec in zip(weights, weight_specs)],
        out_specs=[seg_spec(k) for k in range(len(segments))],
        out_shape=[jax.ShapeDtypeStruct(seg[0].shape, F32) for seg in segments],
        scratch_shapes=scratch_shapes,
        compiler_params=_params(1),
        name=name,
    )(*[a for seg in segments for a in seg],
      *[cw.array if isinstance(cw, _CastWeight) else cw for cw in weights])


def _ffn_math_refs(x, gpre_ref, gpost_ref, wg_ref, wu_ref, wd_ref):
    return _ffn_math(x, gpre_ref[...], gpost_ref[...], wg_ref, wu_ref, wd_ref)


def _ffn_call(xs, l, w, prefix):
    return _token_call(
        _ffn_math_refs, [(x,) for x in xs],
        [w[prefix + "_pre_g"], w[prefix + "_post_g"]]
        + [_CastWeight(w[prefix + name], l, WEIGHT_CAST_ROWS) for name in ("_wg", "_wu", "_wd")],
        [_layer_spec((1, D_MODEL), l), _layer_spec((1, D_MODEL), l), None, None, None],
        prefix, FFN_ROWS)


def _out_ffn_math(x, ys, yl, wos_ref, wol_ref, gmix_ref, gpre_ref, gpost_ref, wg_ref, wu_ref, wd_ref):
    m = _dot(ys.astype(BF16), wos_ref[...]) + _dot(yl.astype(BF16), wol_ref[...])
    x1 = x + _rms(m, gmix_ref[...])
    return _ffn_math(x1, gpre_ref[...], gpost_ref[...], wg_ref, wu_ref, wd_ref)


def _out_ffn_call(segments, l, w):
    return _token_call(
        _out_ffn_math, segments,
        [_CastWeight(w["w_out"], l, WEIGHT_CAST_ROWS, 0, D_SSD),
         _CastWeight(w["w_out"], l, WEIGHT_CAST_ROWS, D_SSD, D_LRU),
         w["mix_post_g"], w["ffn2_pre_g"], w["ffn2_post_g"]]
        + [_CastWeight(w["ffn2" + name], l, WEIGHT_CAST_ROWS) for name in ("_wg", "_wu", "_wd")],
        [None, None, _layer_spec((1, D_MODEL), l), _layer_spec((1, D_MODEL), l),
         _layer_spec((1, D_MODEL), l), None, None, None],
        "out_ffn2", OUT_FFN_ROWS)


def _in_proj_body(bsub, steps, pad_steps, n_sub,
                  x_ref, g_ref, w_ref, scw_ref, scb_ref, lcw_ref, lcb_ref, dtb_ref,
                  wa_ref, wx_ref, ba_ref, bx_ref, lam_ref, sconv0_ref, lconv0_ref, h0_ref,
                  z_ref, xbc_ref, dt_ref, ylru_ref, sconv_ref, lconv_ref, hout_ref,
                  stage, xbuf, sbuf, lbuf, gate_buf, ra_buf, rx_buf, h_carry):
    rows = steps * bsub
    tail = CONV_TAIL * bsub
    sub_steps = steps // n_sub
    sub_rows = sub_steps * bsub
    rc = min(sub_rows, ELEMENTWISE_ROWS)
    strided = sub_steps >= bsub

    def lane_tile(j):
        return slice(j * V7X_LANES, (j + 1) * V7X_LANES)

    def load_block(ref, t0):
        r0 = t0 * bsub
        if not strided:
            for t in range(sub_steps):
                xbuf[r0 + t * bsub:r0 + (t + 1) * bsub, :] = ref[:, t0 + t, :]
            return xbuf[r0:r0 + sub_rows, :]
        n_tiles = ref.shape[2] // V7X_LANES
        for j in range(n_tiles):
            for b in range(bsub):
                stage[j, pl.ds(r0 + b, sub_steps, stride=bsub), :] = ref[b, t0:t0 + sub_steps, lane_tile(j)]
        return jnp.concatenate([stage[j, r0:r0 + sub_rows, :] for j in range(n_tiles)], axis=1)

    def store_block(ref, t0, read):
        r0 = t0 * bsub
        if not strided:
            value = read(slice(0, ref.shape[2]))
            for t in range(sub_steps):
                ref[:, t0 + t, :] = value[t * bsub:(t + 1) * bsub, :]
            return
        n_tiles = ref.shape[2] // V7X_LANES
        for j in range(n_tiles):
            stage[j, r0:r0 + sub_rows, :] = read(lane_tile(j))
        for j in range(n_tiles):
            for b in range(bsub):
                ref[b, t0:t0 + sub_steps, lane_tile(j)] = stage[j, pl.ds(r0 + b, sub_steps, stride=bsub), :]

    def conv_in_place(buf, w_ref, b_ref, act, r_lo):
        sub = V7X_SUBLANES
        for c0 in range(0, buf.shape[1], ELEMENTWISE_COLS):
            cols = slice(c0, c0 + ELEMENTWISE_COLS)
            taps = [w_ref[k * sub:(k + 1) * sub, cols] for k in range(CONV_W)]
            bias = b_ref[:, cols]
            for r0 in range(r_lo, r_lo + sub_rows, sub):
                acc = bias + buf[r0:r0 + sub, cols] * taps[0]
                for k in range(1, CONV_W):
                    acc = acc + buf[r0 + k * bsub:r0 + k * bsub + sub, cols] * taps[k]
                buf[r0:r0 + sub, cols] = act(acc)

    @pl.when(pl.program_id(1) == 0)
    def _():
        for k in range(CONV_TAIL):
            sbuf[k * bsub:(k + 1) * bsub, :] = sconv0_ref[:, k, :]
            lbuf[k * bsub:(k + 1) * bsub, :] = lconv0_ref[:, k, :]
        h_carry[...] = h0_ref[...]

    neg_c_softplus = -LRU_C * _softplus(-lam_ref[...])

    def sub_tile(i):
        t0 = i * sub_steps
        r_lo = t0 * bsub
        out = slice(r_lo, r_lo + sub_rows)
        pre = slice(tail + r_lo, tail + r_lo + sub_rows)
        xn = _rms(load_block(x_ref, t0), g_ref[...]).astype(BF16)
        yield
        lbuf[pre, :] = _dot(xn, w_ref[:, COL_XR:COL_DT])
        gate_buf[out, :] = _dot(xn, w_ref[:, COL_GATE:COL_XR])
        yield
        conv_in_place(lbuf, lcw_ref, lcb_ref, lambda v: v, r_lo)
        yield
        for q in range(LRU_GATE_GROUPS):
            cols = slice(q * V7X_MXU_DIM, (q + 1) * V7X_MXU_DIM)
            xr_bf = lbuf[out, cols].astype(BF16)
            ra_buf[out, cols] = _dot(xr_bf, wa_ref[q])
            rx_buf[out, cols] = _dot(xr_bf, wx_ref[q])
        yield
        for c0 in range(0, D_LRU, ELEMENTWISE_COLS):
            cols = slice(c0, c0 + ELEMENTWISE_COLS)
            ncs, ba, bx = neg_c_softplus[:, cols], ba_ref[:, cols], bx_ref[:, cols]
            for r0 in range(r_lo, r_lo + sub_rows, rc):
                sl = slice(r0, r0 + rc)
                log_a = ncs * _sigmoid(ra_buf[sl, cols] + ba)
                a = jnp.exp(log_a)
                m = -jnp.tanh(log_a) * (a * a + 1.0)
                mult = jnp.where(m > 0.0, m * lax.rsqrt(m), 0.0)
                rx_buf[sl, cols] = mult * _sigmoid(rx_buf[sl, cols] + bx) * lbuf[sl, cols]
                ra_buf[sl, cols] = a
                gate_buf[sl, cols] = _gelu_tanh(gate_buf[sl, cols])
        yield
        z = _dot(xn, w_ref[:, COL_Z:COL_XBC])
        sbuf[pre, :] = _dot(xn, w_ref[:, COL_XBC:COL_GATE])
        dt = _softplus(_dot(xn, w_ref[:, COL_DT:IN_COLS_PAD]) + dtb_ref[...])
        yield
        store_block(z_ref, t0, lambda cols: z[:, cols])
        store_block(dt_ref, t0, lambda cols: dt[:, cols])
        conv_in_place(sbuf, scw_ref, scb_ref, _silu, r_lo)
        yield
        store_block(xbc_ref, t0, lambda cols: sbuf[out, cols])

    waiting = [sub_tile(i) for i in range(n_sub)]
    running = []
    while waiting or running:
        if waiting:
            running.append(waiting.pop(0))
        for stages in list(running):
            if next(stages, "done") == "done":
                running.remove(stages)

    for k in range(CONV_TAIL):
        sconv_ref[:, k, :] = sbuf[rows + k * bsub:rows + (k + 1) * bsub, :]
        lconv_ref[:, k, :] = lbuf[rows + k * bsub:rows + (k + 1) * bsub, :]

    def scan_step(b0, t, h):
        start = t * bsub + b0
        if not isinstance(start, int):
            start = pl.multiple_of(start, V7X_SUBLANES)
        sl = pl.ds(start, V7X_SUBLANES)
        h = ra_buf[sl, :] * h + rx_buf[sl, :]
        gate_buf[sl, :] = h * gate_buf[sl, :]
        return h

    n_groups = bsub // V7X_SUBLANES
    if n_groups * steps <= SCAN_UNROLL_LIMIT:
        for bg in range(n_groups):
            hsl = slice(bg * V7X_SUBLANES, (bg + 1) * V7X_SUBLANES)
            h = h_carry[hsl, :]
            for t in range(steps):
                h = scan_step(bg * V7X_SUBLANES, t, h)
            h_carry[hsl, :] = h
    else:
        def group_scan(bg, carry):
            b0 = pl.multiple_of(bg * V7X_SUBLANES, V7X_SUBLANES)
            hsl = pl.ds(b0, V7X_SUBLANES)
            h_carry[hsl, :] = lax.fori_loop(0, steps, functools.partial(scan_step, b0), h_carry[hsl, :])
            return carry

        lax.fori_loop(0, n_groups, group_scan, 0)
    for i in range(n_sub):
        store_block(ylru_ref, i * sub_steps,
                    lambda cols, i=i: gate_buf[i * sub_rows:(i + 1) * sub_rows, cols])
    hout_ref[...] = h_carry[...]
    sbuf[0:tail, :] = sbuf[rows:rows + tail, :]
    lbuf[0:tail, :] = lbuf[rows:rows + tail, :]
    for t in range(steps, steps + pad_steps):
        for ref in (z_ref, xbc_ref, dt_ref):
            ref[:, t, :] = jnp.zeros((bsub, ref.shape[2]), F32)


def _in_proj_call(x, l, w, sconv0, sconv0_map, lconv0, lconv0_map, h0, h0_map, *,
                  bsub, steps, pad_steps=0, n_sub=1):
    batch, length, _ = x.shape
    n_t = length // steps
    n_b = batch // bsub
    assert length == n_t * steps and batch == n_b * bsub and bsub % V7X_SUBLANES == 0
    assert pad_steps == 0 or n_t == 1
    assert steps >= CONV_TAIL
    assert steps % n_sub == 0
    strided = steps // n_sub >= bsub
    out_steps = steps + pad_steps
    rows = steps * bsub
    tail = CONV_TAIL * bsub

    def tile(n_steps, width):
        return pl.BlockSpec((bsub, n_steps, width), lambda b, t: (b, t, 0))

    def squeeze_lead(a, block):
        return (None,) * (a.ndim - len(block)) + block

    def conv_out(width):
        return pl.BlockSpec((bsub, CONV_TAIL, width), lambda b, t: (b, 0, 0))

    return pl.pallas_call(
        functools.partial(_in_proj_body, bsub, steps, pad_steps, n_sub),
        grid=(n_b, n_t),
        in_specs=[
            tile(steps, D_MODEL),
            _layer_spec((1, D_MODEL), l),
            _layer_spec((D_MODEL, IN_COLS_PAD), l),
            _layer_spec((CONV_W * V7X_SUBLANES, SSD_CONV_DIM), l),
            _layer_spec((V7X_SUBLANES, SSD_CONV_DIM), l),
            _layer_spec((CONV_W * V7X_SUBLANES, D_LRU), l), _layer_spec((V7X_SUBLANES, D_LRU), l),
            _layer_spec((1, DT_PAD), l),
            _layer_spec((LRU_GATE_GROUPS, V7X_MXU_DIM, V7X_MXU_DIM), l),
            _layer_spec((LRU_GATE_GROUPS, V7X_MXU_DIM, V7X_MXU_DIM), l),
            _layer_spec((1, D_LRU), l), _layer_spec((1, D_LRU), l), _layer_spec((1, D_LRU), l),
            pl.BlockSpec(squeeze_lead(sconv0, (bsub, CONV_TAIL, SSD_CONV_DIM)), sconv0_map),
            pl.BlockSpec(squeeze_lead(lconv0, (bsub, CONV_TAIL, D_LRU)), lconv0_map),
            pl.BlockSpec(squeeze_lead(h0, (bsub, D_LRU)), h0_map),
        ],
        out_specs=[
            tile(out_steps, D_SSD), tile(out_steps, SSD_CONV_DIM), tile(out_steps, DT_PAD),
            tile(steps, D_LRU),
            conv_out(SSD_CONV_DIM), conv_out(D_LRU),
            pl.BlockSpec((bsub, D_LRU), lambda b, t: (b, 0)),
        ],
        out_shape=[
            jax.ShapeDtypeStruct((batch, n_t * out_steps, D_SSD), F32),
            jax.ShapeDtypeStruct((batch, n_t * out_steps, SSD_CONV_DIM), F32),
            jax.ShapeDtypeStruct((batch, n_t * out_steps, DT_PAD), F32),
            jax.ShapeDtypeStruct((batch, length, D_LRU), F32),
            jax.ShapeDtypeStruct((batch, CONV_TAIL, SSD_CONV_DIM), F32),
            jax.ShapeDtypeStruct((batch, CONV_TAIL, D_LRU), F32),
            jax.ShapeDtypeStruct((batch, D_LRU), F32),
        ],
        scratch_shapes=[
            pltpu.VMEM((SSD_CONV_DIM // V7X_LANES, rows, V7X_LANES) if strided
                       else (1, V7X_SUBLANES, V7X_LANES), F32),
            pltpu.VMEM((V7X_SUBLANES, V7X_LANES) if strided else (rows, D_MODEL), F32),
            pltpu.VMEM((tail + rows, SSD_CONV_DIM), F32),
            pltpu.VMEM((tail + rows, D_LRU), F32),
            pltpu.VMEM((rows, D_LRU), F32),
            pltpu.VMEM((rows, D_LRU), F32),
            pltpu.VMEM((rows, D_LRU), F32),
            pltpu.VMEM((bsub, D_LRU), F32),
        ],
        compiler_params=_params(2),
        name="in_proj_lru",
    )(x, w["mix_pre_g"], w["w_in"], w["ssd_conv_w"], w["ssd_conv_b"], w["lru_conv_w"], w["lru_conv_b"],
      w["ssd_dt_bias"], w["lru_wa"], w["lru_wx"], w["lru_ba"], w["lru_bx"], w["lru_lambda"],
      sconv0, lconv0, h0)


def _transpose_rows(x, rows):
    lanes = x.shape[1]
    if rows < lanes:
        x = jnp.concatenate([x, jnp.zeros((lanes - rows, lanes), x.dtype)], axis=0)
    return x.T[:, 0:rows]


def _split_bf16(x, terms):
    parts = []
    for _ in range(terms - 1):
        parts.append(x.astype(BF16))
        x = x - parts[-1].astype(F32)
    return parts + [x.astype(BF16)]


def _ssd_body(chunk, n_chunks, out_steps, seqs, first_layer, xbc_ref, dt_ref, z_ref, alog_ref, dvec_ref,
              g_ref, e_ref, h0_ref, *rest):
    (y_ref, hout_ref, ht_s) = rest[-3:]
    if first_layer is not None:
        @pl.when(pl.program_id(1) == 0)
        def _():
            for k in range(hout_ref.shape[0]):
                if k != first_layer:
                    hout_ref[k] = jnp.zeros(hout_ref.shape[1:], F32)
        hout_ref = hout_ref.at[first_layer]
    stages = [_ssd_sequence(chunk, n_chunks, out_steps, xbc_ref.at[s], dt_ref.at[s], z_ref.at[s],
                            alog_ref, dvec_ref, g_ref, e_ref, h0_ref.at[s], y_ref.at[s],
                            hout_ref.at[s], ht_s.at[s]) for s in range(seqs)]
    for _ in itertools.zip_longest(*stages):
        pass


def _ssd_sequence(chunk, n_chunks, out_steps, xbc_ref, dt_ref, z_ref, alog_ref, dvec_ref, g_ref, e_ref,
                  h0_ref, y_ref, hout_ref, ht_s):
    c = pl.program_id(1)
    group_cols = HEADS_PER_GROUP * SSD_HEAD_DIM

    single = n_chunks == 1
    if single:
        h_given = h0_ref[...].reshape(D_SSD, SSD_STATE)
        ht_bf = h_given.astype(BF16).T
    else:
        @pl.when(c == 0)
        def _():
            ht_s[...] = h0_ref[...].reshape(D_SSD, SSD_STATE).T

    dt = dt_ref[...]
    a = -jnp.exp(alog_ref[...])
    row = lax.broadcasted_iota(jnp.int32, (chunk, chunk), 0)
    col = lax.broadcasted_iota(jnp.int32, (chunk, chunk), 1)
    causal = row >= col
    ones_lower = jnp.where(causal, 1.0, 0.0).astype(BF16)
    da_hi, da_mid, da_lo = _split_bf16(dt * a, 3)
    cum = _dot(ones_lower, da_hi) + _dot(ones_lower, da_mid) + _dot(ones_lower, da_lo)
    yield
    cum_t = _transpose_rows(cum, chunk)
    dt_t = _transpose_rows(dt, chunk)
    tot = cum[chunk - 1:chunk, :]
    cdec = jnp.exp(tot)
    factors = jnp.concatenate([jnp.exp(cum), jnp.exp(tot - cum) * dt], axis=0)
    spread = _dot(factors.astype(BF16), e_ref[...])
    yield
    ecum_x = spread[0:chunk, :]
    wend_x = spread[chunk:2 * chunk, :]
    if not single:
        cdec_parts = _split_bf16(jnp.broadcast_to(cdec, (V7X_SUBLANES, DT_PAD)), 3)
        cdec_x = (_dot(cdec_parts[0], e_ref[...]) + _dot(cdec_parts[1], e_ref[...])
                  + _dot(cdec_parts[2], e_ref[...]))[0:1, :]
    low_half = lax.broadcasted_iota(jnp.int32, (chunk, V7X_LANES), 1) < SSD_HEAD_DIM

    y_groups = []
    for g in range(SSD_GROUPS):
        b0 = D_SSD + g * SSD_STATE
        c0 = D_SSD + SSD_GROUPS * SSD_STATE + g * SSD_STATE
        gcols = slice(g * group_cols, (g + 1) * group_cols)
        bg_t = _transpose_rows(xbc_ref[:, b0:b0 + SSD_STATE], chunk).astype(BF16)
        cg = xbc_ref[:, c0:c0 + SSD_STATE].astype(BF16)
        cb = _dot(cg, bg_t)
        yield
        y_pairs = []
        for pair in range(HEADS_PER_GROUP // 2):
            h = g * HEADS_PER_GROUP + 2 * pair
            x_pair = xbc_ref[:, h * SSD_HEAD_DIM:(h + 2) * SSD_HEAD_DIM]
            scores = []
            for hh in (h, h + 1):
                diff = cum[:, hh:hh + 1] - cum_t[hh:hh + 1, :]
                decay = jnp.exp(jnp.where(causal, diff, -jnp.inf))
                scores.append((cb * decay * dt_t[hh:hh + 1, :]).astype(BF16))
            x_lo = jnp.where(low_half, x_pair, 0.0).astype(BF16)
            x_hi = jnp.where(low_half, 0.0, x_pair).astype(BF16)
            if chunk % V7X_LANES == 0:
                y_pairs.append(_dot(jnp.concatenate(scores, axis=1),
                                    jnp.concatenate([x_lo, x_hi], axis=0)))
            else:
                y_pairs.append(_dot(scores[0], x_lo) + _dot(scores[1], x_hi))
            yield
        xg = xbc_ref[:, gcols]
        xw = (xg * wend_x[:, gcols]).astype(BF16)
        if single:
            y_off = _dot(cg, ht_bf[:, gcols]) * ecum_x[:, gcols]
            bg = xbc_ref[:, b0:b0 + SSD_STATE].astype(BF16)
            update = lax.dot_general(xw, bg, (((0,), (0,)), ((), ())), preferred_element_type=F32)
            heads = range(g * HEADS_PER_GROUP, (g + 1) * HEADS_PER_GROUP)
            decay_rows = jnp.concatenate(
                [jnp.broadcast_to(cdec[:, h:h + 1], (SSD_HEAD_DIM, SSD_STATE)) for h in heads], axis=0)
            h_new = decay_rows * h_given[gcols, :] + update
            hout_ref[g * HEADS_PER_GROUP:(g + 1) * HEADS_PER_GROUP] = h_new.reshape(
                HEADS_PER_GROUP, SSD_HEAD_DIM, SSD_STATE)
        else:
            ht_prev = ht_s[:, gcols]
            y_off = _dot(cg, ht_prev.astype(BF16)) * ecum_x[:, gcols]
            ht_s[:, gcols] = cdec_x[:, gcols] * ht_prev + _dot(bg_t, xw)
        y_groups.append(jnp.concatenate(y_pairs, axis=1) + y_off + dvec_ref[:, gcols] * xg)
        yield

    y = _rms(jnp.concatenate(y_groups, axis=1) * _silu(z_ref[...]), g_ref[...])
    y_ref[...] = y[0:out_steps, :]

    if not single:
        @pl.when(c == n_chunks - 1)
        def _():
            hout_ref[...] = ht_s[...].T.reshape(SSD_HEADS, SSD_HEAD_DIM, SSD_STATE)


def _ssd_call(xbc, dt, z, l, w, h0, h0_map, h_stack, *, chunk, out_len, seqs):
    batch, length, _ = xbc.shape
    depth = w["ssd_a_log"].shape[0]
    n_chunks = length // chunk
    assert length == n_chunks * chunk and (out_len == length or n_chunks == 1) and batch % seqs == 0
    out_steps = min(chunk, out_len)

    def per_seq(n_steps, width):
        return pl.BlockSpec((seqs, n_steps, width), lambda b, c: (b, c, 0))

    state_tail = (SSD_HEADS, SSD_HEAD_DIM, SSD_STATE)
    h0_block = (None,) * (h0.ndim - 4) + (seqs,) + state_tail
    operands = [xbc, dt, z, w["ssd_a_log"], w["ssd_d_cols"], w["ssd_norm_g"], w["head_spread"], h0]
    in_specs = [
        per_seq(chunk, SSD_CONV_DIM), per_seq(chunk, DT_PAD), per_seq(chunk, D_SSD),
        _layer_spec((1, DT_PAD), l), _layer_spec((1, D_SSD), l), _layer_spec((1, D_SSD), l),
        pl.BlockSpec((DT_PAD, D_SSD), lambda b, c: (0, 0), pipeline_mode=pl.Buffered(1)),
        pl.BlockSpec(h0_block, h0_map),
    ]
    aliases = {}
    if h_stack is not None:
        aliases = {len(operands): 1}
        operands.append(h_stack)
        in_specs.append(pl.BlockSpec(memory_space=pl.ANY))
    first = h_stack is None
    return pl.pallas_call(
        functools.partial(_ssd_body, chunk, n_chunks, out_steps, seqs, l if first else None),
        grid=(batch // seqs, n_chunks),
        in_specs=in_specs,
        out_specs=[
            per_seq(out_steps, D_SSD),
            pl.BlockSpec((depth, seqs) + state_tail, lambda b, c: (0, b, 0, 0, 0)) if first
            else pl.BlockSpec((None, seqs) + state_tail, lambda b, c: (l, b, 0, 0, 0)),
        ],
        out_shape=[
            jax.ShapeDtypeStruct((batch, out_len, D_SSD), F32),
            jax.ShapeDtypeStruct((depth, batch) + state_tail, F32),
        ],
        scratch_shapes=[pltpu.VMEM((seqs, SSD_STATE, D_SSD), F32)],
        input_output_aliases=aliases,
        compiler_params=_params(2),
        name="ssd_chunk",
    )(*operands)


def _regroup_w_in_body(w_ref, o_ref):
    o_dt = COL_XBC + SSD_CONV_DIM
    o_gate = o_dt + SSD_HEADS
    o_ref[:, COL_Z:COL_GATE] = w_ref[:, COL_Z:o_dt].astype(BF16)
    o_ref[:, COL_GATE:COL_DT] = w_ref[:, o_gate:o_gate + D_LRU + D_LRU].astype(BF16)
    dt_tile = w_ref[:, o_dt:o_dt + DT_PAD]
    lane = lax.broadcasted_iota(jnp.int32, dt_tile.shape, 1)
    o_ref[:, COL_DT:IN_COLS_PAD] = jnp.where(lane < SSD_HEADS, dt_tile, 0.0).astype(BF16)


def _regroup_w_in(w_in):
    depth, d_model, in_cols = w_in.shape
    assert in_cols == D_SSD + SSD_CONV_DIM + SSD_HEADS + 2 * D_LRU
    rows = _row_tile(d_model, W_IN_PREP_ROWS)
    return pl.pallas_call(
        _regroup_w_in_body,
        grid=(depth, d_model // rows),
        in_specs=[pl.BlockSpec((None, rows, in_cols), lambda l, i: (l, i, 0))],
        out_specs=pl.BlockSpec((None, rows, IN_COLS_PAD), lambda l, i: (l, i, 0)),
        out_shape=jax.ShapeDtypeStruct((depth, d_model, IN_COLS_PAD), BF16),
        compiler_params=_params(2),
        name="regroup_w_in",
    )(w_in)


def _prepare_weights(p):
    depth = p["w_in"].shape[0]

    def vec(a):
        return a.reshape(depth, 1, a.shape[-1])

    def pad_heads(a):
        return jnp.pad(a, ((0, 0), (0, DT_PAD - SSD_HEADS))).reshape(depth, 1, DT_PAD)

    def block_diag(a):
        per = V7X_MXU_DIM // LRU_BLOCK_W
        a = a.reshape(depth, LRU_GATE_GROUPS, per, LRU_BLOCK_W, LRU_BLOCK_W)
        eye = jnp.eye(per, dtype=a.dtype)
        a = a[:, :, :, :, None, :] * eye[None, None, :, None, :, None]
        return a.reshape(depth, LRU_GATE_GROUPS, V7X_MXU_DIM, V7X_MXU_DIM).astype(BF16)

    w = {
        "w_in": _regroup_w_in(p["w_in"]),
        "w_out": p["w_out"],
        "lru_wa": block_diag(p["lru_wa"]),
        "lru_wx": block_diag(p["lru_wx"]),
        "ssd_dt_bias": pad_heads(p["ssd_dt_bias"]),
        "ssd_a_log": pad_heads(p["ssd_a_log"]),
        "ssd_d_cols": jnp.repeat(p["ssd_d"], SSD_HEAD_DIM, axis=-1).reshape(depth, 1, D_SSD),
        "head_spread": (jnp.arange(DT_PAD)[:, None] == jnp.arange(D_SSD)[None, :] // SSD_HEAD_DIM
                        ).astype(BF16),
    }
    for name in ("ssd_conv_w", "lru_conv_w"):
        w[name] = jnp.repeat(p[name], V7X_SUBLANES, axis=1)
    for name in ("ssd_conv_b", "lru_conv_b"):
        w[name] = jnp.repeat(p[name][:, None, :], V7X_SUBLANES, axis=1)
    for name in ("ffn1_wg", "ffn1_wu", "ffn1_wd", "ffn2_wg", "ffn2_wu", "ffn2_wd"):
        w[name] = p[name]
    for name in ("ffn1_pre_g", "ffn1_post_g", "mix_pre_g", "mix_post_g", "ffn2_pre_g", "ffn2_post_g",
                 "ssd_norm_g", "lru_ba", "lru_bx", "lru_lambda"):
        w[name] = vec(p[name])
    return w


def _segment_mixer(x, l, w, states, ssd_stack, *, bsub, steps, chunk, pad_steps, seqs, n_sub=1):
    (sconv0, sconv0_map), (lconv0, lconv0_map), (h_lru0, h_lru0_map), (h_ssd0, h_ssd0_map) = states
    length = x.shape[1]
    z, xbc, dt, y_lru, sconv, lconv, h_lru = _in_proj_call(
        x, l, w, sconv0, sconv0_map, lconv0, lconv0_map,
        h_lru0, h_lru0_map, bsub=bsub, steps=steps, pad_steps=pad_steps, n_sub=n_sub)
    y_ssd, ssd_stack = _ssd_call(xbc, dt, z, l, w, h_ssd0, h_ssd0_map, ssd_stack,
                                 chunk=chunk, out_len=length, seqs=seqs)
    return y_ssd, y_lru, (sconv, h_lru, lconv), ssd_stack


def kernel(x_prompt, x_sample, state_ssd, state_ssd_conv, state_lru, state_lru_conv, meta_tokens,
           ffn1_pre_g, ffn1_post_g, ffn1_wg, ffn1_wu, ffn1_wd, mix_pre_g, mix_post_g, w_in,
           ssd_conv_w, ssd_conv_b, ssd_dt_bias, ssd_a_log, ssd_d, ssd_norm_g, lru_conv_w, lru_conv_b,
           lru_wa, lru_ba, lru_wx, lru_bx, lru_lambda, w_out, ffn2_pre_g, ffn2_post_g, ffn2_wg,
           ffn2_wu, ffn2_wd):
    bp, seq, _ = x_prompt.shape
    bs, dec_seq, _ = x_sample.shape
    depth = w_in.shape[0]
    assert bp == V7X_SUBLANES and seq % SSD_CHUNK == 0 and bs % SAMPLE_BATCH_TILE == 0
    w = _prepare_weights(dict(
        ffn1_pre_g=ffn1_pre_g, ffn1_post_g=ffn1_post_g, ffn1_wg=ffn1_wg, ffn1_wu=ffn1_wu,
        ffn1_wd=ffn1_wd, mix_pre_g=mix_pre_g, mix_post_g=mix_post_g, w_in=w_in,
        ssd_conv_w=ssd_conv_w, ssd_conv_b=ssd_conv_b, ssd_dt_bias=ssd_dt_bias, ssd_a_log=ssd_a_log,
        ssd_d=ssd_d, ssd_norm_g=ssd_norm_g, lru_conv_w=lru_conv_w, lru_conv_b=lru_conv_b,
        lru_wa=lru_wa, lru_ba=lru_ba, lru_wx=lru_wx, lru_bx=lru_bx, lru_lambda=lru_lambda,
        w_out=w_out, ffn2_pre_g=ffn2_pre_g, ffn2_post_g=ffn2_post_g, ffn2_wg=ffn2_wg,
        ffn2_wu=ffn2_wu, ffn2_wd=ffn2_wd))

    xm = jnp.broadcast_to(meta_tokens.astype(F32)[None], (bp, N_META, D_MODEL))
    xp = x_prompt
    xs = x_sample
    sample_steps = ((dec_seq + V7X_SUBLANES - 1) // V7X_SUBLANES) * V7X_SUBLANES
    tail3 = lambda b, t: (b, 0, 0)
    vec2 = lambda b, t: (b, 0)
    seq_state = lambda b, c: (b, 0, 0, 0)
    zero_states = ((jnp.zeros((bp, CONV_TAIL, SSD_CONV_DIM), F32), tail3),
                   (jnp.zeros((bp, CONV_TAIL, D_LRU), F32), tail3),
                   (jnp.zeros((bp, D_LRU), F32), vec2),
                   (jnp.zeros((bp, SSD_HEADS, SSD_HEAD_DIM, SSD_STATE), F32), seq_state))

    layer_state = lambda b, c, l: (l, b, 0, 0, 0)
    p_out = [[] for _ in range(3)]
    s_out = [[] for _ in range(3)]
    m_ssd = p_ssd = s_ssd = None
    def flat(a):
        return a.reshape(a.shape[0] * a.shape[1], a.shape[2])

    for l in range(depth):
        at_l = functools.partial(layer_state, l=l)
        shapes = (xp.shape, xs.shape, xm.shape)
        xp, xs, xm = (a.reshape(s) for a, s in
                      zip(_ffn_call([flat(xp), flat(xs), flat(xm)], l, w, "ffn1"), shapes))
        m_ssd_y, m_lru_y, m_st, m_ssd = _segment_mixer(
            xm, l, w, zero_states, m_ssd,
            bsub=bp, steps=N_META, chunk=N_META, pad_steps=0, seqs=PROMPT_SEQS)
        p_ssd_y, p_lru_y, p_st, p_ssd = _segment_mixer(
            xp, l, w, ((m_st[0], tail3), (m_st[2], tail3), (m_st[1], vec2), (m_ssd, at_l)), p_ssd,
            bsub=bp, steps=PROMPT_STEPS, chunk=SSD_CHUNK, pad_steps=0, seqs=PROMPT_SEQS,
            n_sub=PROMPT_SUB_TILES)
        s_ssd_y, s_lru_y, s_st, s_ssd = _segment_mixer(
            xs, l, w, ((state_ssd_conv, lambda b, t, l=l: (l, b, 0, 0)),
                       (state_lru_conv, lambda b, t, l=l: (l, b, 0, 0)),
                       (state_lru, lambda b, t, l=l: (l, b, 0)),
                       (state_ssd, at_l)), s_ssd,
            bsub=SAMPLE_BATCH_TILE, steps=dec_seq, chunk=sample_steps,
            pad_steps=sample_steps - dec_seq, seqs=SAMPLE_SEQS)
        (xp,) = _out_ffn_call([(flat(xp), flat(p_ssd_y), flat(p_lru_y))], l, w)
        xs, xm = _out_ffn_call([(flat(xs), flat(s_ssd_y), flat(s_lru_y)),
                                (flat(xm), flat(m_ssd_y), flat(m_lru_y))], l, w)
        xp, xs, xm = (a.reshape(s) for a, s in zip((xp, xs, xm), shapes))
        for acc, st in ((p_out, p_st), (s_out, s_st)):
            for k in range(3):
                acc[k].append(st[k])

    p_conv, p_lru, p_lconv = (jnp.stack(a) for a in p_out)
    s_conv, s_lru, s_lconv = (jnp.stack(a) for a in s_out)
    return (xp, xs, p_ssd, p_conv, p_lru, p_lconv, s_ssd, s_conv, s_lru, s_lconv)
```

```python
import functools
import itertools

import jax
import jax.numpy as jnp
from jax import lax
from jax.experimental import pallas as pl
from jax.experimental.pallas import tpu as pltpu

F32 = jnp.float32
BF16 = jnp.bfloat16

D_MODEL = 1024
D_SSD = 1024
D_LRU = 1024
SSD_HEADS = 16
SSD_HEAD_DIM = 64
SSD_GROUPS = 2
SSD_STATE = 128
HEADS_PER_GROUP = SSD_HEADS // SSD_GROUPS
CONV_W = 4
CONV_TAIL = CONV_W - 1
SSD_CONV_DIM = D_SSD + 2 * SSD_GROUPS * SSD_STATE
LRU_BLOCKS = 16
LRU_BLOCK_W = D_LRU // LRU_BLOCKS
LRU_C = 8.0
EPS = 1e-6
N_META = 16
SSD_CHUNK = 128

V7X_LANES = 128
V7X_SUBLANES = 8
V7X_MXU_DIM = 256
V7X_VMEM_LIMIT_BYTES = 56 * 1024 * 1024

DT_PAD = V7X_LANES
LRU_GATE_GROUPS = D_LRU // V7X_MXU_DIM
COL_Z = 0
COL_XBC = COL_Z + D_SSD
COL_GATE = COL_XBC + SSD_CONV_DIM
COL_XR = COL_GATE + D_LRU
COL_DT = COL_XR + D_LRU
IN_COLS_PAD = COL_DT + DT_PAD
GELU_K = 0.7978845608028654

FFN_ROWS = 512
FFN_SUB_TILES = 2
OUT_FFN_ROWS = 512
PROMPT_STEPS = 64
SAMPLE_BATCH_TILE = 64
PROMPT_SUB_TILES = 2
SCAN_UNROLL_LIMIT = 128
W_IN_PREP_ROWS = 256
PROMPT_SEQS = 4
SAMPLE_SEQS = 8
ELEMENTWISE_ROWS = 16
ELEMENTWISE_COLS = 512


def _rms(x, g):
    return x * lax.rsqrt(jnp.mean(x * x, axis=-1, keepdims=True) + EPS) * g


def _silu(x):
    return x * _sigmoid(x)


def _sigmoid(x):
    return 0.5 * jnp.tanh(0.5 * x) + 0.5


def _softplus(x):
    return jnp.maximum(x, 0.0) + jnp.log1p(jnp.exp(-jnp.abs(x)))


def _gelu_tanh(x):
    return 0.5 * x * (1.0 + jnp.tanh(GELU_K * (x + 0.044715 * (x * x * x))))


def _dot(a, b):
    return jnp.dot(a, b, preferred_element_type=F32)


def _layer_spec(tail, l):
    zeros = (0,) * len(tail)
    return pl.BlockSpec((None,) + tuple(tail), lambda *_: (l,) + zeros, pipeline_mode=pl.Buffered(1))


def _params(n_axes):
    return pltpu.CompilerParams(dimension_semantics=("arbitrary",) * n_axes,
                                vmem_limit_bytes=V7X_VMEM_LIMIT_BYTES)


def _row_tile(rows, want):
    tm = min(rows, want)
    assert rows % tm == 0
    return tm


def _rows_map(i):
    return (i, 0)


def _ffn_math(x, gpre, gpost, wg_ref, wu_ref, wd_ref):
    n_sub = FFN_SUB_TILES if x.shape[0] >= FFN_ROWS else 1
    rows = x.shape[0] // n_sub
    outs = [None] * n_sub

    def sub_tile(i):
        xi = x[i * rows:(i + 1) * rows]
        xn = _rms(xi, gpre).astype(BF16)
        yield
        hg = _dot(xn, wg_ref[...])
        hu = _dot(xn, wu_ref[...])
        yield
        a = (_silu(hg) * hu).astype(BF16)
        yield
        y = _dot(a, wd_ref[...])
        yield
        outs[i] = xi + 0.5 * _rms(y, gpost)

    _run_skewed([sub_tile(i) for i in range(n_sub)])
    return outs[0] if n_sub == 1 else jnp.concatenate(outs, axis=0)


def _run_skewed(stage_generators):
    waiting, running = list(stage_generators), []
    while waiting or running:
        if waiting:
            running.append(waiting.pop(0))
        for stages in list(running):
            if next(stages, "done") == "done":
                running.remove(stages)


def _token_call(math, segments, weights, weight_specs, name, tile_rows):
    n_ops = len(segments[0])
    tiles = [_row_tile(seg[0].shape[0], tile_rows) for seg in segments]
    counts = [seg[0].shape[0] // tm for seg, tm in zip(segments, tiles)]
    starts = [sum(counts[:k]) for k in range(len(segments))]

    def seg_spec(k):
        mode = {} if counts[k] > 1 else dict(pipeline_mode=pl.Buffered(1))
        return pl.BlockSpec((tiles[k], D_MODEL),
                            lambda i: (jnp.clip(i - starts[k], 0, counts[k] - 1), 0), **mode)

    def body(*refs):
        n_in = n_ops * len(segments)
        x_refs, w_refs, o_refs = refs[:n_in], refs[n_in:n_in + len(weights)], refs[n_in + len(weights):]
        i = pl.program_id(0)

        def run(k):
            operands = [r[...] for r in x_refs[k * n_ops:(k + 1) * n_ops]]
            o_refs[k][...] = math(*operands, *w_refs)

        for k in range(len(segments)):
            pl.when((i >= starts[k]) & (i < starts[k] + counts[k]))(functools.partial(run, k))

    return pl.pallas_call(
        body,
        grid=(sum(counts),),
        in_specs=[seg_spec(k) for k in range(len(segments)) for _ in range(n_ops)] + weight_specs,
        out_specs=[seg_spec(k) for k in range(len(segments))],
        out_shape=[jax.ShapeDtypeStruct(seg[0].shape, F32) for seg in segments],
        compiler_params=_params(1),
        name=name,
    )(*[a for seg in segments for a in seg], *weights)


def _ffn_math_refs(x, gpre_ref, gpost_ref, wg_ref, wu_ref, wd_ref):
    return _ffn_math(x, gpre_ref[...], gpost_ref[...], wg_ref, wu_ref, wd_ref)


def _ffn_call(xs, l, w, prefix):
    d_ff = w[prefix + "_wg"].shape[-1]
    return _token_call(
        _ffn_math_refs, [(x,) for x in xs],
        [w[prefix + "_pre_g"], w[prefix + "_post_g"], w[prefix + "_wg"], w[prefix + "_wu"],
         w[prefix + "_wd"]],
        [_layer_spec((1, D_MODEL), l), _layer_spec((1, D_MODEL), l),
         _layer_spec((D_MODEL, d_ff), l), _layer_spec((D_MODEL, d_ff), l),
         _layer_spec((d_ff, D_MODEL), l)],
        prefix, FFN_ROWS)


def _out_ffn_math(x, ys, yl, wos_ref, wol_ref, gmix_ref, gpre_ref, gpost_ref, wg_ref, wu_ref, wd_ref):
    m = _dot(ys.astype(BF16), wos_ref[...]) + _dot(yl.astype(BF16), wol_ref[...])
    x1 = x + _rms(m, gmix_ref[...])
    return _ffn_math(x1, gpre_ref[...], gpost_ref[...], wg_ref, wu_ref, wd_ref)


def _out_ffn_call(segments, l, w):
    d_ff = w["ffn2_wg"].shape[-1]
    return _token_call(
        _out_ffn_math, segments,
        [w["w_out"], w["w_out"], w["mix_post_g"], w["ffn2_pre_g"], w["ffn2_post_g"],
         w["ffn2_wg"], w["ffn2_wu"], w["ffn2_wd"]],
        [pl.BlockSpec((None, D_SSD, D_MODEL), lambda i: (l, 0, 0), pipeline_mode=pl.Buffered(1)),
         pl.BlockSpec((None, D_LRU, D_MODEL), lambda i: (l, D_SSD // D_LRU, 0),
                      pipeline_mode=pl.Buffered(1)),
         _layer_spec((1, D_MODEL), l), _layer_spec((1, D_MODEL), l), _layer_spec((1, D_MODEL), l),
         _layer_spec((D_MODEL, d_ff), l), _layer_spec((D_MODEL, d_ff), l),
         _layer_spec((d_ff, D_MODEL), l)],
        "out_ffn2", OUT_FFN_ROWS)


def _in_proj_body(bsub, steps, pad_steps, n_sub,
                  x_ref, g_ref, w_ref, scw_ref, scb_ref, lcw_ref, lcb_ref, dtb_ref,
                  wa_ref, wx_ref, ba_ref, bx_ref, lam_ref, sconv0_ref, lconv0_ref, h0_ref,
                  z_ref, xbc_ref, dt_ref, ylru_ref, sconv_ref, lconv_ref, hout_ref,
                  stage, xbuf, sbuf, lbuf, gate_buf, ra_buf, rx_buf, h_carry):
    rows = steps * bsub
    tail = CONV_TAIL * bsub
    sub_steps = steps // n_sub
    sub_rows = sub_steps * bsub
    rc = min(sub_rows, ELEMENTWISE_ROWS)
    strided = sub_steps >= bsub

    def lane_tile(j):
        return slice(j * V7X_LANES, (j + 1) * V7X_LANES)

    def load_block(ref, t0):
        r0 = t0 * bsub
        if not strided:
            for t in range(sub_steps):
                xbuf[r0 + t * bsub:r0 + (t + 1) * bsub, :] = ref[:, t0 + t, :]
            return xbuf[r0:r0 + sub_rows, :]
        n_tiles = ref.shape[2] // V7X_LANES
        for j in range(n_tiles):
            for b in range(bsub):
                stage[j, pl.ds(r0 + b, sub_steps, stride=bsub), :] = ref[b, t0:t0 + sub_steps, lane_tile(j)]
        return jnp.concatenate([stage[j, r0:r0 + sub_rows, :] for j in range(n_tiles)], axis=1)

    def store_block(ref, t0, read):
        r0 = t0 * bsub
        if not strided:
            value = read(slice(0, ref.shape[2]))
            for t in range(sub_steps):
                ref[:, t0 + t, :] = value[t * bsub:(t + 1) * bsub, :]
            return
        n_tiles = ref.shape[2] // V7X_LANES
        for j in range(n_tiles):
            stage[j, r0:r0 + sub_rows, :] = read(lane_tile(j))
        for j in range(n_tiles):
            for b in range(bsub):
                ref[b, t0:t0 + sub_steps, lane_tile(j)] = stage[j, pl.ds(r0 + b, sub_steps, stride=bsub), :]

    def conv_in_place(buf, w_ref, b_ref, act, r_lo):
        sub = V7X_SUBLANES
        for c0 in range(0, buf.shape[1], ELEMENTWISE_COLS):
            cols = slice(c0, c0 + ELEMENTWISE_COLS)
            taps = [w_ref[k * sub:(k + 1) * sub, cols] for k in range(CONV_W)]
            bias = b_ref[:, cols]
            for r0 in range(r_lo, r_lo + sub_rows, sub):
                acc = bias + buf[r0:r0 + sub, cols] * taps[0]
                for k in range(1, CONV_W):
                    acc = acc + buf[r0 + k * bsub:r0 + k * bsub + sub, cols] * taps[k]
                buf[r0:r0 + sub, cols] = act(acc)

    @pl.when(pl.program_id(1) == 0)
    def _():
        for k in range(CONV_TAIL):
            sbuf[k * bsub:(k + 1) * bsub, :] = sconv0_ref[:, k, :]
            lbuf[k * bsub:(k + 1) * bsub, :] = lconv0_ref[:, k, :]
        h_carry[...] = h0_ref[...]

    neg_c_softplus = -LRU_C * _softplus(-lam_ref[...])

    def sub_tile(i):
        t0 = i * sub_steps
        r_lo = t0 * bsub
        out = slice(r_lo, r_lo + sub_rows)
        pre = slice(tail + r_lo, tail + r_lo + sub_rows)
        xn = _rms(load_block(x_ref, t0), g_ref[...]).astype(BF16)
        yield
        lbuf[pre, :] = _dot(xn, w_ref[:, COL_XR:COL_DT])
        gate_buf[out, :] = _dot(xn, w_ref[:, COL_GATE:COL_XR])
        yield
        conv_in_place(lbuf, lcw_ref, lcb_ref, lambda v: v, r_lo)
        yield
        for q in range(LRU_GATE_GROUPS):
            cols = slice(q * V7X_MXU_DIM, (q + 1) * V7X_MXU_DIM)
            xr_bf = lbuf[out, cols].astype(BF16)
            ra_buf[out, cols] = _dot(xr_bf, wa_ref[q])
            rx_buf[out, cols] = _dot(xr_bf, wx_ref[q])
        yield
        for c0 in range(0, D_LRU, ELEMENTWISE_COLS):
            cols = slice(c0, c0 + ELEMENTWISE_COLS)
            ncs, ba, bx = neg_c_softplus[:, cols], ba_ref[:, cols], bx_ref[:, cols]
            for r0 in range(r_lo, r_lo + sub_rows, rc):
                sl = slice(r0, r0 + rc)
                log_a = ncs * _sigmoid(ra_buf[sl, cols] + ba)
                a = jnp.exp(log_a)
                m = -jnp.tanh(log_a) * (a * a + 1.0)
                mult = jnp.where(m > 0.0, m * lax.rsqrt(m), 0.0)
                rx_buf[sl, cols] = mult * _sigmoid(rx_buf[sl, cols] + bx) * lbuf[sl, cols]
                ra_buf[sl, cols] = a
                gate_buf[sl, cols] = _gelu_tanh(gate_buf[sl, cols])
        yield
        z = _dot(xn, w_ref[:, COL_Z:COL_XBC])
        sbuf[pre, :] = _dot(xn, w_ref[:, COL_XBC:COL_GATE])
        dt = _softplus(_dot(xn, w_ref[:, COL_DT:IN_COLS_PAD]) + dtb_ref[...])
        yield
        store_block(z_ref, t0, lambda cols: z[:, cols])
        store_block(dt_ref, t0, lambda cols: dt[:, cols])
        conv_in_place(sbuf, scw_ref, scb_ref, _silu, r_lo)
        yield
        store_block(xbc_ref, t0, lambda cols: sbuf[out, cols])

    _run_skewed([sub_tile(i) for i in range(n_sub)])

    for k in range(CONV_TAIL):
        sconv_ref[:, k, :] = sbuf[rows + k * bsub:rows + (k + 1) * bsub, :]
        lconv_ref[:, k, :] = lbuf[rows + k * bsub:rows + (k + 1) * bsub, :]

    def scan_step(b0, t, h):
        start = t * bsub + b0
        if not isinstance(start, int):
            start = pl.multiple_of(start, V7X_SUBLANES)
        sl = pl.ds(start, V7X_SUBLANES)
        h = ra_buf[sl, :] * h + rx_buf[sl, :]
        gate_buf[sl, :] = h * gate_buf[sl, :]
        return h

    n_groups = bsub // V7X_SUBLANES
    if n_groups * steps <= SCAN_UNROLL_LIMIT:
        for bg in range(n_groups):
            hsl = slice(bg * V7X_SUBLANES, (bg + 1) * V7X_SUBLANES)
            h = h_carry[hsl, :]
            for t in range(steps):
                h = scan_step(bg * V7X_SUBLANES, t, h)
            h_carry[hsl, :] = h
    else:
        def group_scan(bg, carry):
            b0 = pl.multiple_of(bg * V7X_SUBLANES, V7X_SUBLANES)
            hsl = pl.ds(b0, V7X_SUBLANES)
            h_carry[hsl, :] = lax.fori_loop(0, steps, functools.partial(scan_step, b0), h_carry[hsl, :])
            return carry

        lax.fori_loop(0, n_groups, group_scan, 0)
    for i in range(n_sub):
        store_block(ylru_ref, i * sub_steps,
                    lambda cols, i=i: gate_buf[i * sub_rows:(i + 1) * sub_rows, cols])
    hout_ref[...] = h_carry[...]
    sbuf[0:tail, :] = sbuf[rows:rows + tail, :]
    lbuf[0:tail, :] = lbuf[rows:rows + tail, :]
    for t in range(steps, steps + pad_steps):
        for ref in (z_ref, xbc_ref, dt_ref):
            ref[:, t, :] = jnp.zeros((bsub, ref.shape[2]), F32)


def _in_proj_call(x, l, w, sconv0, sconv0_map, lconv0, lconv0_map, h0, h0_map, *,
                  bsub, steps, pad_steps=0, n_sub=1):
    batch, length, _ = x.shape
    n_t = length // steps
    n_b = batch // bsub
    assert length == n_t * steps and batch == n_b * bsub and bsub % V7X_SUBLANES == 0
    assert pad_steps == 0 or n_t == 1
    assert steps >= CONV_TAIL
    assert steps % n_sub == 0
    strided = steps // n_sub >= bsub
    out_steps = steps + pad_steps
    rows = steps * bsub
    tail = CONV_TAIL * bsub

    def tile(n_steps, width):
        return pl.BlockSpec((bsub, n_steps, width), lambda b, t: (b, t, 0))

    def squeeze_lead(a, block):
        return (None,) * (a.ndim - len(block)) + block

    def conv_out(width):
        return pl.BlockSpec((bsub, CONV_TAIL, width), lambda b, t: (b, 0, 0))

    return pl.pallas_call(
        functools.partial(_in_proj_body, bsub, steps, pad_steps, n_sub),
        grid=(n_b, n_t),
        in_specs=[
            tile(steps, D_MODEL),
            _layer_spec((1, D_MODEL), l),
            _layer_spec((D_MODEL, IN_COLS_PAD), l),
            _layer_spec((CONV_W * V7X_SUBLANES, SSD_CONV_DIM), l),
            _layer_spec((V7X_SUBLANES, SSD_CONV_DIM), l),
            _layer_spec((CONV_W * V7X_SUBLANES, D_LRU), l), _layer_spec((V7X_SUBLANES, D_LRU), l),
            _layer_spec((1, DT_PAD), l),
            _layer_spec((LRU_GATE_GROUPS, V7X_MXU_DIM, V7X_MXU_DIM), l),
            _layer_spec((LRU_GATE_GROUPS, V7X_MXU_DIM, V7X_MXU_DIM), l),
            _layer_spec((1, D_LRU), l), _layer_spec((1, D_LRU), l), _layer_spec((1, D_LRU), l),
            pl.BlockSpec(squeeze_lead(sconv0, (bsub, CONV_TAIL, SSD_CONV_DIM)), sconv0_map),
            pl.BlockSpec(squeeze_lead(lconv0, (bsub, CONV_TAIL, D_LRU)), lconv0_map),
            pl.BlockSpec(squeeze_lead(h0, (bsub, D_LRU)), h0_map),
        ],
        out_specs=[
            tile(out_steps, D_SSD), tile(out_steps, SSD_CONV_DIM), tile(out_steps, DT_PAD),
            tile(steps, D_LRU),
            conv_out(SSD_CONV_DIM), conv_out(D_LRU),
            pl.BlockSpec((bsub, D_LRU), lambda b, t: (b, 0)),
        ],
        out_shape=[
            jax.ShapeDtypeStruct((batch, n_t * out_steps, D_SSD), F32),
            jax.ShapeDtypeStruct((batch, n_t * out_steps, SSD_CONV_DIM), F32),
            jax.ShapeDtypeStruct((batch, n_t * out_steps, DT_PAD), F32),
            jax.ShapeDtypeStruct((batch, length, D_LRU), F32),
            jax.ShapeDtypeStruct((batch, CONV_TAIL, SSD_CONV_DIM), F32),
            jax.ShapeDtypeStruct((batch, CONV_TAIL, D_LRU), F32),
            jax.ShapeDtypeStruct((batch, D_LRU), F32),
        ],
        scratch_shapes=[
            pltpu.VMEM((SSD_CONV_DIM // V7X_LANES, rows, V7X_LANES) if strided
                       else (1, V7X_SUBLANES, V7X_LANES), F32),
            pltpu.VMEM((V7X_SUBLANES, V7X_LANES) if strided else (rows, D_MODEL), F32),
            pltpu.VMEM((tail + rows, SSD_CONV_DIM), F32),
            pltpu.VMEM((tail + rows, D_LRU), F32),
            pltpu.VMEM((rows, D_LRU), F32),
            pltpu.VMEM((rows, D_LRU), F32),
            pltpu.VMEM((rows, D_LRU), F32),
            pltpu.VMEM((bsub, D_LRU), F32),
        ],
        compiler_params=_params(2),
        name="in_proj_lru",
    )(x, w["mix_pre_g"], w["w_in"], w["ssd_conv_w"], w["ssd_conv_b"], w["lru_conv_w"], w["lru_conv_b"],
      w["ssd_dt_bias"], w["lru_wa"], w["lru_wx"], w["lru_ba"], w["lru_bx"], w["lru_lambda"],
      sconv0, lconv0, h0)


def _transpose_rows(x, rows):
    lanes = x.shape[1]
    if rows < lanes:
        x = jnp.concatenate([x, jnp.zeros((lanes - rows, lanes), x.dtype)], axis=0)
    return x.T[:, 0:rows]


def _split_bf16(x, terms):
    parts = []
    for _ in range(terms - 1):
        parts.append(x.astype(BF16))
        x = x - parts[-1].astype(F32)
    return parts + [x.astype(BF16)]


def _ssd_body(chunk, n_chunks, out_steps, seqs, first_layer, xbc_ref, dt_ref, z_ref, alog_ref, dvec_ref,
              g_ref, e_ref, h0_ref, *rest):
    (y_ref, hout_ref, ht_s) = rest[-3:]
    if first_layer is not None:
        @pl.when(pl.program_id(1) == 0)
        def _():
            for k in range(hout_ref.shape[0]):
                if k != first_layer:
                    hout_ref[k] = jnp.zeros(hout_ref.shape[1:], F32)
        hout_ref = hout_ref.at[first_layer]
    stages = [_ssd_sequence(chunk, n_chunks, out_steps, xbc_ref.at[s], dt_ref.at[s], z_ref.at[s],
                            alog_ref, dvec_ref, g_ref, e_ref, h0_ref.at[s], y_ref.at[s],
                            hout_ref.at[s], ht_s.at[s]) for s in range(seqs)]
    for _ in itertools.zip_longest(*stages):
        pass


def _ssd_sequence(chunk, n_chunks, out_steps, xbc_ref, dt_ref, z_ref, alog_ref, dvec_ref, g_ref, e_ref,
                  h0_ref, y_ref, hout_ref, ht_s):
    c = pl.program_id(1)
    group_cols = HEADS_PER_GROUP * SSD_HEAD_DIM

    single = n_chunks == 1
    if single:
        h_given = h0_ref[...].reshape(D_SSD, SSD_STATE)
        ht_bf = h_given.astype(BF16).T
    else:
        @pl.when(c == 0)
        def _():
            ht_s[...] = h0_ref[...].reshape(D_SSD, SSD_STATE).T

    dt = dt_ref[...]
    a = -jnp.exp(alog_ref[...])
    row = lax.broadcasted_iota(jnp.int32, (chunk, chunk), 0)
    col = lax.broadcasted_iota(jnp.int32, (chunk, chunk), 1)
    causal = row >= col
    ones_lower = jnp.where(causal, 1.0, 0.0).astype(BF16)
    da_hi, da_mid, da_lo = _split_bf16(dt * a, 3)
    cum = _dot(ones_lower, da_hi) + _dot(ones_lower, da_mid) + _dot(ones_lower, da_lo)
    yield
    cum_t = _transpose_rows(cum, chunk)
    dt_t = _transpose_rows(dt, chunk)
    tot = cum[chunk - 1:chunk, :]
    cdec = jnp.exp(tot)
    factors = jnp.concatenate([jnp.exp(cum), jnp.exp(tot - cum) * dt], axis=0)
    spread = _dot(factors.astype(BF16), e_ref[...])
    yield
    ecum_x = spread[0:chunk, :]
    wend_x = spread[chunk:2 * chunk, :]
    if not single:
        cdec_parts = _split_bf16(jnp.broadcast_to(cdec, (V7X_SUBLANES, DT_PAD)), 3)
        cdec_x = (_dot(cdec_parts[0], e_ref[...]) + _dot(cdec_parts[1], e_ref[...])
                  + _dot(cdec_parts[2], e_ref[...]))[0:1, :]
    low_half = lax.broadcasted_iota(jnp.int32, (chunk, V7X_LANES), 1) < SSD_HEAD_DIM

    y_groups = []
    for g in range(SSD_GROUPS):
        b0 = D_SSD + g * SSD_STATE
        c0 = D_SSD + SSD_GROUPS * SSD_STATE + g * SSD_STATE
        gcols = slice(g * group_cols, (g + 1) * group_cols)
        bg_t = _transpose_rows(xbc_ref[:, b0:b0 + SSD_STATE], chunk).astype(BF16)
        cg = xbc_ref[:, c0:c0 + SSD_STATE].astype(BF16)
        cb = _dot(cg, bg_t)
        yield
        y_pairs = []
        for pair in range(HEADS_PER_GROUP // 2):
            h = g * HEADS_PER_GROUP + 2 * pair
            x_pair = xbc_ref[:, h * SSD_HEAD_DIM:(h + 2) * SSD_HEAD_DIM]
            scores = []
            for hh in (h, h + 1):
                diff = cum[:, hh:hh + 1] - cum_t[hh:hh + 1, :]
                decay = jnp.exp(jnp.where(causal, diff, -jnp.inf))
                scores.append((cb * decay * dt_t[hh:hh + 1, :]).astype(BF16))
            x_lo = jnp.where(low_half, x_pair, 0.0).astype(BF16)
            x_hi = jnp.where(low_half, 0.0, x_pair).astype(BF16)
            if chunk % V7X_LANES == 0:
                y_pairs.append(_dot(jnp.concatenate(scores, axis=1),
                                    jnp.concatenate([x_lo, x_hi], axis=0)))
            else:
                y_pairs.append(_dot(scores[0], x_lo) + _dot(scores[1], x_hi))
            yield
        xg = xbc_ref[:, gcols]
        xw = (xg * wend_x[:, gcols]).astype(BF16)
        if single:
            y_off = _dot(cg, ht_bf[:, gcols]) * ecum_x[:, gcols]
            bg = xbc_ref[:, b0:b0 + SSD_STATE].astype(BF16)
            update = lax.dot_general(xw, bg, (((0,), (0,)), ((), ())), preferred_element_type=F32)
            heads = range(g * HEADS_PER_GROUP, (g + 1) * HEADS_PER_GROUP)
            decay_rows = jnp.concatenate(
                [jnp.broadcast_to(cdec[:, h:h + 1], (SSD_HEAD_DIM, SSD_STATE)) for h in heads], axis=0)
            h_new = decay_rows * h_given[gcols, :] + update
            hout_ref[g * HEADS_PER_GROUP:(g + 1) * HEADS_PER_GROUP] = h_new.reshape(
                HEADS_PER_GROUP, SSD_HEAD_DIM, SSD_STATE)
        else:
            ht_prev = ht_s[:, gcols]
            y_off = _dot(cg, ht_prev.astype(BF16)) * ecum_x[:, gcols]
            ht_s[:, gcols] = cdec_x[:, gcols] * ht_prev + _dot(bg_t, xw)
        y_groups.append(jnp.concatenate(y_pairs, axis=1) + y_off + dvec_ref[:, gcols] * xg)
        yield

    y = _rms(jnp.concatenate(y_groups, axis=1) * _silu(z_ref[...]), g_ref[...])
    y_ref[...] = y[0:out_steps, :]

    if not single:
        @pl.when(c == n_chunks - 1)
        def _():
            hout_ref[...] = ht_s[...].T.reshape(SSD_HEADS, SSD_HEAD_DIM, SSD_STATE)


def _ssd_call(xbc, dt, z, l, w, h0, h0_map, h_stack, *, chunk, out_len, seqs):
    batch, length, _ = xbc.shape
    depth = w["ssd_a_log"].shape[0]
    n_chunks = length // chunk
    assert length == n_chunks * chunk and (out_len == length or n_chunks == 1) and batch % seqs == 0
    out_steps = min(chunk, out_len)

    def per_seq(n_steps, width):
        return pl.BlockSpec((seqs, n_steps, width), lambda b, c: (b, c, 0))

    state_tail = (SSD_HEADS, SSD_HEAD_DIM, SSD_STATE)
    h0_block = (None,) * (h0.ndim - 4) + (seqs,) + state_tail
    operands = [xbc, dt, z, w["ssd_a_log"], w["ssd_d_cols"], w["ssd_norm_g"], w["head_spread"], h0]
    in_specs = [
        per_seq(chunk, SSD_CONV_DIM), per_seq(chunk, DT_PAD), per_seq(chunk, D_SSD),
        _layer_spec((1, DT_PAD), l), _layer_spec((1, D_SSD), l), _layer_spec((1, D_SSD), l),
        pl.BlockSpec((DT_PAD, D_SSD), lambda b, c: (0, 0), pipeline_mode=pl.Buffered(1)),
        pl.BlockSpec(h0_block, h0_map),
    ]
    aliases = {}
    if h_stack is not None:
        aliases = {len(operands): 1}
        operands.append(h_stack)
        in_specs.append(pl.BlockSpec(memory_space=pl.ANY))
    first = h_stack is None
    return pl.pallas_call(
        functools.partial(_ssd_body, chunk, n_chunks, out_steps, seqs, l if first else None),
        grid=(batch // seqs, n_chunks),
        in_specs=in_specs,
        out_specs=[
            per_seq(out_steps, D_SSD),
            pl.BlockSpec((depth, seqs) + state_tail, lambda b, c: (0, b, 0, 0, 0)) if first
            else pl.BlockSpec((None, seqs) + state_tail, lambda b, c: (l, b, 0, 0, 0)),
        ],
        out_shape=[
            jax.ShapeDtypeStruct((batch, out_len, D_SSD), F32),
            jax.ShapeDtypeStruct((depth, batch) + state_tail, F32),
        ],
        scratch_shapes=[pltpu.VMEM((seqs, SSD_STATE, D_SSD), F32)],
        input_output_aliases=aliases,
        compiler_params=_params(2),
        name="ssd_chunk",
    )(*operands)


def _regroup_w_in_body(w_ref, o_ref):
    o_dt = COL_XBC + SSD_CONV_DIM
    o_gate = o_dt + SSD_HEADS
    o_ref[:, COL_Z:COL_GATE] = w_ref[:, COL_Z:o_dt].astype(BF16)
    o_ref[:, COL_GATE:COL_DT] = w_ref[:, o_gate:o_gate + D_LRU + D_LRU].astype(BF16)
    dt_tile = w_ref[:, o_dt:o_dt + DT_PAD]
    lane = lax.broadcasted_iota(jnp.int32, dt_tile.shape, 1)
    o_ref[:, COL_DT:IN_COLS_PAD] = jnp.where(lane < SSD_HEADS, dt_tile, 0.0).astype(BF16)


def _regroup_w_in(w_in):
    depth, d_model, in_cols = w_in.shape
    assert in_cols == D_SSD + SSD_CONV_DIM + SSD_HEADS + 2 * D_LRU
    rows = _row_tile(d_model, W_IN_PREP_ROWS)
    return pl.pallas_call(
        _regroup_w_in_body,
        grid=(depth, d_model // rows),
        in_specs=[pl.BlockSpec((None, rows, in_cols), lambda l, i: (l, i, 0))],
        out_specs=pl.BlockSpec((None, rows, IN_COLS_PAD), lambda l, i: (l, i, 0)),
        out_shape=jax.ShapeDtypeStruct((depth, d_model, IN_COLS_PAD), BF16),
        compiler_params=_params(2),
        name="regroup_w_in",
    )(w_in)


def _prepare_weights(p):
    depth = p["w_in"].shape[0]

    def vec(a):
        return a.reshape(depth, 1, a.shape[-1])

    def pad_heads(a):
        return jnp.pad(a, ((0, 0), (0, DT_PAD - SSD_HEADS))).reshape(depth, 1, DT_PAD)

    def block_diag(a):
        per = V7X_MXU_DIM // LRU_BLOCK_W
        a = a.reshape(depth, LRU_GATE_GROUPS, per, LRU_BLOCK_W, LRU_BLOCK_W)
        eye = jnp.eye(per, dtype=a.dtype)
        a = a[:, :, :, :, None, :] * eye[None, None, :, None, :, None]
        return a.reshape(depth, LRU_GATE_GROUPS, V7X_MXU_DIM, V7X_MXU_DIM).astype(BF16)

    w = {
        "w_in": _regroup_w_in(p["w_in"]),
        "w_out": p["w_out"].astype(BF16),
        "lru_wa": block_diag(p["lru_wa"]),
        "lru_wx": block_diag(p["lru_wx"]),
        "ssd_dt_bias": pad_heads(p["ssd_dt_bias"]),
        "ssd_a_log": pad_heads(p["ssd_a_log"]),
        "ssd_d_cols": jnp.repeat(p["ssd_d"], SSD_HEAD_DIM, axis=-1).reshape(depth, 1, D_SSD),
        "head_spread": (jnp.arange(DT_PAD)[:, None] == jnp.arange(D_SSD)[None, :] // SSD_HEAD_DIM
                        ).astype(BF16),
    }
    for name in ("ssd_conv_w", "lru_conv_w"):
        w[name] = jnp.repeat(p[name], V7X_SUBLANES, axis=1)
    for name in ("ssd_conv_b", "lru_conv_b"):
        w[name] = jnp.repeat(p[name][:, None, :], V7X_SUBLANES, axis=1)
    for name in ("ffn1_wg", "ffn1_wu", "ffn1_wd", "ffn2_wg", "ffn2_wu", "ffn2_wd"):
        w[name] = p[name].astype(BF16)
    for name in ("ffn1_pre_g", "ffn1_post_g", "mix_pre_g", "mix_post_g", "ffn2_pre_g", "ffn2_post_g",
                 "ssd_norm_g", "lru_ba", "lru_bx", "lru_lambda"):
        w[name] = vec(p[name])
    return w


def _segment_mixer(x, l, w, states, ssd_stack, *, bsub, steps, chunk, pad_steps, seqs, n_sub=1):
    (sconv0, sconv0_map), (lconv0, lconv0_map), (h_lru0, h_lru0_map), (h_ssd0, h_ssd0_map) = states
    length = x.shape[1]
    z, xbc, dt, y_lru, sconv, lconv, h_lru = _in_proj_call(
        x, l, w, sconv0, sconv0_map, lconv0, lconv0_map,
        h_lru0, h_lru0_map, bsub=bsub, steps=steps, pad_steps=pad_steps, n_sub=n_sub)
    y_ssd, ssd_stack = _ssd_call(xbc, dt, z, l, w, h_ssd0, h_ssd0_map, ssd_stack,
                                 chunk=chunk, out_len=length, seqs=seqs)
    return y_ssd, y_lru, (sconv, h_lru, lconv), ssd_stack


def kernel(x_prompt, x_sample, state_ssd, state_ssd_conv, state_lru, state_lru_conv, meta_tokens,
           ffn1_pre_g, ffn1_post_g, ffn1_wg, ffn1_wu, ffn1_wd, mix_pre_g, mix_post_g, w_in,
           ssd_conv_w, ssd_conv_b, ssd_dt_bias, ssd_a_log, ssd_d, ssd_norm_g, lru_conv_w, lru_conv_b,
           lru_wa, lru_ba, lru_wx, lru_bx, lru_lambda, w_out, ffn2_pre_g, ffn2_post_g, ffn2_wg,
           ffn2_wu, ffn2_wd):
    bp, seq, _ = x_prompt.shape
    bs, dec_seq, _ = x_sample.shape
    depth = w_in.shape[0]
    assert bp == V7X_SUBLANES and seq % SSD_CHUNK == 0 and bs % SAMPLE_BATCH_TILE == 0
    w = _prepare_weights(dict(
        ffn1_pre_g=ffn1_pre_g, ffn1_post_g=ffn1_post_g, ffn1_wg=ffn1_wg, ffn1_wu=ffn1_wu,
        ffn1_wd=ffn1_wd, mix_pre_g=mix_pre_g, mix_post_g=mix_post_g, w_in=w_in,
        ssd_conv_w=ssd_conv_w, ssd_conv_b=ssd_conv_b, ssd_dt_bias=ssd_dt_bias, ssd_a_log=ssd_a_log,
        ssd_d=ssd_d, ssd_norm_g=ssd_norm_g, lru_conv_w=lru_conv_w, lru_conv_b=lru_conv_b,
        lru_wa=lru_wa, lru_ba=lru_ba, lru_wx=lru_wx, lru_bx=lru_bx, lru_lambda=lru_lambda,
        w_out=w_out, ffn2_pre_g=ffn2_pre_g, ffn2_post_g=ffn2_post_g, ffn2_wg=ffn2_wg,
        ffn2_wu=ffn2_wu, ffn2_wd=ffn2_wd))

    xm = jnp.broadcast_to(meta_tokens.astype(F32)[None], (bp, N_META, D_MODEL))
    xp = x_prompt
    xs = x_sample
    sample_steps = ((dec_seq + V7X_SUBLANES - 1) // V7X_SUBLANES) * V7X_SUBLANES
    tail3 = lambda b, t: (b, 0, 0)
    vec2 = lambda b, t: (b, 0)
    seq_state = lambda b, c: (b, 0, 0, 0)
    zero_states = ((jnp.zeros((bp, CONV_TAIL, SSD_CONV_DIM), F32), tail3),
                   (jnp.zeros((bp, CONV_TAIL, D_LRU), F32), tail3),
                   (jnp.zeros((bp, D_LRU), F32), vec2),
                   (jnp.zeros((bp, SSD_HEADS, SSD_HEAD_DIM, SSD_STATE), F32), seq_state))

    layer_state = lambda b, c, l: (l, b, 0, 0, 0)
    p_out = [[] for _ in range(3)]
    s_out = [[] for _ in range(3)]
    m_ssd = p_ssd = s_ssd = None
    def flat(a):
        return a.reshape(a.shape[0] * a.shape[1], a.shape[2])

    for l in range(depth):
        at_l = functools.partial(layer_state, l=l)
        shapes = (xp.shape, xs.shape, xm.shape)
        xp, xs, xm = (a.reshape(s) for a, s in
                      zip(_ffn_call([flat(xp), flat(xs), flat(xm)], l, w, "ffn1"), shapes))
        m_ssd_y, m_lru_y, m_st, m_ssd = _segment_mixer(
            xm, l, w, zero_states, m_ssd,
            bsub=bp, steps=N_META, chunk=N_META, pad_steps=0, seqs=PROMPT_SEQS)
        p_ssd_y, p_lru_y, p_st, p_ssd = _segment_mixer(
            xp, l, w, ((m_st[0], tail3), (m_st[2], tail3), (m_st[1], vec2), (m_ssd, at_l)), p_ssd,
            bsub=bp, steps=PROMPT_STEPS, chunk=SSD_CHUNK, pad_steps=0, seqs=PROMPT_SEQS,
            n_sub=PROMPT_SUB_TILES)
        s_ssd_y, s_lru_y, s_st, s_ssd = _segment_mixer(
            xs, l, w, ((state_ssd_conv, lambda b, t, l=l: (l, b, 0, 0)),
                       (state_lru_conv, lambda b, t, l=l: (l, b, 0, 0)),
                       (state_lru, lambda b, t, l=l: (l, b, 0)),
                       (state_ssd, at_l)), s_ssd,
            bsub=SAMPLE_BATCH_TILE, steps=dec_seq, chunk=sample_steps,
            pad_steps=sample_steps - dec_seq, seqs=SAMPLE_SEQS)
        (xp,) = _out_ffn_call([(flat(xp), flat(p_ssd_y), flat(p_lru_y))], l, w)
        xs, xm = _out_ffn_call([(flat(xs), flat(s_ssd_y), flat(s_lru_y)),
                                (flat(xm), flat(m_ssd_y), flat(m_lru_y))], l, w)
        xp, xs, xm = (a.reshape(s) for a, s in zip((xp, xs, xm), shapes))
        for acc, st in ((p_out, p_st), (s_out, s_st)):
            for k in range(3):
                acc[k].append(st[k])

    p_conv, p_lru, p_lconv = (jnp.stack(a) for a in p_out)
    s_conv, s_lru, s_lconv = (jnp.stack(a) for a in s_out)
    return (xp, xs, p_ssd, p_conv, p_lru, p_lconv, s_ssd, s_conv, s_lru, s_lconv)
```

```python
import functools
import itertools

import jax
import jax.numpy as jnp
from jax import lax
from jax.experimental import pallas as pl
from jax.experimental.pallas import tpu as pltpu

F32 = jnp.float32
BF16 = jnp.bfloat16

D_MODEL = 1024
D_SSD = 1024
D_LRU = 1024
SSD_HEADS = 16
SSD_HEAD_DIM = 64
SSD_GROUPS = 2
SSD_STATE = 128
HEADS_PER_GROUP = SSD_HEADS // SSD_GROUPS
CONV_W = 4
CONV_TAIL = CONV_W - 1
SSD_CONV_DIM = D_SSD + 2 * SSD_GROUPS * SSD_STATE
LRU_BLOCKS = 16
LRU_BLOCK_W = D_LRU // LRU_BLOCKS
LRU_C = 8.0
EPS = 1e-6
N_META = 16
SSD_CHUNK = 128

V7X_LANES = 128
V7X_SUBLANES = 8
V7X_MXU_DIM = 256
V7X_VMEM_LIMIT_BYTES = 56 * 1024 * 1024

DT_PAD = V7X_LANES
LRU_GATE_GROUPS = D_LRU // V7X_MXU_DIM
COL_Z = 0
COL_XBC = COL_Z + D_SSD
COL_GATE = COL_XBC + SSD_CONV_DIM
COL_XR = COL_GATE + D_LRU
COL_DT = COL_XR + D_LRU
IN_COLS_PAD = COL_DT + DT_PAD
GELU_K = 0.7978845608028654

FFN_ROWS = 512
FFN_SUB_TILES = 2
OUT_FFN_ROWS = 512
PROMPT_STEPS = 64
SAMPLE_BATCH_TILE = 64
PROMPT_SUB_TILES = 2
SCAN_UNROLL_LIMIT = 128
W_IN_PREP_ROWS = 256
PROMPT_SEQS = 4
SAMPLE_SEQS = 8
ELEMENTWISE_ROWS = 16
ELEMENTWISE_COLS = 512


def _rms(x, g):
    return x * lax.rsqrt(jnp.mean(x * x, axis=-1, keepdims=True) + EPS) * g


def _silu(x):
    return x * _sigmoid(x)


def _sigmoid(x):
    return 0.5 * jnp.tanh(0.5 * x) + 0.5


def _softplus(x):
    return jnp.maximum(x, 0.0) + jnp.log1p(jnp.exp(-jnp.abs(x)))


def _gelu_tanh(x):
    return 0.5 * x * (1.0 + jnp.tanh(GELU_K * (x + 0.044715 * (x * x * x))))


def _dot(a, b):
    return jnp.dot(a, b, preferred_element_type=F32)


def _layer_spec(tail, l):
    zeros = (0,) * len(tail)
    return pl.BlockSpec((None,) + tuple(tail), lambda *_: (l,) + zeros, pipeline_mode=pl.Buffered(1))


def _params(n_axes):
    return pltpu.CompilerParams(dimension_semantics=("arbitrary",) * n_axes,
                                vmem_limit_bytes=V7X_VMEM_LIMIT_BYTES)


def _row_tile(rows, want):
    tm = min(rows, want)
    assert rows % tm == 0
    return tm


def _ffn_math(x, gpre, gpost, wg_ref, wu_ref, wd_ref):
    n_sub = FFN_SUB_TILES if x.shape[0] >= FFN_ROWS else 1
    rows = x.shape[0] // n_sub
    outs = [None] * n_sub

    def sub_tile(i):
        xi = x[i * rows:(i + 1) * rows]
        xn = _rms(xi, gpre).astype(BF16)
        yield
        hg = _dot(xn, wg_ref[...])
        hu = _dot(xn, wu_ref[...])
        yield
        a = (_silu(hg) * hu).astype(BF16)
        yield
        y = _dot(a, wd_ref[...])
        yield
        outs[i] = xi + 0.5 * _rms(y, gpost)

    _run_skewed([sub_tile(i) for i in range(n_sub)])
    return outs[0] if n_sub == 1 else jnp.concatenate(outs, axis=0)


def _run_skewed(stage_generators):
    waiting, running = list(stage_generators), []
    while waiting or running:
        if waiting:
            running.append(waiting.pop(0))
        for stages in list(running):
            if next(stages, "done") == "done":
                running.remove(stages)


def _token_call(math, segments, weights, weight_specs, name, tile_rows):
    n_ops = len(segments[0])
    tiles = [_row_tile(seg[0].shape[0], tile_rows) for seg in segments]
    counts = [seg[0].shape[0] // tm for seg, tm in zip(segments, tiles)]
    starts = [sum(counts[:k]) for k in range(len(segments))]

    def seg_spec(k):
        mode = {} if counts[k] > 1 else dict(pipeline_mode=pl.Buffered(1))
        return pl.BlockSpec((tiles[k], D_MODEL),
                            lambda i: (jnp.clip(i - starts[k], 0, counts[k] - 1), 0), **mode)

    def body(*refs):
        n_in = n_ops * len(segments)
        x_refs, w_refs, o_refs = refs[:n_in], refs[n_in:n_in + len(weights)], refs[n_in + len(weights):]
        i = pl.program_id(0)

        def run(k):
            operands = [r[...] for r in x_refs[k * n_ops:(k + 1) * n_ops]]
            o_refs[k][...] = math(*operands, *w_refs)

        for k in range(len(segments)):
            pl.when((i >= starts[k]) & (i < starts[k] + counts[k]))(functools.partial(run, k))

    return pl.pallas_call(
        body,
        grid=(sum(counts),),
        in_specs=[seg_spec(k) for k in range(len(segments)) for _ in range(n_ops)] + weight_specs,
        out_specs=[seg_spec(k) for k in range(len(segments))],
        out_shape=[jax.ShapeDtypeStruct(seg[0].shape, F32) for seg in segments],
        compiler_params=_params(1),
        name=name,
    )(*[a for seg in segments for a in seg], *weights)


def _ffn_math_refs(x, gpre_ref, gpost_ref, wg_ref, wu_ref, wd_ref):
    return _ffn_math(x, gpre_ref[...], gpost_ref[...], wg_ref, wu_ref, wd_ref)


def _ffn_call(xs, l, w, prefix):
    d_ff = w[prefix + "_wg"].shape[-1]
    return _token_call(
        _ffn_math_refs, [(x,) for x in xs],
        [w[prefix + "_pre_g"], w[prefix + "_post_g"], w[prefix + "_wg"], w[prefix + "_wu"],
         w[prefix + "_wd"]],
        [_layer_spec((1, D_MODEL), l), _layer_spec((1, D_MODEL), l),
         _layer_spec((D_MODEL, d_ff), l), _layer_spec((D_MODEL, d_ff), l),
         _layer_spec((d_ff, D_MODEL), l)],
        prefix, FFN_ROWS)


def _out_ffn_math(x, ys, yl, wos_ref, wol_ref, gmix_ref, gpre_ref, gpost_ref, wg_ref, wu_ref, wd_ref):
    m = _dot(ys.astype(BF16), wos_ref[...]) + _dot(yl.astype(BF16), wol_ref[...])
    x1 = x + _rms(m, gmix_ref[...])
    return _ffn_math(x1, gpre_ref[...], gpost_ref[...], wg_ref, wu_ref, wd_ref)


def _out_ffn_call(segments, l, w):
    d_ff = w["ffn2_wg"].shape[-1]
    return _token_call(
        _out_ffn_math, segments,
        [w["w_out"], w["w_out"], w["mix_post_g"], w["ffn2_pre_g"], w["ffn2_post_g"],
         w["ffn2_wg"], w["ffn2_wu"], w["ffn2_wd"]],
        [pl.BlockSpec((None, D_SSD, D_MODEL), lambda i: (l, 0, 0), pipeline_mode=pl.Buffered(1)),
         pl.BlockSpec((None, D_LRU, D_MODEL), lambda i: (l, D_SSD // D_LRU, 0),
                      pipeline_mode=pl.Buffered(1)),
         _layer_spec((1, D_MODEL), l), _layer_spec((1, D_MODEL), l), _layer_spec((1, D_MODEL), l),
         _layer_spec((D_MODEL, d_ff), l), _layer_spec((D_MODEL, d_ff), l),
         _layer_spec((d_ff, D_MODEL), l)],
        "out_ffn2", OUT_FFN_ROWS)


def _in_proj_body(bsub, steps, pad_steps, n_sub,
                  x_ref, g_ref, w_ref, scw_ref, scb_ref, lcw_ref, lcb_ref, dtb_ref,
                  wa_ref, wx_ref, ba_ref, bx_ref, lam_ref, sconv0_ref, lconv0_ref, h0_ref,
                  z_ref, xbc_ref, dt_ref, ylru_ref, sconv_ref, lconv_ref, hout_ref,
                  stage, xbuf, sbuf, lbuf, gate_buf, ra_buf, rx_buf, h_carry):
    rows = steps * bsub
    tail = CONV_TAIL * bsub
    sub_steps = steps // n_sub
    sub_rows = sub_steps * bsub
    rc = min(sub_rows, ELEMENTWISE_ROWS)
    strided = sub_steps >= bsub

    def lane_tile(j):
        return slice(j * V7X_LANES, (j + 1) * V7X_LANES)

    def load_block(ref, t0):
        r0 = t0 * bsub
        if not strided:
            for t in range(sub_steps):
                xbuf[r0 + t * bsub:r0 + (t + 1) * bsub, :] = ref[:, t0 + t, :]
            return xbuf[r0:r0 + sub_rows, :]
        n_tiles = ref.shape[2] // V7X_LANES
        for j in range(n_tiles):
            for b in range(bsub):
                stage[j, pl.ds(r0 + b, sub_steps, stride=bsub), :] = ref[b, t0:t0 + sub_steps, lane_tile(j)]
        return jnp.concatenate([stage[j, r0:r0 + sub_rows, :] for j in range(n_tiles)], axis=1)

    def store_block(ref, t0, read):
        r0 = t0 * bsub
        if not strided:
            value = read(slice(0, ref.shape[2]))
            for t in range(sub_steps):
                ref[:, t0 + t, :] = value[t * bsub:(t + 1) * bsub, :]
            return
        n_tiles = ref.shape[2] // V7X_LANES
        for j in range(n_tiles):
            stage[j, r0:r0 + sub_rows, :] = read(lane_tile(j))
        for j in range(n_tiles):
            for b in range(bsub):
                ref[b, t0:t0 + sub_steps, lane_tile(j)] = stage[j, pl.ds(r0 + b, sub_steps, stride=bsub), :]

    def conv_in_place(buf, w_ref, b_ref, act, r_lo):
        sub = V7X_SUBLANES
        for c0 in range(0, buf.shape[1], ELEMENTWISE_COLS):
            cols = slice(c0, c0 + ELEMENTWISE_COLS)
            taps = [w_ref[k * sub:(k + 1) * sub, cols] for k in range(CONV_W)]
            bias = b_ref[:, cols]
            for r0 in range(r_lo, r_lo + sub_rows, sub):
                acc = bias + buf[r0:r0 + sub, cols] * taps[0]
                for k in range(1, CONV_W):
                    acc = acc + buf[r0 + k * bsub:r0 + k * bsub + sub, cols] * taps[k]
                buf[r0:r0 + sub, cols] = act(acc)

    @pl.when(pl.program_id(1) == 0)
    def _():
        for k in range(CONV_TAIL):
            sbuf[k * bsub:(k + 1) * bsub, :] = sconv0_ref[:, k, :]
            lbuf[k * bsub:(k + 1) * bsub, :] = lconv0_ref[:, k, :]
        h_carry[...] = h0_ref[...]

    neg_c_softplus = -LRU_C * _softplus(-lam_ref[...])

    def sub_tile(i):
        t0 = i * sub_steps
        r_lo = t0 * bsub
        out = slice(r_lo, r_lo + sub_rows)
        pre = slice(tail + r_lo, tail + r_lo + sub_rows)
        xn = _rms(load_block(x_ref, t0), g_ref[...]).astype(BF16)
        yield
        lbuf[pre, :] = _dot(xn, w_ref[:, COL_XR:COL_DT])
        gate_buf[out, :] = _dot(xn, w_ref[:, COL_GATE:COL_XR])
        yield
        conv_in_place(lbuf, lcw_ref, lcb_ref, lambda v: v, r_lo)
        yield
        for q in range(LRU_GATE_GROUPS):
            cols = slice(q * V7X_MXU_DIM, (q + 1) * V7X_MXU_DIM)
            xr_bf = lbuf[out, cols].astype(BF16)
            ra_buf[out, cols] = _dot(xr_bf, wa_ref[q])
            rx_buf[out, cols] = _dot(xr_bf, wx_ref[q])
        yield
        for c0 in range(0, D_LRU, ELEMENTWISE_COLS):
            cols = slice(c0, c0 + ELEMENTWISE_COLS)
            ncs, ba, bx = neg_c_softplus[:, cols], ba_ref[:, cols], bx_ref[:, cols]
            for r0 in range(r_lo, r_lo + sub_rows, rc):
                sl = slice(r0, r0 + rc)
                log_a = ncs * _sigmoid(ra_buf[sl, cols] + ba)
                a = jnp.exp(log_a)
                m = -jnp.tanh(log_a) * (a * a + 1.0)
                mult = jnp.where(m > 0.0, m * lax.rsqrt(m), 0.0)
                rx_buf[sl, cols] = mult * _sigmoid(rx_buf[sl, cols] + bx) * lbuf[sl, cols]
                ra_buf[sl, cols] = a
                gate_buf[sl, cols] = _gelu_tanh(gate_buf[sl, cols])
        yield
        z = _dot(xn, w_ref[:, COL_Z:COL_XBC])
        sbuf[pre, :] = _dot(xn, w_ref[:, COL_XBC:COL_GATE])
        dt = _softplus(_dot(xn, w_ref[:, COL_DT:IN_COLS_PAD]) + dtb_ref[...])
        yield
        store_block(z_ref, t0, lambda cols: z[:, cols])
        store_block(dt_ref, t0, lambda cols: dt[:, cols])
        conv_in_place(sbuf, scw_ref, scb_ref, _silu, r_lo)
        yield
        store_block(xbc_ref, t0, lambda cols: sbuf[out, cols])

    _run_skewed([sub_tile(i) for i in range(n_sub)])

    for k in range(CONV_TAIL):
        sconv_ref[:, k, :] = sbuf[rows + k * bsub:rows + (k + 1) * bsub, :]
        lconv_ref[:, k, :] = lbuf[rows + k * bsub:rows + (k + 1) * bsub, :]

    def scan_step(b0, t, h):
        start = t * bsub + b0
        if not isinstance(start, int):
            start = pl.multiple_of(start, V7X_SUBLANES)
        sl = pl.ds(start, V7X_SUBLANES)
        h = ra_buf[sl, :] * h + rx_buf[sl, :]
        gate_buf[sl, :] = h * gate_buf[sl, :]
        return h

    n_groups = bsub // V7X_SUBLANES
    if n_groups * steps <= SCAN_UNROLL_LIMIT:
        for bg in range(n_groups):
            hsl = slice(bg * V7X_SUBLANES, (bg + 1) * V7X_SUBLANES)
            h = h_carry[hsl, :]
            for t in range(steps):
                h = scan_step(bg * V7X_SUBLANES, t, h)
            h_carry[hsl, :] = h
    else:
        def group_scan(bg, carry):
            b0 = pl.multiple_of(bg * V7X_SUBLANES, V7X_SUBLANES)
            hsl = pl.ds(b0, V7X_SUBLANES)
            h_carry[hsl, :] = lax.fori_loop(0, steps, functools.partial(scan_step, b0), h_carry[hsl, :])
            return carry

        lax.fori_loop(0, n_groups, group_scan, 0)
    for i in range(n_sub):
        store_block(ylru_ref, i * sub_steps,
                    lambda cols, i=i: gate_buf[i * sub_rows:(i + 1) * sub_rows, cols])
    hout_ref[...] = h_carry[...]
    sbuf[0:tail, :] = sbuf[rows:rows + tail, :]
    lbuf[0:tail, :] = lbuf[rows:rows + tail, :]
    for t in range(steps, steps + pad_steps):
        for ref in (z_ref, xbc_ref, dt_ref):
            ref[:, t, :] = jnp.zeros((bsub, ref.shape[2]), F32)


def _in_proj_call(x, l, w, sconv0, sconv0_map, lconv0, lconv0_map, h0, h0_map, *,
                  bsub, steps, pad_steps=0, n_sub=1):
    batch, length, _ = x.shape
    n_t = length // steps
    n_b = batch // bsub
    assert length == n_t * steps and batch == n_b * bsub and bsub % V7X_SUBLANES == 0
    assert pad_steps == 0 or n_t == 1
    assert steps >= CONV_TAIL
    assert steps % n_sub == 0
    strided = steps // n_sub >= bsub
    out_steps = steps + pad_steps
    rows = steps * bsub
    tail = CONV_TAIL * bsub

    def tile(n_steps, width):
        return pl.BlockSpec((bsub, n_steps, width), lambda b, t: (b, t, 0))

    def squeeze_lead(a, block):
        return (None,) * (a.ndim - len(block)) + block

    def conv_out(width):
        return pl.BlockSpec((bsub, CONV_TAIL, width), lambda b, t: (b, 0, 0))

    return pl.pallas_call(
        functools.partial(_in_proj_body, bsub, steps, pad_steps, n_sub),
        grid=(n_b, n_t),
        in_specs=[
            tile(steps, D_MODEL),
            _layer_spec((1, D_MODEL), l),
            _layer_spec((D_MODEL, IN_COLS_PAD), l),
            _layer_spec((CONV_W * V7X_SUBLANES, SSD_CONV_DIM), l),
            _layer_spec((V7X_SUBLANES, SSD_CONV_DIM), l),
            _layer_spec((CONV_W * V7X_SUBLANES, D_LRU), l), _layer_spec((V7X_SUBLANES, D_LRU), l),
            _layer_spec((1, DT_PAD), l),
            _layer_spec((LRU_GATE_GROUPS, V7X_MXU_DIM, V7X_MXU_DIM), l),
            _layer_spec((LRU_GATE_GROUPS, V7X_MXU_DIM, V7X_MXU_DIM), l),
            _layer_spec((1, D_LRU), l), _layer_spec((1, D_LRU), l), _layer_spec((1, D_LRU), l),
            pl.BlockSpec(squeeze_lead(sconv0, (bsub, CONV_TAIL, SSD_CONV_DIM)), sconv0_map),
            pl.BlockSpec(squeeze_lead(lconv0, (bsub, CONV_TAIL, D_LRU)), lconv0_map),
            pl.BlockSpec(squeeze_lead(h0, (bsub, D_LRU)), h0_map),
        ],
        out_specs=[
            tile(out_steps, D_SSD), tile(out_steps, SSD_CONV_DIM), tile(out_steps, DT_PAD),
            tile(steps, D_LRU),
            conv_out(SSD_CONV_DIM), conv_out(D_LRU),
            pl.BlockSpec((bsub, D_LRU), lambda b, t: (b, 0)),
        ],
        out_shape=[
            jax.ShapeDtypeStruct((batch, n_t * out_steps, D_SSD), F32),
            jax.ShapeDtypeStruct((batch, n_t * out_steps, SSD_CONV_DIM), F32),
            jax.ShapeDtypeStruct((batch, n_t * out_steps, DT_PAD), F32),
            jax.ShapeDtypeStruct((batch, length, D_LRU), F32),
            jax.ShapeDtypeStruct((batch, CONV_TAIL, SSD_CONV_DIM), F32),
            jax.ShapeDtypeStruct((batch, CONV_TAIL, D_LRU), F32),
            jax.ShapeDtypeStruct((batch, D_LRU), F32),
        ],
        scratch_shapes=[
            pltpu.VMEM((SSD_CONV_DIM // V7X_LANES, rows, V7X_LANES) if strided
                       else (1, V7X_SUBLANES, V7X_LANES), F32),
            pltpu.VMEM((V7X_SUBLANES, V7X_LANES) if strided else (rows, D_MODEL), F32),
            pltpu.VMEM((tail + rows, SSD_CONV_DIM), F32),
            pltpu.VMEM((tail + rows, D_LRU), F32),
            pltpu.VMEM((rows, D_LRU), F32),
            pltpu.VMEM((rows, D_LRU), F32),
            pltpu.VMEM((rows, D_LRU), F32),
            pltpu.VMEM((bsub, D_LRU), F32),
        ],
        compiler_params=_params(2),
        name="in_proj_lru",
    )(x, w["mix_pre_g"], w["w_in"], w["ssd_conv_w"], w["ssd_conv_b"], w["lru_conv_w"], w["lru_conv_b"],
      w["ssd_dt_bias"], w["lru_wa"], w["lru_wx"], w["lru_ba"], w["lru_bx"], w["lru_lambda"],
      sconv0, lconv0, h0)


def _transpose_rows(x, rows):
    lanes = x.shape[1]
    if rows < lanes:
        x = jnp.concatenate([x, jnp.zeros((lanes - rows, lanes), x.dtype)], axis=0)
    return x.T[:, 0:rows]


def _split_bf16(x, terms):
    parts = []
    for _ in range(terms - 1):
        parts.append(x.astype(BF16))
        x = x - parts[-1].astype(F32)
    return parts + [x.astype(BF16)]


def _ssd_body(chunk, n_chunks, out_steps, seqs, first_layer, xbc_ref, dt_ref, z_ref, alog_ref, dvec_ref,
              g_ref, e_ref, h0_ref, *rest):
    (y_ref, hout_ref, ht_s) = rest[-3:]
    if first_layer is not None:
        @pl.when(pl.program_id(1) == 0)
        def _():
            for k in range(hout_ref.shape[0]):
                if k != first_layer:
                    hout_ref[k] = jnp.zeros(hout_ref.shape[1:], F32)
        hout_ref = hout_ref.at[first_layer]
    stages = [_ssd_sequence(chunk, n_chunks, out_steps, xbc_ref.at[s], dt_ref.at[s], z_ref.at[s],
                            alog_ref, dvec_ref, g_ref, e_ref, h0_ref.at[s], y_ref.at[s],
                            hout_ref.at[s], ht_s.at[s]) for s in range(seqs)]
    for _ in itertools.zip_longest(*stages):
        pass


def _ssd_sequence(chunk, n_chunks, out_steps, xbc_ref, dt_ref, z_ref, alog_ref, dvec_ref, g_ref, e_ref,
                  h0_ref, y_ref, hout_ref, ht_s):
    c = pl.program_id(1)
    group_cols = HEADS_PER_GROUP * SSD_HEAD_DIM

    single = n_chunks == 1
    if single:
        h_given = h0_ref[...].reshape(D_SSD, SSD_STATE)
        ht_bf = h_given.astype(BF16).T
    else:
        @pl.when(c == 0)
        def _():
            ht_s[...] = h0_ref[...].reshape(D_SSD, SSD_STATE).T

    dt = dt_ref[...]
    a = -jnp.exp(alog_ref[...])
    row = lax.broadcasted_iota(jnp.int32, (chunk, chunk), 0)
    col = lax.broadcasted_iota(jnp.int32, (chunk, chunk), 1)
    causal = row >= col
    ones_lower = jnp.where(causal, 1.0, 0.0).astype(BF16)
    da_hi, da_mid, da_lo = _split_bf16(dt * a, 3)
    cum = _dot(ones_lower, da_hi) + _dot(ones_lower, da_mid) + _dot(ones_lower, da_lo)
    yield
    cum_t = _transpose_rows(cum, chunk)
    dt_t = _transpose_rows(dt, chunk)
    tot = cum[chunk - 1:chunk, :]
    cdec = jnp.exp(tot)
    factors = jnp.concatenate([jnp.exp(cum), jnp.exp(tot - cum) * dt], axis=0)
    spread = _dot(factors.astype(BF16), e_ref[...])
    yield
    ecum_x = spread[0:chunk, :]
    wend_x = spread[chunk:2 * chunk, :]
    if not single:
        cdec_parts = _split_bf16(jnp.broadcast_to(cdec, (V7X_SUBLANES, DT_PAD)), 3)
        cdec_x = (_dot(cdec_parts[0], e_ref[...]) + _dot(cdec_parts[1], e_ref[...])
                  + _dot(cdec_parts[2], e_ref[...]))[0:1, :]
    low_half = lax.broadcasted_iota(jnp.int32, (chunk, V7X_LANES), 1) < SSD_HEAD_DIM

    y_groups = []
    for g in range(SSD_GROUPS):
        b0 = D_SSD + g * SSD_STATE
        c0 = D_SSD + SSD_GROUPS * SSD_STATE + g * SSD_STATE
        gcols = slice(g * group_cols, (g + 1) * group_cols)
        bg_t = _transpose_rows(xbc_ref[:, b0:b0 + SSD_STATE], chunk).astype(BF16)
        cg = xbc_ref[:, c0:c0 + SSD_STATE].astype(BF16)
        cb = _dot(cg, bg_t)
        yield
        y_pairs = []
        for pair in range(HEADS_PER_GROUP // 2):
            h = g * HEADS_PER_GROUP + 2 * pair
            x_pair = xbc_ref[:, h * SSD_HEAD_DIM:(h + 2) * SSD_HEAD_DIM]
            scores = []
            for hh in (h, h + 1):
                diff = cum[:, hh:hh + 1] - cum_t[hh:hh + 1, :]
                decay = jnp.exp(jnp.where(causal, diff, -jnp.inf))
                scores.append((cb * decay * dt_t[hh:hh + 1, :]).astype(BF16))
            x_lo = jnp.where(low_half, x_pair, 0.0).astype(BF16)
            x_hi = jnp.where(low_half, 0.0, x_pair).astype(BF16)
            if chunk % V7X_LANES == 0:
                y_pairs.append(_dot(jnp.concatenate(scores, axis=1),
                                    jnp.concatenate([x_lo, x_hi], axis=0)))
            else:
                y_pairs.append(_dot(scores[0], x_lo) + _dot(scores[1], x_hi))
            yield
        xg = xbc_ref[:, gcols]
        xw = (xg * wend_x[:, gcols]).astype(BF16)
        if single:
            y_off = _dot(cg, ht_bf[:, gcols]) * ecum_x[:, gcols]
            bg = xbc_ref[:, b0:b0 + SSD_STATE].astype(BF16)
            update = lax.dot_general(xw, bg, (((0,), (0,)), ((), ())), preferred_element_type=F32)
            heads = range(g * HEADS_PER_GROUP, (g + 1) * HEADS_PER_GROUP)
            decay_rows = jnp.concatenate(
                [jnp.broadcast_to(cdec[:, h:h + 1], (SSD_HEAD_DIM, SSD_STATE)) for h in heads], axis=0)
            h_new = decay_rows * h_given[gcols, :] + update
            hout_ref[g * HEADS_PER_GROUP:(g + 1) * HEADS_PER_GROUP] = h_new.reshape(
                HEADS_PER_GROUP, SSD_HEAD_DIM, SSD_STATE)
        else:
            ht_prev = ht_s[:, gcols]
            y_off = _dot(cg, ht_prev.astype(BF16)) * ecum_x[:, gcols]
            ht_s[:, gcols] = cdec_x[:, gcols] * ht_prev + _dot(bg_t, xw)
        y_groups.append(jnp.concatenate(y_pairs, axis=1) + y_off + dvec_ref[:, gcols] * xg)
        yield

    y = _rms(jnp.concatenate(y_groups, axis=1) * _silu(z_ref[...]), g_ref[...])
    y_ref[...] = y[0:out_steps, :]

    if not single:
        @pl.when(c == n_chunks - 1)
        def _():
            hout_ref[...] = ht_s[...].T.reshape(SSD_HEADS, SSD_HEAD_DIM, SSD_STATE)


def _ssd_call(xbc, dt, z, l, w, h0, h0_map, h_stack, *, chunk, out_len, seqs):
    batch, length, _ = xbc.shape
    depth = w["ssd_a_log"].shape[0]
    n_chunks = length // chunk
    assert length == n_chunks * chunk and (out_len == length or n_chunks == 1) and batch % seqs == 0
    out_steps = min(chunk, out_len)

    def per_seq(n_steps, width):
        return pl.BlockSpec((seqs, n_steps, width), lambda b, c: (b, c, 0))

    state_tail = (SSD_HEADS, SSD_HEAD_DIM, SSD_STATE)
    h0_block = (None,) * (h0.ndim - 4) + (seqs,) + state_tail
    operands = [xbc, dt, z, w["ssd_a_log"], w["ssd_d_cols"], w["ssd_norm_g"], w["head_spread"], h0]
    in_specs = [
        per_seq(chunk, SSD_CONV_DIM), per_seq(chunk, DT_PAD), per_seq(chunk, D_SSD),
        _layer_spec((1, DT_PAD), l), _layer_spec((1, D_SSD), l), _layer_spec((1, D_SSD), l),
        pl.BlockSpec((DT_PAD, D_SSD), lambda b, c: (0, 0), pipeline_mode=pl.Buffered(1)),
        pl.BlockSpec(h0_block, h0_map),
    ]
    aliases = {}
    if h_stack is not None:
        aliases = {len(operands): 1}
        operands.append(h_stack)
        in_specs.append(pl.BlockSpec(memory_space=pl.ANY))
    first = h_stack is None
    return pl.pallas_call(
        functools.partial(_ssd_body, chunk, n_chunks, out_steps, seqs, l if first else None),
        grid=(batch // seqs, n_chunks),
        in_specs=in_specs,
        out_specs=[
            per_seq(out_steps, D_SSD),
            pl.BlockSpec((depth, seqs) + state_tail, lambda b, c: (0, b, 0, 0, 0)) if first
            else pl.BlockSpec((None, seqs) + state_tail, lambda b, c: (l, b, 0, 0, 0)),
        ],
        out_shape=[
            jax.ShapeDtypeStruct((batch, out_len, D_SSD), F32),
            jax.ShapeDtypeStruct((depth, batch) + state_tail, F32),
        ],
        scratch_shapes=[pltpu.VMEM((seqs, SSD_STATE, D_SSD), F32)],
        input_output_aliases=aliases,
        compiler_params=_params(2),
        name="ssd_chunk",
    )(*operands)


def _regroup_w_in_body(w_ref, o_ref):
    o_dt = COL_XBC + SSD_CONV_DIM
    o_gate = o_dt + SSD_HEADS
    o_ref[:, COL_Z:COL_GATE] = w_ref[:, COL_Z:o_dt].astype(BF16)
    o_ref[:, COL_GATE:COL_DT] = w_ref[:, o_gate:o_gate + D_LRU + D_LRU].astype(BF16)
    dt_tile = w_ref[:, o_dt:o_dt + DT_PAD]
    lane = lax.broadcasted_iota(jnp.int32, dt_tile.shape, 1)
    o_ref[:, COL_DT:IN_COLS_PAD] = jnp.where(lane < SSD_HEADS, dt_tile, 0.0).astype(BF16)


def _regroup_w_in(w_in):
    depth, d_model, in_cols = w_in.shape
    assert in_cols == D_SSD + SSD_CONV_DIM + SSD_HEADS + 2 * D_LRU
    rows = _row_tile(d_model, W_IN_PREP_ROWS)
    return pl.pallas_call(
        _regroup_w_in_body,
        grid=(depth, d_model // rows),
        in_specs=[pl.BlockSpec((None, rows, in_cols), lambda l, i: (l, i, 0))],
        out_specs=pl.BlockSpec((None, rows, IN_COLS_PAD), lambda l, i: (l, i, 0)),
        out_shape=jax.ShapeDtypeStruct((depth, d_model, IN_COLS_PAD), BF16),
        compiler_params=_params(2),
        name="regroup_w_in",
    )(w_in)


def _prepare_weights(p):
    depth = p["w_in"].shape[0]

    def vec(a):
        return a.reshape(depth, 1, a.shape[-1])

    def pad_heads(a):
        return jnp.pad(a, ((0, 0), (0, DT_PAD - SSD_HEADS))).reshape(depth, 1, DT_PAD)

    def block_diag(a):
        per = V7X_MXU_DIM // LRU_BLOCK_W
        a = a.reshape(depth, LRU_GATE_GROUPS, per, LRU_BLOCK_W, LRU_BLOCK_W)
        eye = jnp.eye(per, dtype=a.dtype)
        a = a[:, :, :, :, None, :] * eye[None, None, :, None, :, None]
        return a.reshape(depth, LRU_GATE_GROUPS, V7X_MXU_DIM, V7X_MXU_DIM).astype(BF16)

    w = {
        "w_in": _regroup_w_in(p["w_in"]),
        "w_out": p["w_out"].astype(BF16),
        "lru_wa": block_diag(p["lru_wa"]),
        "lru_wx": block_diag(p["lru_wx"]),
        "ssd_dt_bias": pad_heads(p["ssd_dt_bias"]),
        "ssd_a_log": pad_heads(p["ssd_a_log"]),
        "ssd_d_cols": jnp.repeat(p["ssd_d"], SSD_HEAD_DIM, axis=-1).reshape(depth, 1, D_SSD),
        "head_spread": (jnp.arange(DT_PAD)[:, None] == jnp.arange(D_SSD)[None, :] // SSD_HEAD_DIM
                        ).astype(BF16),
    }
    for name in ("ssd_conv_w", "lru_conv_w"):
        w[name] = jnp.repeat(p[name], V7X_SUBLANES, axis=1)
    for name in ("ssd_conv_b", "lru_conv_b"):
        w[name] = jnp.repeat(p[name][:, None, :], V7X_SUBLANES, axis=1)
    for name in ("ffn1_wg", "ffn1_wu", "ffn1_wd", "ffn2_wg", "ffn2_wu", "ffn2_wd"):
        w[name] = p[name].astype(BF16)
    for name in ("ffn1_pre_g", "ffn1_post_g", "mix_pre_g", "mix_post_g", "ffn2_pre_g", "ffn2_post_g",
                 "ssd_norm_g", "lru_ba", "lru_bx", "lru_lambda"):
        w[name] = vec(p[name])
    return w


def _segment_mixer(x, l, w, states, ssd_stack, *, bsub, steps, chunk, pad_steps, seqs, n_sub=1):
    (sconv0, sconv0_map), (lconv0, lconv0_map), (h_lru0, h_lru0_map), (h_ssd0, h_ssd0_map) = states
    length = x.shape[1]
    z, xbc, dt, y_lru, sconv, lconv, h_lru = _in_proj_call(
        x, l, w, sconv0, sconv0_map, lconv0, lconv0_map,
        h_lru0, h_lru0_map, bsub=bsub, steps=steps, pad_steps=pad_steps, n_sub=n_sub)
    y_ssd, ssd_stack = _ssd_call(xbc, dt, z, l, w, h_ssd0, h_ssd0_map, ssd_stack,
                                 chunk=chunk, out_len=length, seqs=seqs)
    return y_ssd, y_lru, (sconv, h_lru, lconv), ssd_stack


def kernel(x_prompt, x_sample, state_ssd, state_ssd_conv, state_lru, state_lru_conv, meta_tokens,
           ffn1_pre_g, ffn1_post_g, ffn1_wg, ffn1_wu, ffn1_wd, mix_pre_g, mix_post_g, w_in,
           ssd_conv_w, ssd_conv_b, ssd_dt_bias, ssd_a_log, ssd_d, ssd_norm_g, lru_conv_w, lru_conv_b,
           lru_wa, lru_ba, lru_wx, lru_bx, lru_lambda, w_out, ffn2_pre_g, ffn2_post_g, ffn2_wg,
           ffn2_wu, ffn2_wd):
    bp, seq, _ = x_prompt.shape
    bs, dec_seq, _ = x_sample.shape
    depth = w_in.shape[0]
    assert bp == V7X_SUBLANES and seq % SSD_CHUNK == 0 and bs % SAMPLE_BATCH_TILE == 0
    w = _prepare_weights(dict(
        ffn1_pre_g=ffn1_pre_g, ffn1_post_g=ffn1_post_g, ffn1_wg=ffn1_wg, ffn1_wu=ffn1_wu,
        ffn1_wd=ffn1_wd, mix_pre_g=mix_pre_g, mix_post_g=mix_post_g, w_in=w_in,
        ssd_conv_w=ssd_conv_w, ssd_conv_b=ssd_conv_b, ssd_dt_bias=ssd_dt_bias, ssd_a_log=ssd_a_log,
        ssd_d=ssd_d, ssd_norm_g=ssd_norm_g, lru_conv_w=lru_conv_w, lru_conv_b=lru_conv_b,
        lru_wa=lru_wa, lru_ba=lru_ba, lru_wx=lru_wx, lru_bx=lru_bx, lru_lambda=lru_lambda,
        w_out=w_out, ffn2_pre_g=ffn2_pre_g, ffn2_post_g=ffn2_post_g, ffn2_wg=ffn2_wg,
        ffn2_wu=ffn2_wu, ffn2_wd=ffn2_wd))

    xm = jnp.broadcast_to(meta_tokens.astype(F32)[None], (bp, N_META, D_MODEL))
    xp = x_prompt
    xs = x_sample
    sample_steps = ((dec_seq + V7X_SUBLANES - 1) // V7X_SUBLANES) * V7X_SUBLANES
    tail3 = lambda b, t: (b, 0, 0)
    vec2 = lambda b, t: (b, 0)
    seq_state = lambda b, c: (b, 0, 0, 0)
    zero_states = ((jnp.zeros((bp, CONV_TAIL, SSD_CONV_DIM), F32), tail3),
                   (jnp.zeros((bp, CONV_TAIL, D_LRU), F32), tail3),
                   (jnp.zeros((bp, D_LRU), F32), vec2),
                   (jnp.zeros((bp, SSD_HEADS, SSD_HEAD_DIM, SSD_STATE), F32), seq_state))

    layer_state = lambda b, c, l: (l, b, 0, 0, 0)
    p_out = [[] for _ in range(3)]
    s_out = [[] for _ in range(3)]
    m_ssd = p_ssd = s_ssd = None
    def flat(a):
        return a.reshape(a.shape[0] * a.shape[1], a.shape[2])

    for l in range(depth):
        at_l = functools.partial(layer_state, l=l)
        shapes = (xp.shape, xs.shape, xm.shape)
        xp, xs, xm = (a.reshape(s) for a, s in
                      zip(_ffn_call([flat(xp), flat(xs), flat(xm)], l, w, "ffn1"), shapes))
        m_ssd_y, m_lru_y, m_st, m_ssd = _segment_mixer(
            xm, l, w, zero_states, m_ssd,
            bsub=bp, steps=N_META, chunk=N_META, pad_steps=0, seqs=PROMPT_SEQS)
        p_ssd_y, p_lru_y, p_st, p_ssd = _segment_mixer(
            xp, l, w, ((m_st[0], tail3), (m_st[2], tail3), (m_st[1], vec2), (m_ssd, at_l)), p_ssd,
            bsub=bp, steps=PROMPT_STEPS, chunk=SSD_CHUNK, pad_steps=0, seqs=PROMPT_SEQS,
            n_sub=PROMPT_SUB_TILES)
        s_ssd_y, s_lru_y, s_st, s_ssd = _segment_mixer(
            xs, l, w, ((state_ssd_conv, lambda b, t, l=l: (l, b, 0, 0)),
                       (state_lru_conv, lambda b, t, l=l: (l, b, 0, 0)),
                       (state_lru, lambda b, t, l=l: (l, b, 0)),
                       (state_ssd, at_l)), s_ssd,
            bsub=SAMPLE_BATCH_TILE, steps=dec_seq, chunk=sample_steps,
            pad_steps=sample_steps - dec_seq, seqs=SAMPLE_SEQS)
        (xp,) = _out_ffn_call([(flat(xp), flat(p_ssd_y), flat(p_lru_y))], l, w)
        xs, xm = _out_ffn_call([(flat(xs), flat(s_ssd_y), flat(s_lru_y)),
                                (flat(xm), flat(m_ssd_y), flat(m_lru_y))], l, w)
        xp, xs, xm = (a.reshape(s) for a, s in zip((xp, xs, xm), shapes))
        for acc, st in ((p_out, p_st), (s_out, s_st)):
            for k in range(3):
                acc[k].append(st[k])

    p_conv, p_lru, p_lconv = (jnp.stack(a) for a in p_out)
    s_conv, s_lru, s_lconv = (jnp.stack(a) for a in s_out)
    return (xp, xs, p_ssd, p_conv, p_lru, p_lconv, s_ssd, s_conv, s_lru, s_lconv)
```

```python
import functools
import itertools

import jax
import jax.numpy as jnp
from jax import lax
from jax.experimental import pallas as pl
from jax.experimental.pallas import tpu as pltpu

F32 = jnp.float32
BF16 = jnp.bfloat16

D_MODEL = 1024
D_SSD = 1024
D_LRU = 1024
SSD_HEADS = 16
SSD_HEAD_DIM = 64
SSD_GROUPS = 2
SSD_STATE = 128
HEADS_PER_GROUP = SSD_HEADS // SSD_GROUPS
CONV_W = 4
CONV_TAIL = CONV_W - 1
SSD_CONV_DIM = D_SSD + 2 * SSD_GROUPS * SSD_STATE
LRU_BLOCKS = 16
LRU_BLOCK_W = D_LRU // LRU_BLOCKS
LRU_C = 8.0
EPS = 1e-6
N_META = 16
SSD_CHUNK = 128

V7X_LANES = 128
V7X_SUBLANES = 8
V7X_MXU_DIM = 256
V7X_VMEM_LIMIT_BYTES = 56 * 1024 * 1024

DT_PAD = V7X_LANES
LRU_GATE_GROUPS = D_LRU // V7X_MXU_DIM
COL_Z = 0
COL_XBC = COL_Z + D_SSD
COL_GATE = COL_XBC + SSD_CONV_DIM
COL_XR = COL_GATE + D_LRU
COL_DT = COL_XR + D_LRU
IN_COLS_PAD = COL_DT + DT_PAD
GELU_K = 0.7978845608028654

FFN_ROWS = 512
FFN_SUB_TILES = 2
OUT_FFN_ROWS = 512
PROMPT_STEPS = 64
SAMPLE_BATCH_TILE = 64
PROMPT_SUB_TILES = 2
SCAN_UNROLL_LIMIT = 128
PROMPT_SEQS = 4
SAMPLE_SEQS = 8
ELEMENTWISE_ROWS = 16
ELEMENTWISE_COLS = 512


def _rms(x, g):
    return x * lax.rsqrt(jnp.mean(x * x, axis=-1, keepdims=True) + EPS) * g


def _silu(x):
    return x * _sigmoid(x)


def _sigmoid(x):
    return 0.5 * jnp.tanh(0.5 * x) + 0.5


def _softplus(x):
    return jnp.maximum(x, 0.0) + jnp.log1p(jnp.exp(-jnp.abs(x)))


def _gelu_tanh(x):
    return 0.5 * x * (1.0 + jnp.tanh(GELU_K * (x + 0.044715 * (x * x * x))))


def _dot(a, b):
    return jnp.dot(a, b, preferred_element_type=F32)


def _layer_spec(tail, l):
    zeros = (0,) * len(tail)
    return pl.BlockSpec((None,) + tuple(tail), lambda *_: (l,) + zeros, pipeline_mode=pl.Buffered(1))


def _params(n_axes):
    return pltpu.CompilerParams(dimension_semantics=("arbitrary",) * n_axes,
                                vmem_limit_bytes=V7X_VMEM_LIMIT_BYTES)


def _row_tile(rows, want):
    tm = min(rows, want)
    assert rows % tm == 0
    return tm


def _ffn_math(x, gpre, gpost, wg_ref, wu_ref, wd_ref):
    n_sub = FFN_SUB_TILES if x.shape[0] >= FFN_ROWS else 1
    rows = x.shape[0] // n_sub
    outs = [None] * n_sub

    def sub_tile(i):
        xi = x[i * rows:(i + 1) * rows]
        xn = _rms(xi, gpre).astype(BF16)
        yield
        hg = _dot(xn, wg_ref[...])
        hu = _dot(xn, wu_ref[...])
        yield
        a = (_silu(hg) * hu).astype(BF16)
        yield
        y = _dot(a, wd_ref[...])
        yield
        outs[i] = xi + 0.5 * _rms(y, gpost)

    _run_skewed([sub_tile(i) for i in range(n_sub)])
    return outs[0] if n_sub == 1 else jnp.concatenate(outs, axis=0)


def _run_skewed(stage_generators):
    waiting, running = list(stage_generators), []
    while waiting or running:
        if waiting:
            running.append(waiting.pop(0))
        for stages in list(running):
            if next(stages, "done") == "done":
                running.remove(stages)


def _token_call(math, segments, weights, weight_specs, name, tile_rows):
    n_ops = len(segments[0])
    tiles = [_row_tile(seg[0].shape[0], tile_rows) for seg in segments]
    counts = [seg[0].shape[0] // tm for seg, tm in zip(segments, tiles)]
    starts = [sum(counts[:k]) for k in range(len(segments))]

    def seg_spec(k):
        mode = {} if counts[k] > 1 else dict(pipeline_mode=pl.Buffered(1))
        return pl.BlockSpec((tiles[k], D_MODEL),
                            lambda i: (jnp.clip(i - starts[k], 0, counts[k] - 1), 0), **mode)

    def body(*refs):
        n_in = n_ops * len(segments)
        x_refs, w_refs, o_refs = refs[:n_in], refs[n_in:n_in + len(weights)], refs[n_in + len(weights):]
        i = pl.program_id(0)

        def run(k):
            operands = [r[...] for r in x_refs[k * n_ops:(k + 1) * n_ops]]
            o_refs[k][...] = math(*operands, *w_refs)

        for k in range(len(segments)):
            pl.when((i >= starts[k]) & (i < starts[k] + counts[k]))(functools.partial(run, k))

    return pl.pallas_call(
        body,
        grid=(sum(counts),),
        in_specs=[seg_spec(k) for k in range(len(segments)) for _ in range(n_ops)] + weight_specs,
        out_specs=[seg_spec(k) for k in range(len(segments))],
        out_shape=[jax.ShapeDtypeStruct(seg[0].shape, F32) for seg in segments],
        compiler_params=_params(1),
        name=name,
    )(*[a for seg in segments for a in seg], *weights)


def _ffn_math_refs(x, gpre_ref, gpost_ref, wg_ref, wu_ref, wd_ref):
    return _ffn_math(x, gpre_ref[...], gpost_ref[...], wg_ref, wu_ref, wd_ref)


def _ffn_call(xs, l, w, prefix):
    d_ff = w[prefix + "_wg"].shape[-1]
    return _token_call(
        _ffn_math_refs, [(x,) for x in xs],
        [w[prefix + "_pre_g"], w[prefix + "_post_g"], w[prefix + "_wg"], w[prefix + "_wu"],
         w[prefix + "_wd"]],
        [_layer_spec((1, D_MODEL), l), _layer_spec((1, D_MODEL), l),
         _layer_spec((D_MODEL, d_ff), l), _layer_spec((D_MODEL, d_ff), l),
         _layer_spec((d_ff, D_MODEL), l)],
        prefix, FFN_ROWS)


def _out_ffn_math(x, ys, yl, wos_ref, wol_ref, gmix_ref, gpre_ref, gpost_ref, wg_ref, wu_ref, wd_ref):
    m = _dot(ys.astype(BF16), wos_ref[...]) + _dot(yl.astype(BF16), wol_ref[...])
    x1 = x + _rms(m, gmix_ref[...])
    return _ffn_math(x1, gpre_ref[...], gpost_ref[...], wg_ref, wu_ref, wd_ref)


def _out_ffn_call(segments, l, w):
    d_ff = w["ffn2_wg"].shape[-1]
    return _token_call(
        _out_ffn_math, segments,
        [w["w_out"], w["w_out"], w["mix_post_g"], w["ffn2_pre_g"], w["ffn2_post_g"],
         w["ffn2_wg"], w["ffn2_wu"], w["ffn2_wd"]],
        [pl.BlockSpec((None, D_SSD, D_MODEL), lambda i: (l, 0, 0), pipeline_mode=pl.Buffered(1)),
         pl.BlockSpec((None, D_LRU, D_MODEL), lambda i: (l, D_SSD // D_LRU, 0),
                      pipeline_mode=pl.Buffered(1)),
         _layer_spec((1, D_MODEL), l), _layer_spec((1, D_MODEL), l), _layer_spec((1, D_MODEL), l),
         _layer_spec((D_MODEL, d_ff), l), _layer_spec((D_MODEL, d_ff), l),
         _layer_spec((d_ff, D_MODEL), l)],
        "out_ffn2", OUT_FFN_ROWS)


def _in_proj_body(bsub, steps, pad_steps, n_sub,
                  x_ref, g_ref, w_ref, scw_ref, scb_ref, lcw_ref, lcb_ref, dtb_ref,
                  wa_ref, wx_ref, ba_ref, bx_ref, lam_ref, sconv0_ref, lconv0_ref, h0_ref,
                  z_ref, xbc_ref, dt_ref, ylru_ref, sconv_ref, lconv_ref, hout_ref,
                  stage, xbuf, sbuf, lbuf, gate_buf, ra_buf, rx_buf, h_carry):
    rows = steps * bsub
    tail = CONV_TAIL * bsub
    sub_steps = steps // n_sub
    sub_rows = sub_steps * bsub
    rc = min(sub_rows, ELEMENTWISE_ROWS)
    strided = sub_steps >= bsub

    def lane_tile(j):
        return slice(j * V7X_LANES, (j + 1) * V7X_LANES)

    def load_block(ref, t0):
        r0 = t0 * bsub
        if not strided:
            for t in range(sub_steps):
                xbuf[r0 + t * bsub:r0 + (t + 1) * bsub, :] = ref[:, t0 + t, :]
            return xbuf[r0:r0 + sub_rows, :]
        n_tiles = ref.shape[2] // V7X_LANES
        for j in range(n_tiles):
            for b in range(bsub):
                stage[j, pl.ds(r0 + b, sub_steps, stride=bsub), :] = ref[b, t0:t0 + sub_steps, lane_tile(j)]
        return jnp.concatenate([stage[j, r0:r0 + sub_rows, :] for j in range(n_tiles)], axis=1)

    def store_block(ref, t0, read):
        r0 = t0 * bsub
        if not strided:
            value = read(slice(0, ref.shape[2]))
            for t in range(sub_steps):
                ref[:, t0 + t, :] = value[t * bsub:(t + 1) * bsub, :]
            return
        n_tiles = ref.shape[2] // V7X_LANES
        for j in range(n_tiles):
            stage[j, r0:r0 + sub_rows, :] = read(lane_tile(j))
        for j in range(n_tiles):
            for b in range(bsub):
                ref[b, t0:t0 + sub_steps, lane_tile(j)] = stage[j, pl.ds(r0 + b, sub_steps, stride=bsub), :]

    def conv_in_place(buf, w_ref, b_ref, act, r_lo):
        sub = V7X_SUBLANES
        for c0 in range(0, buf.shape[1], ELEMENTWISE_COLS):
            cols = slice(c0, c0 + ELEMENTWISE_COLS)
            taps = [w_ref[k * sub:(k + 1) * sub, cols] for k in range(CONV_W)]
            bias = b_ref[:, cols]
            for r0 in range(r_lo, r_lo + sub_rows, sub):
                acc = bias + buf[r0:r0 + sub, cols] * taps[0]
                for k in range(1, CONV_W):
                    acc = acc + buf[r0 + k * bsub:r0 + k * bsub + sub, cols] * taps[k]
                buf[r0:r0 + sub, cols] = act(acc)

    @pl.when(pl.program_id(1) == 0)
    def _():
        for k in range(CONV_TAIL):
            sbuf[k * bsub:(k + 1) * bsub, :] = sconv0_ref[:, k, :]
            lbuf[k * bsub:(k + 1) * bsub, :] = lconv0_ref[:, k, :]
        h_carry[...] = h0_ref[...]

    neg_c_softplus = -LRU_C * _softplus(-lam_ref[...])

    def sub_tile(i):
        t0 = i * sub_steps
        r_lo = t0 * bsub
        out = slice(r_lo, r_lo + sub_rows)
        pre = slice(tail + r_lo, tail + r_lo + sub_rows)
        xn = _rms(load_block(x_ref, t0), g_ref[...]).astype(BF16)
        yield
        lbuf[pre, :] = _dot(xn, w_ref[:, COL_XR:COL_DT])
        gate_buf[out, :] = _dot(xn, w_ref[:, COL_GATE:COL_XR])
        yield
        conv_in_place(lbuf, lcw_ref, lcb_ref, lambda v: v, r_lo)
        yield
        for q in range(LRU_GATE_GROUPS):
            cols = slice(q * V7X_MXU_DIM, (q + 1) * V7X_MXU_DIM)
            xr_bf = lbuf[out, cols].astype(BF16)
            ra_buf[out, cols] = _dot(xr_bf, wa_ref[q])
            rx_buf[out, cols] = _dot(xr_bf, wx_ref[q])
        yield
        for c0 in range(0, D_LRU, ELEMENTWISE_COLS):
            cols = slice(c0, c0 + ELEMENTWISE_COLS)
            ncs, ba, bx = neg_c_softplus[:, cols], ba_ref[:, cols], bx_ref[:, cols]
            for r0 in range(r_lo, r_lo + sub_rows, rc):
                sl = slice(r0, r0 + rc)
                log_a = ncs * _sigmoid(ra_buf[sl, cols] + ba)
                a = jnp.exp(log_a)
                m = -jnp.tanh(log_a) * (a * a + 1.0)
                mult = jnp.where(m > 0.0, m * lax.rsqrt(m), 0.0)
                rx_buf[sl, cols] = mult * _sigmoid(rx_buf[sl, cols] + bx) * lbuf[sl, cols]
                ra_buf[sl, cols] = a
                gate_buf[sl, cols] = _gelu_tanh(gate_buf[sl, cols])
        yield
        z = _dot(xn, w_ref[:, COL_Z:COL_XBC])
        sbuf[pre, :] = _dot(xn, w_ref[:, COL_XBC:COL_GATE])
        dt = _softplus(_dot(xn, w_ref[:, COL_DT:IN_COLS_PAD]) + dtb_ref[...])
        yield
        store_block(z_ref, t0, lambda cols: z[:, cols])
        store_block(dt_ref, t0, lambda cols: dt[:, cols])
        conv_in_place(sbuf, scw_ref, scb_ref, _silu, r_lo)
        yield
        store_block(xbc_ref, t0, lambda cols: sbuf[out, cols])

    _run_skewed([sub_tile(i) for i in range(n_sub)])

    for k in range(CONV_TAIL):
        sconv_ref[:, k, :] = sbuf[rows + k * bsub:rows + (k + 1) * bsub, :]
        lconv_ref[:, k, :] = lbuf[rows + k * bsub:rows + (k + 1) * bsub, :]

    def scan_step(b0, t, h):
        start = t * bsub + b0
        if not isinstance(start, int):
            start = pl.multiple_of(start, V7X_SUBLANES)
        sl = pl.ds(start, V7X_SUBLANES)
        h = ra_buf[sl, :] * h + rx_buf[sl, :]
        gate_buf[sl, :] = h * gate_buf[sl, :]
        return h

    n_groups = bsub // V7X_SUBLANES
    if n_groups * steps <= SCAN_UNROLL_LIMIT:
        for bg in range(n_groups):
            hsl = slice(bg * V7X_SUBLANES, (bg + 1) * V7X_SUBLANES)
            h = h_carry[hsl, :]
            for t in range(steps):
                h = scan_step(bg * V7X_SUBLANES, t, h)
            h_carry[hsl, :] = h
    else:
        def group_scan(bg, carry):
            b0 = pl.multiple_of(bg * V7X_SUBLANES, V7X_SUBLANES)
            hsl = pl.ds(b0, V7X_SUBLANES)
            h_carry[hsl, :] = lax.fori_loop(0, steps, functools.partial(scan_step, b0), h_carry[hsl, :])
            return carry

        lax.fori_loop(0, n_groups, group_scan, 0)
    for i in range(n_sub):
        store_block(ylru_ref, i * sub_steps,
                    lambda cols, i=i: gate_buf[i * sub_rows:(i + 1) * sub_rows, cols])
    hout_ref[...] = h_carry[...]
    sbuf[0:tail, :] = sbuf[rows:rows + tail, :]
    lbuf[0:tail, :] = lbuf[rows:rows + tail, :]
    for t in range(steps, steps + pad_steps):
        for ref in (z_ref, xbc_ref, dt_ref):
            ref[:, t, :] = jnp.zeros((bsub, ref.shape[2]), F32)


def _in_proj_call(x, l, w, sconv0, sconv0_map, lconv0, lconv0_map, h0, h0_map, *,
                  bsub, steps, pad_steps=0, n_sub=1):
    batch, length, _ = x.shape
    n_t = length // steps
    n_b = batch // bsub
    assert length == n_t * steps and batch == n_b * bsub and bsub % V7X_SUBLANES == 0
    assert pad_steps == 0 or n_t == 1
    assert steps >= CONV_TAIL
    assert steps % n_sub == 0
    strided = steps // n_sub >= bsub
    out_steps = steps + pad_steps
    rows = steps * bsub
    tail = CONV_TAIL * bsub

    def tile(n_steps, width):
        return pl.BlockSpec((bsub, n_steps, width), lambda b, t: (b, t, 0))

    def squeeze_lead(a, block):
        return (None,) * (a.ndim - len(block)) + block

    def conv_out(width):
        return pl.BlockSpec((bsub, CONV_TAIL, width), lambda b, t: (b, 0, 0))

    return pl.pallas_call(
        functools.partial(_in_proj_body, bsub, steps, pad_steps, n_sub),
        grid=(n_b, n_t),
        in_specs=[
            tile(steps, D_MODEL),
            _layer_spec((1, D_MODEL), l),
            _layer_spec((D_MODEL, IN_COLS_PAD), l),
            _layer_spec((CONV_W * V7X_SUBLANES, SSD_CONV_DIM), l),
            _layer_spec((V7X_SUBLANES, SSD_CONV_DIM), l),
            _layer_spec((CONV_W * V7X_SUBLANES, D_LRU), l), _layer_spec((V7X_SUBLANES, D_LRU), l),
            _layer_spec((1, DT_PAD), l),
            _layer_spec((LRU_GATE_GROUPS, V7X_MXU_DIM, V7X_MXU_DIM), l),
            _layer_spec((LRU_GATE_GROUPS, V7X_MXU_DIM, V7X_MXU_DIM), l),
            _layer_spec((1, D_LRU), l), _layer_spec((1, D_LRU), l), _layer_spec((1, D_LRU), l),
            pl.BlockSpec(squeeze_lead(sconv0, (bsub, CONV_TAIL, SSD_CONV_DIM)), sconv0_map),
            pl.BlockSpec(squeeze_lead(lconv0, (bsub, CONV_TAIL, D_LRU)), lconv0_map),
            pl.BlockSpec(squeeze_lead(h0, (bsub, D_LRU)), h0_map),
        ],
        out_specs=[
            tile(out_steps, D_SSD), tile(out_steps, SSD_CONV_DIM), tile(out_steps, DT_PAD),
            tile(steps, D_LRU),
            conv_out(SSD_CONV_DIM), conv_out(D_LRU),
            pl.BlockSpec((bsub, D_LRU), lambda b, t: (b, 0)),
        ],
        out_shape=[
            jax.ShapeDtypeStruct((batch, n_t * out_steps, D_SSD), F32),
            jax.ShapeDtypeStruct((batch, n_t * out_steps, SSD_CONV_DIM), F32),
            jax.ShapeDtypeStruct((batch, n_t * out_steps, DT_PAD), F32),
            jax.ShapeDtypeStruct((batch, length, D_LRU), F32),
            jax.ShapeDtypeStruct((batch, CONV_TAIL, SSD_CONV_DIM), F32),
            jax.ShapeDtypeStruct((batch, CONV_TAIL, D_LRU), F32),
            jax.ShapeDtypeStruct((batch, D_LRU), F32),
        ],
        scratch_shapes=[
            pltpu.VMEM((SSD_CONV_DIM // V7X_LANES, rows, V7X_LANES) if strided
                       else (1, V7X_SUBLANES, V7X_LANES), F32),
            pltpu.VMEM((V7X_SUBLANES, V7X_LANES) if strided else (rows, D_MODEL), F32),
            pltpu.VMEM((tail + rows, SSD_CONV_DIM), F32),
            pltpu.VMEM((tail + rows, D_LRU), F32),
            pltpu.VMEM((rows, D_LRU), F32),
            pltpu.VMEM((rows, D_LRU), F32),
            pltpu.VMEM((rows, D_LRU), F32),
            pltpu.VMEM((bsub, D_LRU), F32),
        ],
        compiler_params=_params(2),
        name="in_proj_lru",
    )(x, w["mix_pre_g"], w["w_in"], w["ssd_conv_w"], w["ssd_conv_b"], w["lru_conv_w"], w["lru_conv_b"],
      w["ssd_dt_bias"], w["lru_wa"], w["lru_wx"], w["lru_ba"], w["lru_bx"], w["lru_lambda"],
      sconv0, lconv0, h0)


def _transpose_rows(x, rows):
    lanes = x.shape[1]
    if rows < lanes:
        x = jnp.concatenate([x, jnp.zeros((lanes - rows, lanes), x.dtype)], axis=0)
    return x.T[:, 0:rows]


def _split_bf16(x, terms):
    parts = []
    for _ in range(terms - 1):
        parts.append(x.astype(BF16))
        x = x - parts[-1].astype(F32)
    return parts + [x.astype(BF16)]


def _ssd_body(chunk, n_chunks, out_steps, seqs, first_layer, xbc_ref, dt_ref, z_ref, alog_ref, dvec_ref,
              g_ref, e_ref, h0_ref, *rest):
    (y_ref, hout_ref, ht_s) = rest[-3:]
    if first_layer is not None:
        @pl.when(pl.program_id(1) == 0)
        def _():
            for k in range(hout_ref.shape[0]):
                if k != first_layer:
                    hout_ref[k] = jnp.zeros(hout_ref.shape[1:], F32)
        hout_ref = hout_ref.at[first_layer]
    stages = [_ssd_sequence(chunk, n_chunks, out_steps, xbc_ref.at[s], dt_ref.at[s], z_ref.at[s],
                            alog_ref, dvec_ref, g_ref, e_ref, h0_ref.at[s], y_ref.at[s],
                            hout_ref.at[s], ht_s.at[s]) for s in range(seqs)]
    for _ in itertools.zip_longest(*stages):
        pass


def _ssd_sequence(chunk, n_chunks, out_steps, xbc_ref, dt_ref, z_ref, alog_ref, dvec_ref, g_ref, e_ref,
                  h0_ref, y_ref, hout_ref, ht_s):
    c = pl.program_id(1)
    group_cols = HEADS_PER_GROUP * SSD_HEAD_DIM

    single = n_chunks == 1
    if single:
        h_given = h0_ref[...].reshape(D_SSD, SSD_STATE)
        ht_bf = h_given.astype(BF16).T
    else:
        @pl.when(c == 0)
        def _():
            ht_s[...] = h0_ref[...].reshape(D_SSD, SSD_STATE).T

    dt = dt_ref[...]
    a = -jnp.exp(alog_ref[...])
    row = lax.broadcasted_iota(jnp.int32, (chunk, chunk), 0)
    col = lax.broadcasted_iota(jnp.int32, (chunk, chunk), 1)
    causal = row >= col
    ones_lower = jnp.where(causal, 1.0, 0.0).astype(BF16)
    da_hi, da_mid, da_lo = _split_bf16(dt * a, 3)
    cum = _dot(ones_lower, da_hi) + _dot(ones_lower, da_mid) + _dot(ones_lower, da_lo)
    yield
    cum_t = _transpose_rows(cum, chunk)
    dt_t = _transpose_rows(dt, chunk)
    tot = cum[chunk - 1:chunk, :]
    cdec = jnp.exp(tot)
    factors = jnp.concatenate([jnp.exp(cum), jnp.exp(tot - cum) * dt], axis=0)
    spread = _dot(factors.astype(BF16), e_ref[...])
    yield
    ecum_x = spread[0:chunk, :]
    wend_x = spread[chunk:2 * chunk, :]
    if not single:
        cdec_parts = _split_bf16(jnp.broadcast_to(cdec, (V7X_SUBLANES, DT_PAD)), 3)
        cdec_x = (_dot(cdec_parts[0], e_ref[...]) + _dot(cdec_parts[1], e_ref[...])
                  + _dot(cdec_parts[2], e_ref[...]))[0:1, :]
    low_half = lax.broadcasted_iota(jnp.int32, (chunk, V7X_LANES), 1) < SSD_HEAD_DIM

    y_groups = []
    for g in range(SSD_GROUPS):
        b0 = D_SSD + g * SSD_STATE
        c0 = D_SSD + SSD_GROUPS * SSD_STATE + g * SSD_STATE
        gcols = slice(g * group_cols, (g + 1) * group_cols)
        bg_t = _transpose_rows(xbc_ref[:, b0:b0 + SSD_STATE], chunk).astype(BF16)
        cg = xbc_ref[:, c0:c0 + SSD_STATE].astype(BF16)
        cb = _dot(cg, bg_t)
        yield
        y_pairs = []
        for pair in range(HEADS_PER_GROUP // 2):
            h = g * HEADS_PER_GROUP + 2 * pair
            x_pair = xbc_ref[:, h * SSD_HEAD_DIM:(h + 2) * SSD_HEAD_DIM]
            scores = []
            for hh in (h, h + 1):
                diff = cum[:, hh:hh + 1] - cum_t[hh:hh + 1, :]
                decay = jnp.exp(jnp.where(causal, diff, -jnp.inf))
                scores.append((cb * decay * dt_t[hh:hh + 1, :]).astype(BF16))
            x_lo = jnp.where(low_half, x_pair, 0.0).astype(BF16)
            x_hi = jnp.where(low_half, 0.0, x_pair).astype(BF16)
            if chunk % V7X_LANES == 0:
                y_pairs.append(_dot(jnp.concatenate(scores, axis=1),
                                    jnp.concatenate([x_lo, x_hi], axis=0)))
            else:
                y_pairs.append(_dot(scores[0], x_lo) + _dot(scores[1], x_hi))
            yield
        xg = xbc_ref[:, gcols]
        xw = (xg * wend_x[:, gcols]).astype(BF16)
        if single:
            y_off = _dot(cg, ht_bf[:, gcols]) * ecum_x[:, gcols]
            bg = xbc_ref[:, b0:b0 + SSD_STATE].astype(BF16)
            update = lax.dot_general(xw, bg, (((0,), (0,)), ((), ())), preferred_element_type=F32)
            heads = range(g * HEADS_PER_GROUP, (g + 1) * HEADS_PER_GROUP)
            decay_rows = jnp.concatenate(
                [jnp.broadcast_to(cdec[:, h:h + 1], (SSD_HEAD_DIM, SSD_STATE)) for h in heads], axis=0)
            h_new = decay_rows * h_given[gcols, :] + update
            hout_ref[g * HEADS_PER_GROUP:(g + 1) * HEADS_PER_GROUP] = h_new.reshape(
                HEADS_PER_GROUP, SSD_HEAD_DIM, SSD_STATE)
        else:
            ht_prev = ht_s[:, gcols]
            y_off = _dot(cg, ht_prev.astype(BF16)) * ecum_x[:, gcols]
            ht_s[:, gcols] = cdec_x[:, gcols] * ht_prev + _dot(bg_t, xw)
        y_groups.append(jnp.concatenate(y_pairs, axis=1) + y_off + dvec_ref[:, gcols] * xg)
        yield

    y = _rms(jnp.concatenate(y_groups, axis=1) * _silu(z_ref[...]), g_ref[...])
    y_ref[...] = y[0:out_steps, :]

    if not single:
        @pl.when(c == n_chunks - 1)
        def _():
            hout_ref[...] = ht_s[...].T.reshape(SSD_HEADS, SSD_HEAD_DIM, SSD_STATE)


def _ssd_call(xbc, dt, z, l, w, h0, h0_map, h_stack, *, chunk, out_len, seqs):
    batch, length, _ = xbc.shape
    depth = w["ssd_a_log"].shape[0]
    n_chunks = length // chunk
    assert length == n_chunks * chunk and (out_len == length or n_chunks == 1) and batch % seqs == 0
    out_steps = min(chunk, out_len)

    def per_seq(n_steps, width):
        return pl.BlockSpec((seqs, n_steps, width), lambda b, c: (b, c, 0))

    state_tail = (SSD_HEADS, SSD_HEAD_DIM, SSD_STATE)
    h0_block = (None,) * (h0.ndim - 4) + (seqs,) + state_tail
    operands = [xbc, dt, z, w["ssd_a_log"], w["ssd_d_cols"], w["ssd_norm_g"], w["head_spread"], h0]
    in_specs = [
        per_seq(chunk, SSD_CONV_DIM), per_seq(chunk, DT_PAD), per_seq(chunk, D_SSD),
        _layer_spec((1, DT_PAD), l), _layer_spec((1, D_SSD), l), _layer_spec((1, D_SSD), l),
        pl.BlockSpec((DT_PAD, D_SSD), lambda b, c: (0, 0), pipeline_mode=pl.Buffered(1)),
        pl.BlockSpec(h0_block, h0_map),
    ]
    aliases = {}
    if h_stack is not None:
        aliases = {len(operands): 1}
        operands.append(h_stack)
        in_specs.append(pl.BlockSpec(memory_space=pl.ANY))
    first = h_stack is None
    return pl.pallas_call(
        functools.partial(_ssd_body, chunk, n_chunks, out_steps, seqs, l if first else None),
        grid=(batch // seqs, n_chunks),
        in_specs=in_specs,
        out_specs=[
            per_seq(out_steps, D_SSD),
            pl.BlockSpec((depth, seqs) + state_tail, lambda b, c: (0, b, 0, 0, 0)) if first
            else pl.BlockSpec((None, seqs) + state_tail, lambda b, c: (l, b, 0, 0, 0)),
        ],
        out_shape=[
            jax.ShapeDtypeStruct((batch, out_len, D_SSD), F32),
            jax.ShapeDtypeStruct((depth, batch) + state_tail, F32),
        ],
        scratch_shapes=[pltpu.VMEM((seqs, SSD_STATE, D_SSD), F32)],
        input_output_aliases=aliases,
        compiler_params=_params(2),
        name="ssd_chunk",
    )(*operands)


def _regroup_w_in_body(wt_ref, o_ref):
    o_dt = COL_XBC + SSD_CONV_DIM
    o_gate = o_dt + SSD_HEADS

    def put(c0, rows):
        o_ref[:, c0:c0 + V7X_LANES] = rows.T.astype(BF16)

    for c0 in range(COL_Z, COL_GATE, V7X_LANES):
        put(c0, wt_ref[c0:c0 + V7X_LANES, :])
    for c0 in range(COL_GATE, COL_DT, V7X_LANES):
        r0 = o_gate + c0 - COL_GATE
        put(c0, wt_ref[r0:r0 + V7X_LANES, :])
    put(COL_DT, jnp.concatenate([wt_ref[o_dt:o_gate, :],
                                 jnp.zeros((DT_PAD - SSD_HEADS, D_MODEL), F32)], axis=0))


def _regroup_w_in(w_in):
    depth, d_model, in_cols = w_in.shape
    assert in_cols == D_SSD + SSD_CONV_DIM + SSD_HEADS + 2 * D_LRU and d_model == D_MODEL
    return pl.pallas_call(
        _regroup_w_in_body,
        grid=(depth,),
        in_specs=[pl.BlockSpec((None, in_cols, d_model), lambda l: (l, 0, 0),
                               pipeline_mode=pl.Buffered(1))],
        out_specs=pl.BlockSpec((None, d_model, IN_COLS_PAD), lambda l: (l, 0, 0)),
        out_shape=jax.ShapeDtypeStruct((depth, d_model, IN_COLS_PAD), BF16),
        compiler_params=_params(1),
        name="regroup_w_in",
    )(w_in.transpose(0, 2, 1))


def _prepare_weights(p):
    depth = p["w_in"].shape[0]

    def vec(a):
        return a.reshape(depth, 1, a.shape[-1])

    def pad_heads(a):
        return jnp.pad(a, ((0, 0), (0, DT_PAD - SSD_HEADS))).reshape(depth, 1, DT_PAD)

    def block_diag(a):
        per = V7X_MXU_DIM // LRU_BLOCK_W
        a = a.reshape(depth, LRU_GATE_GROUPS, per, LRU_BLOCK_W, LRU_BLOCK_W)
        eye = jnp.eye(per, dtype=a.dtype)
        a = a[:, :, :, :, None, :] * eye[None, None, :, None, :, None]
        return a.reshape(depth, LRU_GATE_GROUPS, V7X_MXU_DIM, V7X_MXU_DIM).astype(BF16)

    w = {
        "w_in": _regroup_w_in(p["w_in"]),
        "w_out": p["w_out"].astype(BF16),
        "lru_wa": block_diag(p["lru_wa"]),
        "lru_wx": block_diag(p["lru_wx"]),
        "ssd_dt_bias": pad_heads(p["ssd_dt_bias"]),
        "ssd_a_log": pad_heads(p["ssd_a_log"]),
        "ssd_d_cols": jnp.repeat(p["ssd_d"], SSD_HEAD_DIM, axis=-1).reshape(depth, 1, D_SSD),
        "head_spread": (jnp.arange(DT_PAD)[:, None] == jnp.arange(D_SSD)[None, :] // SSD_HEAD_DIM
                        ).astype(BF16),
    }
    for name in ("ssd_conv_w", "lru_conv_w"):
        w[name] = jnp.repeat(p[name], V7X_SUBLANES, axis=1)
    for name in ("ssd_conv_b", "lru_conv_b"):
        w[name] = jnp.repeat(p[name][:, None, :], V7X_SUBLANES, axis=1)
    for name in ("ffn1_wg", "ffn1_wu", "ffn1_wd", "ffn2_wg", "ffn2_wu", "ffn2_wd"):
        w[name] = p[name].astype(BF16)
    for name in ("ffn1_pre_g", "ffn1_post_g", "mix_pre_g", "mix_post_g", "ffn2_pre_g", "ffn2_post_g",
                 "ssd_norm_g", "lru_ba", "lru_bx", "lru_lambda"):
        w[name] = vec(p[name])
    return w


def _segment_mixer(x, l, w, states, ssd_stack, *, bsub, steps, chunk, pad_steps, seqs, n_sub=1):
    (sconv0, sconv0_map), (lconv0, lconv0_map), (h_lru0, h_lru0_map), (h_ssd0, h_ssd0_map) = states
    length = x.shape[1]
    z, xbc, dt, y_lru, sconv, lconv, h_lru = _in_proj_call(
        x, l, w, sconv0, sconv0_map, lconv0, lconv0_map,
        h_lru0, h_lru0_map, bsub=bsub, steps=steps, pad_steps=pad_steps, n_sub=n_sub)
    y_ssd, ssd_stack = _ssd_call(xbc, dt, z, l, w, h_ssd0, h_ssd0_map, ssd_stack,
                                 chunk=chunk, out_len=length, seqs=seqs)
    return y_ssd, y_lru, (sconv, h_lru, lconv), ssd_stack


def kernel(x_prompt, x_sample, state_ssd, state_ssd_conv, state_lru, state_lru_conv, meta_tokens,
           ffn1_pre_g, ffn1_post_g, ffn1_wg, ffn1_wu, ffn1_wd, mix_pre_g, mix_post_g, w_in,
           ssd_conv_w, ssd_conv_b, ssd_dt_bias, ssd_a_log, ssd_d, ssd_norm_g, lru_conv_w, lru_conv_b,
           lru_wa, lru_ba, lru_wx, lru_bx, lru_lambda, w_out, ffn2_pre_g, ffn2_post_g, ffn2_wg,
           ffn2_wu, ffn2_wd):
    bp, seq, _ = x_prompt.shape
    bs, dec_seq, _ = x_sample.shape
    depth = w_in.shape[0]
    assert bp == V7X_SUBLANES and seq % SSD_CHUNK == 0 and bs % SAMPLE_BATCH_TILE == 0
    w = _prepare_weights(dict(
        ffn1_pre_g=ffn1_pre_g, ffn1_post_g=ffn1_post_g, ffn1_wg=ffn1_wg, ffn1_wu=ffn1_wu,
        ffn1_wd=ffn1_wd, mix_pre_g=mix_pre_g, mix_post_g=mix_post_g, w_in=w_in,
        ssd_conv_w=ssd_conv_w, ssd_conv_b=ssd_conv_b, ssd_dt_bias=ssd_dt_bias, ssd_a_log=ssd_a_log,
        ssd_d=ssd_d, ssd_norm_g=ssd_norm_g, lru_conv_w=lru_conv_w, lru_conv_b=lru_conv_b,
        lru_wa=lru_wa, lru_ba=lru_ba, lru_wx=lru_wx, lru_bx=lru_bx, lru_lambda=lru_lambda,
        w_out=w_out, ffn2_pre_g=ffn2_pre_g, ffn2_post_g=ffn2_post_g, ffn2_wg=ffn2_wg,
        ffn2_wu=ffn2_wu, ffn2_wd=ffn2_wd))

    xm = jnp.broadcast_to(meta_tokens.astype(F32)[None], (bp, N_META, D_MODEL))
    xp = x_prompt
    xs = x_sample
    sample_steps = ((dec_seq + V7X_SUBLANES - 1) // V7X_SUBLANES) * V7X_SUBLANES
    tail3 = lambda b, t: (b, 0, 0)
    vec2 = lambda b, t: (b, 0)
    seq_state = lambda b, c: (b, 0, 0, 0)
    zero_states = ((jnp.zeros((bp, CONV_TAIL, SSD_CONV_DIM), F32), tail3),
                   (jnp.zeros((bp, CONV_TAIL, D_LRU), F32), tail3),
                   (jnp.zeros((bp, D_LRU), F32), vec2),
                   (jnp.zeros((bp, SSD_HEADS, SSD_HEAD_DIM, SSD_STATE), F32), seq_state))

    layer_state = lambda b, c, l: (l, b, 0, 0, 0)
    p_out = [[] for _ in range(3)]
    s_out = [[] for _ in range(3)]
    m_ssd = p_ssd = s_ssd = None
    def flat(a):
        return a.reshape(a.shape[0] * a.shape[1], a.shape[2])

    for l in range(depth):
        at_l = functools.partial(layer_state, l=l)
        shapes = (xp.shape, xs.shape, xm.shape)
        xp, xs, xm = (a.reshape(s) for a, s in
                      zip(_ffn_call([flat(xp), flat(xs), flat(xm)], l, w, "ffn1"), shapes))
        m_ssd_y, m_lru_y, m_st, m_ssd = _segment_mixer(
            xm, l, w, zero_states, m_ssd,
            bsub=bp, steps=N_META, chunk=N_META, pad_steps=0, seqs=PROMPT_SEQS)
        p_ssd_y, p_lru_y, p_st, p_ssd = _segment_mixer(
            xp, l, w, ((m_st[0], tail3), (m_st[2], tail3), (m_st[1], vec2), (m_ssd, at_l)), p_ssd,
            bsub=bp, steps=PROMPT_STEPS, chunk=SSD_CHUNK, pad_steps=0, seqs=PROMPT_SEQS,
            n_sub=PROMPT_SUB_TILES)
        s_ssd_y, s_lru_y, s_st, s_ssd = _segment_mixer(
            xs, l, w, ((state_ssd_conv, lambda b, t, l=l: (l, b, 0, 0)),
                       (state_lru_conv, lambda b, t, l=l: (l, b, 0, 0)),
                       (state_lru, lambda b, t, l=l: (l, b, 0)),
                       (state_ssd, at_l)), s_ssd,
            bsub=SAMPLE_BATCH_TILE, steps=dec_seq, chunk=sample_steps,
            pad_steps=sample_steps - dec_seq, seqs=SAMPLE_SEQS)
        (xp,) = _out_ffn_call([(flat(xp), flat(p_ssd_y), flat(p_lru_y))], l, w)
        xs, xm = _out_ffn_call([(flat(xs), flat(s_ssd_y), flat(s_lru_y)),
                                (flat(xm), flat(m_ssd_y), flat(m_lru_y))], l, w)
        xp, xs, xm = (a.reshape(s) for a, s in zip((xp, xs, xm), shapes))
        for acc, st in ((p_out, p_st), (s_out, s_st)):
            for k in range(3):
                acc[k].append(st[k])

    p_conv, p_lru, p_lconv = (jnp.stack(a) for a in p_out)
    s_conv, s_lru, s_lconv = (jnp.stack(a) for a in s_out)
    return (xp, xs, p_ssd, p_conv, p_lru, p_lconv, s_ssd, s_conv, s_lru, s_lconv)
```

```python
import functools
import itertools

import jax
import jax.numpy as jnp
from jax import lax
from jax.experimental import pallas as pl
from jax.experimental.pallas import tpu as pltpu

F32 = jnp.float32
BF16 = jnp.bfloat16

D_MODEL = 1024
D_SSD = 1024
D_LRU = 1024
SSD_HEADS = 16
SSD_HEAD_DIM = 64
SSD_GROUPS = 2
SSD_STATE = 128
HEADS_PER_GROUP = SSD_HEADS // SSD_GROUPS
CONV_W = 4
CONV_TAIL = CONV_W - 1
SSD_CONV_DIM = D_SSD + 2 * SSD_GROUPS * SSD_STATE
LRU_BLOCKS = 16
LRU_BLOCK_W = D_LRU // LRU_BLOCKS
LRU_C = 8.0
EPS = 1e-6
N_META = 16
SSD_CHUNK = 128

V7X_LANES = 128
V7X_SUBLANES = 8
V7X_MXU_DIM = 256
V7X_VMEM_LIMIT_BYTES = 56 * 1024 * 1024

DT_PAD = V7X_LANES
LRU_GATE_GROUPS = D_LRU // V7X_MXU_DIM
COL_Z = 0
COL_XBC = COL_Z + D_SSD
COL_GATE = COL_XBC + SSD_CONV_DIM
COL_XR = COL_GATE + D_LRU
COL_DT = COL_XR + D_LRU
IN_COLS_PAD = COL_DT + DT_PAD
GELU_K = 0.7978845608028654

FFN_ROWS = 512
FFN_SUB_TILES = 2
OUT_FFN_ROWS = 512
PROMPT_STEPS = 64
SAMPLE_BATCH_TILE = 64
PROMPT_SUB_TILES = 2
SCAN_UNROLL_LIMIT = 128
PROMPT_SEQS = 4
SAMPLE_SEQS = 8
ELEMENTWISE_ROWS = 16
ELEMENTWISE_COLS = 512


def _rms(x, g):
    return x * lax.rsqrt(jnp.mean(x * x, axis=-1, keepdims=True) + EPS) * g


def _silu(x):
    return x * _sigmoid(x)


def _sigmoid(x):
    return 0.5 * jnp.tanh(0.5 * x) + 0.5


def _softplus(x):
    return jnp.maximum(x, 0.0) + jnp.log1p(jnp.exp(-jnp.abs(x)))


def _gelu_tanh(x):
    return 0.5 * x * (1.0 + jnp.tanh(GELU_K * (x + 0.044715 * (x * x * x))))


def _dot(a, b):
    return jnp.dot(a, b, preferred_element_type=F32)


def _layer_spec(tail, l):
    zeros = (0,) * len(tail)
    return pl.BlockSpec((None,) + tuple(tail), lambda *_: (l,) + zeros, pipeline_mode=pl.Buffered(1))


def _params(n_axes):
    return pltpu.CompilerParams(dimension_semantics=("arbitrary",) * n_axes,
                                vmem_limit_bytes=V7X_VMEM_LIMIT_BYTES)


def _row_tile(rows, want):
    tm = min(rows, want)
    assert rows % tm == 0
    return tm


def _ffn_math(x, gpre, gpost, wg_ref, wu_ref, wd_ref):
    n_sub = FFN_SUB_TILES if x.shape[0] >= FFN_ROWS else 1
    rows = x.shape[0] // n_sub
    outs = [None] * n_sub

    def sub_tile(i):
        xi = x[i * rows:(i + 1) * rows]
        xn = _rms(xi, gpre).astype(BF16)
        yield
        hg = _dot(xn, wg_ref[...])
        hu = _dot(xn, wu_ref[...])
        yield
        a = (_silu(hg) * hu).astype(BF16)
        yield
        y = _dot(a, wd_ref[...])
        yield
        outs[i] = xi + 0.5 * _rms(y, gpost)

    _run_skewed([sub_tile(i) for i in range(n_sub)])
    return outs[0] if n_sub == 1 else jnp.concatenate(outs, axis=0)


def _run_skewed(stage_generators):
    waiting, running = list(stage_generators), []
    while waiting or running:
        if waiting:
            running.append(waiting.pop(0))
        for stages in list(running):
            if next(stages, "done") == "done":
                running.remove(stages)


def _token_call(math, segments, weights, weight_specs, name, tile_rows):
    n_ops = len(segments[0])
    tiles = [_row_tile(seg[0].shape[0], tile_rows) for seg in segments]
    counts = [seg[0].shape[0] // tm for seg, tm in zip(segments, tiles)]
    starts = [sum(counts[:k]) for k in range(len(segments))]

    def seg_spec(k):
        mode = {} if counts[k] > 1 else dict(pipeline_mode=pl.Buffered(1))
        return pl.BlockSpec((tiles[k], D_MODEL),
                            lambda i: (jnp.clip(i - starts[k], 0, counts[k] - 1), 0), **mode)

    def body(*refs):
        n_in = n_ops * len(segments)
        x_refs, w_refs, o_refs = refs[:n_in], refs[n_in:n_in + len(weights)], refs[n_in + len(weights):]
        i = pl.program_id(0)

        def run(k):
            operands = [r[...] for r in x_refs[k * n_ops:(k + 1) * n_ops]]
            o_refs[k][...] = math(*operands, *w_refs)

        for k in range(len(segments)):
            pl.when((i >= starts[k]) & (i < starts[k] + counts[k]))(functools.partial(run, k))

    return pl.pallas_call(
        body,
        grid=(sum(counts),),
        in_specs=[seg_spec(k) for k in range(len(segments)) for _ in range(n_ops)] + weight_specs,
        out_specs=[seg_spec(k) for k in range(len(segments))],
        out_shape=[jax.ShapeDtypeStruct(seg[0].shape, F32) for seg in segments],
        compiler_params=_params(1),
        name=name,
    )(*[a for seg in segments for a in seg], *weights)


def _ffn_math_refs(x, gpre_ref, gpost_ref, wg_ref, wu_ref, wd_ref):
    return _ffn_math(x, gpre_ref[...], gpost_ref[...], wg_ref, wu_ref, wd_ref)


def _ffn_call(xs, l, w, prefix):
    d_ff = w[prefix + "_wg"].shape[-1]
    return _token_call(
        _ffn_math_refs, [(x,) for x in xs],
        [w[prefix + "_pre_g"], w[prefix + "_post_g"], w[prefix + "_wg"], w[prefix + "_wu"],
         w[prefix + "_wd"]],
        [_layer_spec((1, D_MODEL), l), _layer_spec((1, D_MODEL), l),
         _layer_spec((D_MODEL, d_ff), l), _layer_spec((D_MODEL, d_ff), l),
         _layer_spec((d_ff, D_MODEL), l)],
        prefix, FFN_ROWS)


def _out_ffn_math(x, ys, yl, wos_ref, wol_ref, gmix_ref, gpre_ref, gpost_ref, wg_ref, wu_ref, wd_ref):
    m = _dot(ys.astype(BF16), wos_ref[...]) + _dot(yl.astype(BF16), wol_ref[...])
    x1 = x + _rms(m, gmix_ref[...])
    return _ffn_math(x1, gpre_ref[...], gpost_ref[...], wg_ref, wu_ref, wd_ref)


def _out_ffn_call(segments, l, w):
    d_ff = w["ffn2_wg"].shape[-1]
    return _token_call(
        _out_ffn_math, segments,
        [w["w_out"], w["w_out"], w["mix_post_g"], w["ffn2_pre_g"], w["ffn2_post_g"],
         w["ffn2_wg"], w["ffn2_wu"], w["ffn2_wd"]],
        [pl.BlockSpec((None, D_SSD, D_MODEL), lambda i: (l, 0, 0), pipeline_mode=pl.Buffered(1)),
         pl.BlockSpec((None, D_LRU, D_MODEL), lambda i: (l, D_SSD // D_LRU, 0),
                      pipeline_mode=pl.Buffered(1)),
         _layer_spec((1, D_MODEL), l), _layer_spec((1, D_MODEL), l), _layer_spec((1, D_MODEL), l),
         _layer_spec((D_MODEL, d_ff), l), _layer_spec((D_MODEL, d_ff), l),
         _layer_spec((d_ff, D_MODEL), l)],
        "out_ffn2", OUT_FFN_ROWS)


def _in_proj_body(bsub, steps, pad_steps, n_sub,
                  x_ref, g_ref, w_ref, scw_ref, scb_ref, lcw_ref, lcb_ref, dtb_ref,
                  wa_ref, wx_ref, ba_ref, bx_ref, lam_ref, sconv0_ref, lconv0_ref, h0_ref,
                  z_ref, xbc_ref, dt_ref, ylru_ref, sconv_ref, lconv_ref, hout_ref,
                  stage, xbuf, sbuf, lbuf, gate_buf, ra_buf, rx_buf, h_carry):
    rows = steps * bsub
    tail = CONV_TAIL * bsub
    sub_steps = steps // n_sub
    sub_rows = sub_steps * bsub
    rc = min(sub_rows, ELEMENTWISE_ROWS)
    strided = sub_steps >= bsub

    def lane_tile(j):
        return slice(j * V7X_LANES, (j + 1) * V7X_LANES)

    def load_block(ref, t0):
        r0 = t0 * bsub
        if not strided:
            for t in range(sub_steps):
                xbuf[r0 + t * bsub:r0 + (t + 1) * bsub, :] = ref[:, t0 + t, :]
            return xbuf[r0:r0 + sub_rows, :]
        n_tiles = ref.shape[2] // V7X_LANES
        for j in range(n_tiles):
            for b in range(bsub):
                stage[j, pl.ds(r0 + b, sub_steps, stride=bsub), :] = ref[b, t0:t0 + sub_steps, lane_tile(j)]
        return jnp.concatenate([stage[j, r0:r0 + sub_rows, :] for j in range(n_tiles)], axis=1)

    def store_block(ref, t0, read):
        r0 = t0 * bsub
        if not strided:
            value = read(slice(0, ref.shape[2]))
            for t in range(sub_steps):
                ref[:, t0 + t, :] = value[t * bsub:(t + 1) * bsub, :]
            return
        n_tiles = ref.shape[2] // V7X_LANES
        for j in range(n_tiles):
            stage[j, r0:r0 + sub_rows, :] = read(lane_tile(j))
        for j in range(n_tiles):
            for b in range(bsub):
                ref[b, t0:t0 + sub_steps, lane_tile(j)] = stage[j, pl.ds(r0 + b, sub_steps, stride=bsub), :]

    def conv_in_place(buf, w_ref, b_ref, act, r_lo):
        sub = V7X_SUBLANES
        for c0 in range(0, buf.shape[1], ELEMENTWISE_COLS):
            cols = slice(c0, c0 + ELEMENTWISE_COLS)
            taps = [w_ref[k * sub:(k + 1) * sub, cols] for k in range(CONV_W)]
            bias = b_ref[:, cols]
            for r0 in range(r_lo, r_lo + sub_rows, sub):
                acc = bias + buf[r0:r0 + sub, cols] * taps[0]
                for k in range(1, CONV_W):
                    acc = acc + buf[r0 + k * bsub:r0 + k * bsub + sub, cols] * taps[k]
                buf[r0:r0 + sub, cols] = act(acc)

    @pl.when(pl.program_id(1) == 0)
    def _():
        for k in range(CONV_TAIL):
            sbuf[k * bsub:(k + 1) * bsub, :] = sconv0_ref[k]
            lbuf[k * bsub:(k + 1) * bsub, :] = lconv0_ref[k]
        h_carry[...] = h0_ref[...]

    neg_c_softplus = -LRU_C * _softplus(-lam_ref[...])

    def sub_tile(i):
        t0 = i * sub_steps
        r_lo = t0 * bsub
        out = slice(r_lo, r_lo + sub_rows)
        pre = slice(tail + r_lo, tail + r_lo + sub_rows)
        xn = _rms(load_block(x_ref, t0), g_ref[...]).astype(BF16)
        yield
        lbuf[pre, :] = _dot(xn, w_ref[:, COL_XR:COL_DT])
        gate_buf[out, :] = _dot(xn, w_ref[:, COL_GATE:COL_XR])
        yield
        conv_in_place(lbuf, lcw_ref, lcb_ref, lambda v: v, r_lo)
        yield
        for q in range(LRU_GATE_GROUPS):
            cols = slice(q * V7X_MXU_DIM, (q + 1) * V7X_MXU_DIM)
            xr_bf = lbuf[out, cols].astype(BF16)
            ra_buf[out, cols] = _dot(xr_bf, wa_ref[q])
            rx_buf[out, cols] = _dot(xr_bf, wx_ref[q])
        yield
        for c0 in range(0, D_LRU, ELEMENTWISE_COLS):
            cols = slice(c0, c0 + ELEMENTWISE_COLS)
            ncs, ba, bx = neg_c_softplus[:, cols], ba_ref[:, cols], bx_ref[:, cols]
            for r0 in range(r_lo, r_lo + sub_rows, rc):
                sl = slice(r0, r0 + rc)
                log_a = ncs * _sigmoid(ra_buf[sl, cols] + ba)
                a = jnp.exp(log_a)
                m = -jnp.tanh(log_a) * (a * a + 1.0)
                mult = jnp.where(m > 0.0, m * lax.rsqrt(m), 0.0)
                rx_buf[sl, cols] = mult * _sigmoid(rx_buf[sl, cols] + bx) * lbuf[sl, cols]
                ra_buf[sl, cols] = a
                gate_buf[sl, cols] = _gelu_tanh(gate_buf[sl, cols])
        yield
        z = _dot(xn, w_ref[:, COL_Z:COL_XBC])
        sbuf[pre, :] = _dot(xn, w_ref[:, COL_XBC:COL_GATE])
        dt = _softplus(_dot(xn, w_ref[:, COL_DT:IN_COLS_PAD]) + dtb_ref[...])
        yield
        store_block(z_ref, t0, lambda cols: z[:, cols])
        store_block(dt_ref, t0, lambda cols: dt[:, cols])
        conv_in_place(sbuf, scw_ref, scb_ref, _silu, r_lo)
        yield
        store_block(xbc_ref, t0, lambda cols: sbuf[out, cols])

    _run_skewed([sub_tile(i) for i in range(n_sub)])

    for k in range(CONV_TAIL):
        sconv_ref[k] = sbuf[rows + k * bsub:rows + (k + 1) * bsub, :]
        lconv_ref[k] = lbuf[rows + k * bsub:rows + (k + 1) * bsub, :]

    def scan_step(b0, t, h):
        start = t * bsub + b0
        if not isinstance(start, int):
            start = pl.multiple_of(start, V7X_SUBLANES)
        sl = pl.ds(start, V7X_SUBLANES)
        h = ra_buf[sl, :] * h + rx_buf[sl, :]
        gate_buf[sl, :] = h * gate_buf[sl, :]
        return h

    n_groups = bsub // V7X_SUBLANES
    if n_groups * steps <= SCAN_UNROLL_LIMIT:
        for bg in range(n_groups):
            hsl = slice(bg * V7X_SUBLANES, (bg + 1) * V7X_SUBLANES)
            h = h_carry[hsl, :]
            for t in range(steps):
                h = scan_step(bg * V7X_SUBLANES, t, h)
            h_carry[hsl, :] = h
    else:
        def group_scan(bg, carry):
            b0 = pl.multiple_of(bg * V7X_SUBLANES, V7X_SUBLANES)
            hsl = pl.ds(b0, V7X_SUBLANES)
            h_carry[hsl, :] = lax.fori_loop(0, steps, functools.partial(scan_step, b0), h_carry[hsl, :])
            return carry

        lax.fori_loop(0, n_groups, group_scan, 0)
    for i in range(n_sub):
        store_block(ylru_ref, i * sub_steps,
                    lambda cols, i=i: gate_buf[i * sub_rows:(i + 1) * sub_rows, cols])
    hout_ref[...] = h_carry[...]
    sbuf[0:tail, :] = sbuf[rows:rows + tail, :]
    lbuf[0:tail, :] = lbuf[rows:rows + tail, :]
    for t in range(steps, steps + pad_steps):
        for ref in (z_ref, xbc_ref, dt_ref):
            ref[:, t, :] = jnp.zeros((bsub, ref.shape[2]), F32)


def _in_proj_call(x, l, w, sconv0, sconv0_map, lconv0, lconv0_map, h0, h0_map, *,
                  bsub, steps, pad_steps=0, n_sub=1):
    batch, length, _ = x.shape
    n_t = length // steps
    n_b = batch // bsub
    assert length == n_t * steps and batch == n_b * bsub and bsub % V7X_SUBLANES == 0
    assert pad_steps == 0 or n_t == 1
    assert steps >= CONV_TAIL
    assert steps % n_sub == 0
    strided = steps // n_sub >= bsub
    out_steps = steps + pad_steps
    rows = steps * bsub
    tail = CONV_TAIL * bsub

    def tile(n_steps, width):
        return pl.BlockSpec((bsub, n_steps, width), lambda b, t: (b, t, 0))

    def squeeze_lead(a, block):
        return (None,) * (a.ndim - len(block)) + block

    def conv_out(width):
        return pl.BlockSpec((CONV_TAIL, bsub, width), lambda b, t: (0, b, 0))

    return pl.pallas_call(
        functools.partial(_in_proj_body, bsub, steps, pad_steps, n_sub),
        grid=(n_b, n_t),
        in_specs=[
            tile(steps, D_MODEL),
            _layer_spec((1, D_MODEL), l),
            _layer_spec((D_MODEL, IN_COLS_PAD), l),
            _layer_spec((CONV_W * V7X_SUBLANES, SSD_CONV_DIM), l),
            _layer_spec((V7X_SUBLANES, SSD_CONV_DIM), l),
            _layer_spec((CONV_W * V7X_SUBLANES, D_LRU), l), _layer_spec((V7X_SUBLANES, D_LRU), l),
            _layer_spec((1, DT_PAD), l),
            _layer_spec((LRU_GATE_GROUPS, V7X_MXU_DIM, V7X_MXU_DIM), l),
            _layer_spec((LRU_GATE_GROUPS, V7X_MXU_DIM, V7X_MXU_DIM), l),
            _layer_spec((1, D_LRU), l), _layer_spec((1, D_LRU), l), _layer_spec((1, D_LRU), l),
            pl.BlockSpec(squeeze_lead(sconv0, (CONV_TAIL, bsub, SSD_CONV_DIM)), sconv0_map),
            pl.BlockSpec(squeeze_lead(lconv0, (CONV_TAIL, bsub, D_LRU)), lconv0_map),
            pl.BlockSpec(squeeze_lead(h0, (bsub, D_LRU)), h0_map),
        ],
        out_specs=[
            tile(out_steps, D_SSD), tile(out_steps, SSD_CONV_DIM), tile(out_steps, DT_PAD),
            tile(steps, D_LRU),
            conv_out(SSD_CONV_DIM), conv_out(D_LRU),
            pl.BlockSpec((bsub, D_LRU), lambda b, t: (b, 0)),
        ],
        out_shape=[
            jax.ShapeDtypeStruct((batch, n_t * out_steps, D_SSD), F32),
            jax.ShapeDtypeStruct((batch, n_t * out_steps, SSD_CONV_DIM), F32),
            jax.ShapeDtypeStruct((batch, n_t * out_steps, DT_PAD), F32),
            jax.ShapeDtypeStruct((batch, length, D_LRU), F32),
            jax.ShapeDtypeStruct((CONV_TAIL, batch, SSD_CONV_DIM), F32),
            jax.ShapeDtypeStruct((CONV_TAIL, batch, D_LRU), F32),
            jax.ShapeDtypeStruct((batch, D_LRU), F32),
        ],
        scratch_shapes=[
            pltpu.VMEM((SSD_CONV_DIM // V7X_LANES, rows, V7X_LANES) if strided
                       else (1, V7X_SUBLANES, V7X_LANES), F32),
            pltpu.VMEM((V7X_SUBLANES, V7X_LANES) if strided else (rows, D_MODEL), F32),
            pltpu.VMEM((tail + rows, SSD_CONV_DIM), F32),
            pltpu.VMEM((tail + rows, D_LRU), F32),
            pltpu.VMEM((rows, D_LRU), F32),
            pltpu.VMEM((rows, D_LRU), F32),
            pltpu.VMEM((rows, D_LRU), F32),
            pltpu.VMEM((bsub, D_LRU), F32),
        ],
        compiler_params=_params(2),
        name="in_proj_lru",
    )(x, w["mix_pre_g"], w["w_in"], w["ssd_conv_w"], w["ssd_conv_b"], w["lru_conv_w"], w["lru_conv_b"],
      w["ssd_dt_bias"], w["lru_wa"], w["lru_wx"], w["lru_ba"], w["lru_bx"], w["lru_lambda"],
      sconv0, lconv0, h0)


def _transpose_rows(x, rows):
    lanes = x.shape[1]
    if rows < lanes:
        x = jnp.concatenate([x, jnp.zeros((lanes - rows, lanes), x.dtype)], axis=0)
    return x.T[:, 0:rows]


def _split_bf16(x, terms):
    parts = []
    for _ in range(terms - 1):
        parts.append(x.astype(BF16))
        x = x - parts[-1].astype(F32)
    return parts + [x.astype(BF16)]


def _ssd_body(chunk, n_chunks, out_steps, seqs, first_layer, xbc_ref, dt_ref, z_ref, alog_ref, dvec_ref,
              g_ref, e_ref, h0_ref, *rest):
    (y_ref, hout_ref, ht_s) = rest[-3:]
    if first_layer is not None:
        @pl.when(pl.program_id(1) == 0)
        def _():
            for k in range(hout_ref.shape[0]):
                if k != first_layer:
                    hout_ref[k] = jnp.zeros(hout_ref.shape[1:], F32)
        hout_ref = hout_ref.at[first_layer]
    stages = [_ssd_sequence(chunk, n_chunks, out_steps, xbc_ref.at[s], dt_ref.at[s], z_ref.at[s],
                            alog_ref, dvec_ref, g_ref, e_ref, h0_ref.at[s], y_ref.at[s],
                            hout_ref.at[s], ht_s.at[s]) for s in range(seqs)]
    for _ in itertools.zip_longest(*stages):
        pass


def _ssd_sequence(chunk, n_chunks, out_steps, xbc_ref, dt_ref, z_ref, alog_ref, dvec_ref, g_ref, e_ref,
                  h0_ref, y_ref, hout_ref, ht_s):
    c = pl.program_id(1)
    group_cols = HEADS_PER_GROUP * SSD_HEAD_DIM

    single = n_chunks == 1
    if single:
        h_given = h0_ref[...].reshape(D_SSD, SSD_STATE)
        ht_bf = h_given.astype(BF16).T
    else:
        @pl.when(c == 0)
        def _():
            ht_s[...] = h0_ref[...].reshape(D_SSD, SSD_STATE).T

    dt = dt_ref[...]
    a = -jnp.exp(alog_ref[...])
    row = lax.broadcasted_iota(jnp.int32, (chunk, chunk), 0)
    col = lax.broadcasted_iota(jnp.int32, (chunk, chunk), 1)
    causal = row >= col
    ones_lower = jnp.where(causal, 1.0, 0.0).astype(BF16)
    da_hi, da_mid, da_lo = _split_bf16(dt * a, 3)
    cum = _dot(ones_lower, da_hi) + _dot(ones_lower, da_mid) + _dot(ones_lower, da_lo)
    yield
    cum_t = _transpose_rows(cum, chunk)
    dt_t = _transpose_rows(dt, chunk)
    tot = cum[chunk - 1:chunk, :]
    cdec = jnp.exp(tot)
    factors = jnp.concatenate([jnp.exp(cum), jnp.exp(tot - cum) * dt], axis=0)
    spread = _dot(factors.astype(BF16), e_ref[...])
    yield
    ecum_x = spread[0:chunk, :]
    wend_x = spread[chunk:2 * chunk, :]
    if not single:
        cdec_parts = _split_bf16(jnp.broadcast_to(cdec, (V7X_SUBLANES, DT_PAD)), 3)
        cdec_x = (_dot(cdec_parts[0], e_ref[...]) + _dot(cdec_parts[1], e_ref[...])
                  + _dot(cdec_parts[2], e_ref[...]))[0:1, :]
    low_half = lax.broadcasted_iota(jnp.int32, (chunk, V7X_LANES), 1) < SSD_HEAD_DIM

    y_groups = []
    for g in range(SSD_GROUPS):
        b0 = D_SSD + g * SSD_STATE
        c0 = D_SSD + SSD_GROUPS * SSD_STATE + g * SSD_STATE
        gcols = slice(g * group_cols, (g + 1) * group_cols)
        bg_t = _transpose_rows(xbc_ref[:, b0:b0 + SSD_STATE], chunk).astype(BF16)
        cg = xbc_ref[:, c0:c0 + SSD_STATE].astype(BF16)
        cb = _dot(cg, bg_t)
        yield
        y_pairs = []
        for pair in range(HEADS_PER_GROUP // 2):
            h = g * HEADS_PER_GROUP + 2 * pair
            x_pair = xbc_ref[:, h * SSD_HEAD_DIM:(h + 2) * SSD_HEAD_DIM]
            scores = []
            for hh in (h, h + 1):
                diff = cum[:, hh:hh + 1] - cum_t[hh:hh + 1, :]
                decay = jnp.exp(jnp.where(causal, diff, -jnp.inf))
                scores.append((cb * decay * dt_t[hh:hh + 1, :]).astype(BF16))
            x_lo = jnp.where(low_half, x_pair, 0.0).astype(BF16)
            x_hi = jnp.where(low_half, 0.0, x_pair).astype(BF16)
            if chunk % V7X_LANES == 0:
                y_pairs.append(_dot(jnp.concatenate(scores, axis=1),
                                    jnp.concatenate([x_lo, x_hi], axis=0)))
            else:
                y_pairs.append(_dot(scores[0], x_lo) + _dot(scores[1], x_hi))
            yield
        xg = xbc_ref[:, gcols]
        xw = (xg * wend_x[:, gcols]).astype(BF16)
        if single:
            y_off = _dot(cg, ht_bf[:, gcols]) * ecum_x[:, gcols]
            bg = xbc_ref[:, b0:b0 + SSD_STATE].astype(BF16)
            update = lax.dot_general(xw, bg, (((0,), (0,)), ((), ())), preferred_element_type=F32)
            heads = range(g * HEADS_PER_GROUP, (g + 1) * HEADS_PER_GROUP)
            decay_rows = jnp.concatenate(
                [jnp.broadcast_to(cdec[:, h:h + 1], (SSD_HEAD_DIM, SSD_STATE)) for h in heads], axis=0)
            h_new = decay_rows * h_given[gcols, :] + update
            hout_ref[g * HEADS_PER_GROUP:(g + 1) * HEADS_PER_GROUP] = h_new.reshape(
                HEADS_PER_GROUP, SSD_HEAD_DIM, SSD_STATE)
        else:
            ht_prev = ht_s[:, gcols]
            y_off = _dot(cg, ht_prev.astype(BF16)) * ecum_x[:, gcols]
            ht_s[:, gcols] = cdec_x[:, gcols] * ht_prev + _dot(bg_t, xw)
        y_groups.append(jnp.concatenate(y_pairs, axis=1) + y_off + dvec_ref[:, gcols] * xg)
        yield

    y = _rms(jnp.concatenate(y_groups, axis=1) * _silu(z_ref[...]), g_ref[...])
    y_ref[...] = y[0:out_steps, :]

    if not single:
        @pl.when(c == n_chunks - 1)
        def _():
            hout_ref[...] = ht_s[...].T.reshape(SSD_HEADS, SSD_HEAD_DIM, SSD_STATE)


def _ssd_call(xbc, dt, z, l, w, h0, h0_map, h_stack, *, chunk, out_len, seqs):
    batch, length, _ = xbc.shape
    depth = w["ssd_a_log"].shape[0]
    n_chunks = length // chunk
    assert length == n_chunks * chunk and (out_len == length or n_chunks == 1) and batch % seqs == 0
    out_steps = min(chunk, out_len)

    def per_seq(n_steps, width):
        return pl.BlockSpec((seqs, n_steps, width), lambda b, c: (b, c, 0))

    state_tail = (SSD_HEADS, SSD_HEAD_DIM, SSD_STATE)
    h0_block = (None,) * (h0.ndim - 4) + (seqs,) + state_tail
    operands = [xbc, dt, z, w["ssd_a_log"], w["ssd_d_cols"], w["ssd_norm_g"], w["head_spread"], h0]
    in_specs = [
        per_seq(chunk, SSD_CONV_DIM), per_seq(chunk, DT_PAD), per_seq(chunk, D_SSD),
        _layer_spec((1, DT_PAD), l), _layer_spec((1, D_SSD), l), _layer_spec((1, D_SSD), l),
        pl.BlockSpec((DT_PAD, D_SSD), lambda b, c: (0, 0), pipeline_mode=pl.Buffered(1)),
        pl.BlockSpec(h0_block, h0_map),
    ]
    aliases = {}
    if h_stack is not None:
        aliases = {len(operands): 1}
        operands.append(h_stack)
        in_specs.append(pl.BlockSpec(memory_space=pl.ANY))
    first = h_stack is None
    return pl.pallas_call(
        functools.partial(_ssd_body, chunk, n_chunks, out_steps, seqs, l if first else None),
        grid=(batch // seqs, n_chunks),
        in_specs=in_specs,
        out_specs=[
            per_seq(out_steps, D_SSD),
            pl.BlockSpec((depth, seqs) + state_tail, lambda b, c: (0, b, 0, 0, 0)) if first
            else pl.BlockSpec((None, seqs) + state_tail, lambda b, c: (l, b, 0, 0, 0)),
        ],
        out_shape=[
            jax.ShapeDtypeStruct((batch, out_len, D_SSD), F32),
            jax.ShapeDtypeStruct((depth, batch) + state_tail, F32),
        ],
        scratch_shapes=[pltpu.VMEM((seqs, SSD_STATE, D_SSD), F32)],
        input_output_aliases=aliases,
        compiler_params=_params(2),
        name="ssd_chunk",
    )(*operands)


def _regroup_w_in_body(wt_ref, o_ref):
    o_dt = COL_XBC + SSD_CONV_DIM
    o_gate = o_dt + SSD_HEADS

    def put(c0, rows):
        o_ref[:, c0:c0 + V7X_LANES] = rows.T.astype(BF16)

    for c0 in range(COL_Z, COL_GATE, V7X_LANES):
        put(c0, wt_ref[c0:c0 + V7X_LANES, :])
    for c0 in range(COL_GATE, COL_DT, V7X_LANES):
        r0 = o_gate + c0 - COL_GATE
        put(c0, wt_ref[r0:r0 + V7X_LANES, :])
    put(COL_DT, jnp.concatenate([wt_ref[o_dt:o_gate, :],
                                 jnp.zeros((DT_PAD - SSD_HEADS, D_MODEL), F32)], axis=0))


def _regroup_w_in(w_in):
    depth, d_model, in_cols = w_in.shape
    assert in_cols == D_SSD + SSD_CONV_DIM + SSD_HEADS + 2 * D_LRU and d_model == D_MODEL
    return pl.pallas_call(
        _regroup_w_in_body,
        grid=(depth,),
        in_specs=[pl.BlockSpec((None, in_cols, d_model), lambda l: (l, 0, 0),
                               pipeline_mode=pl.Buffered(1))],
        out_specs=pl.BlockSpec((None, d_model, IN_COLS_PAD), lambda l: (l, 0, 0)),
        out_shape=jax.ShapeDtypeStruct((depth, d_model, IN_COLS_PAD), BF16),
        compiler_params=_params(1),
        name="regroup_w_in",
    )(w_in.transpose(0, 2, 1))


def _prepare_weights(p):
    depth = p["w_in"].shape[0]

    def vec(a):
        return a.reshape(depth, 1, a.shape[-1])

    def pad_heads(a):
        return jnp.pad(a, ((0, 0), (0, DT_PAD - SSD_HEADS))).reshape(depth, 1, DT_PAD)

    def block_diag(a):
        per = V7X_MXU_DIM // LRU_BLOCK_W
        a = a.reshape(depth, LRU_GATE_GROUPS, per, LRU_BLOCK_W, LRU_BLOCK_W)
        eye = jnp.eye(per, dtype=a.dtype)
        a = a[:, :, :, :, None, :] * eye[None, None, :, None, :, None]
        return a.reshape(depth, LRU_GATE_GROUPS, V7X_MXU_DIM, V7X_MXU_DIM).astype(BF16)

    w = {
        "w_in": _regroup_w_in(p["w_in"]),
        "w_out": p["w_out"].astype(BF16),
        "lru_wa": block_diag(p["lru_wa"]),
        "lru_wx": block_diag(p["lru_wx"]),
        "ssd_dt_bias": pad_heads(p["ssd_dt_bias"]),
        "ssd_a_log": pad_heads(p["ssd_a_log"]),
        "ssd_d_cols": jnp.repeat(p["ssd_d"], SSD_HEAD_DIM, axis=-1).reshape(depth, 1, D_SSD),
        "head_spread": (jnp.arange(DT_PAD)[:, None] == jnp.arange(D_SSD)[None, :] // SSD_HEAD_DIM
                        ).astype(BF16),
    }
    for name in ("ssd_conv_w", "lru_conv_w"):
        w[name] = jnp.repeat(p[name], V7X_SUBLANES, axis=1)
    for name in ("ssd_conv_b", "lru_conv_b"):
        w[name] = jnp.repeat(p[name][:, None, :], V7X_SUBLANES, axis=1)
    for name in ("ffn1_wg", "ffn1_wu", "ffn1_wd", "ffn2_wg", "ffn2_wu", "ffn2_wd"):
        w[name] = p[name].astype(BF16)
    for name in ("ffn1_pre_g", "ffn1_post_g", "mix_pre_g", "mix_post_g", "ffn2_pre_g", "ffn2_post_g",
                 "ssd_norm_g", "lru_ba", "lru_bx", "lru_lambda"):
        w[name] = vec(p[name])
    return w


def _segment_mixer(x, l, w, states, ssd_stack, *, bsub, steps, chunk, pad_steps, seqs, n_sub=1):
    (sconv0, sconv0_map), (lconv0, lconv0_map), (h_lru0, h_lru0_map), (h_ssd0, h_ssd0_map) = states
    length = x.shape[1]
    z, xbc, dt, y_lru, sconv, lconv, h_lru = _in_proj_call(
        x, l, w, sconv0, sconv0_map, lconv0, lconv0_map,
        h_lru0, h_lru0_map, bsub=bsub, steps=steps, pad_steps=pad_steps, n_sub=n_sub)
    y_ssd, ssd_stack = _ssd_call(xbc, dt, z, l, w, h_ssd0, h_ssd0_map, ssd_stack,
                                 chunk=chunk, out_len=length, seqs=seqs)
    return y_ssd, y_lru, (sconv, h_lru, lconv), ssd_stack


def kernel(x_prompt, x_sample, state_ssd, state_ssd_conv, state_lru, state_lru_conv, meta_tokens,
           ffn1_pre_g, ffn1_post_g, ffn1_wg, ffn1_wu, ffn1_wd, mix_pre_g, mix_post_g, w_in,
           ssd_conv_w, ssd_conv_b, ssd_dt_bias, ssd_a_log, ssd_d, ssd_norm_g, lru_conv_w, lru_conv_b,
           lru_wa, lru_ba, lru_wx, lru_bx, lru_lambda, w_out, ffn2_pre_g, ffn2_post_g, ffn2_wg,
           ffn2_wu, ffn2_wd):
    bp, seq, _ = x_prompt.shape
    bs, dec_seq, _ = x_sample.shape
    depth = w_in.shape[0]
    assert bp == V7X_SUBLANES and seq % SSD_CHUNK == 0 and bs % SAMPLE_BATCH_TILE == 0
    w = _prepare_weights(dict(
        ffn1_pre_g=ffn1_pre_g, ffn1_post_g=ffn1_post_g, ffn1_wg=ffn1_wg, ffn1_wu=ffn1_wu,
        ffn1_wd=ffn1_wd, mix_pre_g=mix_pre_g, mix_post_g=mix_post_g, w_in=w_in,
        ssd_conv_w=ssd_conv_w, ssd_conv_b=ssd_conv_b, ssd_dt_bias=ssd_dt_bias, ssd_a_log=ssd_a_log,
        ssd_d=ssd_d, ssd_norm_g=ssd_norm_g, lru_conv_w=lru_conv_w, lru_conv_b=lru_conv_b,
        lru_wa=lru_wa, lru_ba=lru_ba, lru_wx=lru_wx, lru_bx=lru_bx, lru_lambda=lru_lambda,
        w_out=w_out, ffn2_pre_g=ffn2_pre_g, ffn2_post_g=ffn2_post_g, ffn2_wg=ffn2_wg,
        ffn2_wu=ffn2_wu, ffn2_wd=ffn2_wd))

    xm = jnp.broadcast_to(meta_tokens.astype(F32)[None], (bp, N_META, D_MODEL))
    xp = x_prompt
    xs = x_sample
    sample_steps = ((dec_seq + V7X_SUBLANES - 1) // V7X_SUBLANES) * V7X_SUBLANES
    sconv_in = state_ssd_conv.transpose(0, 2, 1, 3)
    lconv_in = state_lru_conv.transpose(0, 2, 1, 3)
    tail3 = lambda b, t: (0, b, 0)
    vec2 = lambda b, t: (b, 0)
    seq_state = lambda b, c: (b, 0, 0, 0)
    zero_states = ((jnp.zeros((CONV_TAIL, bp, SSD_CONV_DIM), F32), tail3),
                   (jnp.zeros((CONV_TAIL, bp, D_LRU), F32), tail3),
                   (jnp.zeros((bp, D_LRU), F32), vec2),
                   (jnp.zeros((bp, SSD_HEADS, SSD_HEAD_DIM, SSD_STATE), F32), seq_state))

    layer_state = lambda b, c, l: (l, b, 0, 0, 0)
    p_out = [[] for _ in range(3)]
    s_out = [[] for _ in range(3)]
    m_ssd = p_ssd = s_ssd = None
    def flat(a):
        return a.reshape(a.shape[0] * a.shape[1], a.shape[2])

    for l in range(depth):
        at_l = functools.partial(layer_state, l=l)
        shapes = (xp.shape, xs.shape, xm.shape)
        xp, xs, xm = (a.reshape(s) for a, s in
                      zip(_ffn_call([flat(xp), flat(xs), flat(xm)], l, w, "ffn1"), shapes))
        m_ssd_y, m_lru_y, m_st, m_ssd = _segment_mixer(
            xm, l, w, zero_states, m_ssd,
            bsub=bp, steps=N_META, chunk=N_META, pad_steps=0, seqs=PROMPT_SEQS)
        p_ssd_y, p_lru_y, p_st, p_ssd = _segment_mixer(
            xp, l, w, ((m_st[0], tail3), (m_st[2], tail3), (m_st[1], vec2), (m_ssd, at_l)), p_ssd,
            bsub=bp, steps=PROMPT_STEPS, chunk=SSD_CHUNK, pad_steps=0, seqs=PROMPT_SEQS,
            n_sub=PROMPT_SUB_TILES)
        s_ssd_y, s_lru_y, s_st, s_ssd = _segment_mixer(
            xs, l, w, ((sconv_in, lambda b, t, l=l: (l, 0, b, 0)),
                       (lconv_in, lambda b, t, l=l: (l, 0, b, 0)),
                       (state_lru, lambda b, t, l=l: (l, b, 0)),
                       (state_ssd, at_l)), s_ssd,
            bsub=SAMPLE_BATCH_TILE, steps=dec_seq, chunk=sample_steps,
            pad_steps=sample_steps - dec_seq, seqs=SAMPLE_SEQS)
        (xp,) = _out_ffn_call([(flat(xp), flat(p_ssd_y), flat(p_lru_y))], l, w)
        xs, xm = _out_ffn_call([(flat(xs), flat(s_ssd_y), flat(s_lru_y)),
                                (flat(xm), flat(m_ssd_y), flat(m_lru_y))], l, w)
        xp, xs, xm = (a.reshape(s) for a, s in zip((xp, xs, xm), shapes))
        for acc, st in ((p_out, p_st), (s_out, s_st)):
            for k in range(3):
                acc[k].append(st[k])

    p_conv, p_lru, p_lconv = (jnp.stack(a) for a in p_out)
    s_conv, s_lru, s_lconv = (jnp.stack(a) for a in s_out)
    p_conv, p_lconv, s_conv, s_lconv = (a.transpose(0, 2, 1, 3) for a in (p_conv, p_lconv, s_conv, s_lconv))
    return (xp, xs, p_ssd, p_conv, p_lru, p_lconv, s_ssd, s_conv, s_lru, s_lconv)
```

```python
import functools
import itertools

import jax
import jax.numpy as jnp
from jax import lax
from jax.experimental import pallas as pl
from jax.experimental.pallas import tpu as pltpu

F32 = jnp.float32
BF16 = jnp.bfloat16

D_MODEL = 1024
D_SSD = 1024
D_LRU = 1024
SSD_HEADS = 16
SSD_HEAD_DIM = 64
SSD_GROUPS = 2
SSD_STATE = 128
HEADS_PER_GROUP = SSD_HEADS // SSD_GROUPS
CONV_W = 4
CONV_TAIL = CONV_W - 1
SSD_CONV_DIM = D_SSD + 2 * SSD_GROUPS * SSD_STATE
LRU_BLOCKS = 16
LRU_BLOCK_W = D_LRU // LRU_BLOCKS
LRU_C = 8.0
EPS = 1e-6
N_META = 16
SSD_CHUNK = 128

V7X_LANES = 128
V7X_SUBLANES = 8
V7X_MXU_DIM = 256
V7X_VMEM_LIMIT_BYTES = 56 * 1024 * 1024

DT_PAD = V7X_LANES
LRU_GATE_GROUPS = D_LRU // V7X_MXU_DIM
COL_Z = 0
COL_XBC = COL_Z + D_SSD
COL_GATE = COL_XBC + SSD_CONV_DIM
COL_XR = COL_GATE + D_LRU
COL_DT = COL_XR + D_LRU
IN_COLS_PAD = COL_DT + DT_PAD
GELU_K = 0.7978845608028654

FFN_ROWS = 512
FFN_SUB_TILES = 2
OUT_FFN_ROWS = 512
PROMPT_STEPS = 64
SAMPLE_BATCH_TILE = 64
PROMPT_SUB_TILES = 2
SCAN_UNROLL_LIMIT = 128
PROMPT_SEQS = 4
SAMPLE_SEQS = 8
ELEMENTWISE_ROWS = 16
ELEMENTWISE_COLS = 512


def _rms(x, g):
    return x * lax.rsqrt(jnp.mean(x * x, axis=-1, keepdims=True) + EPS) * g


def _silu(x):
    return x * _sigmoid(x)


def _sigmoid(x):
    return 0.5 * jnp.tanh(0.5 * x) + 0.5


def _softplus(x):
    return jnp.maximum(x, 0.0) + jnp.log1p(jnp.exp(-jnp.abs(x)))


def _gelu_tanh(x):
    return 0.5 * x * (1.0 + jnp.tanh(GELU_K * (x + 0.044715 * (x * x * x))))


def _dot(a, b):
    return jnp.dot(a, b, preferred_element_type=F32)


def _layer_spec(tail, l):
    zeros = (0,) * len(tail)
    return pl.BlockSpec((None,) + tuple(tail), lambda *_: (l,) + zeros, pipeline_mode=pl.Buffered(1))


def _params(n_axes):
    return pltpu.CompilerParams(dimension_semantics=("arbitrary",) * n_axes,
                                vmem_limit_bytes=V7X_VMEM_LIMIT_BYTES)


def _row_tile(rows, want):
    tm = min(rows, want)
    assert rows % tm == 0
    return tm


def _ffn_math(x, gpre, gpost, wg_ref, wu_ref, wd_ref, prologue=None):
    n_sub = FFN_SUB_TILES if x.shape[0] >= FFN_ROWS else 1
    rows = x.shape[0] // n_sub
    outs = [None] * n_sub

    def sub_tile(i):
        sl = slice(i * rows, (i + 1) * rows)
        xi = x[sl]
        if prologue is not None:
            xi = prologue(xi, sl)
            yield
        xn = _rms(xi, gpre).astype(BF16)
        yield
        hg = _dot(xn, wg_ref[...])
        hu = _dot(xn, wu_ref[...])
        yield
        a = (_silu(hg) * hu).astype(BF16)
        yield
        y = _dot(a, wd_ref[...])
        yield
        outs[i] = xi + 0.5 * _rms(y, gpost)

    _run_skewed([sub_tile(i) for i in range(n_sub)])
    return outs[0] if n_sub == 1 else jnp.concatenate(outs, axis=0)


def _run_skewed(stage_generators):
    waiting, running = list(stage_generators), []
    while waiting or running:
        if waiting:
            running.append(waiting.pop(0))
        for stages in list(running):
            if next(stages, "done") == "done":
                running.remove(stages)


def _token_call(math, segments, weights, weight_specs, name, tile_rows):
    n_ops = len(segments[0])
    tiles = [_row_tile(seg[0].shape[0], tile_rows) for seg in segments]
    counts = [seg[0].shape[0] // tm for seg, tm in zip(segments, tiles)]
    starts = [sum(counts[:k]) for k in range(len(segments))]

    def seg_spec(k):
        mode = {} if counts[k] > 1 else dict(pipeline_mode=pl.Buffered(1))
        return pl.BlockSpec((tiles[k], D_MODEL),
                            lambda i: (jnp.clip(i - starts[k], 0, counts[k] - 1), 0), **mode)

    def body(*refs):
        n_in = n_ops * len(segments)
        x_refs, w_refs, o_refs = refs[:n_in], refs[n_in:n_in + len(weights)], refs[n_in + len(weights):]
        i = pl.program_id(0)

        def run(k):
            operands = [r[...] for r in x_refs[k * n_ops:(k + 1) * n_ops]]
            o_refs[k][...] = math(*operands, *w_refs)

        for k in range(len(segments)):
            pl.when((i >= starts[k]) & (i < starts[k] + counts[k]))(functools.partial(run, k))

    return pl.pallas_call(
        body,
        grid=(sum(counts),),
        in_specs=[seg_spec(k) for k in range(len(segments)) for _ in range(n_ops)] + weight_specs,
        out_specs=[seg_spec(k) for k in range(len(segments))],
        out_shape=[jax.ShapeDtypeStruct(seg[0].shape, F32) for seg in segments],
        compiler_params=_params(1),
        name=name,
    )(*[a for seg in segments for a in seg], *weights)


def _ffn_math_refs(x, gpre_ref, gpost_ref, wg_ref, wu_ref, wd_ref):
    return _ffn_math(x, gpre_ref[...], gpost_ref[...], wg_ref, wu_ref, wd_ref)


def _ffn_call(xs, l, w, prefix):
    d_ff = w[prefix + "_wg"].shape[-1]
    return _token_call(
        _ffn_math_refs, [(x,) for x in xs],
        [w[prefix + "_pre_g"], w[prefix + "_post_g"], w[prefix + "_wg"], w[prefix + "_wu"],
         w[prefix + "_wd"]],
        [_layer_spec((1, D_MODEL), l), _layer_spec((1, D_MODEL), l),
         _layer_spec((D_MODEL, d_ff), l), _layer_spec((D_MODEL, d_ff), l),
         _layer_spec((d_ff, D_MODEL), l)],
        prefix, FFN_ROWS)


def _out_ffn_math(x, ys, yl, wos_ref, wol_ref, gmix_ref, gpre_ref, gpost_ref, wg_ref, wu_ref, wd_ref):
    def mixer_residual(x_rows, sl):
        m = _dot(ys[sl].astype(BF16), wos_ref[...]) + _dot(yl[sl].astype(BF16), wol_ref[...])
        return x_rows + _rms(m, gmix_ref[...])

    return _ffn_math(x, gpre_ref[...], gpost_ref[...], wg_ref, wu_ref, wd_ref, mixer_residual)


def _out_ffn_call(segments, l, w):
    d_ff = w["ffn2_wg"].shape[-1]
    return _token_call(
        _out_ffn_math, segments,
        [w["w_out"], w["w_out"], w["mix_post_g"], w["ffn2_pre_g"], w["ffn2_post_g"],
         w["ffn2_wg"], w["ffn2_wu"], w["ffn2_wd"]],
        [pl.BlockSpec((None, D_SSD, D_MODEL), lambda i: (l, 0, 0), pipeline_mode=pl.Buffered(1)),
         pl.BlockSpec((None, D_LRU, D_MODEL), lambda i: (l, D_SSD // D_LRU, 0),
                      pipeline_mode=pl.Buffered(1)),
         _layer_spec((1, D_MODEL), l), _layer_spec((1, D_MODEL), l), _layer_spec((1, D_MODEL), l),
         _layer_spec((D_MODEL, d_ff), l), _layer_spec((D_MODEL, d_ff), l),
         _layer_spec((d_ff, D_MODEL), l)],
        "out_ffn2", OUT_FFN_ROWS)


def _in_proj_body(bsub, steps, pad_steps, n_sub,
                  x_ref, g_ref, w_ref, scw_ref, scb_ref, lcw_ref, lcb_ref, dtb_ref,
                  wa_ref, wx_ref, ba_ref, bx_ref, lam_ref, sconv0_ref, lconv0_ref, h0_ref,
                  z_ref, xbc_ref, dt_ref, ylru_ref, sconv_ref, lconv_ref, hout_ref,
                  stage, xbuf, sbuf, lbuf, gate_buf, ra_buf, rx_buf, h_carry):
    rows = steps * bsub
    tail = CONV_TAIL * bsub
    sub_steps = steps // n_sub
    sub_rows = sub_steps * bsub
    rc = min(sub_rows, ELEMENTWISE_ROWS)
    strided = sub_steps >= bsub

    def lane_tile(j):
        return slice(j * V7X_LANES, (j + 1) * V7X_LANES)

    def load_block(ref, t0):
        r0 = t0 * bsub
        if not strided:
            for t in range(sub_steps):
                xbuf[r0 + t * bsub:r0 + (t + 1) * bsub, :] = ref[:, t0 + t, :]
            return xbuf[r0:r0 + sub_rows, :]
        n_tiles = ref.shape[2] // V7X_LANES
        for j in range(n_tiles):
            for b in range(bsub):
                stage[j, pl.ds(r0 + b, sub_steps, stride=bsub), :] = ref[b, t0:t0 + sub_steps, lane_tile(j)]
        return jnp.concatenate([stage[j, r0:r0 + sub_rows, :] for j in range(n_tiles)], axis=1)

    def store_block(ref, t0, read):
        r0 = t0 * bsub
        if not strided:
            value = read(slice(0, ref.shape[2]))
            for t in range(sub_steps):
                ref[:, t0 + t, :] = value[t * bsub:(t + 1) * bsub, :]
            return
        n_tiles = ref.shape[2] // V7X_LANES
        for j in range(n_tiles):
            stage[j, r0:r0 + sub_rows, :] = read(lane_tile(j))
        for j in range(n_tiles):
            for b in range(bsub):
                ref[b, t0:t0 + sub_steps, lane_tile(j)] = stage[j, pl.ds(r0 + b, sub_steps, stride=bsub), :]

    def conv_in_place(buf, w_ref, b_ref, act, r_lo):
        sub = V7X_SUBLANES
        for c0 in range(0, buf.shape[1], ELEMENTWISE_COLS):
            cols = slice(c0, c0 + ELEMENTWISE_COLS)
            taps = [w_ref[k * sub:(k + 1) * sub, cols] for k in range(CONV_W)]
            bias = b_ref[:, cols]
            for r0 in range(r_lo, r_lo + sub_rows, sub):
                acc = bias + buf[r0:r0 + sub, cols] * taps[0]
                for k in range(1, CONV_W):
                    acc = acc + buf[r0 + k * bsub:r0 + k * bsub + sub, cols] * taps[k]
                buf[r0:r0 + sub, cols] = act(acc)

    @pl.when(pl.program_id(1) == 0)
    def _():
        for k in range(CONV_TAIL):
            sbuf[k * bsub:(k + 1) * bsub, :] = sconv0_ref[k]
            lbuf[k * bsub:(k + 1) * bsub, :] = lconv0_ref[k]
        h_carry[...] = h0_ref[...]

    neg_c_softplus = -LRU_C * _softplus(-lam_ref[...])

    def sub_tile(i):
        t0 = i * sub_steps
        r_lo = t0 * bsub
        out = slice(r_lo, r_lo + sub_rows)
        pre = slice(tail + r_lo, tail + r_lo + sub_rows)
        xn = _rms(load_block(x_ref, t0), g_ref[...]).astype(BF16)
        yield
        lbuf[pre, :] = _dot(xn, w_ref[:, COL_XR:COL_DT])
        gate_buf[out, :] = _dot(xn, w_ref[:, COL_GATE:COL_XR])
        yield
        conv_in_place(lbuf, lcw_ref, lcb_ref, lambda v: v, r_lo)
        yield
        for q in range(LRU_GATE_GROUPS):
            cols = slice(q * V7X_MXU_DIM, (q + 1) * V7X_MXU_DIM)
            xr_bf = lbuf[out, cols].astype(BF16)
            ra_buf[out, cols] = _dot(xr_bf, wa_ref[q])
            rx_buf[out, cols] = _dot(xr_bf, wx_ref[q])
        yield
        for c0 in range(0, D_LRU, ELEMENTWISE_COLS):
            cols = slice(c0, c0 + ELEMENTWISE_COLS)
            ncs, ba, bx = neg_c_softplus[:, cols], ba_ref[:, cols], bx_ref[:, cols]
            for r0 in range(r_lo, r_lo + sub_rows, rc):
                sl = slice(r0, r0 + rc)
                log_a = ncs * _sigmoid(ra_buf[sl, cols] + ba)
                a = jnp.exp(log_a)
                m = -jnp.tanh(log_a) * (a * a + 1.0)
                mult = jnp.where(m > 0.0, m * lax.rsqrt(m), 0.0)
                rx_buf[sl, cols] = mult * _sigmoid(rx_buf[sl, cols] + bx) * lbuf[sl, cols]
                ra_buf[sl, cols] = a
                gate_buf[sl, cols] = _gelu_tanh(gate_buf[sl, cols])
        yield
        z = _dot(xn, w_ref[:, COL_Z:COL_XBC])
        sbuf[pre, :] = _dot(xn, w_ref[:, COL_XBC:COL_GATE])
        dt = _softplus(_dot(xn, w_ref[:, COL_DT:IN_COLS_PAD]) + dtb_ref[...])
        yield
        store_block(z_ref, t0, lambda cols: z[:, cols])
        store_block(dt_ref, t0, lambda cols: dt[:, cols])
        conv_in_place(sbuf, scw_ref, scb_ref, _silu, r_lo)
        yield
        store_block(xbc_ref, t0, lambda cols: sbuf[out, cols])

    _run_skewed([sub_tile(i) for i in range(n_sub)])

    for k in range(CONV_TAIL):
        sconv_ref[k] = sbuf[rows + k * bsub:rows + (k + 1) * bsub, :]
        lconv_ref[k] = lbuf[rows + k * bsub:rows + (k + 1) * bsub, :]

    def scan_step(b0, t, h):
        start = t * bsub + b0
        if not isinstance(start, int):
            start = pl.multiple_of(start, V7X_SUBLANES)
        sl = pl.ds(start, V7X_SUBLANES)
        h = ra_buf[sl, :] * h + rx_buf[sl, :]
        gate_buf[sl, :] = h * gate_buf[sl, :]
        return h

    n_groups = bsub // V7X_SUBLANES
    if n_groups * steps <= SCAN_UNROLL_LIMIT:
        for bg in range(n_groups):
            hsl = slice(bg * V7X_SUBLANES, (bg + 1) * V7X_SUBLANES)
            h = h_carry[hsl, :]
            for t in range(steps):
                h = scan_step(bg * V7X_SUBLANES, t, h)
            h_carry[hsl, :] = h
    else:
        def group_scan(bg, carry):
            b0 = pl.multiple_of(bg * V7X_SUBLANES, V7X_SUBLANES)
            hsl = pl.ds(b0, V7X_SUBLANES)
            h_carry[hsl, :] = lax.fori_loop(0, steps, functools.partial(scan_step, b0), h_carry[hsl, :])
            return carry

        lax.fori_loop(0, n_groups, group_scan, 0)
    for i in range(n_sub):
        store_block(ylru_ref, i * sub_steps,
                    lambda cols, i=i: gate_buf[i * sub_rows:(i + 1) * sub_rows, cols])
    hout_ref[...] = h_carry[...]
    sbuf[0:tail, :] = sbuf[rows:rows + tail, :]
    lbuf[0:tail, :] = lbuf[rows:rows + tail, :]
    for t in range(steps, steps + pad_steps):
        for ref in (z_ref, xbc_ref, dt_ref):
            ref[:, t, :] = jnp.zeros((bsub, ref.shape[2]), F32)


def _in_proj_call(x, l, w, sconv0, sconv0_map, lconv0, lconv0_map, h0, h0_map, *,
                  bsub, steps, pad_steps=0, n_sub=1):
    batch, length, _ = x.shape
    n_t = length // steps
    n_b = batch // bsub
    assert length == n_t * steps and batch == n_b * bsub and bsub % V7X_SUBLANES == 0
    assert pad_steps == 0 or n_t == 1
    assert steps >= CONV_TAIL
    assert steps % n_sub == 0
    strided = steps // n_sub >= bsub
    out_steps = steps + pad_steps
    rows = steps * bsub
    tail = CONV_TAIL * bsub

    def tile(n_steps, width):
        return pl.BlockSpec((bsub, n_steps, width), lambda b, t: (b, t, 0))

    def squeeze_lead(a, block):
        return (None,) * (a.ndim - len(block)) + block

    def conv_out(width):
        return pl.BlockSpec((CONV_TAIL, bsub, width), lambda b, t: (0, b, 0))

    return pl.pallas_call(
        functools.partial(_in_proj_body, bsub, steps, pad_steps, n_sub),
        grid=(n_b, n_t),
        in_specs=[
            tile(steps, D_MODEL),
            _layer_spec((1, D_MODEL), l),
            _layer_spec((D_MODEL, IN_COLS_PAD), l),
            _layer_spec((CONV_W * V7X_SUBLANES, SSD_CONV_DIM), l),
            _layer_spec((V7X_SUBLANES, SSD_CONV_DIM), l),
            _layer_spec((CONV_W * V7X_SUBLANES, D_LRU), l), _layer_spec((V7X_SUBLANES, D_LRU), l),
            _layer_spec((1, DT_PAD), l),
            _layer_spec((LRU_GATE_GROUPS, V7X_MXU_DIM, V7X_MXU_DIM), l),
            _layer_spec((LRU_GATE_GROUPS, V7X_MXU_DIM, V7X_MXU_DIM), l),
            _layer_spec((1, D_LRU), l), _layer_spec((1, D_LRU), l), _layer_spec((1, D_LRU), l),
            pl.BlockSpec(squeeze_lead(sconv0, (CONV_TAIL, bsub, SSD_CONV_DIM)), sconv0_map),
            pl.BlockSpec(squeeze_lead(lconv0, (CONV_TAIL, bsub, D_LRU)), lconv0_map),
            pl.BlockSpec(squeeze_lead(h0, (bsub, D_LRU)), h0_map),
        ],
        out_specs=[
            tile(out_steps, D_SSD), tile(out_steps, SSD_CONV_DIM), tile(out_steps, DT_PAD),
            tile(steps, D_LRU),
            conv_out(SSD_CONV_DIM), conv_out(D_LRU),
            pl.BlockSpec((bsub, D_LRU), lambda b, t: (b, 0)),
        ],
        out_shape=[
            jax.ShapeDtypeStruct((batch, n_t * out_steps, D_SSD), F32),
            jax.ShapeDtypeStruct((batch, n_t * out_steps, SSD_CONV_DIM), F32),
            jax.ShapeDtypeStruct((batch, n_t * out_steps, DT_PAD), F32),
            jax.ShapeDtypeStruct((batch, length, D_LRU), F32),
            jax.ShapeDtypeStruct((CONV_TAIL, batch, SSD_CONV_DIM), F32),
            jax.ShapeDtypeStruct((CONV_TAIL, batch, D_LRU), F32),
            jax.ShapeDtypeStruct((batch, D_LRU), F32),
        ],
        scratch_shapes=[
            pltpu.VMEM((SSD_CONV_DIM // V7X_LANES, rows, V7X_LANES) if strided
                       else (1, V7X_SUBLANES, V7X_LANES), F32),
            pltpu.VMEM((V7X_SUBLANES, V7X_LANES) if strided else (rows, D_MODEL), F32),
            pltpu.VMEM((tail + rows, SSD_CONV_DIM), F32),
            pltpu.VMEM((tail + rows, D_LRU), F32),
            pltpu.VMEM((rows, D_LRU), F32),
            pltpu.VMEM((rows, D_LRU), F32),
            pltpu.VMEM((rows, D_LRU), F32),
            pltpu.VMEM((bsub, D_LRU), F32),
        ],
        compiler_params=_params(2),
        name="in_proj_lru",
    )(x, w["mix_pre_g"], w["w_in"], w["ssd_conv_w"], w["ssd_conv_b"], w["lru_conv_w"], w["lru_conv_b"],
      w["ssd_dt_bias"], w["lru_wa"], w["lru_wx"], w["lru_ba"], w["lru_bx"], w["lru_lambda"],
      sconv0, lconv0, h0)


def _transpose_rows(x, rows):
    lanes = x.shape[1]
    if rows < lanes:
        x = jnp.concatenate([x, jnp.zeros((lanes - rows, lanes), x.dtype)], axis=0)
    return x.T[:, 0:rows]


def _split_bf16(x, terms):
    parts = []
    for _ in range(terms - 1):
        parts.append(x.astype(BF16))
        x = x - parts[-1].astype(F32)
    return parts + [x.astype(BF16)]


def _ssd_body(chunk, n_chunks, out_steps, seqs, first_layer, xbc_ref, dt_ref, z_ref, alog_ref, dvec_ref,
              g_ref, e_ref, h0_ref, *rest):
    (y_ref, hout_ref, ht_s) = rest[-3:]
    if first_layer is not None:
        @pl.when(pl.program_id(1) == 0)
        def _():
            for k in range(hout_ref.shape[0]):
                if k != first_layer:
                    hout_ref[k] = jnp.zeros(hout_ref.shape[1:], F32)
        hout_ref = hout_ref.at[first_layer]
    stages = [_ssd_sequence(chunk, n_chunks, out_steps, xbc_ref.at[s], dt_ref.at[s], z_ref.at[s],
                            alog_ref, dvec_ref, g_ref, e_ref, h0_ref.at[s], y_ref.at[s],
                            hout_ref.at[s], ht_s.at[s]) for s in range(seqs)]
    for _ in itertools.zip_longest(*stages):
        pass


def _ssd_sequence(chunk, n_chunks, out_steps, xbc_ref, dt_ref, z_ref, alog_ref, dvec_ref, g_ref, e_ref,
                  h0_ref, y_ref, hout_ref, ht_s):
    c = pl.program_id(1)
    group_cols = HEADS_PER_GROUP * SSD_HEAD_DIM

    single = n_chunks == 1
    if single:
        h_given = h0_ref[...].reshape(D_SSD, SSD_STATE)
        ht_bf = h_given.astype(BF16).T
    else:
        @pl.when(c == 0)
        def _():
            ht_s[...] = h0_ref[...].reshape(D_SSD, SSD_STATE).T

    dt = dt_ref[...]
    a = -jnp.exp(alog_ref[...])
    row = lax.broadcasted_iota(jnp.int32, (chunk, chunk), 0)
    col = lax.broadcasted_iota(jnp.int32, (chunk, chunk), 1)
    causal = row >= col
    ones_lower = jnp.where(causal, 1.0, 0.0).astype(BF16)
    da_hi, da_mid, da_lo = _split_bf16(dt * a, 3)
    cum = _dot(ones_lower, da_hi) + _dot(ones_lower, da_mid) + _dot(ones_lower, da_lo)
    yield
    cum_t = _transpose_rows(cum, chunk)
    dt_t = _transpose_rows(dt, chunk)
    tot = cum[chunk - 1:chunk, :]
    cdec = jnp.exp(tot)
    factors = jnp.concatenate([jnp.exp(cum), jnp.exp(tot - cum) * dt], axis=0)
    spread = _dot(factors.astype(BF16), e_ref[...])
    yield
    ecum_x = spread[0:chunk, :]
    wend_x = spread[chunk:2 * chunk, :]
    if not single:
        cdec_parts = _split_bf16(jnp.broadcast_to(cdec, (V7X_SUBLANES, DT_PAD)), 3)
        cdec_x = (_dot(cdec_parts[0], e_ref[...]) + _dot(cdec_parts[1], e_ref[...])
                  + _dot(cdec_parts[2], e_ref[...]))[0:1, :]
    low_half = lax.broadcasted_iota(jnp.int32, (chunk, V7X_LANES), 1) < SSD_HEAD_DIM

    y_groups = []
    for g in range(SSD_GROUPS):
        b0 = D_SSD + g * SSD_STATE
        c0 = D_SSD + SSD_GROUPS * SSD_STATE + g * SSD_STATE
        gcols = slice(g * group_cols, (g + 1) * group_cols)
        bg_t = _transpose_rows(xbc_ref[:, b0:b0 + SSD_STATE], chunk).astype(BF16)
        cg = xbc_ref[:, c0:c0 + SSD_STATE].astype(BF16)
        cb = _dot(cg, bg_t)
        yield
        y_pairs = []
        for pair in range(HEADS_PER_GROUP // 2):
            h = g * HEADS_PER_GROUP + 2 * pair
            x_pair = xbc_ref[:, h * SSD_HEAD_DIM:(h + 2) * SSD_HEAD_DIM]
            scores = []
            for hh in (h, h + 1):
                diff = cum[:, hh:hh + 1] - cum_t[hh:hh + 1, :]
                decay = jnp.exp(jnp.where(causal, diff, -jnp.inf))
                scores.append((cb * decay * dt_t[hh:hh + 1, :]).astype(BF16))
            x_lo = jnp.where(low_half, x_pair, 0.0).astype(BF16)
            x_hi = jnp.where(low_half, 0.0, x_pair).astype(BF16)
            if chunk % V7X_LANES == 0:
                y_pairs.append(_dot(jnp.concatenate(scores, axis=1),
                                    jnp.concatenate([x_lo, x_hi], axis=0)))
            else:
                y_pairs.append(_dot(scores[0], x_lo) + _dot(scores[1], x_hi))
            yield
        xg = xbc_ref[:, gcols]
        xw = (xg * wend_x[:, gcols]).astype(BF16)
        if single:
            y_off = _dot(cg, ht_bf[:, gcols]) * ecum_x[:, gcols]
            bg = xbc_ref[:, b0:b0 + SSD_STATE].astype(BF16)
            update = lax.dot_general(xw, bg, (((0,), (0,)), ((), ())), preferred_element_type=F32)
            heads = range(g * HEADS_PER_GROUP, (g + 1) * HEADS_PER_GROUP)
            decay_rows = jnp.concatenate(
                [jnp.broadcast_to(cdec[:, h:h + 1], (SSD_HEAD_DIM, SSD_STATE)) for h in heads], axis=0)
            h_new = decay_rows * h_given[gcols, :] + update
            hout_ref[g * HEADS_PER_GROUP:(g + 1) * HEADS_PER_GROUP] = h_new.reshape(
                HEADS_PER_GROUP, SSD_HEAD_DIM, SSD_STATE)
        else:
            ht_prev = ht_s[:, gcols]
            y_off = _dot(cg, ht_prev.astype(BF16)) * ecum_x[:, gcols]
            ht_s[:, gcols] = cdec_x[:, gcols] * ht_prev + _dot(bg_t, xw)
        y_groups.append(jnp.concatenate(y_pairs, axis=1) + y_off + dvec_ref[:, gcols] * xg)
        yield

    y = _rms(jnp.concatenate(y_groups, axis=1) * _silu(z_ref[...]), g_ref[...])
    y_ref[...] = y[0:out_steps, :]

    if not single:
        @pl.when(c == n_chunks - 1)
        def _():
            hout_ref[...] = ht_s[...].T.reshape(SSD_HEADS, SSD_HEAD_DIM, SSD_STATE)


def _ssd_call(xbc, dt, z, l, w, h0, h0_map, h_stack, *, chunk, out_len, seqs):
    batch, length, _ = xbc.shape
    depth = w["ssd_a_log"].shape[0]
    n_chunks = length // chunk
    assert length == n_chunks * chunk and (out_len == length or n_chunks == 1) and batch % seqs == 0
    out_steps = min(chunk, out_len)

    def per_seq(n_steps, width):
        return pl.BlockSpec((seqs, n_steps, width), lambda b, c: (b, c, 0))

    state_tail = (SSD_HEADS, SSD_HEAD_DIM, SSD_STATE)
    h0_block = (None,) * (h0.ndim - 4) + (seqs,) + state_tail
    operands = [xbc, dt, z, w["ssd_a_log"], w["ssd_d_cols"], w["ssd_norm_g"], w["head_spread"], h0]
    in_specs = [
        per_seq(chunk, SSD_CONV_DIM), per_seq(chunk, DT_PAD), per_seq(chunk, D_SSD),
        _layer_spec((1, DT_PAD), l), _layer_spec((1, D_SSD), l), _layer_spec((1, D_SSD), l),
        pl.BlockSpec((DT_PAD, D_SSD), lambda b, c: (0, 0), pipeline_mode=pl.Buffered(1)),
        pl.BlockSpec(h0_block, h0_map),
    ]
    aliases = {}
    if h_stack is not None:
        aliases = {len(operands): 1}
        operands.append(h_stack)
        in_specs.append(pl.BlockSpec(memory_space=pl.ANY))
    first = h_stack is None
    return pl.pallas_call(
        functools.partial(_ssd_body, chunk, n_chunks, out_steps, seqs, l if first else None),
        grid=(batch // seqs, n_chunks),
        in_specs=in_specs,
        out_specs=[
            per_seq(out_steps, D_SSD),
            pl.BlockSpec((depth, seqs) + state_tail, lambda b, c: (0, b, 0, 0, 0)) if first
            else pl.BlockSpec((None, seqs) + state_tail, lambda b, c: (l, b, 0, 0, 0)),
        ],
        out_shape=[
            jax.ShapeDtypeStruct((batch, out_len, D_SSD), F32),
            jax.ShapeDtypeStruct((depth, batch) + state_tail, F32),
        ],
        scratch_shapes=[pltpu.VMEM((seqs, SSD_STATE, D_SSD), F32)],
        input_output_aliases=aliases,
        compiler_params=_params(2),
        name="ssd_chunk",
    )(*operands)


def _regroup_w_in_body(wt_ref, o_ref):
    o_dt = COL_XBC + SSD_CONV_DIM
    o_gate = o_dt + SSD_HEADS

    def put(c0, rows):
        o_ref[:, c0:c0 + V7X_LANES] = rows.T.astype(BF16)

    for c0 in range(COL_Z, COL_GATE, V7X_LANES):
        put(c0, wt_ref[c0:c0 + V7X_LANES, :])
    for c0 in range(COL_GATE, COL_DT, V7X_LANES):
        r0 = o_gate + c0 - COL_GATE
        put(c0, wt_ref[r0:r0 + V7X_LANES, :])
    put(COL_DT, jnp.concatenate([wt_ref[o_dt:o_gate, :],
                                 jnp.zeros((DT_PAD - SSD_HEADS, D_MODEL), F32)], axis=0))


def _regroup_w_in(w_in):
    depth, d_model, in_cols = w_in.shape
    assert in_cols == D_SSD + SSD_CONV_DIM + SSD_HEADS + 2 * D_LRU and d_model == D_MODEL
    return pl.pallas_call(
        _regroup_w_in_body,
        grid=(depth,),
        in_specs=[pl.BlockSpec((None, in_cols, d_model), lambda l: (l, 0, 0),
                               pipeline_mode=pl.Buffered(1))],
        out_specs=pl.BlockSpec((None, d_model, IN_COLS_PAD), lambda l: (l, 0, 0)),
        out_shape=jax.ShapeDtypeStruct((depth, d_model, IN_COLS_PAD), BF16),
        compiler_params=_params(1),
        name="regroup_w_in",
    )(w_in.transpose(0, 2, 1))


def _prepare_weights(p):
    depth = p["w_in"].shape[0]

    def vec(a):
        return a.reshape(depth, 1, a.shape[-1])

    def pad_heads(a):
        return jnp.pad(a, ((0, 0), (0, DT_PAD - SSD_HEADS))).reshape(depth, 1, DT_PAD)

    def block_diag(a):
        per = V7X_MXU_DIM // LRU_BLOCK_W
        a = a.reshape(depth, LRU_GATE_GROUPS, per, LRU_BLOCK_W, LRU_BLOCK_W)
        eye = jnp.eye(per, dtype=a.dtype)
        a = a[:, :, :, :, None, :] * eye[None, None, :, None, :, None]
        return a.reshape(depth, LRU_GATE_GROUPS, V7X_MXU_DIM, V7X_MXU_DIM).astype(BF16)

    w = {
        "w_in": _regroup_w_in(p["w_in"]),
        "w_out": p["w_out"].astype(BF16),
        "lru_wa": block_diag(p["lru_wa"]),
        "lru_wx": block_diag(p["lru_wx"]),
        "ssd_dt_bias": pad_heads(p["ssd_dt_bias"]),
        "ssd_a_log": pad_heads(p["ssd_a_log"]),
        "ssd_d_cols": jnp.repeat(p["ssd_d"], SSD_HEAD_DIM, axis=-1).reshape(depth, 1, D_SSD),
        "head_spread": (jnp.arange(DT_PAD)[:, None] == jnp.arange(D_SSD)[None, :] // SSD_HEAD_DIM
                        ).astype(BF16),
    }
    for name in ("ssd_conv_w", "lru_conv_w"):
        w[name] = jnp.repeat(p[name], V7X_SUBLANES, axis=1)
    for name in ("ssd_conv_b", "lru_conv_b"):
        w[name] = jnp.repeat(p[name][:, None, :], V7X_SUBLANES, axis=1)
    for name in ("ffn1_wg", "ffn1_wu", "ffn1_wd", "ffn2_wg", "ffn2_wu", "ffn2_wd"):
        w[name] = p[name].astype(BF16)
    for name in ("ffn1_pre_g", "ffn1_post_g", "mix_pre_g", "mix_post_g", "ffn2_pre_g", "ffn2_post_g",
                 "ssd_norm_g", "lru_ba", "lru_bx", "lru_lambda"):
        w[name] = vec(p[name])
    return w


def _segment_mixer(x, l, w, states, ssd_stack, *, bsub, steps, chunk, pad_steps, seqs, n_sub=1):
    (sconv0, sconv0_map), (lconv0, lconv0_map), (h_lru0, h_lru0_map), (h_ssd0, h_ssd0_map) = states
    length = x.shape[1]
    z, xbc, dt, y_lru, sconv, lconv, h_lru = _in_proj_call(
        x, l, w, sconv0, sconv0_map, lconv0, lconv0_map,
        h_lru0, h_lru0_map, bsub=bsub, steps=steps, pad_steps=pad_steps, n_sub=n_sub)
    y_ssd, ssd_stack = _ssd_call(xbc, dt, z, l, w, h_ssd0, h_ssd0_map, ssd_stack,
                                 chunk=chunk, out_len=length, seqs=seqs)
    return y_ssd, y_lru, (sconv, h_lru, lconv), ssd_stack


def kernel(x_prompt, x_sample, state_ssd, state_ssd_conv, state_lru, state_lru_conv, meta_tokens,
           ffn1_pre_g, ffn1_post_g, ffn1_wg, ffn1_wu, ffn1_wd, mix_pre_g, mix_post_g, w_in,
           ssd_conv_w, ssd_conv_b, ssd_dt_bias, ssd_a_log, ssd_d, ssd_norm_g, lru_conv_w, lru_conv_b,
           lru_wa, lru_ba, lru_wx, lru_bx, lru_lambda, w_out, ffn2_pre_g, ffn2_post_g, ffn2_wg,
           ffn2_wu, ffn2_wd):
    bp, seq, _ = x_prompt.shape
    bs, dec_seq, _ = x_sample.shape
    depth = w_in.shape[0]
    assert bp == V7X_SUBLANES and seq % SSD_CHUNK == 0 and bs % SAMPLE_BATCH_TILE == 0
    w = _prepare_weights(dict(
        ffn1_pre_g=ffn1_pre_g, ffn1_post_g=ffn1_post_g, ffn1_wg=ffn1_wg, ffn1_wu=ffn1_wu,
        ffn1_wd=ffn1_wd, mix_pre_g=mix_pre_g, mix_post_g=mix_post_g, w_in=w_in,
        ssd_conv_w=ssd_conv_w, ssd_conv_b=ssd_conv_b, ssd_dt_bias=ssd_dt_bias, ssd_a_log=ssd_a_log,
        ssd_d=ssd_d, ssd_norm_g=ssd_norm_g, lru_conv_w=lru_conv_w, lru_conv_b=lru_conv_b,
        lru_wa=lru_wa, lru_ba=lru_ba, lru_wx=lru_wx, lru_bx=lru_bx, lru_lambda=lru_lambda,
        w_out=w_out, ffn2_pre_g=ffn2_pre_g, ffn2_post_g=ffn2_post_g, ffn2_wg=ffn2_wg,
        ffn2_wu=ffn2_wu, ffn2_wd=ffn2_wd))

    xm = jnp.broadcast_to(meta_tokens.astype(F32)[None], (bp, N_META, D_MODEL))
    xp = x_prompt
    xs = x_sample
    sample_steps = ((dec_seq + V7X_SUBLANES - 1) // V7X_SUBLANES) * V7X_SUBLANES
    sconv_in = state_ssd_conv.transpose(0, 2, 1, 3)
    lconv_in = state_lru_conv.transpose(0, 2, 1, 3)
    tail3 = lambda b, t: (0, b, 0)
    vec2 = lambda b, t: (b, 0)
    seq_state = lambda b, c: (b, 0, 0, 0)
    zero_states = ((jnp.zeros((CONV_TAIL, bp, SSD_CONV_DIM), F32), tail3),
                   (jnp.zeros((CONV_TAIL, bp, D_LRU), F32), tail3),
                   (jnp.zeros((bp, D_LRU), F32), vec2),
                   (jnp.zeros((bp, SSD_HEADS, SSD_HEAD_DIM, SSD_STATE), F32), seq_state))

    layer_state = lambda b, c, l: (l, b, 0, 0, 0)
    p_out = [[] for _ in range(3)]
    s_out = [[] for _ in range(3)]
    m_ssd = p_ssd = s_ssd = None
    def flat(a):
        return a.reshape(a.shape[0] * a.shape[1], a.shape[2])

    for l in range(depth):
        at_l = functools.partial(layer_state, l=l)
        shapes = (xp.shape, xs.shape, xm.shape)
        xp, xs, xm = (a.reshape(s) for a, s in
                      zip(_ffn_call([flat(xp), flat(xs), flat(xm)], l, w, "ffn1"), shapes))
        m_ssd_y, m_lru_y, m_st, m_ssd = _segment_mixer(
            xm, l, w, zero_states, m_ssd,
            bsub=bp, steps=N_META, chunk=N_META, pad_steps=0, seqs=PROMPT_SEQS)
        p_ssd_y, p_lru_y, p_st, p_ssd = _segment_mixer(
            xp, l, w, ((m_st[0], tail3), (m_st[2], tail3), (m_st[1], vec2), (m_ssd, at_l)), p_ssd,
            bsub=bp, steps=PROMPT_STEPS, chunk=SSD_CHUNK, pad_steps=0, seqs=PROMPT_SEQS,
            n_sub=PROMPT_SUB_TILES)
        s_ssd_y, s_lru_y, s_st, s_ssd = _segment_mixer(
            xs, l, w, ((sconv_in, lambda b, t, l=l: (l, 0, b, 0)),
                       (lconv_in, lambda b, t, l=l: (l, 0, b, 0)),
                       (state_lru, lambda b, t, l=l: (l, b, 0)),
                       (state_ssd, at_l)), s_ssd,
            bsub=SAMPLE_BATCH_TILE, steps=dec_seq, chunk=sample_steps,
            pad_steps=sample_steps - dec_seq, seqs=SAMPLE_SEQS)
        (xp,) = _out_ffn_call([(flat(xp), flat(p_ssd_y), flat(p_lru_y))], l, w)
        xs, xm = _out_ffn_call([(flat(xs), flat(s_ssd_y), flat(s_lru_y)),
                                (flat(xm), flat(m_ssd_y), flat(m_lru_y))], l, w)
        xp, xs, xm = (a.reshape(s) for a, s in zip((xp, xs, xm), shapes))
        for acc, st in ((p_out, p_st), (s_out, s_st)):
            for k in range(3):
                acc[k].append(st[k])

    p_conv, p_lru, p_lconv = (jnp.stack(a) for a in p_out)
    s_conv, s_lru, s_lconv = (jnp.stack(a) for a in s_out)
    p_conv, p_lconv, s_conv, s_lconv = (a.transpose(0, 2, 1, 3) for a in (p_conv, p_lconv, s_conv, s_lconv))
    return (xp, xs, p_ssd, p_conv, p_lru, p_lconv, s_ssd, s_conv, s_lru, s_lconv)
```

```python
import functools
import itertools

import jax
import jax.numpy as jnp
from jax import lax
from jax.experimental import pallas as pl
from jax.experimental.pallas import tpu as pltpu

F32 = jnp.float32
BF16 = jnp.bfloat16

D_MODEL = 1024
D_SSD = 1024
D_LRU = 1024
SSD_HEADS = 16
SSD_HEAD_DIM = 64
SSD_GROUPS = 2
SSD_STATE = 128
HEADS_PER_GROUP = SSD_HEADS // SSD_GROUPS
CONV_W = 4
CONV_TAIL = CONV_W - 1
SSD_CONV_DIM = D_SSD + 2 * SSD_GROUPS * SSD_STATE
LRU_BLOCKS = 16
LRU_BLOCK_W = D_LRU // LRU_BLOCKS
LRU_C = 8.0
EPS = 1e-6
N_META = 16
SSD_CHUNK = 128

V7X_LANES = 128
V7X_SUBLANES = 8
V7X_MXU_DIM = 256
V7X_VMEM_LIMIT_BYTES = 56 * 1024 * 1024

DT_PAD = V7X_LANES
LRU_GATE_GROUPS = D_LRU // V7X_MXU_DIM
COL_Z = 0
COL_XBC = COL_Z + D_SSD
COL_GATE = COL_XBC + SSD_CONV_DIM
COL_XR = COL_GATE + D_LRU
COL_DT = COL_XR + D_LRU
IN_COLS_PAD = COL_DT + DT_PAD
GELU_K = 0.7978845608028654

FFN_ROWS = 512
FFN_SUB_TILES = 2
OUT_FFN_ROWS = 512
PROMPT_STEPS = 64
SAMPLE_BATCH_TILE = 64
PROMPT_SUB_TILES = 2
SCAN_UNROLL_LIMIT = 128
PROMPT_SEQS = 4
SAMPLE_SEQS = 8
ELEMENTWISE_ROWS = 16
ELEMENTWISE_COLS = 512


def _rms(x, g):
    return x * lax.rsqrt(jnp.mean(x * x, axis=-1, keepdims=True) + EPS) * g


def _silu(x):
    return x * _sigmoid(x)


def _sigmoid(x):
    return 0.5 * jnp.tanh(0.5 * x) + 0.5


def _softplus(x):
    return jnp.maximum(x, 0.0) + jnp.log1p(jnp.exp(-jnp.abs(x)))


def _gelu_tanh(x):
    return 0.5 * x * (1.0 + jnp.tanh(GELU_K * (x + 0.044715 * (x * x * x))))


def _dot(a, b):
    return jnp.dot(a, b, preferred_element_type=F32)


def _layer_spec(tail, l):
    zeros = (0,) * len(tail)
    return pl.BlockSpec((None,) + tuple(tail), lambda *_: (l,) + zeros, pipeline_mode=pl.Buffered(1))


def _params(n_axes):
    return pltpu.CompilerParams(dimension_semantics=("arbitrary",) * n_axes,
                                vmem_limit_bytes=V7X_VMEM_LIMIT_BYTES)


def _row_tile(rows, want):
    tm = min(rows, want)
    assert rows % tm == 0
    return tm


def _ffn_math(x, gpre, gpost, wg_ref, wu_ref, wd_ref, prologue=None):
    n_sub = FFN_SUB_TILES if x.shape[0] >= FFN_ROWS else 1
    rows = x.shape[0] // n_sub
    outs = [None] * n_sub

    def sub_tile(i):
        sl = slice(i * rows, (i + 1) * rows)
        xi = x[sl]
        if prologue is not None:
            xi = prologue(xi, sl)
            yield
        xn = _rms(xi, gpre).astype(BF16)
        yield
        hg = _dot(xn, wg_ref[...])
        hu = _dot(xn, wu_ref[...])
        yield
        a = (_silu(hg) * hu).astype(BF16)
        yield
        y = _dot(a, wd_ref[...])
        yield
        outs[i] = xi + 0.5 * _rms(y, gpost)

    _run_skewed([sub_tile(i) for i in range(n_sub)])
    return outs[0] if n_sub == 1 else jnp.concatenate(outs, axis=0)


def _run_skewed(stage_generators):
    waiting, running = list(stage_generators), []
    while waiting or running:
        if waiting:
            running.append(waiting.pop(0))
        for stages in list(running):
            if next(stages, "done") == "done":
                running.remove(stages)


def _token_call(math, segments, weights, weight_specs, name, tile_rows):
    n_ops = len(segments[0])
    tiles = [_row_tile(seg[0].shape[0], tile_rows) for seg in segments]
    counts = [seg[0].shape[0] // tm for seg, tm in zip(segments, tiles)]
    starts = [sum(counts[:k]) for k in range(len(segments))]

    def seg_spec(k):
        mode = {} if counts[k] > 1 else dict(pipeline_mode=pl.Buffered(1))
        return pl.BlockSpec((tiles[k], D_MODEL),
                            lambda i: (jnp.clip(i - starts[k], 0, counts[k] - 1), 0), **mode)

    def body(*refs):
        n_in = n_ops * len(segments)
        x_refs, w_refs, o_refs = refs[:n_in], refs[n_in:n_in + len(weights)], refs[n_in + len(weights):]
        i = pl.program_id(0)

        def run(k):
            operands = [r[...] for r in x_refs[k * n_ops:(k + 1) * n_ops]]
            o_refs[k][...] = math(*operands, *w_refs)

        for k in range(len(segments)):
            pl.when((i >= starts[k]) & (i < starts[k] + counts[k]))(functools.partial(run, k))

    return pl.pallas_call(
        body,
        grid=(sum(counts),),
        in_specs=[seg_spec(k) for k in range(len(segments)) for _ in range(n_ops)] + weight_specs,
        out_specs=[seg_spec(k) for k in range(len(segments))],
        out_shape=[jax.ShapeDtypeStruct(seg[0].shape, F32) for seg in segments],
        compiler_params=_params(1),
        name=name,
    )(*[a for seg in segments for a in seg], *weights)


def _ffn_math_refs(x, gpre_ref, gpost_ref, wg_ref, wu_ref, wd_ref):
    return _ffn_math(x, gpre_ref[...], gpost_ref[...], wg_ref, wu_ref, wd_ref)


def _ffn_call(xs, l, w, prefix):
    d_ff = w[prefix + "_wg"].shape[-1]
    return _token_call(
        _ffn_math_refs, [(x,) for x in xs],
        [w[prefix + "_pre_g"], w[prefix + "_post_g"], w[prefix + "_wg"], w[prefix + "_wu"],
         w[prefix + "_wd"]],
        [_layer_spec((1, D_MODEL), l), _layer_spec((1, D_MODEL), l),
         _layer_spec((D_MODEL, d_ff), l), _layer_spec((D_MODEL, d_ff), l),
         _layer_spec((d_ff, D_MODEL), l)],
        prefix, FFN_ROWS)


def _out_ffn_math(x, ys, yl, wos_ref, wol_ref, gmix_ref, gpre_ref, gpost_ref, wg_ref, wu_ref, wd_ref):
    def mixer_residual(x_rows, sl):
        m = _dot(ys[sl].astype(BF16), wos_ref[...]) + _dot(yl[sl].astype(BF16), wol_ref[...])
        return x_rows + _rms(m, gmix_ref[...])

    return _ffn_math(x, gpre_ref[...], gpost_ref[...], wg_ref, wu_ref, wd_ref, mixer_residual)


def _out_ffn_call(segments, l, w):
    d_ff = w["ffn2_wg"].shape[-1]
    return _token_call(
        _out_ffn_math, segments,
        [w["w_out"], w["w_out"], w["mix_post_g"], w["ffn2_pre_g"], w["ffn2_post_g"],
         w["ffn2_wg"], w["ffn2_wu"], w["ffn2_wd"]],
        [pl.BlockSpec((None, D_SSD, D_MODEL), lambda i: (l, 0, 0), pipeline_mode=pl.Buffered(1)),
         pl.BlockSpec((None, D_LRU, D_MODEL), lambda i: (l, D_SSD // D_LRU, 0),
                      pipeline_mode=pl.Buffered(1)),
         _layer_spec((1, D_MODEL), l), _layer_spec((1, D_MODEL), l), _layer_spec((1, D_MODEL), l),
         _layer_spec((D_MODEL, d_ff), l), _layer_spec((D_MODEL, d_ff), l),
         _layer_spec((d_ff, D_MODEL), l)],
        "out_ffn2", OUT_FFN_ROWS)


def _in_proj_body(bsub, steps, pad_steps, n_sub,
                  x_ref, g_ref, w_ref, scw_ref, scb_ref, lcw_ref, lcb_ref, dtb_ref,
                  wa_ref, wx_ref, ba_ref, bx_ref, lam_ref, sconv0_ref, lconv0_ref, h0_ref,
                  z_ref, xbc_ref, dt_ref, ylru_ref, sconv_ref, lconv_ref, hout_ref,
                  stage, xbuf, sbuf, lbuf, gate_buf, ra_buf, rx_buf, h_carry):
    rows = steps * bsub
    tail = CONV_TAIL * bsub
    sub_steps = steps // n_sub
    sub_rows = sub_steps * bsub
    rc = min(sub_rows, ELEMENTWISE_ROWS)
    strided = sub_steps >= bsub

    def lane_tile(j):
        return slice(j * V7X_LANES, (j + 1) * V7X_LANES)

    def load_block(ref, t0):
        r0 = t0 * bsub
        if not strided:
            for t in range(sub_steps):
                xbuf[r0 + t * bsub:r0 + (t + 1) * bsub, :] = ref[:, t0 + t, :]
            return xbuf[r0:r0 + sub_rows, :]
        n_tiles = ref.shape[2] // V7X_LANES
        for j in range(n_tiles):
            for b in range(bsub):
                stage[j, pl.ds(r0 + b, sub_steps, stride=bsub), :] = ref[b, t0:t0 + sub_steps, lane_tile(j)]
        return jnp.concatenate([stage[j, r0:r0 + sub_rows, :] for j in range(n_tiles)], axis=1)

    def store_block(ref, t0, read):
        r0 = t0 * bsub
        if not strided:
            value = read(slice(0, ref.shape[2]))
            for t in range(sub_steps):
                ref[:, t0 + t, :] = value[t * bsub:(t + 1) * bsub, :]
            return
        n_tiles = ref.shape[2] // V7X_LANES
        for j in range(n_tiles):
            stage[j, r0:r0 + sub_rows, :] = read(lane_tile(j))
        for j in range(n_tiles):
            for b in range(bsub):
                ref[b, t0:t0 + sub_steps, lane_tile(j)] = stage[
                    j, pl.ds(r0 + b, sub_steps, stride=bsub), :].astype(ref.dtype)

    def conv_in_place(buf, w_ref, b_ref, act, r_lo):
        sub = V7X_SUBLANES
        for c0 in range(0, buf.shape[1], ELEMENTWISE_COLS):
            cols = slice(c0, c0 + ELEMENTWISE_COLS)
            taps = [w_ref[k * sub:(k + 1) * sub, cols] for k in range(CONV_W)]
            bias = b_ref[:, cols]
            for r0 in range(r_lo, r_lo + sub_rows, sub):
                acc = bias + buf[r0:r0 + sub, cols] * taps[0]
                for k in range(1, CONV_W):
                    acc = acc + buf[r0 + k * bsub:r0 + k * bsub + sub, cols] * taps[k]
                buf[r0:r0 + sub, cols] = act(acc)

    @pl.when(pl.program_id(1) == 0)
    def _():
        for k in range(CONV_TAIL):
            sbuf[k * bsub:(k + 1) * bsub, :] = sconv0_ref[k]
            lbuf[k * bsub:(k + 1) * bsub, :] = lconv0_ref[k]
        h_carry[...] = h0_ref[...]

    neg_c_softplus = -LRU_C * _softplus(-lam_ref[...])

    def sub_tile(i):
        t0 = i * sub_steps
        r_lo = t0 * bsub
        out = slice(r_lo, r_lo + sub_rows)
        pre = slice(tail + r_lo, tail + r_lo + sub_rows)
        xn = _rms(load_block(x_ref, t0), g_ref[...]).astype(BF16)
        yield
        lbuf[pre, :] = _dot(xn, w_ref[:, COL_XR:COL_DT])
        gate_buf[out, :] = _dot(xn, w_ref[:, COL_GATE:COL_XR])
        yield
        conv_in_place(lbuf, lcw_ref, lcb_ref, lambda v: v, r_lo)
        yield
        for q in range(LRU_GATE_GROUPS):
            cols = slice(q * V7X_MXU_DIM, (q + 1) * V7X_MXU_DIM)
            xr_bf = lbuf[out, cols].astype(BF16)
            ra_buf[out, cols] = _dot(xr_bf, wa_ref[q])
            rx_buf[out, cols] = _dot(xr_bf, wx_ref[q])
        yield
        for c0 in range(0, D_LRU, ELEMENTWISE_COLS):
            cols = slice(c0, c0 + ELEMENTWISE_COLS)
            ncs, ba, bx = neg_c_softplus[:, cols], ba_ref[:, cols], bx_ref[:, cols]
            for r0 in range(r_lo, r_lo + sub_rows, rc):
                sl = slice(r0, r0 + rc)
                log_a = ncs * _sigmoid(ra_buf[sl, cols] + ba)
                a = jnp.exp(log_a)
                m = -jnp.tanh(log_a) * (a * a + 1.0)
                mult = jnp.where(m > 0.0, m * lax.rsqrt(m), 0.0)
                rx_buf[sl, cols] = mult * _sigmoid(rx_buf[sl, cols] + bx) * lbuf[sl, cols]
                ra_buf[sl, cols] = a
                gate_buf[sl, cols] = _gelu_tanh(gate_buf[sl, cols])
        yield
        z = _dot(xn, w_ref[:, COL_Z:COL_XBC])
        sbuf[pre, :] = _dot(xn, w_ref[:, COL_XBC:COL_GATE])
        dt = _softplus(_dot(xn, w_ref[:, COL_DT:IN_COLS_PAD]) + dtb_ref[...])
        yield
        store_block(z_ref, t0, lambda cols: z[:, cols])
        store_block(dt_ref, t0, lambda cols: dt[:, cols])
        conv_in_place(sbuf, scw_ref, scb_ref, _silu, r_lo)
        yield
        store_block(xbc_ref, t0, lambda cols: sbuf[out, cols])

    _run_skewed([sub_tile(i) for i in range(n_sub)])

    for k in range(CONV_TAIL):
        sconv_ref[k] = sbuf[rows + k * bsub:rows + (k + 1) * bsub, :]
        lconv_ref[k] = lbuf[rows + k * bsub:rows + (k + 1) * bsub, :]

    def scan_step(b0, t, h):
        start = t * bsub + b0
        if not isinstance(start, int):
            start = pl.multiple_of(start, V7X_SUBLANES)
        sl = pl.ds(start, V7X_SUBLANES)
        h = ra_buf[sl, :] * h + rx_buf[sl, :]
        gate_buf[sl, :] = h * gate_buf[sl, :]
        return h

    n_groups = bsub // V7X_SUBLANES
    if n_groups * steps <= SCAN_UNROLL_LIMIT:
        for bg in range(n_groups):
            hsl = slice(bg * V7X_SUBLANES, (bg + 1) * V7X_SUBLANES)
            h = h_carry[hsl, :]
            for t in range(steps):
                h = scan_step(bg * V7X_SUBLANES, t, h)
            h_carry[hsl, :] = h
    else:
        def group_scan(bg, carry):
            b0 = pl.multiple_of(bg * V7X_SUBLANES, V7X_SUBLANES)
            hsl = pl.ds(b0, V7X_SUBLANES)
            h_carry[hsl, :] = lax.fori_loop(0, steps, functools.partial(scan_step, b0), h_carry[hsl, :])
            return carry

        lax.fori_loop(0, n_groups, group_scan, 0)
    for i in range(n_sub):
        store_block(ylru_ref, i * sub_steps,
                    lambda cols, i=i: gate_buf[i * sub_rows:(i + 1) * sub_rows, cols])
    hout_ref[...] = h_carry[...]
    sbuf[0:tail, :] = sbuf[rows:rows + tail, :]
    lbuf[0:tail, :] = lbuf[rows:rows + tail, :]
    for t in range(steps, steps + pad_steps):
        for ref in (z_ref, xbc_ref, dt_ref):
            ref[:, t, :] = jnp.zeros((bsub, ref.shape[2]), F32)


def _in_proj_call(x, l, w, sconv0, sconv0_map, lconv0, lconv0_map, h0, h0_map, *,
                  bsub, steps, pad_steps=0, n_sub=1):
    batch, length, _ = x.shape
    n_t = length // steps
    n_b = batch // bsub
    assert length == n_t * steps and batch == n_b * bsub and bsub % V7X_SUBLANES == 0
    assert pad_steps == 0 or n_t == 1
    assert steps >= CONV_TAIL
    assert steps % n_sub == 0
    strided = steps // n_sub >= bsub
    out_steps = steps + pad_steps
    rows = steps * bsub
    tail = CONV_TAIL * bsub

    def tile(n_steps, width):
        return pl.BlockSpec((bsub, n_steps, width), lambda b, t: (b, t, 0))

    def squeeze_lead(a, block):
        return (None,) * (a.ndim - len(block)) + block

    def conv_out(width):
        return pl.BlockSpec((CONV_TAIL, bsub, width), lambda b, t: (0, b, 0))

    return pl.pallas_call(
        functools.partial(_in_proj_body, bsub, steps, pad_steps, n_sub),
        grid=(n_b, n_t),
        in_specs=[
            tile(steps, D_MODEL),
            _layer_spec((1, D_MODEL), l),
            _layer_spec((D_MODEL, IN_COLS_PAD), l),
            _layer_spec((CONV_W * V7X_SUBLANES, SSD_CONV_DIM), l),
            _layer_spec((V7X_SUBLANES, SSD_CONV_DIM), l),
            _layer_spec((CONV_W * V7X_SUBLANES, D_LRU), l), _layer_spec((V7X_SUBLANES, D_LRU), l),
            _layer_spec((1, DT_PAD), l),
            _layer_spec((LRU_GATE_GROUPS, V7X_MXU_DIM, V7X_MXU_DIM), l),
            _layer_spec((LRU_GATE_GROUPS, V7X_MXU_DIM, V7X_MXU_DIM), l),
            _layer_spec((1, D_LRU), l), _layer_spec((1, D_LRU), l), _layer_spec((1, D_LRU), l),
            pl.BlockSpec(squeeze_lead(sconv0, (CONV_TAIL, bsub, SSD_CONV_DIM)), sconv0_map),
            pl.BlockSpec(squeeze_lead(lconv0, (CONV_TAIL, bsub, D_LRU)), lconv0_map),
            pl.BlockSpec(squeeze_lead(h0, (bsub, D_LRU)), h0_map),
        ],
        out_specs=[
            tile(out_steps, D_SSD), tile(out_steps, SSD_CONV_DIM), tile(out_steps, DT_PAD),
            tile(steps, D_LRU),
            conv_out(SSD_CONV_DIM), conv_out(D_LRU),
            pl.BlockSpec((bsub, D_LRU), lambda b, t: (b, 0)),
        ],
        out_shape=[
            jax.ShapeDtypeStruct((batch, n_t * out_steps, D_SSD), F32),
            jax.ShapeDtypeStruct((batch, n_t * out_steps, SSD_CONV_DIM), F32),
            jax.ShapeDtypeStruct((batch, n_t * out_steps, DT_PAD), F32),
            jax.ShapeDtypeStruct((batch, length, D_LRU), BF16 if strided else F32),
            jax.ShapeDtypeStruct((CONV_TAIL, batch, SSD_CONV_DIM), F32),
            jax.ShapeDtypeStruct((CONV_TAIL, batch, D_LRU), F32),
            jax.ShapeDtypeStruct((batch, D_LRU), F32),
        ],
        scratch_shapes=[
            pltpu.VMEM((SSD_CONV_DIM // V7X_LANES, rows, V7X_LANES) if strided
                       else (1, V7X_SUBLANES, V7X_LANES), F32),
            pltpu.VMEM((V7X_SUBLANES, V7X_LANES) if strided else (rows, D_MODEL), F32),
            pltpu.VMEM((tail + rows, SSD_CONV_DIM), F32),
            pltpu.VMEM((tail + rows, D_LRU), F32),
            pltpu.VMEM((rows, D_LRU), F32),
            pltpu.VMEM((rows, D_LRU), F32),
            pltpu.VMEM((rows, D_LRU), F32),
            pltpu.VMEM((bsub, D_LRU), F32),
        ],
        compiler_params=_params(2),
        name="in_proj_lru",
    )(x, w["mix_pre_g"], w["w_in"], w["ssd_conv_w"], w["ssd_conv_b"], w["lru_conv_w"], w["lru_conv_b"],
      w["ssd_dt_bias"], w["lru_wa"], w["lru_wx"], w["lru_ba"], w["lru_bx"], w["lru_lambda"],
      sconv0, lconv0, h0)


def _transpose_rows(x, rows):
    lanes = x.shape[1]
    if rows < lanes:
        x = jnp.concatenate([x, jnp.zeros((lanes - rows, lanes), x.dtype)], axis=0)
    return x.T[:, 0:rows]


def _split_bf16(x, terms):
    parts = []
    for _ in range(terms - 1):
        parts.append(x.astype(BF16))
        x = x - parts[-1].astype(F32)
    return parts + [x.astype(BF16)]


def _ssd_body(chunk, n_chunks, out_steps, seqs, first_layer, xbc_ref, dt_ref, z_ref, alog_ref, dvec_ref,
              g_ref, e_ref, h0_ref, *rest):
    (y_ref, hout_ref, ht_s) = rest[-3:]
    if first_layer is not None:
        @pl.when(pl.program_id(1) == 0)
        def _():
            for k in range(hout_ref.shape[0]):
                if k != first_layer:
                    hout_ref[k] = jnp.zeros(hout_ref.shape[1:], F32)
        hout_ref = hout_ref.at[first_layer]
    stages = [_ssd_sequence(chunk, n_chunks, out_steps, xbc_ref.at[s], dt_ref.at[s], z_ref.at[s],
                            alog_ref, dvec_ref, g_ref, e_ref, h0_ref.at[s], y_ref.at[s],
                            hout_ref.at[s], ht_s.at[s]) for s in range(seqs)]
    for _ in itertools.zip_longest(*stages):
        pass


def _ssd_sequence(chunk, n_chunks, out_steps, xbc_ref, dt_ref, z_ref, alog_ref, dvec_ref, g_ref, e_ref,
                  h0_ref, y_ref, hout_ref, ht_s):
    c = pl.program_id(1)
    group_cols = HEADS_PER_GROUP * SSD_HEAD_DIM

    single = n_chunks == 1
    if single:
        h_given = h0_ref[...].reshape(D_SSD, SSD_STATE)
        ht_bf = h_given.astype(BF16).T
    else:
        @pl.when(c == 0)
        def _():
            ht_s[...] = h0_ref[...].reshape(D_SSD, SSD_STATE).T

    dt = dt_ref[...]
    a = -jnp.exp(alog_ref[...])
    row = lax.broadcasted_iota(jnp.int32, (chunk, chunk), 0)
    col = lax.broadcasted_iota(jnp.int32, (chunk, chunk), 1)
    causal = row >= col
    ones_lower = jnp.where(causal, 1.0, 0.0).astype(BF16)
    da_hi, da_mid, da_lo = _split_bf16(dt * a, 3)
    cum = _dot(ones_lower, da_hi) + _dot(ones_lower, da_mid) + _dot(ones_lower, da_lo)
    yield
    cum_t = _transpose_rows(cum, chunk)
    dt_t = _transpose_rows(dt, chunk)
    tot = cum[chunk - 1:chunk, :]
    cdec = jnp.exp(tot)
    factors = jnp.concatenate([jnp.exp(cum), jnp.exp(tot - cum) * dt], axis=0)
    spread = _dot(factors.astype(BF16), e_ref[...])
    yield
    ecum_x = spread[0:chunk, :]
    wend_x = spread[chunk:2 * chunk, :]
    if not single:
        cdec_parts = _split_bf16(jnp.broadcast_to(cdec, (V7X_SUBLANES, DT_PAD)), 3)
        cdec_x = (_dot(cdec_parts[0], e_ref[...]) + _dot(cdec_parts[1], e_ref[...])
                  + _dot(cdec_parts[2], e_ref[...]))[0:1, :]
    low_half = lax.broadcasted_iota(jnp.int32, (chunk, V7X_LANES), 1) < SSD_HEAD_DIM

    y_groups = []
    for g in range(SSD_GROUPS):
        b0 = D_SSD + g * SSD_STATE
        c0 = D_SSD + SSD_GROUPS * SSD_STATE + g * SSD_STATE
        gcols = slice(g * group_cols, (g + 1) * group_cols)
        bg_t = _transpose_rows(xbc_ref[:, b0:b0 + SSD_STATE], chunk).astype(BF16)
        cg = xbc_ref[:, c0:c0 + SSD_STATE].astype(BF16)
        cb = _dot(cg, bg_t)
        yield
        y_pairs = []
        for pair in range(HEADS_PER_GROUP // 2):
            h = g * HEADS_PER_GROUP + 2 * pair
            x_pair = xbc_ref[:, h * SSD_HEAD_DIM:(h + 2) * SSD_HEAD_DIM]
            scores = []
            for hh in (h, h + 1):
                diff = cum[:, hh:hh + 1] - cum_t[hh:hh + 1, :]
                decay = jnp.exp(jnp.where(causal, diff, -jnp.inf))
                scores.append((cb * decay * dt_t[hh:hh + 1, :]).astype(BF16))
            x_lo = jnp.where(low_half, x_pair, 0.0).astype(BF16)
            x_hi = jnp.where(low_half, 0.0, x_pair).astype(BF16)
            if chunk % V7X_LANES == 0:
                y_pairs.append(_dot(jnp.concatenate(scores, axis=1),
                                    jnp.concatenate([x_lo, x_hi], axis=0)))
            else:
                y_pairs.append(_dot(scores[0], x_lo) + _dot(scores[1], x_hi))
            yield
        xg = xbc_ref[:, gcols]
        xw = (xg * wend_x[:, gcols]).astype(BF16)
        if single:
            y_off = _dot(cg, ht_bf[:, gcols]) * ecum_x[:, gcols]
            bg = xbc_ref[:, b0:b0 + SSD_STATE].astype(BF16)
            update = lax.dot_general(xw, bg, (((0,), (0,)), ((), ())), preferred_element_type=F32)
            heads = range(g * HEADS_PER_GROUP, (g + 1) * HEADS_PER_GROUP)
            decay_rows = jnp.concatenate(
                [jnp.broadcast_to(cdec[:, h:h + 1], (SSD_HEAD_DIM, SSD_STATE)) for h in heads], axis=0)
            h_new = decay_rows * h_given[gcols, :] + update
            hout_ref[g * HEADS_PER_GROUP:(g + 1) * HEADS_PER_GROUP] = h_new.reshape(
                HEADS_PER_GROUP, SSD_HEAD_DIM, SSD_STATE)
        else:
            ht_prev = ht_s[:, gcols]
            y_off = _dot(cg, ht_prev.astype(BF16)) * ecum_x[:, gcols]
            ht_s[:, gcols] = cdec_x[:, gcols] * ht_prev + _dot(bg_t, xw)
        y_groups.append(jnp.concatenate(y_pairs, axis=1) + y_off + dvec_ref[:, gcols] * xg)
        yield

    y = _rms(jnp.concatenate(y_groups, axis=1) * _silu(z_ref[...]), g_ref[...])
    y_ref[...] = y[0:out_steps, :].astype(y_ref.dtype)

    if not single:
        @pl.when(c == n_chunks - 1)
        def _():
            hout_ref[...] = ht_s[...].T.reshape(SSD_HEADS, SSD_HEAD_DIM, SSD_STATE)


def _ssd_call(xbc, dt, z, l, w, h0, h0_map, h_stack, *, chunk, out_len, seqs):
    batch, length, _ = xbc.shape
    depth = w["ssd_a_log"].shape[0]
    n_chunks = length // chunk
    assert length == n_chunks * chunk and (out_len == length or n_chunks == 1) and batch % seqs == 0
    out_steps = min(chunk, out_len)

    def per_seq(n_steps, width):
        return pl.BlockSpec((seqs, n_steps, width), lambda b, c: (b, c, 0))

    state_tail = (SSD_HEADS, SSD_HEAD_DIM, SSD_STATE)
    h0_block = (None,) * (h0.ndim - 4) + (seqs,) + state_tail
    operands = [xbc, dt, z, w["ssd_a_log"], w["ssd_d_cols"], w["ssd_norm_g"], w["head_spread"], h0]
    in_specs = [
        per_seq(chunk, SSD_CONV_DIM), per_seq(chunk, DT_PAD), per_seq(chunk, D_SSD),
        _layer_spec((1, DT_PAD), l), _layer_spec((1, D_SSD), l), _layer_spec((1, D_SSD), l),
        pl.BlockSpec((DT_PAD, D_SSD), lambda b, c: (0, 0), pipeline_mode=pl.Buffered(1)),
        pl.BlockSpec(h0_block, h0_map),
    ]
    aliases = {}
    if h_stack is not None:
        aliases = {len(operands): 1}
        operands.append(h_stack)
        in_specs.append(pl.BlockSpec(memory_space=pl.ANY))
    first = h_stack is None
    return pl.pallas_call(
        functools.partial(_ssd_body, chunk, n_chunks, out_steps, seqs, l if first else None),
        grid=(batch // seqs, n_chunks),
        in_specs=in_specs,
        out_specs=[
            per_seq(out_steps, D_SSD),
            pl.BlockSpec((depth, seqs) + state_tail, lambda b, c: (0, b, 0, 0, 0)) if first
            else pl.BlockSpec((None, seqs) + state_tail, lambda b, c: (l, b, 0, 0, 0)),
        ],
        out_shape=[
            jax.ShapeDtypeStruct((batch, out_len, D_SSD), BF16 if out_steps % (2 * V7X_SUBLANES) == 0 else F32),
            jax.ShapeDtypeStruct((depth, batch) + state_tail, F32),
        ],
        scratch_shapes=[pltpu.VMEM((seqs, SSD_STATE, D_SSD), F32)],
        input_output_aliases=aliases,
        compiler_params=_params(2),
        name="ssd_chunk",
    )(*operands)


def _regroup_w_in_body(wt_ref, o_ref):
    o_dt = COL_XBC + SSD_CONV_DIM
    o_gate = o_dt + SSD_HEADS

    def put(c0, rows):
        o_ref[:, c0:c0 + V7X_LANES] = rows.T.astype(BF16)

    for c0 in range(COL_Z, COL_GATE, V7X_LANES):
        put(c0, wt_ref[c0:c0 + V7X_LANES, :])
    for c0 in range(COL_GATE, COL_DT, V7X_LANES):
        r0 = o_gate + c0 - COL_GATE
        put(c0, wt_ref[r0:r0 + V7X_LANES, :])
    put(COL_DT, jnp.concatenate([wt_ref[o_dt:o_gate, :],
                                 jnp.zeros((DT_PAD - SSD_HEADS, D_MODEL), F32)], axis=0))


def _regroup_w_in(w_in):
    depth, d_model, in_cols = w_in.shape
    assert in_cols == D_SSD + SSD_CONV_DIM + SSD_HEADS + 2 * D_LRU and d_model == D_MODEL
    return pl.pallas_call(
        _regroup_w_in_body,
        grid=(depth,),
        in_specs=[pl.BlockSpec((None, in_cols, d_model), lambda l: (l, 0, 0),
                               pipeline_mode=pl.Buffered(1))],
        out_specs=pl.BlockSpec((None, d_model, IN_COLS_PAD), lambda l: (l, 0, 0)),
        out_shape=jax.ShapeDtypeStruct((depth, d_model, IN_COLS_PAD), BF16),
        compiler_params=_params(1),
        name="regroup_w_in",
    )(w_in.transpose(0, 2, 1))


def _prepare_weights(p):
    depth = p["w_in"].shape[0]

    def vec(a):
        return a.reshape(depth, 1, a.shape[-1])

    def pad_heads(a):
        return jnp.pad(a, ((0, 0), (0, DT_PAD - SSD_HEADS))).reshape(depth, 1, DT_PAD)

    def block_diag(a):
        per = V7X_MXU_DIM // LRU_BLOCK_W
        a = a.reshape(depth, LRU_GATE_GROUPS, per, LRU_BLOCK_W, LRU_BLOCK_W)
        eye = jnp.eye(per, dtype=a.dtype)
        a = a[:, :, :, :, None, :] * eye[None, None, :, None, :, None]
        return a.reshape(depth, LRU_GATE_GROUPS, V7X_MXU_DIM, V7X_MXU_DIM).astype(BF16)

    w = {
        "w_in": _regroup_w_in(p["w_in"]),
        "w_out": p["w_out"].astype(BF16),
        "lru_wa": block_diag(p["lru_wa"]),
        "lru_wx": block_diag(p["lru_wx"]),
        "ssd_dt_bias": pad_heads(p["ssd_dt_bias"]),
        "ssd_a_log": pad_heads(p["ssd_a_log"]),
        "ssd_d_cols": jnp.repeat(p["ssd_d"], SSD_HEAD_DIM, axis=-1).reshape(depth, 1, D_SSD),
        "head_spread": (jnp.arange(DT_PAD)[:, None] == jnp.arange(D_SSD)[None, :] // SSD_HEAD_DIM
                        ).astype(BF16),
    }
    for name in ("ssd_conv_w", "lru_conv_w"):
        w[name] = jnp.repeat(p[name], V7X_SUBLANES, axis=1)
    for name in ("ssd_conv_b", "lru_conv_b"):
        w[name] = jnp.repeat(p[name][:, None, :], V7X_SUBLANES, axis=1)
    for name in ("ffn1_wg", "ffn1_wu", "ffn1_wd", "ffn2_wg", "ffn2_wu", "ffn2_wd"):
        w[name] = p[name].astype(BF16)
    for name in ("ffn1_pre_g", "ffn1_post_g", "mix_pre_g", "mix_post_g", "ffn2_pre_g", "ffn2_post_g",
                 "ssd_norm_g", "lru_ba", "lru_bx", "lru_lambda"):
        w[name] = vec(p[name])
    return w


def _segment_mixer(x, l, w, states, ssd_stack, *, bsub, steps, chunk, pad_steps, seqs, n_sub=1):
    (sconv0, sconv0_map), (lconv0, lconv0_map), (h_lru0, h_lru0_map), (h_ssd0, h_ssd0_map) = states
    length = x.shape[1]
    z, xbc, dt, y_lru, sconv, lconv, h_lru = _in_proj_call(
        x, l, w, sconv0, sconv0_map, lconv0, lconv0_map,
        h_lru0, h_lru0_map, bsub=bsub, steps=steps, pad_steps=pad_steps, n_sub=n_sub)
    y_ssd, ssd_stack = _ssd_call(xbc, dt, z, l, w, h_ssd0, h_ssd0_map, ssd_stack,
                                 chunk=chunk, out_len=length, seqs=seqs)
    return y_ssd, y_lru, (sconv, h_lru, lconv), ssd_stack


def kernel(x_prompt, x_sample, state_ssd, state_ssd_conv, state_lru, state_lru_conv, meta_tokens,
           ffn1_pre_g, ffn1_post_g, ffn1_wg, ffn1_wu, ffn1_wd, mix_pre_g, mix_post_g, w_in,
           ssd_conv_w, ssd_conv_b, ssd_dt_bias, ssd_a_log, ssd_d, ssd_norm_g, lru_conv_w, lru_conv_b,
           lru_wa, lru_ba, lru_wx, lru_bx, lru_lambda, w_out, ffn2_pre_g, ffn2_post_g, ffn2_wg,
           ffn2_wu, ffn2_wd):
    bp, seq, _ = x_prompt.shape
    bs, dec_seq, _ = x_sample.shape
    depth = w_in.shape[0]
    assert bp == V7X_SUBLANES and seq % SSD_CHUNK == 0 and bs % SAMPLE_BATCH_TILE == 0
    w = _prepare_weights(dict(
        ffn1_pre_g=ffn1_pre_g, ffn1_post_g=ffn1_post_g, ffn1_wg=ffn1_wg, ffn1_wu=ffn1_wu,
        ffn1_wd=ffn1_wd, mix_pre_g=mix_pre_g, mix_post_g=mix_post_g, w_in=w_in,
        ssd_conv_w=ssd_conv_w, ssd_conv_b=ssd_conv_b, ssd_dt_bias=ssd_dt_bias, ssd_a_log=ssd_a_log,
        ssd_d=ssd_d, ssd_norm_g=ssd_norm_g, lru_conv_w=lru_conv_w, lru_conv_b=lru_conv_b,
        lru_wa=lru_wa, lru_ba=lru_ba, lru_wx=lru_wx, lru_bx=lru_bx, lru_lambda=lru_lambda,
        w_out=w_out, ffn2_pre_g=ffn2_pre_g, ffn2_post_g=ffn2_post_g, ffn2_wg=ffn2_wg,
        ffn2_wu=ffn2_wu, ffn2_wd=ffn2_wd))

    xm = jnp.broadcast_to(meta_tokens.astype(F32)[None], (bp, N_META, D_MODEL))
    xp = x_prompt
    xs = x_sample
    sample_steps = ((dec_seq + V7X_SUBLANES - 1) // V7X_SUBLANES) * V7X_SUBLANES
    sconv_in = state_ssd_conv.transpose(0, 2, 1, 3)
    lconv_in = state_lru_conv.transpose(0, 2, 1, 3)
    tail3 = lambda b, t: (0, b, 0)
    vec2 = lambda b, t: (b, 0)
    seq_state = lambda b, c: (b, 0, 0, 0)
    zero_states = ((jnp.zeros((CONV_TAIL, bp, SSD_CONV_DIM), F32), tail3),
                   (jnp.zeros((CONV_TAIL, bp, D_LRU), F32), tail3),
                   (jnp.zeros((bp, D_LRU), F32), vec2),
                   (jnp.zeros((bp, SSD_HEADS, SSD_HEAD_DIM, SSD_STATE), F32), seq_state))

    layer_state = lambda b, c, l: (l, b, 0, 0, 0)
    p_out = [[] for _ in range(3)]
    s_out = [[] for _ in range(3)]
    m_ssd = p_ssd = s_ssd = None
    def flat(a):
        return a.reshape(a.shape[0] * a.shape[1], a.shape[2])

    for l in range(depth):
        at_l = functools.partial(layer_state, l=l)
        shapes = (xp.shape, xs.shape, xm.shape)
        xp, xs, xm = (a.reshape(s) for a, s in
                      zip(_ffn_call([flat(xp), flat(xs), flat(xm)], l, w, "ffn1"), shapes))
        m_ssd_y, m_lru_y, m_st, m_ssd = _segment_mixer(
            xm, l, w, zero_states, m_ssd,
            bsub=bp, steps=N_META, chunk=N_META, pad_steps=0, seqs=PROMPT_SEQS)
        p_ssd_y, p_lru_y, p_st, p_ssd = _segment_mixer(
            xp, l, w, ((m_st[0], tail3), (m_st[2], tail3), (m_st[1], vec2), (m_ssd, at_l)), p_ssd,
            bsub=bp, steps=PROMPT_STEPS, chunk=SSD_CHUNK, pad_steps=0, seqs=PROMPT_SEQS,
            n_sub=PROMPT_SUB_TILES)
        s_ssd_y, s_lru_y, s_st, s_ssd = _segment_mixer(
            xs, l, w, ((sconv_in, lambda b, t, l=l: (l, 0, b, 0)),
                       (lconv_in, lambda b, t, l=l: (l, 0, b, 0)),
                       (state_lru, lambda b, t, l=l: (l, b, 0)),
                       (state_ssd, at_l)), s_ssd,
            bsub=SAMPLE_BATCH_TILE, steps=dec_seq, chunk=sample_steps,
            pad_steps=sample_steps - dec_seq, seqs=SAMPLE_SEQS)
        xp, xs, xm = (a.reshape(s) for a, s in zip(_out_ffn_call(
            [(flat(xp), flat(p_ssd_y), flat(p_lru_y)), (flat(xs), flat(s_ssd_y), flat(s_lru_y)),
             (flat(xm), flat(m_ssd_y), flat(m_lru_y))], l, w), shapes))
        for acc, st in ((p_out, p_st), (s_out, s_st)):
            for k in range(3):
                acc[k].append(st[k])

    p_conv, p_lru, p_lconv = (jnp.stack(a) for a in p_out)
    s_conv, s_lru, s_lconv = (jnp.stack(a) for a in s_out)
    p_conv, p_lconv, s_conv, s_lconv = (a.transpose(0, 2, 1, 3) for a in (p_conv, p_lconv, s_conv, s_lconv))
    return (xp, xs, p_ssd, p_conv, p_lru, p_lconv, s_ssd, s_conv, s_lru, s_lconv)
```

```python
import functools
import itertools

import jax
import jax.numpy as jnp
from jax import lax
from jax.experimental import pallas as pl
from jax.experimental.pallas import tpu as pltpu

F32 = jnp.float32
BF16 = jnp.bfloat16

D_MODEL = 1024
D_SSD = 1024
D_LRU = 1024
SSD_HEADS = 16
SSD_HEAD_DIM = 64
SSD_GROUPS = 2
SSD_STATE = 128
HEADS_PER_GROUP = SSD_HEADS // SSD_GROUPS
CONV_W = 4
CONV_TAIL = CONV_W - 1
SSD_CONV_DIM = D_SSD + 2 * SSD_GROUPS * SSD_STATE
LRU_BLOCKS = 16
LRU_BLOCK_W = D_LRU // LRU_BLOCKS
LRU_C = 8.0
EPS = 1e-6
N_META = 16
SSD_CHUNK = 128

V7X_LANES = 128
V7X_SUBLANES = 8
V7X_MXU_DIM = 256
V7X_VMEM_LIMIT_BYTES = 56 * 1024 * 1024

DT_PAD = V7X_LANES
LRU_GATE_GROUPS = D_LRU // V7X_MXU_DIM
COL_Z = 0
COL_XBC = COL_Z + D_SSD
COL_GATE = COL_XBC + SSD_CONV_DIM
COL_XR = COL_GATE + D_LRU
COL_DT = COL_XR + D_LRU
IN_COLS_PAD = COL_DT + DT_PAD
GELU_K = 0.7978845608028654

FFN_ROWS = 512
FFN_SUB_TILES = 2
OUT_FFN_ROWS = 512
PROMPT_STEPS = 64
SAMPLE_BATCH_TILE = 64
PROMPT_SUB_TILES = 2
SCAN_UNROLL_LIMIT = 128
PROMPT_SEQS = 4
SAMPLE_SEQS = 8
ELEMENTWISE_ROWS = 16
ELEMENTWISE_COLS = 512


def _rms(x, g):
    return x * lax.rsqrt(jnp.mean(x * x, axis=-1, keepdims=True) + EPS) * g


def _silu(x):
    return x * _sigmoid(x)


def _sigmoid(x):
    return 0.5 * jnp.tanh(0.5 * x) + 0.5


def _softplus(x):
    return jnp.maximum(x, 0.0) + jnp.log1p(jnp.exp(-jnp.abs(x)))


def _gelu_tanh(x):
    return 0.5 * x * (1.0 + jnp.tanh(GELU_K * (x + 0.044715 * (x * x * x))))


def _dot(a, b):
    return jnp.dot(a, b, preferred_element_type=F32)


def _layer_spec(tail, l):
    zeros = (0,) * len(tail)
    return pl.BlockSpec((None,) + tuple(tail), lambda *_: (l,) + zeros, pipeline_mode=pl.Buffered(1))


def _params(n_axes):
    return pltpu.CompilerParams(dimension_semantics=("arbitrary",) * n_axes,
                                vmem_limit_bytes=V7X_VMEM_LIMIT_BYTES)


def _row_tile(rows, want):
    tm = min(rows, want)
    assert rows % tm == 0
    return tm


def _ffn_math(x, gpre, gpost, wg_ref, wu_ref, wd_ref, prologue=None):
    n_sub = FFN_SUB_TILES if x.shape[0] >= FFN_ROWS else 1
    rows = x.shape[0] // n_sub
    outs = [None] * n_sub

    def sub_tile(i):
        sl = slice(i * rows, (i + 1) * rows)
        xi = x[sl]
        if prologue is not None:
            xi = prologue(xi, sl)
            yield
        xn = _rms(xi, gpre).astype(BF16)
        yield
        hg = _dot(xn, wg_ref[...])
        hu = _dot(xn, wu_ref[...])
        yield
        a = (_silu(hg) * hu).astype(BF16)
        yield
        y = _dot(a, wd_ref[...])
        yield
        outs[i] = xi + 0.5 * _rms(y, gpost)

    _run_skewed([sub_tile(i) for i in range(n_sub)])
    return outs[0] if n_sub == 1 else jnp.concatenate(outs, axis=0)


def _run_skewed(stage_generators):
    waiting, running = list(stage_generators), []
    while waiting or running:
        if waiting:
            running.append(waiting.pop(0))
        for stages in list(running):
            if next(stages, "done") == "done":
                running.remove(stages)


def _token_call(math, segments, weights, weight_specs, name, tile_rows):
    n_ops = len(segments[0])
    tiles = [_row_tile(seg[0].shape[0], tile_rows) for seg in segments]
    counts = [seg[0].shape[0] // tm for seg, tm in zip(segments, tiles)]
    starts = [sum(counts[:k]) for k in range(len(segments))]

    def seg_spec(k):
        mode = {} if counts[k] > 1 else dict(pipeline_mode=pl.Buffered(1))
        return pl.BlockSpec((tiles[k], D_MODEL),
                            lambda i: (jnp.clip(i - starts[k], 0, counts[k] - 1), 0), **mode)

    def body(*refs):
        n_in = n_ops * len(segments)
        x_refs, w_refs, o_refs = refs[:n_in], refs[n_in:n_in + len(weights)], refs[n_in + len(weights):]
        i = pl.program_id(0)

        def run(k):
            operands = [r[...] for r in x_refs[k * n_ops:(k + 1) * n_ops]]
            o_refs[k][...] = math(*operands, *w_refs)

        for k in range(len(segments)):
            pl.when((i >= starts[k]) & (i < starts[k] + counts[k]))(functools.partial(run, k))

    return pl.pallas_call(
        body,
        grid=(sum(counts),),
        in_specs=[seg_spec(k) for k in range(len(segments)) for _ in range(n_ops)] + weight_specs,
        out_specs=[seg_spec(k) for k in range(len(segments))],
        out_shape=[jax.ShapeDtypeStruct(seg[0].shape, F32) for seg in segments],
        compiler_params=_params(1),
        name=name,
    )(*[a for seg in segments for a in seg], *weights)


def _ffn_math_refs(x, gpre_ref, gpost_ref, wg_ref, wu_ref, wd_ref):
    return _ffn_math(x, gpre_ref[...], gpost_ref[...], wg_ref, wu_ref, wd_ref)


def _ffn_call(xs, l, w, prefix):
    d_ff = w[prefix + "_wg"].shape[-1]
    return _token_call(
        _ffn_math_refs, [(x,) for x in xs],
        [w[prefix + "_pre_g"], w[prefix + "_post_g"], w[prefix + "_wg"], w[prefix + "_wu"],
         w[prefix + "_wd"]],
        [_layer_spec((1, D_MODEL), l), _layer_spec((1, D_MODEL), l),
         _layer_spec((D_MODEL, d_ff), l), _layer_spec((D_MODEL, d_ff), l),
         _layer_spec((d_ff, D_MODEL), l)],
        prefix, FFN_ROWS)


def _out_ffn_math(x, ys, yl, wos_ref, wol_ref, gmix_ref, gpre_ref, gpost_ref, wg_ref, wu_ref, wd_ref):
    def mixer_residual(x_rows, sl):
        m = _dot(ys[sl].astype(BF16), wos_ref[...]) + _dot(yl[sl].astype(BF16), wol_ref[...])
        return x_rows + _rms(m, gmix_ref[...])

    return _ffn_math(x, gpre_ref[...], gpost_ref[...], wg_ref, wu_ref, wd_ref, mixer_residual)


def _out_ffn_call(segments, l, w):
    d_ff = w["ffn2_wg"].shape[-1]
    return _token_call(
        _out_ffn_math, segments,
        [w["w_out"], w["w_out"], w["mix_post_g"], w["ffn2_pre_g"], w["ffn2_post_g"],
         w["ffn2_wg"], w["ffn2_wu"], w["ffn2_wd"]],
        [pl.BlockSpec((None, D_SSD, D_MODEL), lambda i: (l, 0, 0), pipeline_mode=pl.Buffered(1)),
         pl.BlockSpec((None, D_LRU, D_MODEL), lambda i: (l, D_SSD // D_LRU, 0),
                      pipeline_mode=pl.Buffered(1)),
         _layer_spec((1, D_MODEL), l), _layer_spec((1, D_MODEL), l), _layer_spec((1, D_MODEL), l),
         _layer_spec((D_MODEL, d_ff), l), _layer_spec((D_MODEL, d_ff), l),
         _layer_spec((d_ff, D_MODEL), l)],
        "out_ffn2", OUT_FFN_ROWS)


def _in_proj_body(bsub, steps, pad_steps, n_sub,
                  x_ref, g_ref, w_ref, scw_ref, scb_ref, lcw_ref, lcb_ref, dtb_ref,
                  wa_ref, wx_ref, ba_ref, bx_ref, lam_ref, sconv0_ref, lconv0_ref, h0_ref,
                  z_ref, xbc_ref, dt_ref, ylru_ref, sconv_ref, lconv_ref, hout_ref,
                  stage, xbuf, sbuf, lbuf, gate_buf, ra_buf, rx_buf, h_carry):
    rows = steps * bsub
    tail = CONV_TAIL * bsub
    sub_steps = steps // n_sub
    sub_rows = sub_steps * bsub
    rc = min(sub_rows, ELEMENTWISE_ROWS)
    strided = sub_steps >= bsub

    def lane_tile(j):
        return slice(j * V7X_LANES, (j + 1) * V7X_LANES)

    def load_block(ref, t0):
        r0 = t0 * bsub
        if not strided:
            for t in range(sub_steps):
                xbuf[r0 + t * bsub:r0 + (t + 1) * bsub, :] = ref[:, t0 + t, :]
            return xbuf[r0:r0 + sub_rows, :]
        n_tiles = ref.shape[2] // V7X_LANES
        for j in range(n_tiles):
            for b in range(bsub):
                stage[j, pl.ds(r0 + b, sub_steps, stride=bsub), :] = ref[b, t0:t0 + sub_steps, lane_tile(j)]
        return jnp.concatenate([stage[j, r0:r0 + sub_rows, :] for j in range(n_tiles)], axis=1)

    def store_block(ref, t0, read):
        r0 = t0 * bsub
        if not strided:
            value = read(slice(0, ref.shape[2]))
            for t in range(sub_steps):
                ref[:, t0 + t, :] = value[t * bsub:(t + 1) * bsub, :]
            return
        n_tiles = ref.shape[2] // V7X_LANES
        for j in range(n_tiles):
            stage[j, r0:r0 + sub_rows, :] = read(lane_tile(j))
        for j in range(n_tiles):
            for b in range(bsub):
                ref[b, t0:t0 + sub_steps, lane_tile(j)] = stage[
                    j, pl.ds(r0 + b, sub_steps, stride=bsub), :].astype(ref.dtype)

    def conv_in_place(buf, w_ref, b_ref, act, r_lo):
        sub = V7X_SUBLANES
        for c0 in range(0, buf.shape[1], ELEMENTWISE_COLS):
            cols = slice(c0, c0 + ELEMENTWISE_COLS)
            taps = [w_ref[k * sub:(k + 1) * sub, cols] for k in range(CONV_W)]
            bias = b_ref[:, cols]
            for r0 in range(r_lo, r_lo + sub_rows, sub):
                acc = bias + buf[r0:r0 + sub, cols] * taps[0]
                for k in range(1, CONV_W):
                    acc = acc + buf[r0 + k * bsub:r0 + k * bsub + sub, cols] * taps[k]
                buf[r0:r0 + sub, cols] = act(acc)

    @pl.when(pl.program_id(1) == 0)
    def _():
        for k in range(CONV_TAIL):
            sbuf[k * bsub:(k + 1) * bsub, :] = sconv0_ref[k]
            lbuf[k * bsub:(k + 1) * bsub, :] = lconv0_ref[k]
        h_carry[...] = h0_ref[...]

    neg_c_softplus = -LRU_C * _softplus(-lam_ref[...])

    def sub_tile(i):
        t0 = i * sub_steps
        r_lo = t0 * bsub
        out = slice(r_lo, r_lo + sub_rows)
        pre = slice(tail + r_lo, tail + r_lo + sub_rows)
        xn = _rms(load_block(x_ref, t0), g_ref[...]).astype(BF16)
        yield
        lbuf[pre, :] = _dot(xn, w_ref[:, COL_XR:COL_DT])
        gate_buf[out, :] = _dot(xn, w_ref[:, COL_GATE:COL_XR])
        yield
        conv_in_place(lbuf, lcw_ref, lcb_ref, lambda v: v, r_lo)
        yield
        for q in range(LRU_GATE_GROUPS):
            cols = slice(q * V7X_MXU_DIM, (q + 1) * V7X_MXU_DIM)
            xr_bf = lbuf[out, cols].astype(BF16)
            ra_buf[out, cols] = _dot(xr_bf, wa_ref[q])
            rx_buf[out, cols] = _dot(xr_bf, wx_ref[q])
        yield
        for c0 in range(0, D_LRU, ELEMENTWISE_COLS):
            cols = slice(c0, c0 + ELEMENTWISE_COLS)
            ncs, ba, bx = neg_c_softplus[:, cols], ba_ref[:, cols], bx_ref[:, cols]
            for r0 in range(r_lo, r_lo + sub_rows, rc):
                sl = slice(r0, r0 + rc)
                log_a = ncs * _sigmoid(ra_buf[sl, cols] + ba)
                a = jnp.exp(log_a)
                m = -jnp.tanh(log_a) * (a * a + 1.0)
                mult = jnp.where(m > 0.0, m * lax.rsqrt(m), 0.0)
                rx_buf[sl, cols] = mult * _sigmoid(rx_buf[sl, cols] + bx) * lbuf[sl, cols]
                ra_buf[sl, cols] = a
                gate_buf[sl, cols] = _gelu_tanh(gate_buf[sl, cols])
        yield
        z = _dot(xn, w_ref[:, COL_Z:COL_XBC])
        sbuf[pre, :] = _dot(xn, w_ref[:, COL_XBC:COL_GATE])
        dt = _softplus(_dot(xn, w_ref[:, COL_DT:IN_COLS_PAD]) + dtb_ref[...])
        yield
        store_block(z_ref, t0, lambda cols: z[:, cols])
        store_block(dt_ref, t0, lambda cols: dt[:, cols])
        conv_in_place(sbuf, scw_ref, scb_ref, _silu, r_lo)
        yield
        store_block(xbc_ref, t0, lambda cols: sbuf[out, cols])

    _run_skewed([sub_tile(i) for i in range(n_sub)])

    for k in range(CONV_TAIL):
        sconv_ref[k] = sbuf[rows + k * bsub:rows + (k + 1) * bsub, :]
        lconv_ref[k] = lbuf[rows + k * bsub:rows + (k + 1) * bsub, :]

    def scan_step(b0, t, h):
        start = t * bsub + b0
        if not isinstance(start, int):
            start = pl.multiple_of(start, V7X_SUBLANES)
        sl = pl.ds(start, V7X_SUBLANES)
        h = ra_buf[sl, :] * h + rx_buf[sl, :]
        gate_buf[sl, :] = h * gate_buf[sl, :]
        return h

    n_groups = bsub // V7X_SUBLANES
    if n_groups * steps <= SCAN_UNROLL_LIMIT:
        for bg in range(n_groups):
            hsl = slice(bg * V7X_SUBLANES, (bg + 1) * V7X_SUBLANES)
            h = h_carry[hsl, :]
            for t in range(steps):
                h = scan_step(bg * V7X_SUBLANES, t, h)
            h_carry[hsl, :] = h
    else:
        def group_scan(bg, carry):
            b0 = pl.multiple_of(bg * V7X_SUBLANES, V7X_SUBLANES)
            hsl = pl.ds(b0, V7X_SUBLANES)
            h_carry[hsl, :] = lax.fori_loop(0, steps, functools.partial(scan_step, b0), h_carry[hsl, :])
            return carry

        lax.fori_loop(0, n_groups, group_scan, 0)
    for i in range(n_sub):
        store_block(ylru_ref, i * sub_steps,
                    lambda cols, i=i: gate_buf[i * sub_rows:(i + 1) * sub_rows, cols])
    hout_ref[...] = h_carry[...]
    sbuf[0:tail, :] = sbuf[rows:rows + tail, :]
    lbuf[0:tail, :] = lbuf[rows:rows + tail, :]
    for t in range(steps, steps + pad_steps):
        for ref in (z_ref, xbc_ref, dt_ref):
            ref[:, t, :] = jnp.zeros((bsub, ref.shape[2]), F32)


def _in_proj_call(x, l, w, sconv0, sconv0_map, lconv0, lconv0_map, h0, h0_map, *,
                  bsub, steps, pad_steps=0, n_sub=1):
    batch, length, _ = x.shape
    n_t = length // steps
    n_b = batch // bsub
    assert length == n_t * steps and batch == n_b * bsub and bsub % V7X_SUBLANES == 0
    assert pad_steps == 0 or n_t == 1
    assert steps >= CONV_TAIL
    assert steps % n_sub == 0
    strided = steps // n_sub >= bsub
    out_steps = steps + pad_steps
    rows = steps * bsub
    tail = CONV_TAIL * bsub

    def tile(n_steps, width):
        return pl.BlockSpec((bsub, n_steps, width), lambda b, t: (b, t, 0))

    def squeeze_lead(a, block):
        return (None,) * (a.ndim - len(block)) + block

    def conv_out(width):
        return pl.BlockSpec((CONV_TAIL, bsub, width), lambda b, t: (0, b, 0))

    return pl.pallas_call(
        functools.partial(_in_proj_body, bsub, steps, pad_steps, n_sub),
        grid=(n_b, n_t),
        in_specs=[
            tile(steps, D_MODEL),
            _layer_spec((1, D_MODEL), l),
            _layer_spec((D_MODEL, IN_COLS_PAD), l),
            _layer_spec((CONV_W * V7X_SUBLANES, SSD_CONV_DIM), l),
            _layer_spec((V7X_SUBLANES, SSD_CONV_DIM), l),
            _layer_spec((CONV_W * V7X_SUBLANES, D_LRU), l), _layer_spec((V7X_SUBLANES, D_LRU), l),
            _layer_spec((1, DT_PAD), l),
            _layer_spec((LRU_GATE_GROUPS, V7X_MXU_DIM, V7X_MXU_DIM), l),
            _layer_spec((LRU_GATE_GROUPS, V7X_MXU_DIM, V7X_MXU_DIM), l),
            _layer_spec((1, D_LRU), l), _layer_spec((1, D_LRU), l), _layer_spec((1, D_LRU), l),
            pl.BlockSpec(squeeze_lead(sconv0, (CONV_TAIL, bsub, SSD_CONV_DIM)), sconv0_map),
            pl.BlockSpec(squeeze_lead(lconv0, (CONV_TAIL, bsub, D_LRU)), lconv0_map),
            pl.BlockSpec(squeeze_lead(h0, (bsub, D_LRU)), h0_map),
        ],
        out_specs=[
            tile(out_steps, D_SSD), tile(out_steps, SSD_CONV_DIM), tile(out_steps, DT_PAD),
            tile(steps, D_LRU),
            conv_out(SSD_CONV_DIM), conv_out(D_LRU),
            pl.BlockSpec((bsub, D_LRU), lambda b, t: (b, 0)),
        ],
        out_shape=[
            jax.ShapeDtypeStruct((batch, n_t * out_steps, D_SSD), BF16 if strided else F32),
            jax.ShapeDtypeStruct((batch, n_t * out_steps, SSD_CONV_DIM), F32),
            jax.ShapeDtypeStruct((batch, n_t * out_steps, DT_PAD), F32),
            jax.ShapeDtypeStruct((batch, length, D_LRU), BF16 if strided else F32),
            jax.ShapeDtypeStruct((CONV_TAIL, batch, SSD_CONV_DIM), F32),
            jax.ShapeDtypeStruct((CONV_TAIL, batch, D_LRU), F32),
            jax.ShapeDtypeStruct((batch, D_LRU), F32),
        ],
        scratch_shapes=[
            pltpu.VMEM((SSD_CONV_DIM // V7X_LANES, rows, V7X_LANES) if strided
                       else (1, V7X_SUBLANES, V7X_LANES), F32),
            pltpu.VMEM((V7X_SUBLANES, V7X_LANES) if strided else (rows, D_MODEL), F32),
            pltpu.VMEM((tail + rows, SSD_CONV_DIM), F32),
            pltpu.VMEM((tail + rows, D_LRU), F32),
            pltpu.VMEM((rows, D_LRU), F32),
            pltpu.VMEM((rows, D_LRU), F32),
            pltpu.VMEM((rows, D_LRU), F32),
            pltpu.VMEM((bsub, D_LRU), F32),
        ],
        compiler_params=_params(2),
        name="in_proj_lru",
    )(x, w["mix_pre_g"], w["w_in"], w["ssd_conv_w"], w["ssd_conv_b"], w["lru_conv_w"], w["lru_conv_b"],
      w["ssd_dt_bias"], w["lru_wa"], w["lru_wx"], w["lru_ba"], w["lru_bx"], w["lru_lambda"],
      sconv0, lconv0, h0)


def _transpose_rows(x, rows):
    lanes = x.shape[1]
    if rows < lanes:
        x = jnp.concatenate([x, jnp.zeros((lanes - rows, lanes), x.dtype)], axis=0)
    return x.T[:, 0:rows]


def _split_bf16(x, terms):
    parts = []
    for _ in range(terms - 1):
        parts.append(x.astype(BF16))
        x = x - parts[-1].astype(F32)
    return parts + [x.astype(BF16)]


def _ssd_body(chunk, n_chunks, out_steps, seqs, first_layer, xbc_ref, dt_ref, z_ref, alog_ref, dvec_ref,
              g_ref, e_ref, h0_ref, *rest):
    (y_ref, hout_ref, ht_s) = rest[-3:]
    if first_layer is not None:
        @pl.when(pl.program_id(1) == 0)
        def _():
            for k in range(hout_ref.shape[0]):
                if k != first_layer:
                    hout_ref[k] = jnp.zeros(hout_ref.shape[1:], F32)
        hout_ref = hout_ref.at[first_layer]
    stages = [_ssd_sequence(chunk, n_chunks, out_steps, xbc_ref.at[s], dt_ref.at[s], z_ref.at[s],
                            alog_ref, dvec_ref, g_ref, e_ref, h0_ref.at[s], y_ref.at[s],
                            hout_ref.at[s], ht_s.at[s]) for s in range(seqs)]
    for _ in itertools.zip_longest(*stages):
        pass


def _ssd_sequence(chunk, n_chunks, out_steps, xbc_ref, dt_ref, z_ref, alog_ref, dvec_ref, g_ref, e_ref,
                  h0_ref, y_ref, hout_ref, ht_s):
    c = pl.program_id(1)
    group_cols = HEADS_PER_GROUP * SSD_HEAD_DIM

    single = n_chunks == 1
    if single:
        h_given = h0_ref[...].reshape(D_SSD, SSD_STATE)
        ht_bf = h_given.astype(BF16).T
    else:
        @pl.when(c == 0)
        def _():
            ht_s[...] = h0_ref[...].reshape(D_SSD, SSD_STATE).T

    dt = dt_ref[...]
    a = -jnp.exp(alog_ref[...])
    row = lax.broadcasted_iota(jnp.int32, (chunk, chunk), 0)
    col = lax.broadcasted_iota(jnp.int32, (chunk, chunk), 1)
    causal = row >= col
    ones_lower = jnp.where(causal, 1.0, 0.0).astype(BF16)
    da_hi, da_mid, da_lo = _split_bf16(dt * a, 3)
    cum = _dot(ones_lower, da_hi) + _dot(ones_lower, da_mid) + _dot(ones_lower, da_lo)
    yield
    cum_t = _transpose_rows(cum, chunk)
    dt_t = _transpose_rows(dt, chunk)
    tot = cum[chunk - 1:chunk, :]
    cdec = jnp.exp(tot)
    factors = jnp.concatenate([jnp.exp(cum), jnp.exp(tot - cum) * dt], axis=0)
    spread = _dot(factors.astype(BF16), e_ref[...])
    yield
    ecum_x = spread[0:chunk, :]
    wend_x = spread[chunk:2 * chunk, :]
    if not single:
        cdec_parts = _split_bf16(jnp.broadcast_to(cdec, (V7X_SUBLANES, DT_PAD)), 3)
        cdec_x = (_dot(cdec_parts[0], e_ref[...]) + _dot(cdec_parts[1], e_ref[...])
                  + _dot(cdec_parts[2], e_ref[...]))[0:1, :]
    low_half = lax.broadcasted_iota(jnp.int32, (chunk, V7X_LANES), 1) < SSD_HEAD_DIM

    y_groups = []
    for g in range(SSD_GROUPS):
        b0 = D_SSD + g * SSD_STATE
        c0 = D_SSD + SSD_GROUPS * SSD_STATE + g * SSD_STATE
        gcols = slice(g * group_cols, (g + 1) * group_cols)
        bg_t = _transpose_rows(xbc_ref[:, b0:b0 + SSD_STATE], chunk).astype(BF16)
        cg = xbc_ref[:, c0:c0 + SSD_STATE].astype(BF16)
        cb = _dot(cg, bg_t)
        yield
        y_pairs = []
        for pair in range(HEADS_PER_GROUP // 2):
            h = g * HEADS_PER_GROUP + 2 * pair
            x_pair = xbc_ref[:, h * SSD_HEAD_DIM:(h + 2) * SSD_HEAD_DIM]
            scores = []
            for hh in (h, h + 1):
                diff = cum[:, hh:hh + 1] - cum_t[hh:hh + 1, :]
                decay = jnp.exp(jnp.where(causal, diff, -jnp.inf))
                scores.append((cb * decay * dt_t[hh:hh + 1, :]).astype(BF16))
            x_lo = jnp.where(low_half, x_pair, 0.0).astype(BF16)
            x_hi = jnp.where(low_half, 0.0, x_pair).astype(BF16)
            if chunk % V7X_LANES == 0:
                y_pairs.append(_dot(jnp.concatenate(scores, axis=1),
                                    jnp.concatenate([x_lo, x_hi], axis=0)))
            else:
                y_pairs.append(_dot(scores[0], x_lo) + _dot(scores[1], x_hi))
            yield
        xg = xbc_ref[:, gcols]
        xw = (xg * wend_x[:, gcols]).astype(BF16)
        if single:
            y_off = _dot(cg, ht_bf[:, gcols]) * ecum_x[:, gcols]
            bg = xbc_ref[:, b0:b0 + SSD_STATE].astype(BF16)
            update = lax.dot_general(xw, bg, (((0,), (0,)), ((), ())), preferred_element_type=F32)
            heads = range(g * HEADS_PER_GROUP, (g + 1) * HEADS_PER_GROUP)
            decay_rows = jnp.concatenate(
                [jnp.broadcast_to(cdec[:, h:h + 1], (SSD_HEAD_DIM, SSD_STATE)) for h in heads], axis=0)
            h_new = decay_rows * h_given[gcols, :] + update
            hout_ref[g * HEADS_PER_GROUP:(g + 1) * HEADS_PER_GROUP] = h_new.reshape(
                HEADS_PER_GROUP, SSD_HEAD_DIM, SSD_STATE)
        else:
            ht_prev = ht_s[:, gcols]
            y_off = _dot(cg, ht_prev.astype(BF16)) * ecum_x[:, gcols]
            ht_s[:, gcols] = cdec_x[:, gcols] * ht_prev + _dot(bg_t, xw)
        y_groups.append(jnp.concatenate(y_pairs, axis=1) + y_off + dvec_ref[:, gcols] * xg)
        yield

    y = _rms(jnp.concatenate(y_groups, axis=1) * _silu(z_ref[...].astype(F32)), g_ref[...])
    y_ref[...] = y[0:out_steps, :].astype(y_ref.dtype)

    if not single:
        @pl.when(c == n_chunks - 1)
        def _():
            hout_ref[...] = ht_s[...].T.reshape(SSD_HEADS, SSD_HEAD_DIM, SSD_STATE)


def _ssd_call(xbc, dt, z, l, w, h0, h0_map, h_stack, *, chunk, out_len, seqs):
    batch, length, _ = xbc.shape
    depth = w["ssd_a_log"].shape[0]
    n_chunks = length // chunk
    assert length == n_chunks * chunk and (out_len == length or n_chunks == 1) and batch % seqs == 0
    out_steps = min(chunk, out_len)

    def per_seq(n_steps, width):
        return pl.BlockSpec((seqs, n_steps, width), lambda b, c: (b, c, 0))

    state_tail = (SSD_HEADS, SSD_HEAD_DIM, SSD_STATE)
    h0_block = (None,) * (h0.ndim - 4) + (seqs,) + state_tail
    operands = [xbc, dt, z, w["ssd_a_log"], w["ssd_d_cols"], w["ssd_norm_g"], w["head_spread"], h0]
    in_specs = [
        per_seq(chunk, SSD_CONV_DIM), per_seq(chunk, DT_PAD), per_seq(chunk, D_SSD),
        _layer_spec((1, DT_PAD), l), _layer_spec((1, D_SSD), l), _layer_spec((1, D_SSD), l),
        pl.BlockSpec((DT_PAD, D_SSD), lambda b, c: (0, 0), pipeline_mode=pl.Buffered(1)),
        pl.BlockSpec(h0_block, h0_map),
    ]
    aliases = {}
    if h_stack is not None:
        aliases = {len(operands): 1}
        operands.append(h_stack)
        in_specs.append(pl.BlockSpec(memory_space=pl.ANY))
    first = h_stack is None
    return pl.pallas_call(
        functools.partial(_ssd_body, chunk, n_chunks, out_steps, seqs, l if first else None),
        grid=(batch // seqs, n_chunks),
        in_specs=in_specs,
        out_specs=[
            per_seq(out_steps, D_SSD),
            pl.BlockSpec((depth, seqs) + state_tail, lambda b, c: (0, b, 0, 0, 0)) if first
            else pl.BlockSpec((None, seqs) + state_tail, lambda b, c: (l, b, 0, 0, 0)),
        ],
        out_shape=[
            jax.ShapeDtypeStruct((batch, out_len, D_SSD), BF16 if out_steps % (2 * V7X_SUBLANES) == 0 else F32),
            jax.ShapeDtypeStruct((depth, batch) + state_tail, F32),
        ],
        scratch_shapes=[pltpu.VMEM((seqs, SSD_STATE, D_SSD), F32)],
        input_output_aliases=aliases,
        compiler_params=_params(2),
        name="ssd_chunk",
    )(*operands)


def _regroup_w_in_body(wt_ref, o_ref):
    o_dt = COL_XBC + SSD_CONV_DIM
    o_gate = o_dt + SSD_HEADS

    def put(c0, rows):
        o_ref[:, c0:c0 + V7X_LANES] = rows.T.astype(BF16)

    for c0 in range(COL_Z, COL_GATE, V7X_LANES):
        put(c0, wt_ref[c0:c0 + V7X_LANES, :])
    for c0 in range(COL_GATE, COL_DT, V7X_LANES):
        r0 = o_gate + c0 - COL_GATE
        put(c0, wt_ref[r0:r0 + V7X_LANES, :])
    put(COL_DT, jnp.concatenate([wt_ref[o_dt:o_gate, :],
                                 jnp.zeros((DT_PAD - SSD_HEADS, D_MODEL), F32)], axis=0))


def _regroup_w_in(w_in):
    depth, d_model, in_cols = w_in.shape
    assert in_cols == D_SSD + SSD_CONV_DIM + SSD_HEADS + 2 * D_LRU and d_model == D_MODEL
    return pl.pallas_call(
        _regroup_w_in_body,
        grid=(depth,),
        in_specs=[pl.BlockSpec((None, in_cols, d_model), lambda l: (l, 0, 0),
                               pipeline_mode=pl.Buffered(1))],
        out_specs=pl.BlockSpec((None, d_model, IN_COLS_PAD), lambda l: (l, 0, 0)),
        out_shape=jax.ShapeDtypeStruct((depth, d_model, IN_COLS_PAD), BF16),
        compiler_params=_params(1),
        name="regroup_w_in",
    )(w_in.transpose(0, 2, 1))


def _prepare_weights(p):
    depth = p["w_in"].shape[0]

    def vec(a):
        return a.reshape(depth, 1, a.shape[-1])

    def pad_heads(a):
        return jnp.pad(a, ((0, 0), (0, DT_PAD - SSD_HEADS))).reshape(depth, 1, DT_PAD)

    def block_diag(a):
        per = V7X_MXU_DIM // LRU_BLOCK_W
        a = a.reshape(depth, LRU_GATE_GROUPS, per, LRU_BLOCK_W, LRU_BLOCK_W)
        eye = jnp.eye(per, dtype=a.dtype)
        a = a[:, :, :, :, None, :] * eye[None, None, :, None, :, None]
        return a.reshape(depth, LRU_GATE_GROUPS, V7X_MXU_DIM, V7X_MXU_DIM).astype(BF16)

    w = {
        "w_in": _regroup_w_in(p["w_in"]),
        "w_out": p["w_out"].astype(BF16),
        "lru_wa": block_diag(p["lru_wa"]),
        "lru_wx": block_diag(p["lru_wx"]),
        "ssd_dt_bias": pad_heads(p["ssd_dt_bias"]),
        "ssd_a_log": pad_heads(p["ssd_a_log"]),
        "ssd_d_cols": jnp.repeat(p["ssd_d"], SSD_HEAD_DIM, axis=-1).reshape(depth, 1, D_SSD),
        "head_spread": (jnp.arange(DT_PAD)[:, None] == jnp.arange(D_SSD)[None, :] // SSD_HEAD_DIM
                        ).astype(BF16),
    }
    for name in ("ssd_conv_w", "lru_conv_w"):
        w[name] = jnp.repeat(p[name], V7X_SUBLANES, axis=1)
    for name in ("ssd_conv_b", "lru_conv_b"):
        w[name] = jnp.repeat(p[name][:, None, :], V7X_SUBLANES, axis=1)
    for name in ("ffn1_wg", "ffn1_wu", "ffn1_wd", "ffn2_wg", "ffn2_wu", "ffn2_wd"):
        w[name] = p[name].astype(BF16)
    for name in ("ffn1_pre_g", "ffn1_post_g", "mix_pre_g", "mix_post_g", "ffn2_pre_g", "ffn2_post_g",
                 "ssd_norm_g", "lru_ba", "lru_bx", "lru_lambda"):
        w[name] = vec(p[name])
    return w


def _segment_mixer(x, l, w, states, ssd_stack, *, bsub, steps, chunk, pad_steps, seqs, n_sub=1):
    (sconv0, sconv0_map), (lconv0, lconv0_map), (h_lru0, h_lru0_map), (h_ssd0, h_ssd0_map) = states
    length = x.shape[1]
    z, xbc, dt, y_lru, sconv, lconv, h_lru = _in_proj_call(
        x, l, w, sconv0, sconv0_map, lconv0, lconv0_map,
        h_lru0, h_lru0_map, bsub=bsub, steps=steps, pad_steps=pad_steps, n_sub=n_sub)
    y_ssd, ssd_stack = _ssd_call(xbc, dt, z, l, w, h_ssd0, h_ssd0_map, ssd_stack,
                                 chunk=chunk, out_len=length, seqs=seqs)
    return y_ssd, y_lru, (sconv, h_lru, lconv), ssd_stack


def kernel(x_prompt, x_sample, state_ssd, state_ssd_conv, state_lru, state_lru_conv, meta_tokens,
           ffn1_pre_g, ffn1_post_g, ffn1_wg, ffn1_wu, ffn1_wd, mix_pre_g, mix_post_g, w_in,
           ssd_conv_w, ssd_conv_b, ssd_dt_bias, ssd_a_log, ssd_d, ssd_norm_g, lru_conv_w, lru_conv_b,
           lru_wa, lru_ba, lru_wx, lru_bx, lru_lambda, w_out, ffn2_pre_g, ffn2_post_g, ffn2_wg,
           ffn2_wu, ffn2_wd):
    bp, seq, _ = x_prompt.shape
    bs, dec_seq, _ = x_sample.shape
    depth = w_in.shape[0]
    assert bp == V7X_SUBLANES and seq % SSD_CHUNK == 0 and bs % SAMPLE_BATCH_TILE == 0
    w = _prepare_weights(dict(
        ffn1_pre_g=ffn1_pre_g, ffn1_post_g=ffn1_post_g, ffn1_wg=ffn1_wg, ffn1_wu=ffn1_wu,
        ffn1_wd=ffn1_wd, mix_pre_g=mix_pre_g, mix_post_g=mix_post_g, w_in=w_in,
        ssd_conv_w=ssd_conv_w, ssd_conv_b=ssd_conv_b, ssd_dt_bias=ssd_dt_bias, ssd_a_log=ssd_a_log,
        ssd_d=ssd_d, ssd_norm_g=ssd_norm_g, lru_conv_w=lru_conv_w, lru_conv_b=lru_conv_b,
        lru_wa=lru_wa, lru_ba=lru_ba, lru_wx=lru_wx, lru_bx=lru_bx, lru_lambda=lru_lambda,
        w_out=w_out, ffn2_pre_g=ffn2_pre_g, ffn2_post_g=ffn2_post_g, ffn2_wg=ffn2_wg,
        ffn2_wu=ffn2_wu, ffn2_wd=ffn2_wd))

    xm = jnp.broadcast_to(meta_tokens.astype(F32)[None], (bp, N_META, D_MODEL))
    xp = x_prompt
    xs = x_sample
    sample_steps = ((dec_seq + V7X_SUBLANES - 1) // V7X_SUBLANES) * V7X_SUBLANES
    sconv_in = state_ssd_conv.transpose(0, 2, 1, 3)
    lconv_in = state_lru_conv.transpose(0, 2, 1, 3)
    tail3 = lambda b, t: (0, b, 0)
    vec2 = lambda b, t: (b, 0)
    seq_state = lambda b, c: (b, 0, 0, 0)
    zero_states = ((jnp.zeros((CONV_TAIL, bp, SSD_CONV_DIM), F32), tail3),
                   (jnp.zeros((CONV_TAIL, bp, D_LRU), F32), tail3),
                   (jnp.zeros((bp, D_LRU), F32), vec2),
                   (jnp.zeros((bp, SSD_HEADS, SSD_HEAD_DIM, SSD_STATE), F32), seq_state))

    layer_state = lambda b, c, l: (l, b, 0, 0, 0)
    p_out = [[] for _ in range(3)]
    s_out = [[] for _ in range(3)]
    m_ssd = p_ssd = s_ssd = None
    def flat(a):
        return a.reshape(a.shape[0] * a.shape[1], a.shape[2])

    for l in range(depth):
        at_l = functools.partial(layer_state, l=l)
        shapes = (xp.shape, xs.shape, xm.shape)
        xp, xs, xm = (a.reshape(s) for a, s in
                      zip(_ffn_call([flat(xp), flat(xs), flat(xm)], l, w, "ffn1"), shapes))
        m_ssd_y, m_lru_y, m_st, m_ssd = _segment_mixer(
            xm, l, w, zero_states, m_ssd,
            bsub=bp, steps=N_META, chunk=N_META, pad_steps=0, seqs=PROMPT_SEQS)
        p_ssd_y, p_lru_y, p_st, p_ssd = _segment_mixer(
            xp, l, w, ((m_st[0], tail3), (m_st[2], tail3), (m_st[1], vec2), (m_ssd, at_l)), p_ssd,
            bsub=bp, steps=PROMPT_STEPS, chunk=SSD_CHUNK, pad_steps=0, seqs=PROMPT_SEQS,
            n_sub=PROMPT_SUB_TILES)
        s_ssd_y, s_lru_y, s_st, s_ssd = _segment_mixer(
            xs, l, w, ((sconv_in, lambda b, t, l=l: (l, 0, b, 0)),
                       (lconv_in, lambda b, t, l=l: (l, 0, b, 0)),
                       (state_lru, lambda b, t, l=l: (l, b, 0)),
                       (state_ssd, at_l)), s_ssd,
            bsub=SAMPLE_BATCH_TILE, steps=dec_seq, chunk=sample_steps,
            pad_steps=sample_steps - dec_seq, seqs=SAMPLE_SEQS)
        xp, xs, xm = (a.reshape(s) for a, s in zip(_out_ffn_call(
            [(flat(xp), flat(p_ssd_y), flat(p_lru_y)), (flat(xs), flat(s_ssd_y), flat(s_lru_y)),
             (flat(xm), flat(m_ssd_y), flat(m_lru_y))], l, w), shapes))
        for acc, st in ((p_out, p_st), (s_out, s_st)):
            for k in range(3):
                acc[k].append(st[k])

    p_conv, p_lru, p_lconv = (jnp.stack(a) for a in p_out)
    s_conv, s_lru, s_lconv = (jnp.stack(a) for a in s_out)
    p_conv, p_lconv, s_conv, s_lconv = (a.transpose(0, 2, 1, 3) for a in (p_conv, p_lconv, s_conv, s_lconv))
    return (xp, xs, p_ssd, p_conv, p_lru, p_lconv, s_ssd, s_conv, s_lru, s_lconv)
```
